```python
import math
import jax, jax.numpy as jnp
from jax import lax
import numpy as np

D_MODEL = 1024
BATCH = 4
SEQ = 8192
DEPTH = 2

HG_HEADS = 4
HG_DK = 128
HG_DV = 128
HG_KWIDTH = HG_HEADS * HG_DK
HG_WIDTH = HG_HEADS * HG_DV
HG_CHUNK = 64
NSA_HEADS = 8
NSA_GROUPS = 2
NSA_HPG = NSA_HEADS // NSA_GROUPS
NSA_DH = 64
NSA_WIDTH = NSA_HEADS * NSA_DH
KV_WIDTH = NSA_GROUPS * NSA_DH
CMP_LEN = 32
CMP_STRIDE = 16
CMP_HIDDEN = 2 * NSA_DH
SEL_BLOCK = 64
N_SEL = 16
WINDOW = 512
Q_BLOCK = 128
FORCE_SCORE = 1e9
NEG_BIG = -1e30
REL_BUCKETS = 32
REL_MAX_DIST = 2048
D_FF = 2752
N_EXPERTS = 8
TOP_K = 2
D_FF_EXPERT = 3584
MOE_ROW_BLOCK = 512
N_DENSE = (DEPTH + 1) // 2
N_MOE = DEPTH // 2
DN_ALPHA = (2 * DEPTH) ** 0.25
DN_BETA = (8 * DEPTH) ** -0.25
LN_EPS = 1e-5
IN_SIZES = (HG_KWIDTH, HG_KWIDTH, HG_WIDTH, HG_WIDTH, NSA_WIDTH,
            KV_WIDTH, KV_WIDTH, KV_WIDTH, KV_WIDTH, KV_WIDTH, KV_WIDTH,
            3 * NSA_HEADS, D_MODEL, D_MODEL)
IN_WIDTH = sum(IN_SIZES)
VALUE_SLOTS = (2, 6, 8, 10)

kernel_name = "hgrn2_nsa_gated_hybrid_deepnorm_moe"


def layer_norm(x, g, b):
    xf = x.astype(jnp.float32)
    mu = jnp.mean(xf, -1, keepdims=True)
    var = jnp.mean(jnp.square(xf - mu), -1, keepdims=True)
    return ((xf - mu) * lax.rsqrt(var + LN_EPS) * g + b).astype(x.dtype)


def rel_bucket(dist):
    n = jnp.maximum(dist, 0)
    exact = REL_BUCKETS // 2
    large = exact + (jnp.log(jnp.maximum(n, exact).astype(jnp.float32) / exact)
                     / math.log(REL_MAX_DIST / exact) * (REL_BUCKETS - exact)).astype(jnp.int32)
    return jnp.where(n < exact, n, jnp.minimum(large, REL_BUCKETS - 1))


def masked_softmax(logits, mask):
    logits = jnp.where(mask, logits, NEG_BIG)
    m = jnp.max(logits, -1, keepdims=True)
    p = jnp.where(mask, jnp.exp(logits - m), 0.0)
    return p / jnp.maximum(jnp.sum(p, -1, keepdims=True), 1e-30)


def hgrn2_mix(q, f_logit, i, g, lb, norm_w):
    B, S, _ = q.shape
    f32 = jnp.float32
    nc = S // HG_CHUNK
    lb = lb.astype(f32)
    k = (1.0 - lb) * jax.nn.sigmoid(-f_logit.astype(f32))
    log_f = jnp.log1p(-k)

    def to_chunks(t, d):
        return t.astype(f32).reshape(B, nc, HG_CHUNK, HG_HEADS, d).transpose(1, 0, 3, 2, 4)

    qc, kc, vc = to_chunks(q, HG_DK), to_chunks(k, HG_DK), to_chunks(i, HG_DV)
    bc = jnp.cumsum(to_chunks(log_f, HG_DK), axis=3)
    causal = jnp.tril(jnp.ones((HG_CHUNK, HG_CHUNK), bool))[:, :, None]

    def step(state, inp):
        qt, kt, vt, bt = inp
        o_inter = jnp.einsum('bhtd,bhdv->bhtv', qt * jnp.exp(bt), state)
        diff = jnp.where(causal, bt[:, :, :, None, :] - bt[:, :, None, :, :], NEG_BIG)
        att = jnp.einsum('bhtsd,bhsd->bhts', qt[:, :, :, None, :] * jnp.exp(diff), kt)
        o_intra = jnp.einsum('bhts,bhsv->bhtv', att, vt)
        b_last = bt[:, :, -1:, :]
        state = (jnp.exp(b_last[:, :, 0, :, None]) * state
                 + jnp.einsum('bhsd,bhsv->bhdv', kt * jnp.exp(b_last - bt), vt))
        return state, o_inter + o_intra

    s0 = jnp.zeros((B, HG_HEADS, HG_DK, HG_DV), f32)
    _, o = lax.scan(step, s0, (qc, kc, vc, bc))
    o = o.transpose(1, 0, 3, 2, 4).reshape(B, S, HG_HEADS, HG_DV)
    o = o * lax.rsqrt(jnp.mean(o * o, -1, keepdims=True) + 1e-6)
    o = o.reshape(B, S, HG_WIDTH) * norm_w * jax.nn.silu(g.astype(f32))
    return o.astype(q.dtype)


def compress_blocks(kv, pos, w1, w2):
    B, S = kv.shape[:2]
    nc = (S - CMP_LEN) // CMP_STRIDE + 1
    idx = np.arange(nc)[:, None] * CMP_STRIDE + np.arange(CMP_LEN)[None, :]
    blk = kv[:, idx] + pos[:, None, :]
    blk = blk.transpose(0, 3, 1, 2, 4).reshape(B, NSA_GROUPS, nc, CMP_LEN * NSA_DH)
    return jax.nn.gelu(blk @ w1) @ w2


def nsa_mix(q, kc_raw, vc_raw, ks, vs, kw, vw, gate_logit, cmp_pos, cmp_w1, cmp_w2, rel_bias):
    B, S, _ = q.shape
    f32 = jnp.float32
    G, P, dh = NSA_GROUPS, NSA_HPG, NSA_DH
    nc = (S - CMP_LEN) // CMP_STRIDE + 1
    ns = S // SEL_BLOCK
    n_sel = min(N_SEL, ns)
    scale = NSA_DH ** -0.5

    def kv_heads(t):
        return t.reshape(B, S, G, dh).transpose(0, 2, 1, 3)

    qh = q.reshape(B, S, G, P, dh).transpose(0, 2, 3, 1, 4)
    k_cmp = compress_blocks(kc_raw.reshape(B, S, G, dh), cmp_pos[0], cmp_w1[0], cmp_w2[0])
    v_cmp = compress_blocks(vc_raw.reshape(B, S, G, dh), cmp_pos[1], cmp_w1[1], cmp_w2[1]).astype(f32)
    ks_blk = kv_heads(ks).reshape(B, G, ns, SEL_BLOCK, dh)
    vs_blk = kv_heads(vs).reshape(B, G, ns, SEL_BLOCK, dh)
    pad = ((0, 0), (0, 0), (WINDOW, 0), (0, 0))
    kw_pad = jnp.pad(kv_heads(kw), pad)
    vw_pad = jnp.pad(kv_heads(vw), pad)
    gates = jax.nn.sigmoid(gate_logit.astype(f32)).reshape(B, S, G, P, 3).transpose(0, 2, 3, 1, 4)
    table = rel_bias.astype(f32).reshape(REL_BUCKETS, G, P).transpose(1, 0, 2)
    cmp_end = jnp.arange(nc) * CMP_STRIDE + CMP_LEN - 1
    cs = np.arange(nc)[:, None] * CMP_STRIDE
    ss = np.arange(ns)[None, :] * SEL_BLOCK
    overlap = np.clip(np.minimum(cs + CMP_LEN, ss + SEL_BLOCK) - np.maximum(cs, ss), 0, None) / CMP_LEN
    cmp_to_sel = jnp.asarray(overlap, f32)
    b_idx = jnp.arange(B)[:, None, None, None]
    g_idx = jnp.arange(G)[None, :, None, None]

    def bias_2d(dist):
        return table[:, rel_bucket(dist), :].transpose(0, 3, 1, 2)

    def block(qb):
        t0 = qb * Q_BLOCK
        tpos = t0 + jnp.arange(Q_BLOCK)
        qq = lax.dynamic_slice_in_dim(qh, t0, Q_BLOCK, axis=3)
        gg = lax.dynamic_slice_in_dim(gates, t0, Q_BLOCK, axis=3)
        dist_c = tpos[:, None] - cmp_end[None, :]
        lc = jnp.einsum('bgpqd,bgnd->bgpqn', qq, k_cmp, preferred_element_type=f32) * scale + bias_2d(dist_c)
        pc = masked_softmax(lc, dist_c >= 0)
        o_cmp = jnp.einsum('bgpqn,bgnd->bgpqd', pc, v_cmp)
        imp = jnp.einsum('bgpqn,ns->bgqs', pc, cmp_to_sel)
        sblk = jnp.arange(ns)[None, :]
        cur = (tpos // SEL_BLOCK)[:, None]
        forced = (sblk == 0) | (sblk == cur) | (sblk == cur - 1)
        score = jnp.where(forced, FORCE_SCORE, jnp.where(sblk <= cur, imp, NEG_BIG))
        _, sel = lax.top_k(score, n_sel)
        k_sel = ks_blk[b_idx, g_idx, sel].reshape(B, G, Q_BLOCK, n_sel * SEL_BLOCK, dh)
        v_sel = vs_blk[b_idx, g_idx, sel].reshape(B, G, Q_BLOCK, n_sel * SEL_BLOCK, dh).astype(f32)
        kpos = (sel[..., None] * SEL_BLOCK + jnp.arange(SEL_BLOCK)).reshape(B, G, Q_BLOCK, n_sel * SEL_BLOCK)
        dist_s = tpos[None, None, :, None] - kpos
        bias_s = table[g_idx, rel_bucket(dist_s)].transpose(0, 1, 4, 2, 3)
        ls = jnp.einsum('bgpqd,bgqkd->bgpqk', qq, k_sel, preferred_element_type=f32) * scale + bias_s
        ps = masked_softmax(ls, (dist_s >= 0)[:, :, None])
        o_sel = jnp.einsum('bgpqk,bgqkd->bgpqd', ps, v_sel)
        k_win = lax.dynamic_slice_in_dim(kw_pad, t0, WINDOW + Q_BLOCK, axis=2)
        v_win = lax.dynamic_slice_in_dim(vw_pad, t0, WINDOW + Q_BLOCK, axis=2).astype(f32)
        kpos_w = t0 - WINDOW + jnp.arange(WINDOW + Q_BLOCK)
        dist_w = tpos[:, None] - kpos_w[None, :]
        valid_w = (dist_w >= 0) & (dist_w < WINDOW) & (kpos_w[None, :] >= 0)
        lw = jnp.einsum('bgpqd,bgkd->bgpqk', qq, k_win, preferred_element_type=f32) * scale + bias_2d(dist_w)
        pw = masked_softmax(lw, valid_w)
        o_win = jnp.einsum('bgpqk,bgkd->bgpqd', pw, v_win)
        return gg[..., 0:1] * o_cmp + gg[..., 1:2] * o_sel + gg[..., 2:3] * o_win

    o = lax.map(block, jnp.arange(S // Q_BLOCK))
    o = o.transpose(1, 0, 4, 2, 3, 5).reshape(B, S, NSA_WIDTH)
    return o.astype(q.dtype)


def token_mixer(h, w_in, lb, hg_norm_w, cmp_pos, cmp_w1, cmp_w2, rel_bias, w_branch_a, w_branch_b, w_out):
    split_points = [int(c) for c in np.cumsum(IN_SIZES)[:-1]]
    proj = h @ w_in
    (hq, hf, hi, hg, nq, kcr, vcr, ksl, vsl, kwn, vwn, ngate, gate_a, gate_b) = jnp.split(proj, split_points, axis=-1)
    o_a = hgrn2_mix(hq, hf, hi, hg, lb, hg_norm_w)
    o_b = nsa_mix(nq, kcr, vcr, ksl, vsl, kwn, vwn, ngate, cmp_pos, cmp_w1, cmp_w2, rel_bias)
    y = jax.nn.sigmoid(gate_a) * (o_a @ w_branch_a) + jax.nn.sigmoid(gate_b) * (o_b @ w_branch_b)
    return y @ w_out


def swiglu(x, wg, wu, wd):
    return (jax.nn.silu(x @ wg) * (x @ wu)) @ wd


def moe_swiglu(x, w_router, wg, wu, wd):
    T, D = x.shape
    TK = T * TOP_K
    logits = jnp.matmul(x, w_router, preferred_element_type=jnp.float32)
    top_v, top_e = lax.top_k(logits, TOP_K)
    gate = jax.nn.softmax(top_v, -1).astype(x.dtype)
    flat_e = top_e.reshape(-1)
    flat_tok = jnp.arange(TK) // TOP_K
    order = jnp.argsort(flat_e)
    e_sorted = flat_e[order]
    tok_sorted = flat_tok[order]
    gate_sorted = gate.reshape(-1)[order]
    counts = jnp.bincount(flat_e, length=N_EXPERTS)
    padded = (counts + MOE_ROW_BLOCK - 1) // MOE_ROW_BLOCK * MOE_ROW_BLOCK
    start = jnp.cumsum(counts) - counts
    pstart = jnp.cumsum(padded) - padded
    dest = pstart[e_sorted] + jnp.arange(TK) - start[e_sorted]
    n_blocks = -(-(TK + N_EXPERTS * (MOE_ROW_BLOCK - 1)) // MOE_ROW_BLOCK)
    n_rows = n_blocks * MOE_ROW_BLOCK
    row_tok = jnp.zeros((n_rows,), jnp.int32).at[dest].set(tok_sorted)
    blk_e = jnp.minimum(jnp.searchsorted(jnp.cumsum(padded), jnp.arange(n_blocks) * MOE_ROW_BLOCK, side='right'),
                        N_EXPERTS - 1)
    xs = x[row_tok].reshape(n_blocks, MOE_ROW_BLOCK, D)
    ys = lax.map(lambda a: swiglu(a[0], wg[a[1]], wu[a[1]], wd[a[1]]), (xs, blk_e))
    ys = ys.reshape(n_rows, D)[dest] * gate_sorted[:, None]
    return jnp.zeros_like(x).at[tok_sorted].add(ys)


def setup_inputs(seed: int = 0) -> dict:
    key = jax.random.key(seed)
    k = jax.random.split(key, 22)
    f32 = jnp.float32

    def nrm(kk, shape, s):
        return jax.random.normal(kk, shape, f32) * s

    col_scale = jnp.asarray(np.concatenate(
        [np.full((n,), DN_BETA if j in VALUE_SLOTS else 1.0, np.float32) for j, n in enumerate(IN_SIZES)]))
    return {
        "x": nrm(k[0], (BATCH, SEQ, D_MODEL), 1.0),
        "w_in": nrm(k[1], (DEPTH, D_MODEL, IN_WIDTH), D_MODEL ** -0.5) * col_scale,
        "hg_lb_logits": nrm(k[2], (DEPTH, HG_KWIDTH), 0.5),
        "hg_norm_w": 1.0 + nrm(k[3], (DEPTH, HG_WIDTH), 0.02),
        "cmp_pos": nrm(k[4], (DEPTH, 2, CMP_LEN, NSA_DH), 0.1),
        "cmp_w1": nrm(k[5], (DEPTH, 2, CMP_LEN * NSA_DH, CMP_HIDDEN), (CMP_LEN * NSA_DH) ** -0.5),
        "cmp_w2": nrm(k[6], (DEPTH, 2, CMP_HIDDEN, NSA_DH), CMP_HIDDEN ** -0.5),
        "rel_bias": nrm(k[7], (REL_BUCKETS, NSA_HEADS), 0.3),
        "w_branch_a": nrm(k[8], (DEPTH, HG_WIDTH, D_MODEL), HG_WIDTH ** -0.5 * DN_BETA),
        "w_branch_b": nrm(k[9], (DEPTH, NSA_WIDTH, D_MODEL), NSA_WIDTH ** -0.5 * DN_BETA),
        "w_out": nrm(k[10], (DEPTH, D_MODEL, D_MODEL), D_MODEL ** -0.5 * DN_BETA),
        "ln1_g": 1.0 + nrm(k[11], (DEPTH, D_MODEL), 0.02),
        "ln1_b": nrm(k[12], (DEPTH, D_MODEL), 0.02),
        "ln2_g": 1.0 + nrm(k[13], (DEPTH, D_MODEL), 0.02),
        "ln2_b": nrm(k[14], (DEPTH, D_MODEL), 0.02),
        "ffn_w_gate": nrm(k[15], (N_DENSE, D_MODEL, D_FF), D_MODEL ** -0.5),
        "ffn_w_up": nrm(k[16], (N_DENSE, D_MODEL, D_FF), D_MODEL ** -0.5 * DN_BETA),
        "ffn_w_down": nrm(k[17], (N_DENSE, D_FF, D_MODEL), D_FF ** -0.5 * DN_BETA),
        "moe_router": nrm(k[18], (N_MOE, D_MODEL, N_EXPERTS), D_MODEL ** -0.5),
        "moe_w_gate": nrm(k[19], (N_MOE, N_EXPERTS, D_MODEL, D_FF_EXPERT), D_MODEL ** -0.5),
        "moe_w_up": nrm(k[20], (N_MOE, N_EXPERTS, D_MODEL, D_FF_EXPERT), D_MODEL ** -0.5 * DN_BETA),
        "moe_w_down": nrm(k[21], (N_MOE, N_EXPERTS, D_FF_EXPERT, D_MODEL), D_FF_EXPERT ** -0.5 * DN_BETA),
    }


def reference(x, w_in, hg_lb_logits, hg_norm_w, cmp_pos, cmp_w1, cmp_w2, rel_bias, w_branch_a, w_branch_b,
              w_out, ln1_g, ln1_b, ln2_g, ln2_b, ffn_w_gate, ffn_w_up, ffn_w_down, moe_router, moe_w_gate,
              moe_w_up, moe_w_down):
    B, S, D = x.shape
    p_lb = jax.nn.softmax(hg_lb_logits.astype(jnp.float32), axis=0)
    lbs = jnp.cumsum(p_lb, axis=0) - p_lb[0]
    for l in range(DEPTH):
        mix = token_mixer(x, w_in[l], lbs[l], hg_norm_w[l], cmp_pos[l], cmp_w1[l], cmp_w2[l], rel_bias,
                          w_branch_a[l], w_branch_b[l], w_out[l])
        x = layer_norm(DN_ALPHA * x + mix, ln1_g[l], ln1_b[l])
        h = x.reshape(B * S, D)
        if l % 2 == 0:
            f = swiglu(h, ffn_w_gate[l // 2], ffn_w_up[l // 2], ffn_w_down[l // 2])
        else:
            f = moe_swiglu(h, moe_router[l // 2], moe_w_gate[l // 2], moe_w_up[l // 2], moe_w_down[l // 2])
        x = layer_norm(DN_ALPHA * x + f.reshape(B, S, D), ln2_g[l], ln2_b[l])
    return x
```

```python
import functools
import math

import jax
import jax.numpy as jnp
import numpy as np
from jax import lax
from jax.experimental import pallas as pl
from jax.experimental.pallas import tpu as pltpu

F32 = jnp.float32
BF16 = jnp.bfloat16

D_MODEL = 1024
DEPTH = 2
HG_HEADS = 4
HG_DK = 128
HG_WIDTH = HG_HEADS * HG_DK
HG_CHUNK = 64
HG_SUB = 16
NSA_HEADS = 8
NSA_GROUPS = 2
NSA_HPG = NSA_HEADS // NSA_GROUPS
NSA_DH = 64
NSA_WIDTH = NSA_HEADS * NSA_DH
KV_WIDTH = NSA_GROUPS * NSA_DH
CMP_LEN = 32
CMP_STRIDE = 16
CMP_HIDDEN = 2 * NSA_DH
SEL_BLOCK = 64
N_SEL = 16
WINDOW = 512
FORCE_SCORE = 1e9
NEG_BIG = -1e30
REL_BUCKETS = 32
REL_MAX_DIST = 2048
D_FF = 2752
N_EXPERTS = 8
TOP_K = 2
D_FF_EXPERT = 3584
MOE_ROW_BLOCK = 512
DN_ALPHA = (2 * DEPTH) ** 0.25
LN_EPS = 1e-5
IN_SIZES = (HG_WIDTH, HG_WIDTH, HG_WIDTH, HG_WIDTH, NSA_WIDTH,
            KV_WIDTH, KV_WIDTH, KV_WIDTH, KV_WIDTH, KV_WIDTH, KV_WIDTH,
            3 * NSA_HEADS, D_MODEL, D_MODEL)

LANES = 128
COL_GA = 0
COL_HG = 2 * D_MODEL
COL_NQ = COL_HG + 4 * HG_WIDTH
COL_KV = COL_NQ + NSA_WIDTH
COL_NG = COL_KV + 6 * KV_WIDTH
PROJ_W = COL_NG + LANES
PROJ_TN = 512
PROJ_WP = -(-PROJ_W // PROJ_TN) * PROJ_TN

ATT_TQ = 256
ATT_TK = 512
BIAS_ND = -(-(REL_MAX_DIST + ATT_TK) // ATT_TQ)
VMEM_LIMIT = 48 * 1024 * 1024


def _cparams(sem):
    return pltpu.CompilerParams(dimension_semantics=sem, vmem_limit_bytes=VMEM_LIMIT)


def _proj_kernel(x_ref, w_ref, o_ref, xb_ref):
    @pl.when(pl.program_id(1) == 0)
    def _():
        xb_ref[...] = x_ref[...].astype(BF16)

    o_ref[...] = jnp.dot(xb_ref[...], w_ref[...], preferred_element_type=F32)


def _project(x, w, tm, tn):
    m, k = x.shape
    n = w.shape[1]
    return pl.pallas_call(
        _proj_kernel,
        grid=(m // tm, n // tn),
        in_specs=[pl.BlockSpec((tm, k), lambda i, j: (i, 0)),
                  pl.BlockSpec((k, tn), lambda i, j: (0, j))],
        out_specs=pl.BlockSpec((tm, tn), lambda i, j: (i, j)),
        out_shape=jax.ShapeDtypeStruct((m, n), F32),
        scratch_shapes=[pltpu.VMEM((tm, k), BF16)],
        compiler_params=_cparams(("parallel", "arbitrary")),
    )(x, w)


def _layer_norm(y, g, b):
    mu = jnp.mean(y, -1, keepdims=True)
    yc = y - mu
    var = jnp.mean(yc * yc, -1, keepdims=True)
    return yc * lax.rsqrt(var + LN_EPS) * g + b


def _hgrn_kernel(q_ref, f_ref, i_ref, g_ref, lb_ref, nw_ref, o_ref, st_ref, *, n_chunks):
    @pl.when(pl.program_id(2) == 0)
    def _():
        st_ref[...] = jnp.zeros_like(st_ref)

    c = HG_CHUNK
    one_minus_lb = 1.0 - lb_ref[...]
    nw = nw_ref[...]
    row = lax.broadcasted_iota(jnp.int32, (c, c), 0)
    col = lax.broadcasted_iota(jnp.int32, (c, c), 1)
    tril = (row >= col).astype(F32)
    sub_pos = lax.broadcasted_iota(jnp.int32, (c, 1), 0) % HG_SUB

    def chunk(ci, carry):
        r0 = pl.multiple_of(ci * c, c)
        q = q_ref[pl.ds(r0, c), :]
        z = f_ref[pl.ds(r0, c), :]
        v = i_ref[pl.ds(r0, c), :]
        g = g_ref[pl.ds(r0, c), :]
        k = one_minus_lb * jax.nn.sigmoid(-z)
        log_f = jnp.log1p(-k)
        b = jnp.dot(tril, log_f, preferred_element_type=F32, precision=lax.Precision.HIGHEST)
        b_last = b[c - 1:c, :]
        st = st_ref[...]
        o = lax.dot_general(q * jnp.exp(b), st, (((1,), (1,)), ((), ())), preferred_element_type=F32)
        parts = [jnp.zeros((HG_SUB, HG_DK), F32)]
        for sb in range(1, c // HG_SUB):
            lo = sb * HG_SUB
            ref_b = b[lo - 1:lo, :]
            qt = q[lo:lo + HG_SUB, :] * jnp.exp(b[lo:lo + HG_SUB, :] - ref_b)
            kt = k[:lo, :] * jnp.exp(ref_b - b[:lo, :])
            att = lax.dot_general(qt, kt, (((1,), (1,)), ((), ())), preferred_element_type=F32)
            parts.append(jnp.dot(att, v[:lo, :], preferred_element_type=F32))
        o = o + jnp.concatenate(parts, axis=0)
        for lag in range(HG_SUB):
            if lag == 0:
                ks, bs, vs = k, b, v
            else:
                ks = pltpu.roll(k, lag, 0)
                bs = pltpu.roll(b, lag, 0)
                vs = pltpu.roll(v, lag, 0)
            valid = sub_pos >= lag
            e = jnp.exp(jnp.where(valid, b - bs, 0.0))
            a = jnp.sum(q * ks * e, axis=1, keepdims=True)
            o = o + jnp.where(valid, a, 0.0) * vs
        khat = k * jnp.exp(b_last - b)
        st_ref[...] = st * jnp.exp(b_last) + lax.dot_general(
            v, khat, (((0,), (0,)), ((), ())), preferred_element_type=F32)
        o = o * lax.rsqrt(jnp.mean(o * o, -1, keepdims=True) + 1e-6)
        o_ref[pl.ds(r0, c), :] = o * nw * (g * jax.nn.sigmoid(g))
        return carry

    lax.fori_loop(0, n_chunks, chunk, 0)


def _hgrn(proj, lb, norm_w, ts):
    bsz, s, _ = proj.shape
    nblk = HG_WIDTH // LANES

    def col_spec(off):
        return pl.BlockSpec((None, ts, LANES), lambda b, h, t: (b, t, off + h))

    head_spec = pl.BlockSpec((1, LANES), lambda b, h, t: (0, h))
    return pl.pallas_call(
        functools.partial(_hgrn_kernel, n_chunks=ts // HG_CHUNK),
        grid=(bsz, HG_HEADS, s // ts),
        in_specs=[col_spec(COL_HG // LANES), col_spec(COL_HG // LANES + nblk),
                  col_spec(COL_HG // LANES + 2 * nblk), col_spec(COL_HG // LANES + 3 * nblk),
                  head_spec, head_spec],
        out_specs=pl.BlockSpec((None, ts, LANES), lambda b, h, t: (b, t, h)),
        out_shape=jax.ShapeDtypeStruct((bsz, s, HG_WIDTH), F32),
        scratch_shapes=[pltpu.VMEM((HG_DK, HG_DK), F32)],
        compiler_params=_cparams(("parallel", "parallel", "arbitrary")),
    )(proj, proj, proj, proj, lb, norm_w)


def _compress_kernel(a_ref, pos_ref, w1_ref, w2_ref, o_ref):
    half = CMP_STRIDE * NSA_DH
    a = a_ref[...]
    n = a.shape[0]
    a1 = (a + pos_ref[:, :half]).astype(BF16)
    a2 = (a + pos_ref[:, half:]).astype(BF16)
    y1 = jnp.dot(a1, w1_ref[:half, :], preferred_element_type=F32)
    y2 = jnp.dot(a2, w1_ref[half:, :], preferred_element_type=F32)
    hid = jax.nn.gelu(y1 + pltpu.roll(y2, n - 1, 0))
    o_ref[...] = jnp.dot(hid.astype(BF16), w2_ref[...], preferred_element_type=F32).astype(o_ref.dtype)


def _compress(a, pos, w1, w2):
    _, bsz, g, n, width = a.shape
    return pl.pallas_call(
        _compress_kernel,
        grid=(2, bsz, g),
        in_specs=[pl.BlockSpec((None, None, None, n, width), lambda c, b, h: (c, b, h, 0, 0)),
                  pl.BlockSpec((None, 1, 2 * width), lambda c, b, h: (c, 0, 0)),
                  pl.BlockSpec((None, 2 * width, CMP_HIDDEN), lambda c, b, h: (c, 0, 0)),
                  pl.BlockSpec((None, CMP_HIDDEN, NSA_DH), lambda c, b, h: (c, 0, 0))],
        out_specs=pl.BlockSpec((None, None, None, n, NSA_DH), lambda c, b, h: (c, b, h, 0, 0)),
        out_shape=jax.ShapeDtypeStruct((2, bsz, g, n, NSA_DH), BF16),
        compiler_params=_cparams(("parallel", "parallel", "parallel")),
    )(a, pos, w1, w2)


def _bucket_of(dist):
    n = jnp.maximum(dist, 0)
    exact = REL_BUCKETS // 2
    large = exact + (jnp.log(jnp.maximum(n, exact).astype(F32) / exact)
                     / math.log(REL_MAX_DIST / exact) * (REL_BUCKETS - exact)).astype(jnp.int32)
    return jnp.where(n < exact, n, jnp.minimum(large, REL_BUCKETS - 1))


def _bias_kernel(rb_ref, o_ref, *, dist_fn):
    shape = o_ref.shape
    row = lax.broadcasted_iota(jnp.int32, shape, 0)
    col = lax.broadcasted_iota(jnp.int32, shape, 1)
    bucket = _bucket_of(dist_fn(pl.program_id(1), row, col))
    h = pl.program_id(0)

    def pick(kb, acc):
        return jnp.where(bucket == kb, rb_ref[kb, h], acc)

    o_ref[...] = lax.fori_loop(0, REL_BUCKETS, pick, jnp.zeros(shape, F32))


def _bias_tiles(rel_bias, n_tiles, rows, cols, dist_fn):
    return pl.pallas_call(
        functools.partial(_bias_kernel, dist_fn=dist_fn),
        grid=(NSA_HEADS, n_tiles),
        in_specs=[pl.BlockSpec(memory_space=pltpu.SMEM)],
        out_specs=pl.BlockSpec((None, None, rows, cols), lambda h, t: (h, t, 0, 0)),
        out_shape=jax.ShapeDtypeStruct((NSA_HEADS, n_tiles, rows, cols), F32),
        compiler_params=_cparams(("parallel", "parallel")),
    )(rel_bias)


def _cmp_kernel(q_ref, k_ref, v_ref, bias_ref, c2s_ref, gl_ref, o_ref, mem_ref, *, tq):
    p = NSA_HPG
    t0 = pl.program_id(0) * tq
    ncp = k_ref.shape[0]
    q = q_ref[...].reshape(p * tq, NSA_DH)
    logits = lax.dot_general(q, k_ref[...], (((1,), (1,)), ((), ())), preferred_element_type=F32)
    logits = logits.reshape(p, tq, ncp) + bias_ref[:, 0]
    tpos = t0 + lax.broadcasted_iota(jnp.int32, (tq, ncp), 0)
    cend = lax.broadcasted_iota(jnp.int32, (tq, ncp), 1) * CMP_STRIDE + (CMP_LEN - 1)
    mask = (tpos >= cend)[None]
    logits = jnp.where(mask, logits, NEG_BIG)
    m = jnp.max(logits, -1, keepdims=True)
    e = jnp.where(mask, jnp.exp(logits - m), 0.0)
    pc = e / jnp.maximum(jnp.sum(e, -1, keepdims=True), 1e-30)
    o = jnp.dot(pc.reshape(p * tq, ncp).astype(BF16), v_ref[...], preferred_element_type=F32)
    gate = jax.nn.sigmoid(gl_ref[:, :, 0:1])
    o_ref[...] = o.reshape(p, tq, NSA_DH) * gate
    imp = jnp.dot(jnp.sum(pc, axis=0), c2s_ref[...], preferred_element_type=F32,
                  precision=lax.Precision.HIGHEST)
    lane = lax.broadcasted_iota(jnp.int32, (tq, LANES), 1)
    cur = (t0 + lax.broadcasted_iota(jnp.int32, (tq, LANES), 0)) // SEL_BLOCK
    forced = (lane == 0) | (lane == cur) | (lane == cur - 1)
    score = jnp.where(forced, FORCE_SCORE, jnp.where(lane <= cur, imp, NEG_BIG))
    lane_f = lane.astype(F32)
    member = jnp.zeros((tq, LANES), F32)
    for _ in range(N_SEL):
        best = jnp.max(score, -1, keepdims=True)
        first = jnp.min(jnp.where(score == best, lane_f, float(LANES)), -1, keepdims=True)
        hit = lane_f == first
        member = jnp.where(hit, 1.0, member)
        score = jnp.where(hit, -jnp.inf, score)
    mem_ref[...] = member.astype(mem_ref.dtype)


def _cmp_branch(q5, kc, vc, bias_c, c2s, gl5, tq):
    bsz, g, p, s, dh = q5.shape
    ncp = kc.shape[2]
    return pl.pallas_call(
        functools.partial(_cmp_kernel, tq=tq),
        grid=(s // tq, g, bsz),
        in_specs=[pl.BlockSpec((None, None, p, tq, dh), lambda i, h, b: (b, h, 0, i, 0)),
                  pl.BlockSpec((None, None, ncp, dh), lambda i, h, b: (b, h, 0, 0)),
                  pl.BlockSpec((None, None, ncp, dh), lambda i, h, b: (b, h, 0, 0)),
                  pl.BlockSpec((p, 1, tq, ncp), lambda i, h, b: (h, i, 0, 0)),
                  pl.BlockSpec((ncp, LANES), lambda i, h, b: (0, 0)),
                  pl.BlockSpec((None, None, p, tq, 3), lambda i, h, b: (b, h, 0, i, 0))],
        out_specs=[pl.BlockSpec((None, None, p, tq, dh), lambda i, h, b: (b, h, 0, i, 0)),
                   pl.BlockSpec((None, None, tq, LANES), lambda i, h, b: (b, h, i, 0))],
        out_shape=[jax.ShapeDtypeStruct((bsz, g, p, s, dh), F32),
                   jax.ShapeDtypeStruct((bsz, g, s, LANES), BF16)],
        compiler_params=_cparams(("parallel", "parallel", "parallel")),
    )(q5, kc, vc, bias_c, c2s, gl5)


def _win_kernel(q_ref, kp_ref, kc_ref, vp_ref, vc_ref, bp_ref, bc_ref, gl_ref, acc_ref, o_ref, *, tq, tk):
    p = NSA_HPG
    i = pl.program_id(1)
    t0 = i * tq
    s_cur = (t0 // tk) * tk
    q = q_ref[...].reshape(p * tq, NSA_DH)
    tpos = t0 + lax.broadcasted_iota(jnp.int32, (tq, tk), 0)
    col = lax.broadcasted_iota(jnp.int32, (tq, tk), 1)

    def tile(k_ref, b_ref, s0):
        kpos = s0 + col
        dist = tpos - kpos
        mask = ((dist >= 0) & (dist < WINDOW) & (kpos >= 0))[None]
        lg = lax.dot_general(q, k_ref[...], (((1,), (1,)), ((), ())), preferred_element_type=F32)
        return jnp.where(mask, lg.reshape(p, tq, tk) + b_ref[:, 0], NEG_BIG), mask

    l_prev, m_prev = tile(kp_ref, bp_ref, s_cur - tk)
    l_cur, m_cur = tile(kc_ref, bc_ref, s_cur)
    m = jnp.maximum(jnp.max(l_prev, -1, keepdims=True), jnp.max(l_cur, -1, keepdims=True))
    e_prev = jnp.where(m_prev, jnp.exp(l_prev - m), 0.0)
    e_cur = jnp.where(m_cur, jnp.exp(l_cur - m), 0.0)
    den = jnp.maximum(jnp.sum(e_prev, -1, keepdims=True) + jnp.sum(e_cur, -1, keepdims=True), 1e-30)
    o = (jnp.dot(e_prev.reshape(p * tq, tk).astype(BF16), vp_ref[...], preferred_element_type=F32)
         + jnp.dot(e_cur.reshape(p * tq, tk).astype(BF16), vc_ref[...], preferred_element_type=F32))
    gate = jax.nn.sigmoid(gl_ref[:, :, 2:3])
    o_ref[...] = acc_ref[...] + o.reshape(p, tq, NSA_DH) / den * gate


def _win_branch(q5, kw, vw, bias_t, gl5, acc, tq, tk):
    bsz, g, p, s, dh = q5.shape
    r = tk // tq
    cur = lambda i, h, b: (b, h, i // r, 0)
    prev = lambda i, h, b: (b, h, jnp.maximum(i // r - 1, 0), 0)
    qmap = lambda i, h, b: (b, h, 0, i, 0)
    return pl.pallas_call(
        functools.partial(_win_kernel, tq=tq, tk=tk),
        grid=(g, s // tq, bsz),
        in_specs=[pl.BlockSpec((None, None, p, tq, dh), lambda h, i, b: qmap(i, h, b)),
                  pl.BlockSpec((None, None, tk, dh), lambda h, i, b: prev(i, h, b)),
                  pl.BlockSpec((None, None, tk, dh), lambda h, i, b: cur(i, h, b)),
                  pl.BlockSpec((None, None, tk, dh), lambda h, i, b: prev(i, h, b)),
                  pl.BlockSpec((None, None, tk, dh), lambda h, i, b: cur(i, h, b)),
                  pl.BlockSpec((p, 1, tq, tk), lambda h, i, b: (h, i % r + r, 0, 0)),
                  pl.BlockSpec((p, 1, tq, tk), lambda h, i, b: (h, i % r, 0, 0)),
                  pl.BlockSpec((None, None, p, tq, 3), lambda h, i, b: qmap(i, h, b)),
                  pl.BlockSpec((None, None, p, tq, dh), lambda h, i, b: qmap(i, h, b))],
        out_specs=pl.BlockSpec((None, None, p, tq, dh), lambda h, i, b: qmap(i, h, b)),
        out_shape=jax.ShapeDtypeStruct((bsz, g, p, s, dh), F32),
        input_output_aliases={8: 0},
        compiler_params=_cparams(("parallel", "parallel", "parallel")),
    )(q5, kw, kw, vw, vw, bias_t, bias_t, gl5, acc)


def _sel_kernel(it_ref, jt_ref, q_ref, k_ref, v_ref, mem_ref, bias_ref, gl_ref, acc_ref, o_ref,
                m_sc, l_sc, a_sc, *, tq, tk, bsz):
    p = NSA_HPG
    step = pl.program_id(1)
    i = it_ref[step]
    j = jt_ref[step]
    t0 = i * tq
    s0 = j * tk

    @pl.when(j == 0)
    def _():
        m_sc[...] = jnp.full_like(m_sc, NEG_BIG)
        l_sc[...] = jnp.zeros_like(l_sc)
        a_sc[...] = jnp.zeros_like(a_sc)

    causal = (t0 + lax.broadcasted_iota(jnp.int32, (tq, tk), 0)
              >= s0 + lax.broadcasted_iota(jnp.int32, (tq, tk), 1))
    expand = (lax.broadcasted_iota(jnp.int32, (LANES, tk), 0)
              == (s0 + lax.broadcasted_iota(jnp.int32, (LANES, tk), 1)) // SEL_BLOCK).astype(BF16)
    bias = bias_ref[:, 0]

    def per_batch(b, carry):
        q = q_ref[b].reshape(p * tq, NSA_DH)
        picked = jnp.dot(mem_ref[b], expand, preferred_element_type=F32) > 0.5
        mask = (picked & causal)[None]
        lg = lax.dot_general(q, k_ref[b], (((1,), (1,)), ((), ())), preferred_element_type=F32)
        lg = jnp.where(mask, lg.reshape(p, tq, tk) + bias, NEG_BIG).reshape(p * tq, tk)
        m_old = m_sc[b]
        m_new = jnp.maximum(m_old, jnp.max(lg, -1, keepdims=True))
        alpha = jnp.exp(m_old - m_new)
        e = jnp.exp(lg - m_new)
        l_sc[b] = alpha * l_sc[b] + jnp.sum(e, -1, keepdims=True)
        a_sc[b] = alpha * a_sc[b] + jnp.dot(e.astype(BF16), v_ref[b], preferred_element_type=F32)
        m_sc[b] = m_new
        return carry

    lax.fori_loop(0, bsz, per_batch, 0)

    @pl.when(s0 + tk > t0)
    def _():
        gate = jax.nn.sigmoid(gl_ref[:, :, :, 1:2])
        o = (a_sc[...] / l_sc[...]).reshape(bsz, p, tq, NSA_DH)
        o_ref[...] = acc_ref[...] + o * gate


def _sel_branch(q5, ks, vs, member, bias_t, gl5, acc, tq, tk):
    bsz, g, p, s, dh = q5.shape
    r = tk // tq
    pairs = [(i, j) for i in range(s // tq) for j in range(i // r + 1)]
    it = jnp.asarray([ij[0] for ij in pairs], jnp.int32)
    jt = jnp.asarray([ij[1] for ij in pairs], jnp.int32)
    nd = bias_t.shape[1] - 1
    qmap = lambda h, t, it, jt: (0, h, 0, it[t], 0)
    kmap = lambda h, t, it, jt: (0, h, jt[t], 0)
    grid_spec = pltpu.PrefetchScalarGridSpec(
        num_scalar_prefetch=2,
        grid=(g, len(pairs)),
        in_specs=[pl.BlockSpec((bsz, None, p, tq, dh), qmap),
                  pl.BlockSpec((bsz, None, tk, dh), kmap),
                  pl.BlockSpec((bsz, None, tk, dh), kmap),
                  pl.BlockSpec((bsz, None, tq, LANES), lambda h, t, it, jt: (0, h, it[t], 0)),
                  pl.BlockSpec((p, 1, tq, tk),
                               lambda h, t, it, jt: (h, jnp.minimum(it[t] - r * jt[t], nd), 0, 0)),
                  pl.BlockSpec((bsz, None, p, tq, 3), qmap),
                  pl.BlockSpec((bsz, None, p, tq, dh), qmap)],
        out_specs=pl.BlockSpec((bsz, None, p, tq, dh), qmap),
        scratch_shapes=[pltpu.VMEM((bsz, p * tq, 1), F32),
                        pltpu.VMEM((bsz, p * tq, 1), F32),
                        pltpu.VMEM((bsz, p * tq, dh), F32)],
    )
    return pl.pallas_call(
        functools.partial(_sel_kernel, tq=tq, tk=tk, bsz=bsz),
        grid_spec=grid_spec,
        out_shape=jax.ShapeDtypeStruct((bsz, g, p, s, dh), F32),
        input_output_aliases={8: 0},
        compiler_params=_cparams(("parallel", "arbitrary")),
    )(it, jt, q5, ks, vs, member, bias_t, gl5, acc)


def _merge_kernel(x_ref, oa_ref, ob_ref, ga_ref, gb_ref, wa_ref, wb_ref, wo_ref, g_ref, b_ref, o_ref):
    ya = jnp.dot(oa_ref[...].astype(BF16), wa_ref[...], preferred_element_type=F32)
    yb = jnp.dot(ob_ref[...].astype(BF16), wb_ref[...], preferred_element_type=F32)
    y = jax.nn.sigmoid(ga_ref[...]) * ya + jax.nn.sigmoid(gb_ref[...]) * yb
    mix = jnp.dot(y.astype(BF16), wo_ref[...], preferred_element_type=F32)
    o_ref[...] = _layer_norm(DN_ALPHA * x_ref[...] + mix, g_ref[...], b_ref[...])


def _merge(x, o_a, o_b, proj, wa, wb, wo, g, b, tm):
    t, d = x.shape
    nga = COL_GA // d
    row = lambda i: (i, 0)
    const = lambda i: (0, 0)
    return pl.pallas_call(
        _merge_kernel,
        grid=(t // tm,),
        in_specs=[pl.BlockSpec((tm, d), row),
                  pl.BlockSpec((tm, HG_WIDTH), row),
                  pl.BlockSpec((tm, NSA_WIDTH), row),
                  pl.BlockSpec((tm, d), lambda i: (i, nga)),
                  pl.BlockSpec((tm, d), lambda i: (i, nga + 1)),
                  pl.BlockSpec((HG_WIDTH, d), const),
                  pl.BlockSpec((NSA_WIDTH, d), const),
                  pl.BlockSpec((d, d), const),
                  pl.BlockSpec((1, d), const),
                  pl.BlockSpec((1, d), const)],
        out_specs=pl.BlockSpec((tm, d), row),
        out_shape=jax.ShapeDtypeStruct((t, d), F32),
        compiler_params=_cparams(("parallel",)),
    )(x, o_a, o_b, proj, proj, wa, wb, wo, g, b)


def _swiglu_step(xb, wg_ref, wu_ref, wd_ref):
    hg = jnp.dot(xb, wg_ref[...], preferred_element_type=F32)
    hu = jnp.dot(xb, wu_ref[...], preferred_element_type=F32)
    h = (hg * jax.nn.sigmoid(hg)) * hu
    return jnp.dot(h.astype(BF16), wd_ref[...], preferred_element_type=F32)


def _ffn_kernel(x_ref, wg_ref, wu_ref, wd_ref, g_ref, b_ref, o_ref, xb_ref, acc_ref):
    j = pl.program_id(1)

    @pl.when(j == 0)
    def _():
        xb_ref[...] = x_ref[...].astype(BF16)
        acc_ref[...] = jnp.zeros_like(acc_ref)

    acc_ref[...] += _swiglu_step(xb_ref[...], wg_ref, wu_ref, wd_ref)

    @pl.when(j == pl.num_programs(1) - 1)
    def _():
        o_ref[...] = _layer_norm(DN_ALPHA * x_ref[...] + acc_ref[...], g_ref[...], b_ref[...])


def _ffn(x, wg, wu, wd, g, b, tm, tf):
    t, d = x.shape
    f = wg.shape[1]
    return pl.pallas_call(
        _ffn_kernel,
        grid=(t // tm, f // tf),
        in_specs=[pl.BlockSpec((tm, d), lambda i, j: (i, 0)),
                  pl.BlockSpec((d, tf), lambda i, j: (0, j)),
                  pl.BlockSpec((d, tf), lambda i, j: (0, j)),
                  pl.BlockSpec((tf, d), lambda i, j: (j, 0)),
                  pl.BlockSpec((1, d), lambda i, j: (0, 0)),
                  pl.BlockSpec((1, d), lambda i, j: (0, 0))],
        out_specs=pl.BlockSpec((tm, d), lambda i, j: (i, 0)),
        out_shape=jax.ShapeDtypeStruct((t, d), F32),
        scratch_shapes=[pltpu.VMEM((tm, d), BF16), pltpu.VMEM((tm, d), F32)],
        compiler_params=_cparams(("parallel", "arbitrary")),
    )(x, wg, wu, wd, g, b)


def _router_kernel(x_ref, w_ref, o_ref):
    logits = jnp.dot(x_ref[...], w_ref[...], preferred_element_type=F32, precision=lax.Precision.HIGHEST)
    lane = lax.broadcasted_iota(jnp.int32, logits.shape, 1).astype(F32)
    logits = jnp.where(lane < N_EXPERTS, logits, -jnp.inf)
    v1 = jnp.max(logits, -1, keepdims=True)
    e1 = jnp.min(jnp.where(logits == v1, lane, float(LANES)), -1, keepdims=True)
    rest = jnp.where(lane == e1, -jnp.inf, logits)
    v2 = jnp.max(rest, -1, keepdims=True)
    e2 = jnp.min(jnp.where(rest == v2, lane, float(LANES)), -1, keepdims=True)
    x2 = jnp.exp(v2 - v1)
    den = 1.0 + x2
    o_ref[...] = jnp.where(lane == 0, e1, jnp.where(lane == 1, e2, jnp.where(
        lane == 2, 1.0 / den, jnp.where(lane == 3, x2 / den, 0.0))))


def _router(x, w, tm):
    t, d = x.shape
    return pl.pallas_call(
        _router_kernel,
        grid=(t // tm,),
        in_specs=[pl.BlockSpec((tm, d), lambda i: (i, 0)), pl.BlockSpec((d, LANES), lambda i: (0, 0))],
        out_specs=pl.BlockSpec((tm, LANES), lambda i: (i, 0)),
        out_shape=jax.ShapeDtypeStruct((t, LANES), F32),
        compiler_params=_cparams(("parallel",)),
    )(x, w)


def _expert_kernel(be_ref, x_ref, wg_ref, wu_ref, wd_ref, o_ref, acc_ref):
    j = pl.program_id(1)

    @pl.when(j == 0)
    def _():
        acc_ref[...] = jnp.zeros_like(acc_ref)

    acc_ref[...] += _swiglu_step(x_ref[...], wg_ref, wu_ref, wd_ref)

    @pl.when(j == pl.num_programs(1) - 1)
    def _():
        o_ref[...] = acc_ref[...]


def _experts(blk_e, xs, wg, wu, wd, tf):
    rows, d = xs.shape
    f = wg.shape[2]
    tm = MOE_ROW_BLOCK
    grid_spec = pltpu.PrefetchScalarGridSpec(
        num_scalar_prefetch=1,
        grid=(rows // tm, f // tf),
        in_specs=[pl.BlockSpec((tm, d), lambda i, j, be: (i, 0)),
                  pl.BlockSpec((None, d, tf), lambda i, j, be: (be[i], 0, j)),
                  pl.BlockSpec((None, d, tf), lambda i, j, be: (be[i], 0, j)),
                  pl.BlockSpec((None, tf, d), lambda i, j, be: (be[i], j, 0))],
        out_specs=pl.BlockSpec((tm, d), lambda i, j, be: (i, 0)),
        scratch_shapes=[pltpu.VMEM((tm, d), F32)],
    )
    return pl.pallas_call(
        _expert_kernel,
        grid_spec=grid_spec,
        out_shape=jax.ShapeDtypeStruct((rows, d), F32),
        compiler_params=_cparams(("parallel", "arbitrary")),
    )(blk_e, xs, wg, wu, wd)


def _combine_kernel(x_ref, y1_ref, y2_ref, gt_ref, g_ref, b_ref, o_ref):
    f = y1_ref[...] * gt_ref[:, 2:3] + y2_ref[...] * gt_ref[:, 3:4]
    o_ref[...] = _layer_norm(DN_ALPHA * x_ref[...] + f, g_ref[...], b_ref[...])


def _combine(x, y1, y2, route, g, b, tm):
    t, d = x.shape
    row = lambda i: (i, 0)
    const = lambda i: (0, 0)
    return pl.pallas_call(
        _combine_kernel,
        grid=(t // tm,),
        in_specs=[pl.BlockSpec((tm, d), row), pl.BlockSpec((tm, d), row), pl.BlockSpec((tm, d), row),
                  pl.BlockSpec((tm, LANES), row), pl.BlockSpec((1, d), const), pl.BlockSpec((1, d), const)],
        out_specs=pl.BlockSpec((tm, d), row),
        out_shape=jax.ShapeDtypeStruct((t, d), F32),
        compiler_params=_cparams(("parallel",)),
    )(x, y1, y2, route, g, b)


def _moe(x, w_router, wg, wu, wd, g, b):
    t, d = x.shape
    tk_ = t * TOP_K
    route = _router(x, jnp.pad(w_router, ((0, 0), (0, LANES - N_EXPERTS))), 512)
    flat_e = route[:, :TOP_K].astype(jnp.int32).reshape(-1)
    onehot = (flat_e[:, None] == jnp.arange(N_EXPERTS)[None, :]).astype(jnp.int32)
    csum = jnp.cumsum(onehot, axis=0)
    counts = csum[-1]
    rank = jnp.sum(onehot * csum, axis=1) - 1
    padded = (counts + MOE_ROW_BLOCK - 1) // MOE_ROW_BLOCK * MOE_ROW_BLOCK
    pend = jnp.cumsum(padded)
    dest = (pend - padded)[flat_e] + rank
    n_blocks = -(-(tk_ + N_EXPERTS * (MOE_ROW_BLOCK - 1)) // MOE_ROW_BLOCK)
    n_rows = n_blocks * MOE_ROW_BLOCK
    row_tok = jnp.zeros((n_rows,), jnp.int32).at[dest].set(jnp.arange(tk_, dtype=jnp.int32) // TOP_K)
    blk_e = jnp.minimum(jnp.searchsorted(pend, jnp.arange(n_blocks) * MOE_ROW_BLOCK, side='right'),
                        N_EXPERTS - 1).astype(jnp.int32)
    xs = x.astype(BF16)[row_tok]
    ys = _experts(blk_e, xs, wg, wu, wd, 512)
    yk = ys[dest].reshape(t, TOP_K, d)
    return _combine(x, yk[:, 0], yk[:, 1], route, g, b, 512)


def _pack_w_in(w_in):
    offs = np.concatenate([[0], np.cumsum(IN_SIZES)])
    seg = [w_in[:, offs[j]:offs[j + 1]] for j in range(len(IN_SIZES))]
    seg[4] = seg[4] * (NSA_DH ** -0.5)
    seg[11] = jnp.pad(seg[11], ((0, 0), (0, LANES - 3 * NSA_HEADS)))
    seg = seg[12:14] + seg[0:12]
    packed = jnp.concatenate(seg + [jnp.zeros((w_in.shape[0], PROJ_WP - PROJ_W), w_in.dtype)], axis=1)
    return packed.astype(BF16)


def _cmp_to_sel(n_cmp_pad, n_cmp):
    cs = np.arange(n_cmp_pad)[:, None] * CMP_STRIDE
    ss = np.arange(LANES)[None, :] * SEL_BLOCK
    overlap = np.clip(np.minimum(cs + CMP_LEN, ss + SEL_BLOCK) - np.maximum(cs, ss), 0, None) / CMP_LEN
    overlap[n_cmp:] = 0.0
    return jnp.asarray(overlap, F32)


def _token_mixer(x, w_in_p, lb, hg_norm_w, cmp_pos, cmp_w1, cmp_w2, bias_c, bias_t, wa, wb, wo, ln_g, ln_b):
    bsz, s, d = x.shape
    g, p, dh = NSA_GROUPS, NSA_HPG, NSA_DH
    xf = x.reshape(bsz * s, d)
    proj = _project(xf, w_in_p, 1024, PROJ_TN).reshape(bsz, s, PROJ_WP)
    o_a = _hgrn(proj, lb, hg_norm_w, 512)

    def heads(c0, width):
        return proj[:, :, c0:c0 + width]

    q5 = heads(COL_NQ, NSA_WIDTH).astype(BF16).reshape(bsz, s, g, p, dh).transpose(0, 2, 3, 1, 4)
    kv = heads(COL_KV, 6 * KV_WIDTH).reshape(bsz, s, 6, g, dh)
    n16 = s // CMP_STRIDE
    a = kv[:, :, 0:2].reshape(bsz, n16, CMP_STRIDE, 2, g, dh).transpose(3, 0, 4, 1, 2, 5)
    a = a.reshape(2, bsz, g, n16, CMP_STRIDE * dh)
    kvc = _compress(a, cmp_pos.reshape(2, 1, CMP_LEN * dh), cmp_w1.astype(BF16), cmp_w2.astype(BF16))
    kvh = kv[:, :, 2:6].astype(BF16).transpose(2, 0, 3, 1, 4)
    gl5 = heads(COL_NG, 3 * NSA_HEADS).reshape(bsz, s, g, p, 3).transpose(0, 2, 3, 1, 4)
    n_cmp = (s - CMP_LEN) // CMP_STRIDE + 1
    acc, member = _cmp_branch(q5, kvc[0], kvc[1], bias_c, _cmp_to_sel(n16, n_cmp), gl5, ATT_TQ)
    acc = _win_branch(q5, kvh[2], kvh[3], bias_t, gl5, acc, ATT_TQ, ATT_TK)
    o_b = _sel_branch(q5, kvh[0], kvh[1], member, bias_t, gl5, acc, ATT_TQ, ATT_TK)
    o_b = o_b.transpose(0, 3, 1, 2, 4).reshape(bsz * s, NSA_WIDTH)
    return _merge(xf, o_a.reshape(bsz * s, HG_WIDTH), o_b, proj.reshape(bsz * s, PROJ_WP),
                  wa, wb, wo, ln_g, ln_b, 512)


def kernel(x, w_in, hg_lb_logits, hg_norm_w, cmp_pos, cmp_w1, cmp_w2, rel_bias, w_branch_a, w_branch_b,
           w_out, ln1_g, ln1_b, ln2_g, ln2_b, ffn_w_gate, ffn_w_up, ffn_w_down, moe_router, moe_w_gate,
           moe_w_up, moe_w_down):
    bsz, s, d = x.shape
    depth = w_in.shape[0]
    p_lb = jax.nn.softmax(hg_lb_logits.astype(F32), axis=0)
    lbs = jnp.cumsum(p_lb, axis=0) - p_lb[0]
    n16 = s // CMP_STRIDE
    bias_c = _bias_tiles(rel_bias, s // ATT_TQ, ATT_TQ, n16,
                         lambda t, r, c: t * ATT_TQ + r - (c * CMP_STRIDE + CMP_LEN - 1))
    bias_t = _bias_tiles(rel_bias, BIAS_ND + 1, ATT_TQ, ATT_TK, lambda t, r, c: t * ATT_TQ + r - c)
    f_pad = -(-D_FF // LANES) * LANES - D_FF
    xf = x.reshape(bsz * s, d)
    for l in range(depth):
        xf = _token_mixer(xf.reshape(bsz, s, d), _pack_w_in(w_in[l]), lbs[l][None], hg_norm_w[l][None],
                          cmp_pos[l], cmp_w1[l], cmp_w2[l], bias_c, bias_t,
                          w_branch_a[l].astype(BF16), w_branch_b[l].astype(BF16), w_out[l].astype(BF16),
                          ln1_g[l][None], ln1_b[l][None])
        if l % 2 == 0:
            wg = jnp.pad(ffn_w_gate[l // 2], ((0, 0), (0, f_pad))).astype(BF16)
            wu = jnp.pad(ffn_w_up[l // 2], ((0, 0), (0, f_pad))).astype(BF16)
            wd = jnp.pad(ffn_w_down[l // 2], ((0, f_pad), (0, 0))).astype(BF16)
            xf = _ffn(xf, wg, wu, wd, ln2_g[l][None], ln2_b[l][None], 512, wg.shape[1] // 2)
        else:
            xf = _moe(xf, moe_router[l // 2], moe_w_gate[l // 2].astype(BF16), moe_w_up[l // 2].astype(BF16),
                      moe_w_down[l // 2].astype(BF16), ln2_g[l][None], ln2_b[l][None])
    return xf.reshape(bsz, s, d)
```

```python
import functools
import math

import jax
import jax.numpy as jnp
import numpy as np
from jax import lax
from jax.experimental import pallas as pl
from jax.experimental.pallas import tpu as pltpu

F32 = jnp.float32
BF16 = jnp.bfloat16

D_MODEL = 1024
DEPTH = 2
HG_HEADS = 4
HG_DK = 128
HG_WIDTH = HG_HEADS * HG_DK
HG_CHUNK = 64
HG_SUB = 16
NSA_HEADS = 8
NSA_GROUPS = 2
NSA_HPG = NSA_HEADS // NSA_GROUPS
NSA_DH = 64
NSA_WIDTH = NSA_HEADS * NSA_DH
KV_WIDTH = NSA_GROUPS * NSA_DH
CMP_LEN = 32
CMP_STRIDE = 16
CMP_HIDDEN = 2 * NSA_DH
SEL_BLOCK = 64
N_SEL = 16
WINDOW = 512
FORCE_SCORE = 1e9
NEG_BIG = -1e30
REL_BUCKETS = 32
REL_MAX_DIST = 2048
D_FF = 2752
N_EXPERTS = 8
TOP_K = 2
D_FF_EXPERT = 3584
MOE_ROW_BLOCK = 512
DN_ALPHA = (2 * DEPTH) ** 0.25
LN_EPS = 1e-5
IN_SIZES = (HG_WIDTH, HG_WIDTH, HG_WIDTH, HG_WIDTH, NSA_WIDTH,
            KV_WIDTH, KV_WIDTH, KV_WIDTH, KV_WIDTH, KV_WIDTH, KV_WIDTH,
            3 * NSA_HEADS, D_MODEL, D_MODEL)

LANES = 128
LOG2E = 1.0 / math.log(2.0)
COL_GA = 0
COL_HG = 2 * D_MODEL
COL_NQ = COL_HG + 4 * HG_WIDTH
COL_KV = COL_NQ + NSA_WIDTH
COL_NG = COL_KV + 6 * KV_WIDTH
PROJ_W = COL_NG + LANES
PROJ_TN = 512
PROJ_WP = -(-PROJ_W // PROJ_TN) * PROJ_TN

ATT_TQ = 256
ATT_TK = 512
BIAS_ND = -(-(REL_MAX_DIST + ATT_TK) // ATT_TQ)
VMEM_LIMIT = 48 * 1024 * 1024


def _cparams(sem):
    return pltpu.CompilerParams(dimension_semantics=sem, vmem_limit_bytes=VMEM_LIMIT)


def _proj_kernel(x_ref, w_ref, o_ref, xb_ref):
    @pl.when(pl.program_id(1) == 0)
    def _():
        xb_ref[...] = x_ref[...].astype(BF16)

    o_ref[...] = jnp.dot(xb_ref[...], w_ref[...], preferred_element_type=F32)


def _project(x, w, tm, tn):
    m, k = x.shape
    n = w.shape[1]
    return pl.pallas_call(
        _proj_kernel,
        grid=(m // tm, n // tn),
        in_specs=[pl.BlockSpec((tm, k), lambda i, j: (i, 0)),
                  pl.BlockSpec((k, tn), lambda i, j: (0, j))],
        out_specs=pl.BlockSpec((tm, tn), lambda i, j: (i, j)),
        out_shape=jax.ShapeDtypeStruct((m, n), F32),
        scratch_shapes=[pltpu.VMEM((tm, k), BF16)],
        compiler_params=_cparams(("parallel", "arbitrary")),
    )(x, w)


def _layer_norm(y, g, b):
    mu = jnp.mean(y, -1, keepdims=True)
    yc = y - mu
    var = jnp.mean(yc * yc, -1, keepdims=True)
    return yc * lax.rsqrt(var + LN_EPS) * g + b


def _hgrn_kernel(q_ref, f_ref, i_ref, g_ref, lb_ref, nw_ref, o_ref, st_ref, *, n_chunks):
    @pl.when(pl.program_id(2) == 0)
    def _():
        st_ref[...] = jnp.zeros_like(st_ref)

    c = HG_CHUNK
    one_minus_lb = 1.0 - lb_ref[...]
    nw = nw_ref[...]
    row = lax.broadcasted_iota(jnp.int32, (c, c), 0)
    col = lax.broadcasted_iota(jnp.int32, (c, c), 1)
    tril = (row >= col).astype(F32)
    sub_pos = lax.broadcasted_iota(jnp.int32, (c, 1), 0) % HG_SUB

    def chunk(ci, carry):
        r0 = pl.multiple_of(ci * c, c)
        q = q_ref[pl.ds(r0, c), :]
        z = f_ref[pl.ds(r0, c), :]
        v = i_ref[pl.ds(r0, c), :]
        g = g_ref[pl.ds(r0, c), :]
        k = one_minus_lb * jax.nn.sigmoid(-z)
        log_f = jnp.log1p(-k)
        b = jnp.dot(tril, log_f, preferred_element_type=F32, precision=lax.Precision.HIGHEST)
        b_last = b[c - 1:c, :]
        st = st_ref[...]
        o = lax.dot_general(q * jnp.exp(b), st, (((1,), (1,)), ((), ())), preferred_element_type=F32)
        parts = [jnp.zeros((HG_SUB, HG_DK), F32)]
        for sb in range(1, c // HG_SUB):
            lo = sb * HG_SUB
            ref_b = b[lo - 1:lo, :]
            qt = q[lo:lo + HG_SUB, :] * jnp.exp(b[lo:lo + HG_SUB, :] - ref_b)
            kt = k[:lo, :] * jnp.exp(ref_b - b[:lo, :])
            att = lax.dot_general(qt, kt, (((1,), (1,)), ((), ())), preferred_element_type=F32)
            parts.append(jnp.dot(att, v[:lo, :], preferred_element_type=F32))
        o = o + jnp.concatenate(parts, axis=0)
        for lag in range(HG_SUB):
            if lag == 0:
                ks, bs, vs = k, b, v
            else:
                ks = pltpu.roll(k, lag, 0)
                bs = pltpu.roll(b, lag, 0)
                vs = pltpu.roll(v, lag, 0)
            valid = sub_pos >= lag
            e = jnp.exp(jnp.where(valid, b - bs, 0.0))
            a = jnp.sum(q * ks * e, axis=1, keepdims=True)
            o = o + jnp.where(valid, a, 0.0) * vs
        khat = k * jnp.exp(b_last - b)
        st_ref[...] = st * jnp.exp(b_last) + lax.dot_general(
            v, khat, (((0,), (0,)), ((), ())), preferred_element_type=F32)
        o = o * lax.rsqrt(jnp.mean(o * o, -1, keepdims=True) + 1e-6)
        o_ref[pl.ds(r0, c), :] = o * nw * (g * jax.nn.sigmoid(g))
        return carry

    lax.fori_loop(0, n_chunks, chunk, 0)


def _hgrn(proj, lb, norm_w, ts):
    bsz, s, _ = proj.shape
    nblk = HG_WIDTH // LANES

    def col_spec(off):
        return pl.BlockSpec((None, ts, LANES), lambda b, h, t: (b, t, off + h))

    head_spec = pl.BlockSpec((1, LANES), lambda b, h, t: (0, h))
    return pl.pallas_call(
        functools.partial(_hgrn_kernel, n_chunks=ts // HG_CHUNK),
        grid=(bsz, HG_HEADS, s // ts),
        in_specs=[col_spec(COL_HG // LANES), col_spec(COL_HG // LANES + nblk),
                  col_spec(COL_HG // LANES + 2 * nblk), col_spec(COL_HG // LANES + 3 * nblk),
                  head_spec, head_spec],
        out_specs=pl.BlockSpec((None, ts, LANES), lambda b, h, t: (b, t, h)),
        out_shape=jax.ShapeDtypeStruct((bsz, s, HG_WIDTH), F32),
        scratch_shapes=[pltpu.VMEM((HG_DK, HG_DK), F32)],
        compiler_params=_cparams(("parallel", "parallel", "arbitrary")),
    )(proj, proj, proj, proj, lb, norm_w)


def _compress_kernel(a_ref, pos_ref, w1_ref, w2_ref, o_ref):
    half = CMP_STRIDE * NSA_DH
    a = a_ref[...]
    n = a.shape[0]
    a1 = (a + pos_ref[:, :half]).astype(BF16)
    a2 = (a + pos_ref[:, half:]).astype(BF16)
    y1 = jnp.dot(a1, w1_ref[:half, :], preferred_element_type=F32)
    y2 = jnp.dot(a2, w1_ref[half:, :], preferred_element_type=F32)
    hid = jax.nn.gelu(y1 + pltpu.roll(y2, n - 1, 0))
    o_ref[...] = jnp.dot(hid.astype(BF16), w2_ref[...], preferred_element_type=F32).astype(o_ref.dtype)


def _compress(a, pos, w1, w2):
    _, bsz, g, n, width = a.shape
    return pl.pallas_call(
        _compress_kernel,
        grid=(2, bsz, g),
        in_specs=[pl.BlockSpec((None, None, None, n, width), lambda c, b, h: (c, b, h, 0, 0)),
                  pl.BlockSpec((None, 1, 2 * width), lambda c, b, h: (c, 0, 0)),
                  pl.BlockSpec((None, 2 * width, CMP_HIDDEN), lambda c, b, h: (c, 0, 0)),
                  pl.BlockSpec((None, CMP_HIDDEN, NSA_DH), lambda c, b, h: (c, 0, 0))],
        out_specs=pl.BlockSpec((None, None, None, n, NSA_DH), lambda c, b, h: (c, b, h, 0, 0)),
        out_shape=jax.ShapeDtypeStruct((2, bsz, g, n, NSA_DH), BF16),
        compiler_params=_cparams(("parallel", "parallel", "parallel")),
    )(a, pos, w1, w2)


def _bucket_of(dist):
    n = jnp.maximum(dist, 0)
    exact = REL_BUCKETS // 2
    large = exact + (jnp.log(jnp.maximum(n, exact).astype(F32) / exact)
                     / math.log(REL_MAX_DIST / exact) * (REL_BUCKETS - exact)).astype(jnp.int32)
    return jnp.where(n < exact, n, jnp.minimum(large, REL_BUCKETS - 1))


def _bias_kernel(rb_ref, o_ref, *, dist_fn):
    shape = o_ref.shape
    row = lax.broadcasted_iota(jnp.int32, shape, 0)
    col = lax.broadcasted_iota(jnp.int32, shape, 1)
    bucket = _bucket_of(dist_fn(pl.program_id(1), row, col))
    h = pl.program_id(0)

    def pick(kb, acc):
        return jnp.where(bucket == kb, rb_ref[kb, h] * LOG2E, acc)

    o_ref[...] = lax.fori_loop(0, REL_BUCKETS, pick, jnp.zeros(shape, F32))


def _bias_tiles(rel_bias, n_tiles, rows, cols, dist_fn):
    return pl.pallas_call(
        functools.partial(_bias_kernel, dist_fn=dist_fn),
        grid=(NSA_HEADS, n_tiles),
        in_specs=[pl.BlockSpec(memory_space=pltpu.SMEM)],
        out_specs=pl.BlockSpec((None, None, rows, cols), lambda h, t: (h, t, 0, 0)),
        out_shape=jax.ShapeDtypeStruct((NSA_HEADS, n_tiles, rows, cols), F32),
        compiler_params=_cparams(("parallel", "parallel")),
    )(rel_bias)


def _cmp_kernel(q_ref, k_ref, v_ref, bias_ref, c2s_ref, gl_ref, o_ref, pen_ref, *, tq):
    p = NSA_HPG
    t0 = pl.program_id(0) * tq
    ncp = k_ref.shape[0]
    q = q_ref[...].reshape(p * tq, NSA_DH)
    logits = lax.dot_general(q, k_ref[...], (((1,), (1,)), ((), ())), preferred_element_type=F32)
    logits = logits.reshape(p, tq, ncp) + bias_ref[:, 0]
    tpos = t0 + lax.broadcasted_iota(jnp.int32, (tq, ncp), 0)
    cend = lax.broadcasted_iota(jnp.int32, (tq, ncp), 1) * CMP_STRIDE + (CMP_LEN - 1)
    mask = (tpos >= cend)[None]
    logits = jnp.where(mask, logits, NEG_BIG)
    m = jnp.max(logits, -1, keepdims=True)
    e = jnp.where(mask, jnp.exp2(logits - m), 0.0)
    pc = e / jnp.maximum(jnp.sum(e, -1, keepdims=True), 1e-30)
    o = jnp.dot(pc.reshape(p * tq, ncp).astype(BF16), v_ref[...], preferred_element_type=F32)
    gate = jax.nn.sigmoid(gl_ref[:, :, 0:1])
    o_ref[...] = o.reshape(p, tq, NSA_DH) * gate
    imp = jnp.dot(jnp.sum(pc, axis=0), c2s_ref[...], preferred_element_type=F32,
                  precision=lax.Precision.HIGHEST)
    lane = lax.broadcasted_iota(jnp.int32, (tq, LANES), 1)
    cur = (t0 + lax.broadcasted_iota(jnp.int32, (tq, LANES), 0)) // SEL_BLOCK
    forced = (lane == 0) | (lane == cur) | (lane == cur - 1)
    score = jnp.where(forced, FORCE_SCORE, jnp.where(lane <= cur, imp, NEG_BIG))
    lane_f = lane.astype(F32)
    member = jnp.zeros((tq, LANES), F32)
    for _ in range(N_SEL):
        best = jnp.max(score, -1, keepdims=True)
        first = jnp.min(jnp.where(score == best, lane_f, float(LANES)), -1, keepdims=True)
        hit = lane_f == first
        member = jnp.where(hit, 1.0, member)
        score = jnp.where(hit, -jnp.inf, score)
    pen_ref[...] = jnp.where((member > 0.5) & (lane <= cur), 0.0, NEG_BIG).astype(pen_ref.dtype)


def _cmp_branch(q5, kc, vc, bias_c, c2s, gl5, tq):
    bsz, g, p, s, dh = q5.shape
    ncp = kc.shape[2]
    return pl.pallas_call(
        functools.partial(_cmp_kernel, tq=tq),
        grid=(s // tq, g, bsz),
        in_specs=[pl.BlockSpec((None, None, p, tq, dh), lambda i, h, b: (b, h, 0, i, 0)),
                  pl.BlockSpec((None, None, ncp, dh), lambda i, h, b: (b, h, 0, 0)),
                  pl.BlockSpec((None, None, ncp, dh), lambda i, h, b: (b, h, 0, 0)),
                  pl.BlockSpec((p, 1, tq, ncp), lambda i, h, b: (h, i, 0, 0)),
                  pl.BlockSpec((ncp, LANES), lambda i, h, b: (0, 0)),
                  pl.BlockSpec((None, None, p, tq, 3), lambda i, h, b: (b, h, 0, i, 0))],
        out_specs=[pl.BlockSpec((None, None, p, tq, dh), lambda i, h, b: (b, h, 0, i, 0)),
                   pl.BlockSpec((None, None, tq, LANES), lambda i, h, b: (b, h, i, 0))],
        out_shape=[jax.ShapeDtypeStruct((bsz, g, p, s, dh), F32),
                   jax.ShapeDtypeStruct((bsz, g, s, LANES), BF16)],
        compiler_params=_cparams(("parallel", "parallel", "parallel")),
    )(q5, kc, vc, bias_c, c2s, gl5)


def _win_kernel(q_ref, kp_ref, kc_ref, vp_ref, vc_ref, bp_ref, bc_ref, gl_ref, acc_ref, o_ref, *, tq, tk):
    p = NSA_HPG
    i = pl.program_id(1)
    t0 = i * tq
    s_cur = (t0 // tk) * tk
    q = q_ref[...].reshape(p * tq, NSA_DH)
    tpos = t0 + lax.broadcasted_iota(jnp.int32, (tq, tk), 0)
    col = lax.broadcasted_iota(jnp.int32, (tq, tk), 1)

    def tile(k_ref, b_ref, s0):
        kpos = s0 + col
        dist = tpos - kpos
        mask = ((dist >= 0) & (dist < WINDOW) & (kpos >= 0))[None]
        lg = lax.dot_general(q, k_ref[...], (((1,), (1,)), ((), ())), preferred_element_type=F32)
        return jnp.where(mask, lg.reshape(p, tq, tk) + b_ref[:, 0], NEG_BIG), mask

    l_prev, m_prev = tile(kp_ref, bp_ref, s_cur - tk)
    l_cur, m_cur = tile(kc_ref, bc_ref, s_cur)
    m = jnp.maximum(jnp.max(l_prev, -1, keepdims=True), jnp.max(l_cur, -1, keepdims=True))
    e_prev = jnp.where(m_prev, jnp.exp2(l_prev - m), 0.0)
    e_cur = jnp.where(m_cur, jnp.exp2(l_cur - m), 0.0)
    den = jnp.maximum(jnp.sum(e_prev, -1, keepdims=True) + jnp.sum(e_cur, -1, keepdims=True), 1e-30)
    o = (jnp.dot(e_prev.reshape(p * tq, tk).astype(BF16), vp_ref[...], preferred_element_type=F32)
         + jnp.dot(e_cur.reshape(p * tq, tk).astype(BF16), vc_ref[...], preferred_element_type=F32))
    gate = jax.nn.sigmoid(gl_ref[:, :, 2:3])
    o_ref[...] = acc_ref[...] + o.reshape(p, tq, NSA_DH) / den * gate


def _win_branch(q5, kw, vw, bias_t, gl5, acc, tq, tk):
    bsz, g, p, s, dh = q5.shape
    r = tk // tq
    cur = lambda i, h, b: (b, h, i // r, 0)
    prev = lambda i, h, b: (b, h, jnp.maximum(i // r - 1, 0), 0)
    qmap = lambda i, h, b: (b, h, 0, i, 0)
    return pl.pallas_call(
        functools.partial(_win_kernel, tq=tq, tk=tk),
        grid=(g, s // tq, bsz),
        in_specs=[pl.BlockSpec((None, None, p, tq, dh), lambda h, i, b: qmap(i, h, b)),
                  pl.BlockSpec((None, None, tk, dh), lambda h, i, b: prev(i, h, b)),
                  pl.BlockSpec((None, None, tk, dh), lambda h, i, b: cur(i, h, b)),
                  pl.BlockSpec((None, None, tk, dh), lambda h, i, b: prev(i, h, b)),
                  pl.BlockSpec((None, None, tk, dh), lambda h, i, b: cur(i, h, b)),
                  pl.BlockSpec((p, 1, tq, tk), lambda h, i, b: (h, i % r + r, 0, 0)),
                  pl.BlockSpec((p, 1, tq, tk), lambda h, i, b: (h, i % r, 0, 0)),
                  pl.BlockSpec((None, None, p, tq, 3), lambda h, i, b: qmap(i, h, b)),
                  pl.BlockSpec((None, None, p, tq, dh), lambda h, i, b: qmap(i, h, b))],
        out_specs=pl.BlockSpec((None, None, p, tq, dh), lambda h, i, b: qmap(i, h, b)),
        out_shape=jax.ShapeDtypeStruct((bsz, g, p, s, dh), F32),
        input_output_aliases={8: 0},
        compiler_params=_cparams(("parallel", "parallel", "parallel")),
    )(q5, kw, kw, vw, vw, bias_t, bias_t, gl5, acc)


def _sel_kernel(it_ref, jt_ref, q_ref, k_ref, v_ref, bias_ref, gl_ref, acc_ref, o_ref, m_sc, a_sc,
                *, tq, tk, bsz):
    p = NSA_HPG
    step = pl.program_id(1)
    i = it_ref[step]
    j = jt_ref[step]
    t0 = i * tq
    s0 = j * tk
    last = s0 + tk > t0

    @pl.when(j == 0)
    def _():
        m_sc[...] = jnp.full_like(m_sc, NEG_BIG)
        a_sc[...] = jnp.zeros_like(a_sc)

    def accumulate(on_diagonal):
        bias = bias_ref[:, 0]
        if on_diagonal:
            causal = (t0 + lax.broadcasted_iota(jnp.int32, (tq, tk), 0)
                      >= s0 + lax.broadcasted_iota(jnp.int32, (tq, tk), 1))[None]
        for b in range(bsz):
            q = q_ref[b].reshape(p * tq, 2 * LANES)
            lg = lax.dot_general(q, k_ref[b], (((1,), (1,)), ((), ())), preferred_element_type=F32)
            lg = lg.reshape(p, tq, tk) + bias
            if on_diagonal:
                lg = jnp.where(causal, lg, NEG_BIG)
            lg = lg.reshape(p * tq, tk)
            m_old = m_sc[b]
            m_new = jnp.maximum(m_old, jnp.max(lg, -1, keepdims=True))
            alpha = jnp.exp2(m_old - m_new)
            e = jnp.exp2(lg - pltpu.repeat(m_new, tk // LANES, 1))
            a_sc[b] = alpha * a_sc[b] + jnp.dot(e.astype(BF16), v_ref[b], preferred_element_type=F32)
            m_sc[b] = m_new

    @pl.when(jnp.logical_not(last))
    def _():
        accumulate(False)

    @pl.when(last)
    def _():
        accumulate(True)
        gate = jax.nn.sigmoid(gl_ref[:, :, :, 1:2])
        a = a_sc[...]
        o = (a / pltpu.roll(a, NSA_DH, 2))[:, :, :NSA_DH].reshape(bsz, p, tq, NSA_DH)
        o_ref[...] = acc_ref[...] + o * gate


def _sel_branch(q_aug, k_aug, v_aug, bias_t, gl5, acc, tq, tk):
    bsz, g, p, s, wq = q_aug.shape
    dh = acc.shape[-1]
    r = tk // tq
    pairs = [(i, j) for i in range(s // tq) for j in range(i // r + 1)]
    it = jnp.asarray([ij[0] for ij in pairs], jnp.int32)
    jt = jnp.asarray([ij[1] for ij in pairs], jnp.int32)
    nd = bias_t.shape[1] - 1
    qmap = lambda h, t, it, jt: (0, h, 0, it[t], 0)
    kmap = lambda h, t, it, jt: (0, h, jt[t], 0)
    grid_spec = pltpu.PrefetchScalarGridSpec(
        num_scalar_prefetch=2,
        grid=(g, len(pairs)),
        in_specs=[pl.BlockSpec((bsz, None, p, tq, wq), qmap),
                  pl.BlockSpec((bsz, None, tk, wq), kmap),
                  pl.BlockSpec((bsz, None, tk, LANES), kmap),
                  pl.BlockSpec((p, 1, tq, tk),
                               lambda h, t, it, jt: (h, jnp.minimum(it[t] - r * jt[t], nd), 0, 0)),
                  pl.BlockSpec((bsz, None, p, tq, 3), qmap),
                  pl.BlockSpec((bsz, None, p, tq, dh), qmap)],
        out_specs=pl.BlockSpec((bsz, None, p, tq, dh), qmap),
        scratch_shapes=[pltpu.VMEM((bsz, p * tq, LANES), F32),
                        pltpu.VMEM((bsz, p * tq, LANES), F32)],
    )
    return pl.pallas_call(
        functools.partial(_sel_kernel, tq=tq, tk=tk, bsz=bsz),
        grid_spec=grid_spec,
        out_shape=jax.ShapeDtypeStruct((bsz, g, p, s, dh), F32),
        input_output_aliases={7: 0},
        compiler_params=_cparams(("parallel", "arbitrary")),
    )(it, jt, q_aug, k_aug, v_aug, bias_t, gl5, acc)


def _merge_kernel(x_ref, oa_ref, ob_ref, ga_ref, gb_ref, wa_ref, wb_ref, wo_ref, g_ref, b_ref, o_ref):
    ya = jnp.dot(oa_ref[...].astype(BF16), wa_ref[...], preferred_element_type=F32)
    yb = jnp.dot(ob_ref[...].astype(BF16), wb_ref[...], preferred_element_type=F32)
    y = jax.nn.sigmoid(ga_ref[...]) * ya + jax.nn.sigmoid(gb_ref[...]) * yb
    mix = jnp.dot(y.astype(BF16), wo_ref[...], preferred_element_type=F32)
    o_ref[...] = _layer_norm(DN_ALPHA * x_ref[...] + mix, g_ref[...], b_ref[...])


def _merge(x, o_a, o_b, proj, wa, wb, wo, g, b, tm):
    t, d = x.shape
    nga = COL_GA // d
    row = lambda i: (i, 0)
    const = lambda i: (0, 0)
    return pl.pallas_call(
        _merge_kernel,
        grid=(t // tm,),
        in_specs=[pl.BlockSpec((tm, d), row),
                  pl.BlockSpec((tm, HG_WIDTH), row),
                  pl.BlockSpec((tm, NSA_WIDTH), row),
                  pl.BlockSpec((tm, d), lambda i: (i, nga)),
                  pl.BlockSpec((tm, d), lambda i: (i, nga + 1)),
                  pl.BlockSpec((HG_WIDTH, d), const),
                  pl.BlockSpec((NSA_WIDTH, d), const),
                  pl.BlockSpec((d, d), const),
                  pl.BlockSpec((1, d), const),
                  pl.BlockSpec((1, d), const)],
        out_specs=pl.BlockSpec((tm, d), row),
        out_shape=jax.ShapeDtypeStruct((t, d), F32),
        compiler_params=_cparams(("parallel",)),
    )(x, o_a, o_b, proj, proj, wa, wb, wo, g, b)


def _swiglu_step(xb, wg_ref, wu_ref, wd_ref):
    hg = jnp.dot(xb, wg_ref[...], preferred_element_type=F32)
    hu = jnp.dot(xb, wu_ref[...], preferred_element_type=F32)
    h = (hg * jax.nn.sigmoid(hg)) * hu
    return jnp.dot(h.astype(BF16), wd_ref[...], preferred_element_type=F32)


def _ffn_kernel(x_ref, wg_ref, wu_ref, wd_ref, g_ref, b_ref, o_ref, xb_ref, acc_ref):
    j = pl.program_id(1)

    @pl.when(j == 0)
    def _():
        xb_ref[...] = x_ref[...].astype(BF16)
        acc_ref[...] = jnp.zeros_like(acc_ref)

    acc_ref[...] += _swiglu_step(xb_ref[...], wg_ref, wu_ref, wd_ref)

    @pl.when(j == pl.num_programs(1) - 1)
    def _():
        o_ref[...] = _layer_norm(DN_ALPHA * x_ref[...] + acc_ref[...], g_ref[...], b_ref[...])


def _ffn(x, wg, wu, wd, g, b, tm, tf):
    t, d = x.shape
    f = wg.shape[1]
    return pl.pallas_call(
        _ffn_kernel,
        grid=(t // tm, f // tf),
        in_specs=[pl.BlockSpec((tm, d), lambda i, j: (i, 0)),
                  pl.BlockSpec((d, tf), lambda i, j: (0, j)),
                  pl.BlockSpec((d, tf), lambda i, j: (0, j)),
                  pl.BlockSpec((tf, d), lambda i, j: (j, 0)),
                  pl.BlockSpec((1, d), lambda i, j: (0, 0)),
                  pl.BlockSpec((1, d), lambda i, j: (0, 0))],
        out_specs=pl.BlockSpec((tm, d), lambda i, j: (i, 0)),
        out_shape=jax.ShapeDtypeStruct((t, d), F32),
        scratch_shapes=[pltpu.VMEM((tm, d), BF16), pltpu.VMEM((tm, d), F32)],
        compiler_params=_cparams(("parallel", "arbitrary")),
    )(x, wg, wu, wd, g, b)


def _router_kernel(x_ref, w_ref, o_ref):
    logits = jnp.dot(x_ref[...], w_ref[...], preferred_element_type=F32, precision=lax.Precision.HIGHEST)
    lane = lax.broadcasted_iota(jnp.int32, logits.shape, 1).astype(F32)
    logits = jnp.where(lane < N_EXPERTS, logits, -jnp.inf)
    v1 = jnp.max(logits, -1, keepdims=True)
    e1 = jnp.min(jnp.where(logits == v1, lane, float(LANES)), -1, keepdims=True)
    rest = jnp.where(lane == e1, -jnp.inf, logits)
    v2 = jnp.max(rest, -1, keepdims=True)
    e2 = jnp.min(jnp.where(rest == v2, lane, float(LANES)), -1, keepdims=True)
    x2 = jnp.exp(v2 - v1)
    den = 1.0 + x2
    o_ref[...] = jnp.where(lane == 0, e1, jnp.where(lane == 1, e2, jnp.where(
        lane == 2, 1.0 / den, jnp.where(lane == 3, x2 / den, 0.0))))


def _router(x, w, tm):
    t, d = x.shape
    return pl.pallas_call(
        _router_kernel,
        grid=(t // tm,),
        in_specs=[pl.BlockSpec((tm, d), lambda i: (i, 0)), pl.BlockSpec((d, LANES), lambda i: (0, 0))],
        out_specs=pl.BlockSpec((tm, LANES), lambda i: (i, 0)),
        out_shape=jax.ShapeDtypeStruct((t, LANES), F32),
        compiler_params=_cparams(("parallel",)),
    )(x, w)


def _expert_kernel(be_ref, x_ref, wg_ref, wu_ref, wd_ref, o_ref, acc_ref):
    j = pl.program_id(1)

    @pl.when(j == 0)
    def _():
        acc_ref[...] = jnp.zeros_like(acc_ref)

    acc_ref[...] += _swiglu_step(x_ref[...], wg_ref, wu_ref, wd_ref)

    @pl.when(j == pl.num_programs(1) - 1)
    def _():
        o_ref[...] = acc_ref[...]


def _experts(blk_e, xs, wg, wu, wd, tf):
    rows, d = xs.shape
    f = wg.shape[2]
    tm = MOE_ROW_BLOCK
    grid_spec = pltpu.PrefetchScalarGridSpec(
        num_scalar_prefetch=1,
        grid=(rows // tm, f // tf),
        in_specs=[pl.BlockSpec((tm, d), lambda i, j, be: (i, 0)),
                  pl.BlockSpec((None, d, tf), lambda i, j, be: (be[i], 0, j)),
                  pl.BlockSpec((None, d, tf), lambda i, j, be: (be[i], 0, j)),
                  pl.BlockSpec((None, tf, d), lambda i, j, be: (be[i], j, 0))],
        out_specs=pl.BlockSpec((tm, d), lambda i, j, be: (i, 0)),
        scratch_shapes=[pltpu.VMEM((tm, d), F32)],
    )
    return pl.pallas_call(
        _expert_kernel,
        grid_spec=grid_spec,
        out_shape=jax.ShapeDtypeStruct((rows, d), F32),
        compiler_params=_cparams(("parallel", "arbitrary")),
    )(blk_e, xs, wg, wu, wd)


def _combine_kernel(x_ref, y1_ref, y2_ref, gt_ref, g_ref, b_ref, o_ref):
    f = y1_ref[...] * gt_ref[:, 2:3] + y2_ref[...] * gt_ref[:, 3:4]
    o_ref[...] = _layer_norm(DN_ALPHA * x_ref[...] + f, g_ref[...], b_ref[...])


def _combine(x, y1, y2, route, g, b, tm):
    t, d = x.shape
    row = lambda i: (i, 0)
    const = lambda i: (0, 0)
    return pl.pallas_call(
        _combine_kernel,
        grid=(t // tm,),
        in_specs=[pl.BlockSpec((tm, d), row), pl.BlockSpec((tm, d), row), pl.BlockSpec((tm, d), row),
                  pl.BlockSpec((tm, LANES), row), pl.BlockSpec((1, d), const), pl.BlockSpec((1, d), const)],
        out_specs=pl.BlockSpec((tm, d), row),
        out_shape=jax.ShapeDtypeStruct((t, d), F32),
        compiler_params=_cparams(("parallel",)),
    )(x, y1, y2, route, g, b)


def _moe(x, w_router, wg, wu, wd, g, b):
    t, d = x.shape
    tk_ = t * TOP_K
    route = _router(x, jnp.pad(w_router, ((0, 0), (0, LANES - N_EXPERTS))), 512)
    flat_e = route[:, :TOP_K].astype(jnp.int32).reshape(-1)
    onehot = (flat_e[:, None] == jnp.arange(N_EXPERTS)[None, :]).astype(jnp.int32)
    csum = jnp.cumsum(onehot, axis=0)
    counts = csum[-1]
    rank = jnp.sum(onehot * csum, axis=1) - 1
    padded = (counts + MOE_ROW_BLOCK - 1) // MOE_ROW_BLOCK * MOE_ROW_BLOCK
    pend = jnp.cumsum(padded)
    dest = (pend - padded)[flat_e] + rank
    n_blocks = -(-(tk_ + N_EXPERTS * (MOE_ROW_BLOCK - 1)) // MOE_ROW_BLOCK)
    n_rows = n_blocks * MOE_ROW_BLOCK
    row_tok = jnp.zeros((n_rows,), jnp.int32).at[dest].set(jnp.arange(tk_, dtype=jnp.int32) // TOP_K)
    blk_e = jnp.minimum(jnp.searchsorted(pend, jnp.arange(n_blocks) * MOE_ROW_BLOCK, side='right'),
                        N_EXPERTS - 1).astype(jnp.int32)
    xs = x.astype(BF16)[row_tok]
    ys = _experts(blk_e, xs, wg, wu, wd, 512)
    yk = ys[dest].reshape(t, TOP_K, d)
    return _combine(x, yk[:, 0], yk[:, 1], route, g, b, 512)


def _pack_w_in(w_in):
    offs = np.concatenate([[0], np.cumsum(IN_SIZES)])
    seg = [w_in[:, offs[j]:offs[j + 1]] for j in range(len(IN_SIZES))]
    seg[4] = seg[4] * (NSA_DH ** -0.5 * LOG2E)
    seg[11] = jnp.pad(seg[11], ((0, 0), (0, LANES - 3 * NSA_HEADS)))
    seg = seg[12:14] + seg[0:12]
    packed = jnp.concatenate(seg + [jnp.zeros((w_in.shape[0], PROJ_WP - PROJ_W), w_in.dtype)], axis=1)
    return packed.astype(BF16)


def _cmp_to_sel(n_cmp_pad, n_cmp):
    cs = np.arange(n_cmp_pad)[:, None] * CMP_STRIDE
    ss = np.arange(LANES)[None, :] * SEL_BLOCK
    overlap = np.clip(np.minimum(cs + CMP_LEN, ss + SEL_BLOCK) - np.maximum(cs, ss), 0, None) / CMP_LEN
    overlap[n_cmp:] = 0.0
    return jnp.asarray(overlap, F32)


def _token_mixer(x, w_in_p, lb, hg_norm_w, cmp_pos, cmp_w1, cmp_w2, bias_c, bias_t, wa, wb, wo, ln_g, ln_b):
    bsz, s, d = x.shape
    g, p, dh = NSA_GROUPS, NSA_HPG, NSA_DH
    xf = x.reshape(bsz * s, d)
    proj = _project(xf, w_in_p, 1024, PROJ_TN).reshape(bsz, s, PROJ_WP)
    o_a = _hgrn(proj, lb, hg_norm_w, 512)

    def heads(c0, width):
        return proj[:, :, c0:c0 + width]

    q5 = heads(COL_NQ, NSA_WIDTH).astype(BF16).reshape(bsz, s, g, p, dh).transpose(0, 2, 3, 1, 4)
    kv = heads(COL_KV, 6 * KV_WIDTH).reshape(bsz, s, 6, g, dh)
    n16 = s // CMP_STRIDE
    a = kv[:, :, 0:2].reshape(bsz, n16, CMP_STRIDE, 2, g, dh).transpose(3, 0, 4, 1, 2, 5)
    a = a.reshape(2, bsz, g, n16, CMP_STRIDE * dh)
    kvc = _compress(a, cmp_pos.reshape(2, 1, CMP_LEN * dh), cmp_w1.astype(BF16), cmp_w2.astype(BF16))
    kvh = kv[:, :, 2:6].astype(BF16).transpose(2, 0, 3, 1, 4)
    gl5 = heads(COL_NG, 3 * NSA_HEADS).reshape(bsz, s, g, p, 3).transpose(0, 2, 3, 1, 4)
    n_cmp = (s - CMP_LEN) // CMP_STRIDE + 1
    acc, pen = _cmp_branch(q5, kvc[0], kvc[1], bias_c, _cmp_to_sel(n16, n_cmp), gl5, ATT_TQ)
    acc = _win_branch(q5, kvh[2], kvh[3], bias_t, gl5, acc, ATT_TQ, ATT_TK)
    q_aug = jnp.concatenate([jnp.broadcast_to(pen[:, :, None], (bsz, g, p, s, LANES)), q5,
                             jnp.zeros((bsz, g, p, s, LANES - dh), BF16)], axis=-1)
    block_of_key = (np.arange(s)[:, None] // SEL_BLOCK == np.arange(LANES)[None, :])
    k_aug = jnp.concatenate([jnp.broadcast_to(jnp.asarray(block_of_key, BF16), (bsz, g, s, LANES)), kvh[0],
                             jnp.zeros((bsz, g, s, LANES - dh), BF16)], axis=-1)
    v_aug = jnp.concatenate([kvh[1], jnp.ones((bsz, g, s, LANES - dh), BF16)], axis=-1)
    o_b = _sel_branch(q_aug, k_aug, v_aug, bias_t, gl5, acc, ATT_TQ, ATT_TK)
    o_b = o_b.transpose(0, 3, 1, 2, 4).reshape(bsz * s, NSA_WIDTH)
    return _merge(xf, o_a.reshape(bsz * s, HG_WIDTH), o_b, proj.reshape(bsz * s, PROJ_WP),
                  wa, wb, wo, ln_g, ln_b, 512)


def kernel(x, w_in, hg_lb_logits, hg_norm_w, cmp_pos, cmp_w1, cmp_w2, rel_bias, w_branch_a, w_branch_b,
           w_out, ln1_g, ln1_b, ln2_g, ln2_b, ffn_w_gate, ffn_w_up, ffn_w_down, moe_router, moe_w_gate,
           moe_w_up, moe_w_down):
    bsz, s, d = x.shape
    depth = w_in.shape[0]
    p_lb = jax.nn.softmax(hg_lb_logits.astype(F32), axis=0)
    lbs = jnp.cumsum(p_lb, axis=0) - p_lb[0]
    n16 = s // CMP_STRIDE
    bias_c = _bias_tiles(rel_bias, s // ATT_TQ, ATT_TQ, n16,
                         lambda t, r, c: t * ATT_TQ + r - (c * CMP_STRIDE + CMP_LEN - 1))
    bias_t = _bias_tiles(rel_bias, BIAS_ND + 1, ATT_TQ, ATT_TK, lambda t, r, c: t * ATT_TQ + r - c)
    f_pad = -(-D_FF // LANES) * LANES - D_FF
    xf = x.reshape(bsz * s, d)
    for l in range(depth):
        xf = _token_mixer(xf.reshape(bsz, s, d), _pack_w_in(w_in[l]), lbs[l][None], hg_norm_w[l][None],
                          cmp_pos[l], cmp_w1[l], cmp_w2[l], bias_c, bias_t,
                          w_branch_a[l].astype(BF16), w_branch_b[l].astype(BF16), w_out[l].astype(BF16),
                          ln1_g[l][None], ln1_b[l][None])
        if l % 2 == 0:
            wg = jnp.pad(ffn_w_gate[l // 2], ((0, 0), (0, f_pad))).astype(BF16)
            wu = jnp.pad(ffn_w_up[l // 2], ((0, 0), (0, f_pad))).astype(BF16)
            wd = jnp.pad(ffn_w_down[l // 2], ((0, f_pad), (0, 0))).astype(BF16)
            xf = _ffn(xf, wg, wu, wd, ln2_g[l][None], ln2_b[l][None], 512, wg.shape[1] // 2)
        else:
            xf = _moe(xf, moe_router[l // 2], moe_w_gate[l // 2].astype(BF16), moe_w_up[l // 2].astype(BF16),
                      moe_w_down[l // 2].astype(BF16), ln2_g[l][None], ln2_b[l][None])
    return xf.reshape(bsz, s, d)
```

```python
import functools
import math

import jax
import jax.numpy as jnp
import numpy as np
from jax import lax
from jax.experimental import pallas as pl
from jax.experimental.pallas import tpu as pltpu

F32 = jnp.float32
BF16 = jnp.bfloat16

D_MODEL = 1024
DEPTH = 2
HG_HEADS = 4
HG_DK = 128
HG_WIDTH = HG_HEADS * HG_DK
HG_CHUNK = 64
HG_SUB = 16
NSA_HEADS = 8
NSA_GROUPS = 2
NSA_HPG = NSA_HEADS // NSA_GROUPS
NSA_DH = 64
NSA_WIDTH = NSA_HEADS * NSA_DH
KV_WIDTH = NSA_GROUPS * NSA_DH
CMP_LEN = 32
CMP_STRIDE = 16
CMP_HIDDEN = 2 * NSA_DH
SEL_BLOCK = 64
N_SEL = 16
WINDOW = 512
FORCE_SCORE = 1e9
NEG_BIG = -1e30
REL_BUCKETS = 32
REL_MAX_DIST = 2048
D_FF = 2752
N_EXPERTS = 8
TOP_K = 2
D_FF_EXPERT = 3584
MOE_ROW_BLOCK = 512
DN_ALPHA = (2 * DEPTH) ** 0.25
LN_EPS = 1e-5
IN_SIZES = (HG_WIDTH, HG_WIDTH, HG_WIDTH, HG_WIDTH, NSA_WIDTH,
            KV_WIDTH, KV_WIDTH, KV_WIDTH, KV_WIDTH, KV_WIDTH, KV_WIDTH,
            3 * NSA_HEADS, D_MODEL, D_MODEL)

LANES = 128
LOG2E = 1.0 / math.log(2.0)
COL_GA = 0
COL_HG = 2 * D_MODEL
COL_NQ = COL_HG + 4 * HG_WIDTH
COL_KV = COL_NQ + NSA_WIDTH
COL_NG = COL_KV + 6 * KV_WIDTH
PROJ_W = COL_NG + LANES
PROJ_TN = 512
PROJ_WP = -(-PROJ_W // PROJ_TN) * PROJ_TN

ATT_TQ = 256
ATT_TK = 512
BIAS_ND = -(-(REL_MAX_DIST + ATT_TK) // ATT_TQ)
VMEM_LIMIT = 48 * 1024 * 1024


def _cparams(sem):
    return pltpu.CompilerParams(dimension_semantics=sem, vmem_limit_bytes=VMEM_LIMIT)


def _proj_kernel(x_ref, w_ref, o_ref, xb_ref):
    @pl.when(pl.program_id(1) == 0)
    def _():
        xb_ref[...] = x_ref[...].astype(BF16)

    o_ref[...] = jnp.dot(xb_ref[...], w_ref[...], preferred_element_type=F32)


def _project(x, w, tm, tn):
    m, k = x.shape
    n = w.shape[1]
    return pl.pallas_call(
        _proj_kernel,
        grid=(m // tm, n // tn),
        in_specs=[pl.BlockSpec((tm, k), lambda i, j: (i, 0)),
                  pl.BlockSpec((k, tn), lambda i, j: (0, j))],
        out_specs=pl.BlockSpec((tm, tn), lambda i, j: (i, j)),
        out_shape=jax.ShapeDtypeStruct((m, n), F32),
        scratch_shapes=[pltpu.VMEM((tm, k), BF16)],
        compiler_params=_cparams(("parallel", "arbitrary")),
    )(x, w)


def _layer_norm(y, g, b):
    mu = jnp.mean(y, -1, keepdims=True)
    yc = y - mu
    var = jnp.mean(yc * yc, -1, keepdims=True)
    return yc * lax.rsqrt(var + LN_EPS) * g + b


def _cumsum_rows(tril, x):
    total = None
    for _ in range(3):
        part = x.astype(BF16)
        term = jnp.dot(tril, part, preferred_element_type=F32)
        total = term if total is None else total + term
        x = x - part.astype(F32)
    return total


def _hgrn_kernel(q_ref, f_ref, i_ref, g_ref, lb_ref, nw_ref, o_ref, st_ref, *, n_chunks):
    @pl.when(pl.program_id(1) == 0)
    def _():
        st_ref[...] = jnp.zeros_like(st_ref)

    c = HG_CHUNK
    nt = (((1,), (1,)), ((), ()))
    row = lax.broadcasted_iota(jnp.int32, (c, c), 0)
    col = lax.broadcasted_iota(jnp.int32, (c, c), 1)
    tril = (row >= col).astype(BF16)
    sub_pos = lax.broadcasted_iota(jnp.int32, (c, 1), 0) % HG_SUB

    def head_chunk(h, r0):
        lanes = slice(h * HG_DK, (h + 1) * HG_DK)
        q = q_ref[pl.ds(r0, c), lanes]
        z = f_ref[pl.ds(r0, c), lanes]
        v = i_ref[pl.ds(r0, c), lanes]
        g = g_ref[pl.ds(r0, c), lanes]
        k = (1.0 - lb_ref[:, lanes]) * jax.nn.sigmoid(-z)
        b = _cumsum_rows(tril, jnp.log1p(-k))
        b_last = b[c - 1:c, :]
        st = st_ref[h]
        vb = v.astype(BF16)
        o = lax.dot_general((q * jnp.exp(b)).astype(BF16), st.astype(BF16), nt, preferred_element_type=F32)
        parts = [jnp.zeros((HG_SUB, HG_DK), F32)]
        for sb in range(1, c // HG_SUB):
            lo = sb * HG_SUB
            ref_b = b[lo - 1:lo, :]
            qt = q[lo:lo + HG_SUB, :] * jnp.exp(b[lo:lo + HG_SUB, :] - ref_b)
            kt = k[:lo, :] * jnp.exp(ref_b - b[:lo, :])
            att = lax.dot_general(qt.astype(BF16), kt.astype(BF16), nt, preferred_element_type=F32)
            parts.append(jnp.dot(att.astype(BF16), vb[:lo, :], preferred_element_type=F32))
        o = o + jnp.concatenate(parts, axis=0)
        for lag in range(HG_SUB):
            if lag == 0:
                ks, bs, vs = k, b, v
            else:
                ks = pltpu.roll(k, lag, 0)
                bs = pltpu.roll(b, lag, 0)
                vs = pltpu.roll(v, lag, 0)
            valid = sub_pos >= lag
            e = jnp.exp(jnp.where(valid, b - bs, 0.0))
            a = jnp.sum(q * ks * e, axis=1, keepdims=True)
            o = o + jnp.where(valid, a, 0.0) * vs
        khat = (k * jnp.exp(b_last - b)).astype(BF16)
        st_ref[h] = st * jnp.exp(b_last) + lax.dot_general(
            vb, khat, (((0,), (0,)), ((), ())), preferred_element_type=F32)
        o = o * lax.rsqrt(jnp.mean(o * o, -1, keepdims=True) + 1e-6)
        o_ref[pl.ds(r0, c), lanes] = o * nw_ref[:, lanes] * (g * jax.nn.sigmoid(g))

    def chunk(ci, carry):
        r0 = pl.multiple_of(ci * c, c)
        for h in range(HG_HEADS):
            head_chunk(h, r0)
        return carry

    lax.fori_loop(0, n_chunks, chunk, 0)


def _hgrn(proj, lb, norm_w, ts):
    bsz, s, _ = proj.shape
    col0 = COL_HG // HG_WIDTH

    def col_spec(n):
        return pl.BlockSpec((None, ts, HG_WIDTH), lambda b, t: (b, t, col0 + n))

    head_spec = pl.BlockSpec((1, HG_WIDTH), lambda b, t: (0, 0))
    return pl.pallas_call(
        functools.partial(_hgrn_kernel, n_chunks=ts // HG_CHUNK),
        grid=(bsz, s // ts),
        in_specs=[col_spec(0), col_spec(1), col_spec(2), col_spec(3), head_spec, head_spec],
        out_specs=pl.BlockSpec((None, ts, HG_WIDTH), lambda b, t: (b, t, 0)),
        out_shape=jax.ShapeDtypeStruct((bsz, s, HG_WIDTH), F32),
        scratch_shapes=[pltpu.VMEM((HG_HEADS, HG_DK, HG_DK), F32)],
        compiler_params=_cparams(("parallel", "arbitrary")),
    )(proj, proj, proj, proj, lb, norm_w)


def _compress_kernel(a_ref, pos_ref, w1_ref, w2_ref, o_ref):
    half = CMP_STRIDE * NSA_DH
    a = a_ref[...]
    n = a.shape[0]
    a1 = (a + pos_ref[:, :half]).astype(BF16)
    a2 = (a + pos_ref[:, half:]).astype(BF16)
    y1 = jnp.dot(a1, w1_ref[:half, :], preferred_element_type=F32)
    y2 = jnp.dot(a2, w1_ref[half:, :], preferred_element_type=F32)
    hid = jax.nn.gelu(y1 + pltpu.roll(y2, n - 1, 0))
    o_ref[...] = jnp.dot(hid.astype(BF16), w2_ref[...], preferred_element_type=F32).astype(o_ref.dtype)


def _compress(a, pos, w1, w2):
    _, bsz, g, n, width = a.shape
    return pl.pallas_call(
        _compress_kernel,
        grid=(2, bsz, g),
        in_specs=[pl.BlockSpec((None, None, None, n, width), lambda c, b, h: (c, b, h, 0, 0)),
                  pl.BlockSpec((None, 1, 2 * width), lambda c, b, h: (c, 0, 0)),
                  pl.BlockSpec((None, 2 * width, CMP_HIDDEN), lambda c, b, h: (c, 0, 0)),
                  pl.BlockSpec((None, CMP_HIDDEN, NSA_DH), lambda c, b, h: (c, 0, 0))],
        out_specs=pl.BlockSpec((None, None, None, n, NSA_DH), lambda c, b, h: (c, b, h, 0, 0)),
        out_shape=jax.ShapeDtypeStruct((2, bsz, g, n, NSA_DH), BF16),
        compiler_params=_cparams(("parallel", "parallel", "parallel")),
    )(a, pos, w1, w2)


def _bucket_of(dist):
    n = jnp.maximum(dist, 0)
    exact = REL_BUCKETS // 2
    large = exact + (jnp.log(jnp.maximum(n, exact).astype(F32) / exact)
                     / math.log(REL_MAX_DIST / exact) * (REL_BUCKETS - exact)).astype(jnp.int32)
    return jnp.where(n < exact, n, jnp.minimum(large, REL_BUCKETS - 1))


def _bias_kernel(tab_ref, o_ref, *, dist_fn):
    rows, cols = o_ref.shape[2], o_ref.shape[3]
    row = lax.broadcasted_iota(jnp.int32, (rows, cols), 0)
    col = lax.broadcasted_iota(jnp.int32, (rows, cols), 1)
    bucket = _bucket_of(dist_fn(pl.program_id(0), row, col))
    for h in range(NSA_HEADS):
        tab = jnp.broadcast_to(tab_ref[h:h + 1, :] * LOG2E, (rows, LANES))
        for c0 in range(0, cols, LANES):
            o_ref[h, 0, :, c0:c0 + LANES] = jnp.take_along_axis(tab, bucket[:, c0:c0 + LANES], axis=1)


def _bias_tiles(rel_bias, n_tiles, rows, cols, dist_fn):
    tab = jnp.pad(rel_bias.T, ((0, 0), (0, LANES - REL_BUCKETS)))
    return pl.pallas_call(
        functools.partial(_bias_kernel, dist_fn=dist_fn),
        grid=(n_tiles,),
        in_specs=[pl.BlockSpec((NSA_HEADS, LANES), lambda t: (0, 0))],
        out_specs=pl.BlockSpec((NSA_HEADS, 1, rows, cols), lambda t: (0, t, 0, 0)),
        out_shape=jax.ShapeDtypeStruct((NSA_HEADS, n_tiles, rows, cols), F32),
        compiler_params=_cparams(("parallel",)),
    )(tab)


def _cmp_kernel(q_ref, k_ref, v_ref, bias_ref, c2s_ref, gl_ref, o_ref, pen_ref, *, tq):
    p = NSA_HPG
    t0 = pl.program_id(1) * tq
    bsz, ncp = k_ref.shape[0], k_ref.shape[1]
    bias = bias_ref[:, 0]
    tpos = t0 + lax.broadcasted_iota(jnp.int32, (tq, ncp), 0)
    cend = lax.broadcasted_iota(jnp.int32, (tq, ncp), 1) * CMP_STRIDE + (CMP_LEN - 1)
    mask = (tpos >= cend)[None]
    lane = lax.broadcasted_iota(jnp.int32, (tq, LANES), 1)
    cur = (t0 + lax.broadcasted_iota(jnp.int32, (tq, LANES), 0)) // SEL_BLOCK
    forced = (lane == 0) | (lane == cur) | (lane == cur - 1)
    visible = lane <= cur
    for b in range(bsz):
        q = q_ref[b].reshape(p * tq, NSA_DH)
        logits = lax.dot_general(q, k_ref[b], (((1,), (1,)), ((), ())), preferred_element_type=F32)
        logits = jnp.where(mask, logits.reshape(p, tq, ncp) + bias, NEG_BIG)
        m = jnp.max(logits, -1, keepdims=True)
        e = jnp.where(mask, jnp.exp2(logits - m), 0.0)
        pc = e / jnp.maximum(jnp.sum(e, -1, keepdims=True), 1e-30)
        o = jnp.dot(pc.reshape(p * tq, ncp).astype(BF16), v_ref[b], preferred_element_type=F32)
        o_ref[b] = o.reshape(p, tq, NSA_DH) * jax.nn.sigmoid(gl_ref[b, :, :, 0:1])
        imp = jnp.dot(jnp.sum(pc, axis=0), c2s_ref[...], preferred_element_type=F32,
                      precision=lax.Precision.HIGHEST)
        score = jnp.where(forced, FORCE_SCORE, jnp.where(visible, imp, NEG_BIG))
        member = jnp.zeros((tq, LANES), F32)
        for _ in range(N_SEL):
            hit = lane == jnp.argmax(score, axis=-1, keepdims=True)
            member = jnp.where(hit, 1.0, member)
            score = jnp.where(hit, -jnp.inf, score)
        pen_ref[b] = jnp.where((member > 0.5) & visible, 0.0, NEG_BIG).astype(pen_ref.dtype)


def _cmp_branch(q5, kc, vc, bias_c, c2s, gl5, tq):
    bsz, g, p, s, dh = q5.shape
    ncp = kc.shape[2]
    qmap = lambda h, i: (0, h, 0, i, 0)
    return pl.pallas_call(
        functools.partial(_cmp_kernel, tq=tq),
        grid=(g, s // tq),
        in_specs=[pl.BlockSpec((bsz, None, p, tq, dh), qmap),
                  pl.BlockSpec((bsz, None, ncp, dh), lambda h, i: (0, h, 0, 0)),
                  pl.BlockSpec((bsz, None, ncp, dh), lambda h, i: (0, h, 0, 0)),
                  pl.BlockSpec((p, 1, tq, ncp), lambda h, i: (h, i, 0, 0)),
                  pl.BlockSpec((ncp, LANES), lambda h, i: (0, 0)),
                  pl.BlockSpec((bsz, None, p, tq, 3), qmap)],
        out_specs=[pl.BlockSpec((bsz, None, p, tq, dh), qmap),
                   pl.BlockSpec((bsz, None, tq, LANES), lambda h, i: (0, h, i, 0))],
        out_shape=[jax.ShapeDtypeStruct((bsz, g, p, s, dh), F32),
                   jax.ShapeDtypeStruct((bsz, g, s, LANES), BF16)],
        compiler_params=_cparams(("parallel", "parallel")),
    )(q5, kc, vc, bias_c, c2s, gl5)


def _win_kernel(q_ref, kp_ref, kc_ref, vp_ref, vc_ref, bp_ref, bc_ref, gl_ref, acc_ref, o_ref, *, tq, tk):
    p = NSA_HPG
    bsz = q_ref.shape[0]
    t0 = pl.program_id(1) * tq
    s_cur = (t0 // tk) * tk
    tpos = t0 + lax.broadcasted_iota(jnp.int32, (tq, tk), 0)
    col = lax.broadcasted_iota(jnp.int32, (tq, tk), 1)

    def tile_mask(s0):
        kpos = s0 + col
        dist = tpos - kpos
        return ((dist >= 0) & (dist < WINDOW) & (kpos >= 0))[None]

    mask_prev, mask_cur = tile_mask(s_cur - tk), tile_mask(s_cur)
    bias_prev, bias_cur = bp_ref[:, 0], bc_ref[:, 0]
    nt = (((1,), (1,)), ((), ()))
    for b in range(bsz):
        q = q_ref[b].reshape(p * tq, NSA_DH)
        l_prev = lax.dot_general(q, kp_ref[b], nt, preferred_element_type=F32).reshape(p, tq, tk)
        l_cur = lax.dot_general(q, kc_ref[b], nt, preferred_element_type=F32).reshape(p, tq, tk)
        l_prev = jnp.where(mask_prev, l_prev + bias_prev, NEG_BIG)
        l_cur = jnp.where(mask_cur, l_cur + bias_cur, NEG_BIG)
        m = jnp.maximum(jnp.max(l_prev, -1, keepdims=True), jnp.max(l_cur, -1, keepdims=True))
        e_prev = jnp.exp2(l_prev - m).reshape(p * tq, tk).astype(BF16)
        e_cur = jnp.exp2(l_cur - m).reshape(p * tq, tk).astype(BF16)
        o = (jnp.dot(e_prev, vp_ref[b], preferred_element_type=F32)
             + jnp.dot(e_cur, vc_ref[b], preferred_element_type=F32))
        o = (o / pltpu.roll(o, NSA_DH, 1))[:, :NSA_DH].reshape(p, tq, NSA_DH)
        o_ref[b] = acc_ref[b] + o * jax.nn.sigmoid(gl_ref[b, :, :, 2:3])


def _win_branch(q5, kw, vw_aug, bias_t, gl5, acc, tq, tk):
    bsz, g, p, s, dh = q5.shape
    r = tk // tq
    cur = lambda h, i: (0, h, i // r, 0)
    prev = lambda h, i: (0, h, jnp.maximum(i // r - 1, 0), 0)
    qmap = lambda h, i: (0, h, 0, i, 0)
    return pl.pallas_call(
        functools.partial(_win_kernel, tq=tq, tk=tk),
        grid=(g, s // tq),
        in_specs=[pl.BlockSpec((bsz, None, p, tq, dh), qmap),
                  pl.BlockSpec((bsz, None, tk, dh), prev),
                  pl.BlockSpec((bsz, None, tk, dh), cur),
                  pl.BlockSpec((bsz, None, tk, LANES), prev),
                  pl.BlockSpec((bsz, None, tk, LANES), cur),
                  pl.BlockSpec((p, 1, tq, tk), lambda h, i: (h, i % r + r, 0, 0)),
                  pl.BlockSpec((p, 1, tq, tk), lambda h, i: (h, i % r, 0, 0)),
                  pl.BlockSpec((bsz, None, p, tq, 3), qmap),
                  pl.BlockSpec((bsz, None, p, tq, dh), qmap)],
        out_specs=pl.BlockSpec((bsz, None, p, tq, dh), qmap),
        out_shape=jax.ShapeDtypeStruct((bsz, g, p, s, dh), F32),
        input_output_aliases={8: 0},
        compiler_params=_cparams(("parallel", "parallel")),
    )(q5, kw, kw, vw_aug, vw_aug, bias_t, bias_t, gl5, acc)


def _sel_kernel(it_ref, jt_ref, q_ref, k_ref, v_ref, bias_ref, gl_ref, acc_ref, o_ref, m_sc, a_sc,
                *, tq, tk, bsz):
    p = NSA_HPG
    step = pl.program_id(1)
    i = it_ref[step]
    j = jt_ref[step]
    t0 = i * tq
    s0 = j * tk
    last = s0 + tk > t0

    @pl.when(j == 0)
    def _():
        m_sc[...] = jnp.full_like(m_sc, NEG_BIG)
        a_sc[...] = jnp.zeros_like(a_sc)

    def accumulate(on_diagonal):
        bias = bias_ref[:, 0]
        if on_diagonal:
            causal = (t0 + lax.broadcasted_iota(jnp.int32, (tq, tk), 0)
                      >= s0 + lax.broadcasted_iota(jnp.int32, (tq, tk), 1))[None]
        for b in range(bsz):
            q = q_ref[b].reshape(p * tq, 2 * LANES)
            lg = lax.dot_general(q, k_ref[b], (((1,), (1,)), ((), ())), preferred_element_type=F32)
            lg = lg.reshape(p, tq, tk) + bias
            if on_diagonal:
                lg = jnp.where(causal, lg, NEG_BIG)
            lg = lg.reshape(p * tq, tk)
            m_old = m_sc[b]
            m_new = jnp.maximum(m_old, jnp.max(lg, -1, keepdims=True))
            alpha = jnp.exp2(m_old - m_new)
            e = jnp.exp2(lg - jnp.concatenate([m_new] * (tk // LANES), axis=1))
            a_sc[b] = alpha * a_sc[b] + jnp.dot(e.astype(BF16), v_ref[b], preferred_element_type=F32)
            m_sc[b] = m_new

    @pl.when(jnp.logical_not(last))
    def _():
        accumulate(False)

    @pl.when(last)
    def _():
        accumulate(True)
        gate = jax.nn.sigmoid(gl_ref[:, :, :, 1:2])
        a = a_sc[...]
        o = (a / pltpu.roll(a, NSA_DH, 2))[:, :, :NSA_DH].reshape(bsz, p, tq, NSA_DH)
        o_ref[...] = acc_ref[...] + o * gate


def _sel_branch(q_aug, k_aug, v_aug, bias_t, gl5, acc, tq, tk):
    bsz, g, p, s, wq = q_aug.shape
    dh = acc.shape[-1]
    r = tk // tq
    pairs = [(i, j) for i in range(s // tq) for j in range(i // r + 1)]
    it = jnp.asarray([ij[0] for ij in pairs], jnp.int32)
    jt = jnp.asarray([ij[1] for ij in pairs], jnp.int32)
    nd = bias_t.shape[1] - 1
    qmap = lambda h, t, it, jt: (0, h, 0, it[t], 0)
    kmap = lambda h, t, it, jt: (0, h, jt[t], 0)
    grid_spec = pltpu.PrefetchScalarGridSpec(
        num_scalar_prefetch=2,
        grid=(g, len(pairs)),
        in_specs=[pl.BlockSpec((bsz, None, p, tq, wq), qmap),
                  pl.BlockSpec((bsz, None, tk, wq), kmap),
                  pl.BlockSpec((bsz, None, tk, LANES), kmap),
                  pl.BlockSpec((p, 1, tq, tk),
                               lambda h, t, it, jt: (h, jnp.minimum(it[t] - r * jt[t], nd), 0, 0)),
                  pl.BlockSpec((bsz, None, p, tq, 3), qmap),
                  pl.BlockSpec((bsz, None, p, tq, dh), qmap)],
        out_specs=pl.BlockSpec((bsz, None, p, tq, dh), qmap),
        scratch_shapes=[pltpu.VMEM((bsz, p * tq, LANES), F32),
                        pltpu.VMEM((bsz, p * tq, LANES), F32)],
    )
    return pl.pallas_call(
        functools.partial(_sel_kernel, tq=tq, tk=tk, bsz=bsz),
        grid_spec=grid_spec,
        out_shape=jax.ShapeDtypeStruct((bsz, g, p, s, dh), F32),
        input_output_aliases={7: 0},
        compiler_params=_cparams(("parallel", "arbitrary")),
    )(it, jt, q_aug, k_aug, v_aug, bias_t, gl5, acc)


def _merge_kernel(x_ref, oa_ref, ob_ref, ga_ref, gb_ref, wa_ref, wb_ref, wo_ref, g_ref, b_ref, o_ref):
    ya = jnp.dot(oa_ref[...].astype(BF16), wa_ref[...], preferred_element_type=F32)
    yb = jnp.dot(ob_ref[...].astype(BF16), wb_ref[...], preferred_element_type=F32)
    y = jax.nn.sigmoid(ga_ref[...]) * ya + jax.nn.sigmoid(gb_ref[...]) * yb
    mix = jnp.dot(y.astype(BF16), wo_ref[...], preferred_element_type=F32)
    o_ref[...] = _layer_norm(DN_ALPHA * x_ref[...] + mix, g_ref[...], b_ref[...])


def _merge(x, o_a, o_b, proj, wa, wb, wo, g, b, tm):
    t, d = x.shape
    nga = COL_GA // d
    row = lambda i: (i, 0)
    const = lambda i: (0, 0)
    return pl.pallas_call(
        _merge_kernel,
        grid=(t // tm,),
        in_specs=[pl.BlockSpec((tm, d), row),
                  pl.BlockSpec((tm, HG_WIDTH), row),
                  pl.BlockSpec((tm, NSA_WIDTH), row),
                  pl.BlockSpec((tm, d), lambda i: (i, nga)),
                  pl.BlockSpec((tm, d), lambda i: (i, nga + 1)),
                  pl.BlockSpec((HG_WIDTH, d), const),
                  pl.BlockSpec((NSA_WIDTH, d), const),
                  pl.BlockSpec((d, d), const),
                  pl.BlockSpec((1, d), const),
                  pl.BlockSpec((1, d), const)],
        out_specs=pl.BlockSpec((tm, d), row),
        out_shape=jax.ShapeDtypeStruct((t, d), F32),
        compiler_params=_cparams(("parallel",)),
    )(x, o_a, o_b, proj, proj, wa, wb, wo, g, b)


def _swiglu_step(xb, wg_ref, wu_ref, wd_ref):
    hg = jnp.dot(xb, wg_ref[...], preferred_element_type=F32)
    hu = jnp.dot(xb, wu_ref[...], preferred_element_type=F32)
    h = (hg * jax.nn.sigmoid(hg)) * hu
    return jnp.dot(h.astype(BF16), wd_ref[...], preferred_element_type=F32)


def _ffn_kernel(x_ref, wg_ref, wu_ref, wd_ref, g_ref, b_ref, o_ref, xb_ref, acc_ref):
    j = pl.program_id(1)

    @pl.when(j == 0)
    def _():
        xb_ref[...] = x_ref[...].astype(BF16)
        acc_ref[...] = jnp.zeros_like(acc_ref)

    acc_ref[...] += _swiglu_step(xb_ref[...], wg_ref, wu_ref, wd_ref)

    @pl.when(j == pl.num_programs(1) - 1)
    def _():
        o_ref[...] = _layer_norm(DN_ALPHA * x_ref[...] + acc_ref[...], g_ref[...], b_ref[...])


def _ffn(x, wg, wu, wd, g, b, tm, tf):
    t, d = x.shape
    f = wg.shape[1]
    return pl.pallas_call(
        _ffn_kernel,
        grid=(t // tm, f // tf),
        in_specs=[pl.BlockSpec((tm, d), lambda i, j: (i, 0)),
                  pl.BlockSpec((d, tf), lambda i, j: (0, j)),
                  pl.BlockSpec((d, tf), lambda i, j: (0, j)),
                  pl.BlockSpec((tf, d), lambda i, j: (j, 0)),
                  pl.BlockSpec((1, d), lambda i, j: (0, 0)),
                  pl.BlockSpec((1, d), lambda i, j: (0, 0))],
        out_specs=pl.BlockSpec((tm, d), lambda i, j: (i, 0)),
        out_shape=jax.ShapeDtypeStruct((t, d), F32),
        scratch_shapes=[pltpu.VMEM((tm, d), BF16), pltpu.VMEM((tm, d), F32)],
        compiler_params=_cparams(("parallel", "arbitrary")),
    )(x, wg, wu, wd, g, b)


def _router_kernel(x_ref, w_ref, o_ref):
    logits = jnp.dot(x_ref[...], w_ref[...], preferred_element_type=F32, precision=lax.Precision.HIGHEST)
    lane = lax.broadcasted_iota(jnp.int32, logits.shape, 1).astype(F32)
    logits = jnp.where(lane < N_EXPERTS, logits, -jnp.inf)
    v1 = jnp.max(logits, -1, keepdims=True)
    e1 = jnp.min(jnp.where(logits == v1, lane, float(LANES)), -1, keepdims=True)
    rest = jnp.where(lane == e1, -jnp.inf, logits)
    v2 = jnp.max(rest, -1, keepdims=True)
    e2 = jnp.min(jnp.where(rest == v2, lane, float(LANES)), -1, keepdims=True)
    x2 = jnp.exp(v2 - v1)
    den = 1.0 + x2
    o_ref[...] = jnp.where(lane == 0, e1, jnp.where(lane == 1, e2, jnp.where(
        lane == 2, 1.0 / den, jnp.where(lane == 3, x2 / den, 0.0))))


def _router(x, w, tm):
    t, d = x.shape
    return pl.pallas_call(
        _router_kernel,
        grid=(t // tm,),
        in_specs=[pl.BlockSpec((tm, d), lambda i: (i, 0)), pl.BlockSpec((d, LANES), lambda i: (0, 0))],
        out_specs=pl.BlockSpec((tm, LANES), lambda i: (i, 0)),
        out_shape=jax.ShapeDtypeStruct((t, LANES), F32),
        compiler_params=_cparams(("parallel",)),
    )(x, w)


def _expert_kernel(be_ref, x_ref, wg_ref, wu_ref, wd_ref, o_ref, acc_ref):
    j = pl.program_id(1)

    @pl.when(j == 0)
    def _():
        acc_ref[...] = jnp.zeros_like(acc_ref)

    acc_ref[...] += _swiglu_step(x_ref[...], wg_ref, wu_ref, wd_ref)

    @pl.when(j == pl.num_programs(1) - 1)
    def _():
        o_ref[...] = acc_ref[...]


def _experts(blk_e, xs, wg, wu, wd, tf):
    rows, d = xs.shape
    f = wg.shape[2]
    tm = MOE_ROW_BLOCK
    grid_spec = pltpu.PrefetchScalarGridSpec(
        num_scalar_prefetch=1,
        grid=(rows // tm, f // tf),
        in_specs=[pl.BlockSpec((tm, d), lambda i, j, be: (i, 0)),
                  pl.BlockSpec((None, d, tf), lambda i, j, be: (be[i], 0, j)),
                  pl.BlockSpec((None, d, tf), lambda i, j, be: (be[i], 0, j)),
                  pl.BlockSpec((None, tf, d), lambda i, j, be: (be[i], j, 0))],
        out_specs=pl.BlockSpec((tm, d), lambda i, j, be: (i, 0)),
        scratch_shapes=[pltpu.VMEM((tm, d), F32)],
    )
    return pl.pallas_call(
        _expert_kernel,
        grid_spec=grid_spec,
        out_shape=jax.ShapeDtypeStruct((rows, d), F32),
        compiler_params=_cparams(("parallel", "arbitrary")),
    )(blk_e, xs, wg, wu, wd)


def _combine_kernel(x_ref, y1_ref, y2_ref, gt_ref, g_ref, b_ref, o_ref):
    f = y1_ref[...] * gt_ref[:, 2:3] + y2_ref[...] * gt_ref[:, 3:4]
    o_ref[...] = _layer_norm(DN_ALPHA * x_ref[...] + f, g_ref[...], b_ref[...])


def _combine(x, y1, y2, route, g, b, tm):
    t, d = x.shape
    row = lambda i: (i, 0)
    const = lambda i: (0, 0)
    return pl.pallas_call(
        _combine_kernel,
        grid=(t // tm,),
        in_specs=[pl.BlockSpec((tm, d), row), pl.BlockSpec((tm, d), row), pl.BlockSpec((tm, d), row),
                  pl.BlockSpec((tm, LANES), row), pl.BlockSpec((1, d), const), pl.BlockSpec((1, d), const)],
        out_specs=pl.BlockSpec((tm, d), row),
        out_shape=jax.ShapeDtypeStruct((t, d), F32),
        compiler_params=_cparams(("parallel",)),
    )(x, y1, y2, route, g, b)


def _moe(x, w_router, wg, wu, wd, g, b):
    t, d = x.shape
    tk_ = t * TOP_K
    route = _router(x, jnp.pad(w_router, ((0, 0), (0, LANES - N_EXPERTS))), 512)
    flat_e = route[:, :TOP_K].astype(jnp.int32).reshape(-1)
    onehot = (flat_e[:, None] == jnp.arange(N_EXPERTS)[None, :]).astype(jnp.int32)
    csum = jnp.cumsum(onehot, axis=0)
    counts = csum[-1]
    rank = jnp.sum(onehot * csum, axis=1) - 1
    padded = (counts + MOE_ROW_BLOCK - 1) // MOE_ROW_BLOCK * MOE_ROW_BLOCK
    pend = jnp.cumsum(padded)
    dest = (pend - padded)[flat_e] + rank
    n_blocks = -(-(tk_ + N_EXPERTS * (MOE_ROW_BLOCK - 1)) // MOE_ROW_BLOCK)
    n_rows = n_blocks * MOE_ROW_BLOCK
    row_tok = jnp.zeros((n_rows,), jnp.int32).at[dest].set(jnp.arange(tk_, dtype=jnp.int32) // TOP_K)
    blk_e = jnp.minimum(jnp.searchsorted(pend, jnp.arange(n_blocks) * MOE_ROW_BLOCK, side='right'),
                        N_EXPERTS - 1).astype(jnp.int32)
    xs = x.astype(BF16)[row_tok]
    ys = _experts(blk_e, xs, wg, wu, wd, 512)
    yk = ys[dest].reshape(t, TOP_K, d)
    return _combine(x, yk[:, 0], yk[:, 1], route, g, b, 512)


def _pack_w_in(w_in):
    offs = np.concatenate([[0], np.cumsum(IN_SIZES)])
    seg = [w_in[:, offs[j]:offs[j + 1]] for j in range(len(IN_SIZES))]
    seg[4] = seg[4] * (NSA_DH ** -0.5 * LOG2E)
    seg[11] = jnp.pad(seg[11], ((0, 0), (0, LANES - 3 * NSA_HEADS)))
    seg = seg[12:14] + seg[0:12]
    packed = jnp.concatenate(seg + [jnp.zeros((w_in.shape[0], PROJ_WP - PROJ_W), w_in.dtype)], axis=1)
    return packed.astype(BF16)


def _cmp_to_sel(n_cmp_pad, n_cmp):
    cs = np.arange(n_cmp_pad)[:, None] * CMP_STRIDE
    ss = np.arange(LANES)[None, :] * SEL_BLOCK
    overlap = np.clip(np.minimum(cs + CMP_LEN, ss + SEL_BLOCK) - np.maximum(cs, ss), 0, None) / CMP_LEN
    overlap[n_cmp:] = 0.0
    return jnp.asarray(overlap, F32)


def _token_mixer(x, w_in_p, lb, hg_norm_w, cmp_pos, cmp_w1, cmp_w2, bias_c, bias_t, wa, wb, wo, ln_g, ln_b):
    bsz, s, d = x.shape
    g, p, dh = NSA_GROUPS, NSA_HPG, NSA_DH
    xf = x.reshape(bsz * s, d)
    proj = _project(xf, w_in_p, 1024, PROJ_TN).reshape(bsz, s, PROJ_WP)
    o_a = _hgrn(proj, lb, hg_norm_w, 512)

    def heads(c0, width):
        return proj[:, :, c0:c0 + width]

    q5 = heads(COL_NQ, NSA_WIDTH).astype(BF16).reshape(bsz, s, g, p, dh).transpose(0, 2, 3, 1, 4)
    kv = heads(COL_KV, 6 * KV_WIDTH).reshape(bsz, s, 6, g, dh)
    n16 = s // CMP_STRIDE
    a = kv[:, :, 0:2].reshape(bsz, n16, CMP_STRIDE, 2, g, dh).transpose(3, 0, 4, 1, 2, 5)
    a = a.reshape(2, bsz, g, n16, CMP_STRIDE * dh)
    kvc = _compress(a, cmp_pos.reshape(2, 1, CMP_LEN * dh), cmp_w1.astype(BF16), cmp_w2.astype(BF16))
    kvh = kv[:, :, 2:6].astype(BF16).transpose(2, 0, 3, 1, 4)
    gl5 = heads(COL_NG, 3 * NSA_HEADS).reshape(bsz, s, g, p, 3).transpose(0, 2, 3, 1, 4)
    n_cmp = (s - CMP_LEN) // CMP_STRIDE + 1
    acc, pen = _cmp_branch(q5, kvc[0], kvc[1], bias_c, _cmp_to_sel(n16, n_cmp), gl5, ATT_TQ)
    ones = jnp.ones((bsz, g, s, LANES - dh), BF16)
    acc = _win_branch(q5, kvh[2], jnp.concatenate([kvh[3], ones], axis=-1), bias_t, gl5, acc, ATT_TQ, ATT_TK)
    q_aug = jnp.concatenate([jnp.broadcast_to(pen[:, :, None], (bsz, g, p, s, LANES)), q5,
                             jnp.zeros((bsz, g, p, s, LANES - dh), BF16)], axis=-1)
    block_of_key = (np.arange(s)[:, None] // SEL_BLOCK == np.arange(LANES)[None, :])
    k_aug = jnp.concatenate([jnp.broadcast_to(jnp.asarray(block_of_key, BF16), (bsz, g, s, LANES)), kvh[0],
                             jnp.zeros((bsz, g, s, LANES - dh), BF16)], axis=-1)
    v_aug = jnp.concatenate([kvh[1], ones], axis=-1)
    o_b = _sel_branch(q_aug, k_aug, v_aug, bias_t, gl5, acc, ATT_TQ, ATT_TK)
    o_b = o_b.transpose(0, 3, 1, 2, 4).reshape(bsz * s, NSA_WIDTH)
    return _merge(xf, o_a.reshape(bsz * s, HG_WIDTH), o_b, proj.reshape(bsz * s, PROJ_WP),
                  wa, wb, wo, ln_g, ln_b, 512)


def kernel(x, w_in, hg_lb_logits, hg_norm_w, cmp_pos, cmp_w1, cmp_w2, rel_bias, w_branch_a, w_branch_b,
           w_out, ln1_g, ln1_b, ln2_g, ln2_b, ffn_w_gate, ffn_w_up, ffn_w_down, moe_router, moe_w_gate,
           moe_w_up, moe_w_down):
    bsz, s, d = x.shape
    depth = w_in.shape[0]
    p_lb = jax.nn.softmax(hg_lb_logits.astype(F32), axis=0)
    lbs = jnp.cumsum(p_lb, axis=0) - p_lb[0]
    n16 = s // CMP_STRIDE
    bias_c = _bias_tiles(rel_bias, s // ATT_TQ, ATT_TQ, n16,
                         lambda t, r, c: t * ATT_TQ + r - (c * CMP_STRIDE + CMP_LEN - 1))
    bias_t = _bias_tiles(rel_bias, BIAS_ND + 1, ATT_TQ, ATT_TK, lambda t, r, c: t * ATT_TQ + r - c)
    f_pad = -(-D_FF // LANES) * LANES - D_FF
    xf = x.reshape(bsz * s, d)
    for l in range(depth):
        xf = _token_mixer(xf.reshape(bsz, s, d), _pack_w_in(w_in[l]), lbs[l][None], hg_norm_w[l][None],
                          cmp_pos[l], cmp_w1[l], cmp_w2[l], bias_c, bias_t,
                          w_branch_a[l].astype(BF16), w_branch_b[l].astype(BF16), w_out[l].astype(BF16),
                          ln1_g[l][None], ln1_b[l][None])
        if l % 2 == 0:
            wg = jnp.pad(ffn_w_gate[l // 2], ((0, 0), (0, f_pad))).astype(BF16)
            wu = jnp.pad(ffn_w_up[l // 2], ((0, 0), (0, f_pad))).astype(BF16)
            wd = jnp.pad(ffn_w_down[l // 2], ((0, f_pad), (0, 0))).astype(BF16)
            xf = _ffn(xf, wg, wu, wd, ln2_g[l][None], ln2_b[l][None], 512, wg.shape[1] // 2)
        else:
            xf = _moe(xf, moe_router[l // 2], moe_w_gate[l // 2].astype(BF16), moe_w_up[l // 2].astype(BF16),
                      moe_w_down[l // 2].astype(BF16), ln2_g[l][None], ln2_b[l][None])
    return xf.reshape(bsz, s, d)
```

```python
import functools
import math

import jax
import jax.numpy as jnp
import numpy as np
from jax import lax
from jax.experimental import pallas as pl
from jax.experimental.pallas import tpu as pltpu

F32 = jnp.float32
BF16 = jnp.bfloat16

D_MODEL = 1024
DEPTH = 2
HG_HEADS = 4
HG_DK = 128
HG_WIDTH = HG_HEADS * HG_DK
HG_CHUNK = 64
HG_SUB = 16
HG_HALF = HG_SUB // 2
NSA_HEADS = 8
NSA_GROUPS = 2
NSA_HPG = NSA_HEADS // NSA_GROUPS
NSA_DH = 64
NSA_WIDTH = NSA_HEADS * NSA_DH
KV_WIDTH = NSA_GROUPS * NSA_DH
CMP_LEN = 32
CMP_STRIDE = 16
CMP_HIDDEN = 2 * NSA_DH
SEL_BLOCK = 64
N_SEL = 16
WINDOW = 512
FORCE_SCORE = 1e9
NEG_BIG = -1e30
REL_BUCKETS = 32
REL_MAX_DIST = 2048
D_FF = 2752
N_EXPERTS = 8
TOP_K = 2
D_FF_EXPERT = 3584
MOE_ROW_BLOCK = 512
DN_ALPHA = (2 * DEPTH) ** 0.25
LN_EPS = 1e-5
IN_SIZES = (HG_WIDTH, HG_WIDTH, HG_WIDTH, HG_WIDTH, NSA_WIDTH,
            KV_WIDTH, KV_WIDTH, KV_WIDTH, KV_WIDTH, KV_WIDTH, KV_WIDTH,
            3 * NSA_HEADS, D_MODEL, D_MODEL)

LANES = 128
LOG2E = 1.0 / math.log(2.0)
COL_GA = 0
COL_HG = 2 * D_MODEL
COL_NQ = COL_HG + 4 * HG_WIDTH
COL_KV = COL_NQ + NSA_WIDTH
COL_NG = COL_KV + 6 * KV_WIDTH
PROJ_W = COL_NG + LANES
PROJ_TN = 512
PROJ_WP = -(-PROJ_W // PROJ_TN) * PROJ_TN

ATT_TQ = 256
ATT_TK = 512
BIAS_ND = -(-(REL_MAX_DIST + ATT_TK) // ATT_TQ)
VMEM_LIMIT = 48 * 1024 * 1024


def _cparams(sem):
    return pltpu.CompilerParams(dimension_semantics=sem, vmem_limit_bytes=VMEM_LIMIT)


def _proj_kernel(x_ref, w_ref, o_ref, xb_ref):
    @pl.when(pl.program_id(1) == 0)
    def _():
        xb_ref[...] = x_ref[...].astype(BF16)

    o_ref[...] = jnp.dot(xb_ref[...], w_ref[...], preferred_element_type=F32).astype(o_ref.dtype)


def _project(x, w, tm, tn):
    m, k = x.shape
    n = w.shape[1]
    return pl.pallas_call(
        _proj_kernel,
        grid=(m // tm, n // tn),
        in_specs=[pl.BlockSpec((tm, k), lambda i, j: (i, 0)),
                  pl.BlockSpec((k, tn), lambda i, j: (0, j))],
        out_specs=pl.BlockSpec((tm, tn), lambda i, j: (i, j)),
        out_shape=jax.ShapeDtypeStruct((m, n), BF16),
        scratch_shapes=[pltpu.VMEM((tm, k), BF16)],
        compiler_params=_cparams(("parallel", "arbitrary")),
    )(x, w)


def _layer_norm(y, g, b):
    mu = jnp.mean(y, -1, keepdims=True)
    yc = y - mu
    var = jnp.mean(yc * yc, -1, keepdims=True)
    return yc * lax.rsqrt(var + LN_EPS) * g + b


def _cumsum_rows(tril, x):
    total = None
    for _ in range(3):
        part = x.astype(BF16)
        term = jnp.dot(tril, part, preferred_element_type=F32)
        total = term if total is None else total + term
        x = x - part.astype(F32)
    return total


def _hgrn_kernel(q_ref, f_ref, i_ref, g_ref, lb_ref, nw_ref, o_ref, st_ref, *, n_chunks):
    @pl.when(pl.program_id(1) == 0)
    def _():
        st_ref[...] = jnp.zeros_like(st_ref)

    c = HG_CHUNK
    nt = (((1,), (1,)), ((), ()))
    row = lax.broadcasted_iota(jnp.int32, (c, c), 0)
    col = lax.broadcasted_iota(jnp.int32, (c, c), 1)
    tril = (row >= col).astype(BF16)
    row1 = lax.broadcasted_iota(jnp.int32, (c, 1), 0)
    half_pos = row1 % HG_HALF
    second_half = row1 % HG_SUB >= HG_HALF
    same_sub = row // HG_SUB == col // HG_SUB

    w = HG_HEADS * HG_DK
    head_lanes = [slice(h * HG_DK, (h + 1) * HG_DK) for h in range(HG_HEADS)]

    def per_head(fn):
        return jnp.concatenate([fn(h, head_lanes[h]) for h in range(HG_HEADS)], axis=1)

    def shift_in_half(x, lag):
        return pltpu.roll(x.reshape(c // HG_HALF, HG_HALF, w), lag, 1).reshape(c, w)

    def chunk(ci, carry):
        r0 = pl.multiple_of(ci * c, c)
        q = q_ref[pl.ds(r0, c), :].astype(F32)
        z = f_ref[pl.ds(r0, c), :].astype(F32)
        v = i_ref[pl.ds(r0, c), :].astype(F32)
        g = g_ref[pl.ds(r0, c), :].astype(F32)
        k = (1.0 - lb_ref[...]) * jax.nn.sigmoid(-z)
        b = _cumsum_rows(tril, jnp.log1p(-k))
        b_last = b[c - 1:c, :]
        vb = v.astype(BF16)
        qe = (q * jnp.exp(b)).astype(BF16)
        o = per_head(lambda h, hl: lax.dot_general(qe[:, hl], st_ref[h].astype(BF16), nt,
                                                  preferred_element_type=F32))
        att_rows = [[jnp.zeros((HG_SUB, c), F32)] * HG_HEADS]
        for sb in range(1, c // HG_SUB):
            lo = sb * HG_SUB
            ref_b = b[lo - 1:lo, :]
            qt = (q[lo:lo + HG_SUB, :] * jnp.exp(b[lo:lo + HG_SUB, :] - ref_b)).astype(BF16)
            kt = (k * jnp.exp(jnp.where(row1 < lo, ref_b - b, -jnp.inf))).astype(BF16)
            att_rows.append([lax.dot_general(qt[:, hl], kt[:, hl], nt, preferred_element_type=F32)
                             for hl in head_lanes])
        mid = jnp.concatenate(
            [jnp.broadcast_to(b[lo + HG_HALF - 1:lo + HG_HALF, :], (HG_SUB, w)) for lo in range(0, c, HG_SUB)],
            axis=0)
        q2 = (q * jnp.exp(jnp.where(second_half, b - mid, -jnp.inf))).astype(BF16)
        k2 = (k * jnp.exp(jnp.where(second_half, -jnp.inf, mid - b))).astype(BF16)
        att_half = [lax.dot_general(q2[:, hl], k2[:, hl], nt, preferred_element_type=F32) for hl in head_lanes]

        def intra(h, hl):
            att = jnp.concatenate([rows[h] for rows in att_rows], axis=0) + jnp.where(same_sub, att_half[h], 0.0)
            return jnp.dot(att.astype(BF16), vb[:, hl], preferred_element_type=F32)

        o = o + per_head(intra)
        for lag in range(HG_HALF):
            if lag == 0:
                ks, bs, vs = k, b, v
            else:
                ks, bs, vs = shift_in_half(k, lag), shift_in_half(b, lag), shift_in_half(v, lag)
            valid = half_pos >= lag
            prod = q * ks * jnp.exp(jnp.where(valid, b - bs, 0.0))
            a = per_head(lambda h, hl: jnp.broadcast_to(
                jnp.sum(prod[:, hl], axis=1, keepdims=True), (c, HG_DK)))
            o = o + jnp.where(valid, a, 0.0) * vs
        khat = (k * jnp.exp(b_last - b)).astype(BF16)
        decay = jnp.exp(b_last)
        for h, hl in enumerate(head_lanes):
            st_ref[h] = st_ref[h] * decay[:, hl] + lax.dot_general(
                vb[:, hl], khat[:, hl], (((0,), (0,)), ((), ())), preferred_element_type=F32)
        sq = o * o
        ms = per_head(lambda h, hl: jnp.broadcast_to(jnp.mean(sq[:, hl], -1, keepdims=True), (c, HG_DK)))
        o = o * lax.rsqrt(ms + 1e-6)
        o_ref[pl.ds(r0, c), :] = (o * nw_ref[...] * (g * jax.nn.sigmoid(g))).astype(o_ref.dtype)
        return carry

    lax.fori_loop(0, n_chunks, chunk, 0)


def _hgrn(proj, lb, norm_w, ts):
    bsz, s, _ = proj.shape
    col0 = COL_HG // HG_WIDTH

    def col_spec(n):
        return pl.BlockSpec((None, ts, HG_WIDTH), lambda b, t: (b, t, col0 + n))

    head_spec = pl.BlockSpec((1, HG_WIDTH), lambda b, t: (0, 0))
    return pl.pallas_call(
        functools.partial(_hgrn_kernel, n_chunks=ts // HG_CHUNK),
        grid=(bsz, s // ts),
        in_specs=[col_spec(0), col_spec(1), col_spec(2), col_spec(3), head_spec, head_spec],
        out_specs=pl.BlockSpec((None, ts, HG_WIDTH), lambda b, t: (b, t, 0)),
        out_shape=jax.ShapeDtypeStruct((bsz, s, HG_WIDTH), BF16),
        scratch_shapes=[pltpu.VMEM((HG_HEADS, HG_DK, HG_DK), F32)],
        compiler_params=_cparams(("parallel", "arbitrary")),
    )(proj, proj, proj, proj, lb, norm_w)


def _compress_kernel(a_ref, pos_ref, w1_ref, w2_ref, o_ref):
    half = CMP_STRIDE * NSA_DH
    a = a_ref[...].astype(F32)
    n = a.shape[0]
    a1 = (a + pos_ref[:, :half]).astype(BF16)
    a2 = (a + pos_ref[:, half:]).astype(BF16)
    y1 = jnp.dot(a1, w1_ref[:half, :], preferred_element_type=F32)
    y2 = jnp.dot(a2, w1_ref[half:, :], preferred_element_type=F32)
    hid = jax.nn.gelu(y1 + pltpu.roll(y2, n - 1, 0))
    o_ref[...] = jnp.dot(hid.astype(BF16), w2_ref[...], preferred_element_type=F32).astype(o_ref.dtype)


def _compress(a, pos, w1, w2):
    _, bsz, g, n, width = a.shape
    return pl.pallas_call(
        _compress_kernel,
        grid=(2, bsz, g),
        in_specs=[pl.BlockSpec((None, None, None, n, width), lambda c, b, h: (c, b, h, 0, 0)),
                  pl.BlockSpec((None, 1, 2 * width), lambda c, b, h: (c, 0, 0)),
                  pl.BlockSpec((None, 2 * width, CMP_HIDDEN), lambda c, b, h: (c, 0, 0)),
                  pl.BlockSpec((None, CMP_HIDDEN, NSA_DH), lambda c, b, h: (c, 0, 0))],
        out_specs=pl.BlockSpec((None, None, None, n, NSA_DH), lambda c, b, h: (c, b, h, 0, 0)),
        out_shape=jax.ShapeDtypeStruct((2, bsz, g, n, NSA_DH), BF16),
        compiler_params=_cparams(("parallel", "parallel", "parallel")),
    )(a, pos, w1, w2)


def _bucket_of(dist):
    n = jnp.maximum(dist, 0)
    exact = REL_BUCKETS // 2
    large = exact + (jnp.log(jnp.maximum(n, exact).astype(F32) / exact)
                     / math.log(REL_MAX_DIST / exact) * (REL_BUCKETS - exact)).astype(jnp.int32)
    return jnp.where(n < exact, n, jnp.minimum(large, REL_BUCKETS - 1))


def _bias_kernel(tab_ref, o_ref, *, dist_fn):
    rows, cols = o_ref.shape[2], o_ref.shape[3]
    row = lax.broadcasted_iota(jnp.int32, (rows, cols), 0)
    col = lax.broadcasted_iota(jnp.int32, (rows, cols), 1)
    bucket = _bucket_of(dist_fn(pl.program_id(0), row, col))
    for h in range(NSA_HEADS):
        tab = jnp.broadcast_to(tab_ref[h:h + 1, :] * LOG2E, (rows, LANES))
        for c0 in range(0, cols, LANES):
            o_ref[h, 0, :, c0:c0 + LANES] = jnp.take_along_axis(tab, bucket[:, c0:c0 + LANES], axis=1)


def _bias_tiles(rel_bias, n_tiles, rows, cols, dist_fn):
    tab = jnp.pad(rel_bias.T, ((0, 0), (0, LANES - REL_BUCKETS)))
    return pl.pallas_call(
        functools.partial(_bias_kernel, dist_fn=dist_fn),
        grid=(n_tiles,),
        in_specs=[pl.BlockSpec((NSA_HEADS, LANES), lambda t: (0, 0))],
        out_specs=pl.BlockSpec((NSA_HEADS, 1, rows, cols), lambda t: (0, t, 0, 0)),
        out_shape=jax.ShapeDtypeStruct((NSA_HEADS, n_tiles, rows, cols), F32),
        compiler_params=_cparams(("parallel",)),
    )(tab)


def _cmp_kernel(q_ref, k_ref, v_ref, bias_ref, c2s_ref, gl_ref, o_ref, pen_ref, *, tq):
    p = NSA_HPG
    t0 = pl.program_id(1) * tq
    bsz, ncp = k_ref.shape[0], k_ref.shape[1]
    bias = bias_ref[:, 0]
    tpos = t0 + lax.broadcasted_iota(jnp.int32, (tq, ncp), 0)
    cend = lax.broadcasted_iota(jnp.int32, (tq, ncp), 1) * CMP_STRIDE + (CMP_LEN - 1)
    mask = (tpos >= cend)[None]
    lane = lax.broadcasted_iota(jnp.int32, (tq, LANES), 1)
    cur = (t0 + lax.broadcasted_iota(jnp.int32, (tq, LANES), 0)) // SEL_BLOCK
    forced = (lane == 0) | (lane == cur) | (lane == cur - 1)
    visible = lane <= cur
    for b in range(bsz):
        q = q_ref[b].reshape(p * tq, NSA_DH)
        logits = lax.dot_general(q, k_ref[b], (((1,), (1,)), ((), ())), preferred_element_type=F32)
        logits = jnp.where(mask, logits.reshape(p, tq, ncp) + bias, NEG_BIG)
        m = jnp.max(logits, -1, keepdims=True)
        e = jnp.where(mask, jnp.exp2(logits - m), 0.0)
        pc = e / jnp.maximum(jnp.sum(e, -1, keepdims=True), 1e-30)
        o = jnp.dot(pc.reshape(p * tq, ncp).astype(BF16), v_ref[b], preferred_element_type=F32)
        o_ref[b] = o.reshape(p, tq, NSA_DH) * jax.nn.sigmoid(gl_ref[b, :, :, 0:1].astype(F32))
        imp = jnp.dot(jnp.sum(pc, axis=0), c2s_ref[...], preferred_element_type=F32,
                      precision=lax.Precision.HIGHEST)
        score = jnp.where(forced, FORCE_SCORE, jnp.where(visible, imp, NEG_BIG))
        member = jnp.zeros((tq, LANES), F32)
        for _ in range(N_SEL):
            hit = lane == jnp.argmax(score, axis=-1, keepdims=True)
            member = jnp.where(hit, 1.0, member)
            score = jnp.where(hit, -jnp.inf, score)
        pen_ref[b] = jnp.where((member > 0.5) & visible, 0.0, NEG_BIG).astype(pen_ref.dtype)


def _cmp_branch(q5, kc, vc, bias_c, c2s, gl5, tq):
    bsz, g, p, s, dh = q5.shape
    ncp = kc.shape[2]
    qmap = lambda h, i: (0, h, 0, i, 0)
    return pl.pallas_call(
        functools.partial(_cmp_kernel, tq=tq),
        grid=(g, s // tq),
        in_specs=[pl.BlockSpec((bsz, None, p, tq, dh), qmap),
                  pl.BlockSpec((bsz, None, ncp, dh), lambda h, i: (0, h, 0, 0)),
                  pl.BlockSpec((bsz, None, ncp, dh), lambda h, i: (0, h, 0, 0)),
                  pl.BlockSpec((p, 1, tq, ncp), lambda h, i: (h, i, 0, 0)),
                  pl.BlockSpec((ncp, LANES), lambda h, i: (0, 0)),
                  pl.BlockSpec((bsz, None, p, tq, 3), qmap)],
        out_specs=[pl.BlockSpec((bsz, None, p, tq, dh), qmap),
                   pl.BlockSpec((bsz, None, tq, LANES), lambda h, i: (0, h, i, 0))],
        out_shape=[jax.ShapeDtypeStruct((bsz, g, p, s, dh), F32),
                   jax.ShapeDtypeStruct((bsz, g, s, LANES), BF16)],
        compiler_params=_cparams(("parallel", "parallel")),
    )(q5, kc, vc, bias_c, c2s, gl5)


def _win_kernel(q_ref, kp_ref, kc_ref, vp_ref, vc_ref, bp_ref, bc_ref, gl_ref, acc_ref, o_ref, *, tq, tk):
    p = NSA_HPG
    bsz = q_ref.shape[0]
    t0 = pl.program_id(1) * tq
    s_cur = (t0 // tk) * tk
    tpos = t0 + lax.broadcasted_iota(jnp.int32, (tq, tk), 0)
    col = lax.broadcasted_iota(jnp.int32, (tq, tk), 1)

    def tile_mask(s0):
        kpos = s0 + col
        dist = tpos - kpos
        return ((dist >= 0) & (dist < WINDOW) & (kpos >= 0))[None]

    mask_prev, mask_cur = tile_mask(s_cur - tk), tile_mask(s_cur)
    bias_prev, bias_cur = bp_ref[:, 0], bc_ref[:, 0]
    nt = (((1,), (1,)), ((), ()))
    for b in range(bsz):
        q = q_ref[b].reshape(p * tq, NSA_DH)
        l_prev = lax.dot_general(q, kp_ref[b], nt, preferred_element_type=F32).reshape(p, tq, tk)
        l_cur = lax.dot_general(q, kc_ref[b], nt, preferred_element_type=F32).reshape(p, tq, tk)
        l_prev = jnp.where(mask_prev, l_prev + bias_prev, NEG_BIG)
        l_cur = jnp.where(mask_cur, l_cur + bias_cur, NEG_BIG)
        m = jnp.maximum(jnp.max(l_prev, -1, keepdims=True), jnp.max(l_cur, -1, keepdims=True))
        e_prev = jnp.exp2(l_prev - m).reshape(p * tq, tk).astype(BF16)
        e_cur = jnp.exp2(l_cur - m).reshape(p * tq, tk).astype(BF16)
        o = (jnp.dot(e_prev, vp_ref[b], preferred_element_type=F32)
             + jnp.dot(e_cur, vc_ref[b], preferred_element_type=F32))
        o = (o / pltpu.roll(o, NSA_DH, 1))[:, :NSA_DH].reshape(p, tq, NSA_DH)
        o_ref[b] = acc_ref[b] + o * jax.nn.sigmoid(gl_ref[b, :, :, 2:3].astype(F32))


def _win_branch(q5, kw, vw_aug, bias_t, gl5, acc, tq, tk):
    bsz, g, p, s, dh = q5.shape
    r = tk // tq
    cur = lambda h, i: (0, h, i // r, 0)
    prev = lambda h, i: (0, h, jnp.maximum(i // r - 1, 0), 0)
    qmap = lambda h, i: (0, h, 0, i, 0)
    return pl.pallas_call(
        functools.partial(_win_kernel, tq=tq, tk=tk),
        grid=(g, s // tq),
        in_specs=[pl.BlockSpec((bsz, None, p, tq, dh), qmap),
                  pl.BlockSpec((bsz, None, tk, dh), prev),
                  pl.BlockSpec((bsz, None, tk, dh), cur),
                  pl.BlockSpec((bsz, None, tk, LANES), prev),
                  pl.BlockSpec((bsz, None, tk, LANES), cur),
                  pl.BlockSpec((p, 1, tq, tk), lambda h, i: (h, i % r + r, 0, 0)),
                  pl.BlockSpec((p, 1, tq, tk), lambda h, i: (h, i % r, 0, 0)),
                  pl.BlockSpec((bsz, None, p, tq, 3), qmap),
                  pl.BlockSpec((bsz, None, p, tq, dh), qmap)],
        out_specs=pl.BlockSpec((bsz, None, p, tq, dh), qmap),
        out_shape=jax.ShapeDtypeStruct((bsz, g, p, s, dh), F32),
        input_output_aliases={8: 0},
        compiler_params=_cparams(("parallel", "parallel")),
    )(q5, kw, kw, vw_aug, vw_aug, bias_t, bias_t, gl5, acc)


def _sel_kernel(it_ref, jt_ref, q_ref, k_ref, v_ref, bias_ref, gl_ref, acc_ref, o_ref, m_sc, a_sc,
                *, tq, tk, bsz):
    p = NSA_HPG
    step = pl.program_id(1)
    i = it_ref[step]
    j = jt_ref[step]
    t0 = i * tq
    s0 = j * tk
    last = s0 + tk > t0

    @pl.when(j == 0)
    def _():
        m_sc[...] = jnp.full_like(m_sc, NEG_BIG)
        a_sc[...] = jnp.zeros_like(a_sc)

    def accumulate(on_diagonal):
        bias = bias_ref[:, 0]
        if on_diagonal:
            causal = (t0 + lax.broadcasted_iota(jnp.int32, (tq, tk), 0)
                      >= s0 + lax.broadcasted_iota(jnp.int32, (tq, tk), 1))[None]
        for b in range(bsz):
            q = q_ref[b].reshape(p * tq, 2 * LANES)
            lg = lax.dot_general(q, k_ref[b], (((1,), (1,)), ((), ())), preferred_element_type=F32)
            lg = lg.reshape(p, tq, tk) + bias
            if on_diagonal:
                lg = jnp.where(causal, lg, NEG_BIG)
            lg = lg.reshape(p * tq, tk)
            m_old = m_sc[b]
            m_new = jnp.maximum(m_old, jnp.max(lg, -1, keepdims=True))
            alpha = jnp.exp2(m_old - m_new)
            e = jnp.exp2(lg - jnp.concatenate([m_new] * (tk // LANES), axis=1))
            a_sc[b] = alpha * a_sc[b] + jnp.dot(e.astype(BF16), v_ref[b], preferred_element_type=F32)
            m_sc[b] = m_new

    @pl.when(jnp.logical_not(last))
    def _():
        accumulate(False)

    @pl.when(last)
    def _():
        accumulate(True)
        gate = jax.nn.sigmoid(gl_ref[:, :, :, 1:2].astype(F32))
        a = a_sc[...]
        o = (a / pltpu.roll(a, NSA_DH, 2))[:, :, :NSA_DH].reshape(bsz, p, tq, NSA_DH)
        o_ref[...] = (acc_ref[...] + o * gate).astype(o_ref.dtype)


def _sel_branch(q_aug, k_aug, v_aug, bias_t, gl5, acc, tq, tk):
    bsz, g, p, s, wq = q_aug.shape
    dh = acc.shape[-1]
    r = tk // tq
    pairs = [(i, j) for i in range(s // tq) for j in range(i // r + 1)]
    it = jnp.asarray([ij[0] for ij in pairs], jnp.int32)
    jt = jnp.asarray([ij[1] for ij in pairs], jnp.int32)
    nd = bias_t.shape[1] - 1
    qmap = lambda h, t, it, jt: (0, h, 0, it[t], 0)
    kmap = lambda h, t, it, jt: (0, h, jt[t], 0)
    grid_spec = pltpu.PrefetchScalarGridSpec(
        num_scalar_prefetch=2,
        grid=(g, len(pairs)),
        in_specs=[pl.BlockSpec((bsz, None, p, tq, wq), qmap),
                  pl.BlockSpec((bsz, None, tk, wq), kmap),
                  pl.BlockSpec((bsz, None, tk, LANES), kmap),
                  pl.BlockSpec((p, 1, tq, tk),
                               lambda h, t, it, jt: (h, jnp.minimum(it[t] - r * jt[t], nd), 0, 0)),
                  pl.BlockSpec((bsz, None, p, tq, 3), qmap),
                  pl.BlockSpec((bsz, None, p, tq, dh), qmap)],
        out_specs=pl.BlockSpec((bsz, None, p, tq, dh), qmap),
        scratch_shapes=[pltpu.VMEM((bsz, p * tq, LANES), F32),
                        pltpu.VMEM((bsz, p * tq, LANES), F32)],
    )
    return pl.pallas_call(
        functools.partial(_sel_kernel, tq=tq, tk=tk, bsz=bsz),
        grid_spec=grid_spec,
        out_shape=jax.ShapeDtypeStruct((bsz, g, p, s, dh), BF16),
        compiler_params=_cparams(("parallel", "arbitrary")),
    )(it, jt, q_aug, k_aug, v_aug, bias_t, gl5, acc)


def _merge_kernel(x_ref, oa_ref, ob_ref, ga_ref, gb_ref, wa_ref, wb_ref, wo_ref, g_ref, b_ref, o_ref):
    ya = jnp.dot(oa_ref[...].astype(BF16), wa_ref[...], preferred_element_type=F32)
    yb = jnp.dot(ob_ref[...].astype(BF16), wb_ref[...], preferred_element_type=F32)
    y = jax.nn.sigmoid(ga_ref[...].astype(F32)) * ya + jax.nn.sigmoid(gb_ref[...].astype(F32)) * yb
    mix = jnp.dot(y.astype(BF16), wo_ref[...], preferred_element_type=F32)
    o_ref[...] = _layer_norm(DN_ALPHA * x_ref[...] + mix, g_ref[...], b_ref[...])


def _merge(x, o_a, o_b, proj, wa, wb, wo, g, b, tm):
    t, d = x.shape
    nga = COL_GA // d
    row = lambda i: (i, 0)
    const = lambda i: (0, 0)
    return pl.pallas_call(
        _merge_kernel,
        grid=(t // tm,),
        in_specs=[pl.BlockSpec((tm, d), row),
                  pl.BlockSpec((tm, HG_WIDTH), row),
                  pl.BlockSpec((tm, NSA_WIDTH), row),
                  pl.BlockSpec((tm, d), lambda i: (i, nga)),
                  pl.BlockSpec((tm, d), lambda i: (i, nga + 1)),
                  pl.BlockSpec((HG_WIDTH, d), const),
                  pl.BlockSpec((NSA_WIDTH, d), const),
                  pl.BlockSpec((d, d), const),
                  pl.BlockSpec((1, d), const),
                  pl.BlockSpec((1, d), const)],
        out_specs=pl.BlockSpec((tm, d), row),
        out_shape=jax.ShapeDtypeStruct((t, d), F32),
        compiler_params=_cparams(("parallel",)),
    )(x, o_a, o_b, proj, proj, wa, wb, wo, g, b)


def _swiglu_step(xb, wg_ref, wu_ref, wd_ref):
    hg = jnp.dot(xb, wg_ref[...], preferred_element_type=F32)
    hu = jnp.dot(xb, wu_ref[...], preferred_element_type=F32)
    h = (hg * jax.nn.sigmoid(hg)) * hu
    return jnp.dot(h.astype(BF16), wd_ref[...], preferred_element_type=F32)


def _ffn_kernel(x_ref, wg_ref, wu_ref, wd_ref, g_ref, b_ref, o_ref, xb_ref, acc_ref):
    j = pl.program_id(1)

    @pl.when(j == 0)
    def _():
        xb_ref[...] = x_ref[...].astype(BF16)
        acc_ref[...] = jnp.zeros_like(acc_ref)

    acc_ref[...] += _swiglu_step(xb_ref[...], wg_ref, wu_ref, wd_ref)

    @pl.when(j == pl.num_programs(1) - 1)
    def _():
        o_ref[...] = _layer_norm(DN_ALPHA * x_ref[...] + acc_ref[...], g_ref[...], b_ref[...])


def _ffn(x, wg, wu, wd, g, b, tm, tf):
    t, d = x.shape
    f = wg.shape[1]
    return pl.pallas_call(
        _ffn_kernel,
        grid=(t // tm, f // tf),
        in_specs=[pl.BlockSpec((tm, d), lambda i, j: (i, 0)),
                  pl.BlockSpec((d, tf), lambda i, j: (0, j)),
                  pl.BlockSpec((d, tf), lambda i, j: (0, j)),
                  pl.BlockSpec((tf, d), lambda i, j: (j, 0)),
                  pl.BlockSpec((1, d), lambda i, j: (0, 0)),
                  pl.BlockSpec((1, d), lambda i, j: (0, 0))],
        out_specs=pl.BlockSpec((tm, d), lambda i, j: (i, 0)),
        out_shape=jax.ShapeDtypeStruct((t, d), F32),
        scratch_shapes=[pltpu.VMEM((tm, d), BF16), pltpu.VMEM((tm, d), F32)],
        compiler_params=_cparams(("parallel", "arbitrary")),
    )(x, wg, wu, wd, g, b)


def _router_kernel(x_ref, w_ref, o_ref):
    logits = jnp.dot(x_ref[...], w_ref[...], preferred_element_type=F32, precision=lax.Precision.HIGHEST)
    lane = lax.broadcasted_iota(jnp.int32, logits.shape, 1).astype(F32)
    logits = jnp.where(lane < N_EXPERTS, logits, -jnp.inf)
    v1 = jnp.max(logits, -1, keepdims=True)
    e1 = jnp.min(jnp.where(logits == v1, lane, float(LANES)), -1, keepdims=True)
    rest = jnp.where(lane == e1, -jnp.inf, logits)
    v2 = jnp.max(rest, -1, keepdims=True)
    e2 = jnp.min(jnp.where(rest == v2, lane, float(LANES)), -1, keepdims=True)
    x2 = jnp.exp(v2 - v1)
    den = 1.0 + x2
    o_ref[...] = jnp.where(lane == 0, e1, jnp.where(lane == 1, e2, jnp.where(
        lane == 2, 1.0 / den, jnp.where(lane == 3, x2 / den, 0.0))))


def _router(x, w, tm):
    t, d = x.shape
    return pl.pallas_call(
        _router_kernel,
        grid=(t // tm,),
        in_specs=[pl.BlockSpec((tm, d), lambda i: (i, 0)), pl.BlockSpec((d, LANES), lambda i: (0, 0))],
        out_specs=pl.BlockSpec((tm, LANES), lambda i: (i, 0)),
        out_shape=jax.ShapeDtypeStruct((t, LANES), F32),
        compiler_params=_cparams(("parallel",)),
    )(x, w)


def _expert_kernel(be_ref, x_ref, wg_ref, wu_ref, wd_ref, o_ref, acc_ref):
    j = pl.program_id(1)

    @pl.when(j == 0)
    def _():
        acc_ref[...] = jnp.zeros_like(acc_ref)

    acc_ref[...] += _swiglu_step(x_ref[...], wg_ref, wu_ref, wd_ref)

    @pl.when(j == pl.num_programs(1) - 1)
    def _():
        o_ref[...] = acc_ref[...]


def _experts(blk_e, xs, wg, wu, wd, tf):
    rows, d = xs.shape
    f = wg.shape[2]
    tm = MOE_ROW_BLOCK
    grid_spec = pltpu.PrefetchScalarGridSpec(
        num_scalar_prefetch=1,
        grid=(rows // tm, f // tf),
        in_specs=[pl.BlockSpec((tm, d), lambda i, j, be: (i, 0)),
                  pl.BlockSpec((None, d, tf), lambda i, j, be: (be[i], 0, j)),
                  pl.BlockSpec((None, d, tf), lambda i, j, be: (be[i], 0, j)),
                  pl.BlockSpec((None, tf, d), lambda i, j, be: (be[i], j, 0))],
        out_specs=pl.BlockSpec((tm, d), lambda i, j, be: (i, 0)),
        scratch_shapes=[pltpu.VMEM((tm, d), F32)],
    )
    return pl.pallas_call(
        _expert_kernel,
        grid_spec=grid_spec,
        out_shape=jax.ShapeDtypeStruct((rows, d), F32),
        compiler_params=_cparams(("parallel", "arbitrary")),
    )(blk_e, xs, wg, wu, wd)


def _combine_kernel(x_ref, y1_ref, y2_ref, gt_ref, g_ref, b_ref, o_ref):
    f = y1_ref[...] * gt_ref[:, 2:3] + y2_ref[...] * gt_ref[:, 3:4]
    o_ref[...] = _layer_norm(DN_ALPHA * x_ref[...] + f, g_ref[...], b_ref[...])


def _combine(x, y1, y2, route, g, b, tm):
    t, d = x.shape
    row = lambda i: (i, 0)
    const = lambda i: (0, 0)
    return pl.pallas_call(
        _combine_kernel,
        grid=(t // tm,),
        in_specs=[pl.BlockSpec((tm, d), row), pl.BlockSpec((tm, d), row), pl.BlockSpec((tm, d), row),
                  pl.BlockSpec((tm, LANES), row), pl.BlockSpec((1, d), const), pl.BlockSpec((1, d), const)],
        out_specs=pl.BlockSpec((tm, d), row),
        out_shape=jax.ShapeDtypeStruct((t, d), F32),
        compiler_params=_cparams(("parallel",)),
    )(x, y1, y2, route, g, b)


def _moe(x, w_router, wg, wu, wd, g, b):
    t, d = x.shape
    tk_ = t * TOP_K
    route = _router(x, jnp.pad(w_router, ((0, 0), (0, LANES - N_EXPERTS))), 512)
    flat_e = route[:, :TOP_K].astype(jnp.int32).reshape(-1)
    onehot = (flat_e[:, None] == jnp.arange(N_EXPERTS)[None, :]).astype(jnp.int32)
    csum = jnp.cumsum(onehot, axis=0)
    counts = csum[-1]
    rank = jnp.sum(onehot * csum, axis=1) - 1
    padded = (counts + MOE_ROW_BLOCK - 1) // MOE_ROW_BLOCK * MOE_ROW_BLOCK
    pend = jnp.cumsum(padded)
    dest = (pend - padded)[flat_e] + rank
    n_blocks = -(-(tk_ + N_EXPERTS * (MOE_ROW_BLOCK - 1)) // MOE_ROW_BLOCK)
    n_rows = n_blocks * MOE_ROW_BLOCK
    row_tok = jnp.zeros((n_rows,), jnp.int32).at[dest].set(jnp.arange(tk_, dtype=jnp.int32) // TOP_K)
    blk_e = jnp.minimum(jnp.searchsorted(pend, jnp.arange(n_blocks) * MOE_ROW_BLOCK, side='right'),
                        N_EXPERTS - 1).astype(jnp.int32)
    xs = x.astype(BF16)[row_tok]
    ys = _experts(blk_e, xs, wg, wu, wd, 896)
    yk = ys[dest].reshape(t, TOP_K, d)
    return _combine(x, yk[:, 0], yk[:, 1], route, g, b, 512)


def _pack_w_in(w_in):
    offs = np.concatenate([[0], np.cumsum(IN_SIZES)])
    seg = [w_in[:, offs[j]:offs[j + 1]] for j in range(len(IN_SIZES))]
    seg[4] = seg[4] * (NSA_DH ** -0.5 * LOG2E)
    seg[11] = jnp.pad(seg[11], ((0, 0), (0, LANES - 3 * NSA_HEADS)))
    seg = seg[12:14] + seg[0:12]
    packed = jnp.concatenate(seg + [jnp.zeros((w_in.shape[0], PROJ_WP - PROJ_W), w_in.dtype)], axis=1)
    return packed.astype(BF16)


def _cmp_to_sel(n_cmp_pad, n_cmp):
    cs = np.arange(n_cmp_pad)[:, None] * CMP_STRIDE
    ss = np.arange(LANES)[None, :] * SEL_BLOCK
    overlap = np.clip(np.minimum(cs + CMP_LEN, ss + SEL_BLOCK) - np.maximum(cs, ss), 0, None) / CMP_LEN
    overlap[n_cmp:] = 0.0
    return jnp.asarray(overlap, F32)


def _token_mixer(x, w_in_p, lb, hg_norm_w, cmp_pos, cmp_w1, cmp_w2, bias_c, bias_t, wa, wb, wo, ln_g, ln_b):
    bsz, s, d = x.shape
    g, p, dh = NSA_GROUPS, NSA_HPG, NSA_DH
    xf = x.reshape(bsz * s, d)
    proj = _project(xf, w_in_p, 1024, PROJ_TN).reshape(bsz, s, PROJ_WP)
    o_a = _hgrn(proj, lb, hg_norm_w, 512)

    def heads(c0, width):
        return proj[:, :, c0:c0 + width]

    q5 = heads(COL_NQ, NSA_WIDTH).astype(BF16).reshape(bsz, s, g, p, dh).transpose(0, 2, 3, 1, 4)
    kv = heads(COL_KV, 6 * KV_WIDTH).reshape(bsz, s, 6, g, dh)
    n16 = s // CMP_STRIDE
    a = kv[:, :, 0:2].reshape(bsz, n16, CMP_STRIDE, 2, g, dh).transpose(3, 0, 4, 1, 2, 5)
    a = a.reshape(2, bsz, g, n16, CMP_STRIDE * dh)
    kvc = _compress(a, cmp_pos.reshape(2, 1, CMP_LEN * dh), cmp_w1.astype(BF16), cmp_w2.astype(BF16))
    kvh = kv[:, :, 2:6].astype(BF16).transpose(2, 0, 3, 1, 4)
    gl5 = heads(COL_NG, 3 * NSA_HEADS).reshape(bsz, s, g, p, 3).transpose(0, 2, 3, 1, 4)
    n_cmp = (s - CMP_LEN) // CMP_STRIDE + 1
    acc, pen = _cmp_branch(q5, kvc[0], kvc[1], bias_c, _cmp_to_sel(n16, n_cmp), gl5, ATT_TQ)
    ones = jnp.ones((bsz, g, s, LANES - dh), BF16)
    acc = _win_branch(q5, kvh[2], jnp.concatenate([kvh[3], ones], axis=-1), bias_t, gl5, acc, ATT_TQ, ATT_TK)
    q_aug = jnp.concatenate([jnp.broadcast_to(pen[:, :, None], (bsz, g, p, s, LANES)), q5,
                             jnp.zeros((bsz, g, p, s, LANES - dh), BF16)], axis=-1)
    block_of_key = (np.arange(s)[:, None] // SEL_BLOCK == np.arange(LANES)[None, :])
    k_aug = jnp.concatenate([jnp.broadcast_to(jnp.asarray(block_of_key, BF16), (bsz, g, s, LANES)), kvh[0],
                             jnp.zeros((bsz, g, s, LANES - dh), BF16)], axis=-1)
    v_aug = jnp.concatenate([kvh[1], ones], axis=-1)
    o_b = _sel_branch(q_aug, k_aug, v_aug, bias_t, gl5, acc, ATT_TQ, ATT_TK)
    o_b = o_b.transpose(0, 3, 1, 2, 4).reshape(bsz * s, NSA_WIDTH)
    return _merge(xf, o_a.reshape(bsz * s, HG_WIDTH), o_b, proj.reshape(bsz * s, PROJ_WP),
                  wa, wb, wo, ln_g, ln_b, 512)


def kernel(x, w_in, hg_lb_logits, hg_norm_w, cmp_pos, cmp_w1, cmp_w2, rel_bias, w_branch_a, w_branch_b,
           w_out, ln1_g, ln1_b, ln2_g, ln2_b, ffn_w_gate, ffn_w_up, ffn_w_down, moe_router, moe_w_gate,
           moe_w_up, moe_w_down):
    bsz, s, d = x.shape
    depth = w_in.shape[0]
    p_lb = jax.nn.softmax(hg_lb_logits.astype(F32), axis=0)
    lbs = jnp.cumsum(p_lb, axis=0) - p_lb[0]
    n16 = s // CMP_STRIDE
    bias_c = _bias_tiles(rel_bias, s // ATT_TQ, ATT_TQ, n16,
                         lambda t, r, c: t * ATT_TQ + r - (c * CMP_STRIDE + CMP_LEN - 1))
    bias_t = _bias_tiles(rel_bias, BIAS_ND + 1, ATT_TQ, ATT_TK, lambda t, r, c: t * ATT_TQ + r - c)
    f_pad = -(-D_FF // LANES) * LANES - D_FF
    xf = x.reshape(bsz * s, d)
    for l in range(depth):
        xf = _token_mixer(xf.reshape(bsz, s, d), _pack_w_in(w_in[l]), lbs[l][None], hg_norm_w[l][None],
                          cmp_pos[l], cmp_w1[l], cmp_w2[l], bias_c, bias_t,
                          w_branch_a[l].astype(BF16), w_branch_b[l].astype(BF16), w_out[l].astype(BF16),
                          ln1_g[l][None], ln1_b[l][None])
        if l % 2 == 0:
            wg = jnp.pad(ffn_w_gate[l // 2], ((0, 0), (0, f_pad))).astype(BF16)
            wu = jnp.pad(ffn_w_up[l // 2], ((0, 0), (0, f_pad))).astype(BF16)
            wd = jnp.pad(ffn_w_down[l // 2], ((0, f_pad), (0, 0))).astype(BF16)
            xf = _ffn(xf, wg, wu, wd, ln2_g[l][None], ln2_b[l][None], 512, wg.shape[1] // 2)
        else:
            xf = _moe(xf, moe_router[l // 2], moe_w_gate[l // 2].astype(BF16), moe_w_up[l // 2].astype(BF16),
                      moe_w_down[l // 2].astype(BF16), ln2_g[l][None], ln2_b[l][None])
    return xf.reshape(bsz, s, d)
```

```python
import functools
import math

import jax
import jax.numpy as jnp
import numpy as np
from jax import lax
from jax.experimental import pallas as pl
from jax.experimental.pallas import tpu as pltpu

F32 = jnp.float32
BF16 = jnp.bfloat16

D_MODEL = 1024
DEPTH = 2
HG_HEADS = 4
HG_DK = 128
HG_WIDTH = HG_HEADS * HG_DK
HG_CHUNK = 64
HG_SUB = 16
HG_HALF = HG_SUB // 2
NSA_HEADS = 8
NSA_GROUPS = 2
NSA_HPG = NSA_HEADS // NSA_GROUPS
NSA_DH = 64
NSA_WIDTH = NSA_HEADS * NSA_DH
KV_WIDTH = NSA_GROUPS * NSA_DH
CMP_LEN = 32
CMP_STRIDE = 16
CMP_HIDDEN = 2 * NSA_DH
SEL_BLOCK = 64
N_SEL = 16
WINDOW = 512
FORCE_SCORE = 1e9
NEG_BIG = -1e30
REL_BUCKETS = 32
REL_MAX_DIST = 2048
D_FF = 2752
N_EXPERTS = 8
TOP_K = 2
D_FF_EXPERT = 3584
MOE_ROW_BLOCK = 1024
DN_ALPHA = (2 * DEPTH) ** 0.25
LN_EPS = 1e-5
IN_SIZES = (HG_WIDTH, HG_WIDTH, HG_WIDTH, HG_WIDTH, NSA_WIDTH,
            KV_WIDTH, KV_WIDTH, KV_WIDTH, KV_WIDTH, KV_WIDTH, KV_WIDTH,
            3 * NSA_HEADS, D_MODEL, D_MODEL)

LANES = 128
LOG2E = 1.0 / math.log(2.0)
COL_GA = 0
COL_HG = 2 * D_MODEL
COL_NQ = COL_HG + 4 * HG_WIDTH
COL_KV = COL_NQ + NSA_WIDTH
COL_NG = COL_KV + 6 * KV_WIDTH
PROJ_W = COL_NG + LANES
PROJ_TN = 512
PROJ_WP = -(-PROJ_W // PROJ_TN) * PROJ_TN

ATT_TQ = 256
ATT_TK = 512
BIAS_ND = -(-(REL_MAX_DIST + ATT_TK) // ATT_TQ)
VMEM_LIMIT = 48 * 1024 * 1024


def _cparams(sem):
    return pltpu.CompilerParams(dimension_semantics=sem, vmem_limit_bytes=VMEM_LIMIT)


def _proj_kernel(x_ref, w_ref, o_ref, xb_ref):
    @pl.when(pl.program_id(1) == 0)
    def _():
        xb_ref[...] = x_ref[...].astype(BF16)

    o_ref[...] = jnp.dot(xb_ref[...], w_ref[...], preferred_element_type=F32).astype(o_ref.dtype)


def _project(x, w, tm, tn):
    m, k = x.shape
    n = w.shape[1]
    return pl.pallas_call(
        _proj_kernel,
        grid=(m // tm, n // tn),
        in_specs=[pl.BlockSpec((tm, k), lambda i, j: (i, 0)),
                  pl.BlockSpec((k, tn), lambda i, j: (0, j))],
        out_specs=pl.BlockSpec((tm, tn), lambda i, j: (i, j)),
        out_shape=jax.ShapeDtypeStruct((m, n), BF16),
        scratch_shapes=[pltpu.VMEM((tm, k), BF16)],
        compiler_params=_cparams(("parallel", "arbitrary")),
    )(x, w)


def _layer_norm(y, g, b):
    mu = jnp.mean(y, -1, keepdims=True)
    yc = y - mu
    var = jnp.mean(yc * yc, -1, keepdims=True)
    return yc * lax.rsqrt(var + LN_EPS) * g + b


def _split3(x):
    parts = []
    for _ in range(3):
        part = x.astype(BF16)
        parts.append(part)
        x = x - part.astype(F32)
    return parts


def _cumsum_rows(tril, x):
    return sum(jnp.dot(tril, part, preferred_element_type=F32) for part in _split3(x))


def _hgrn_kernel(q_ref, f_ref, i_ref, g_ref, lb_ref, nw_ref, o_ref, st_ref, *, n_chunks):
    @pl.when(pl.program_id(1) == 0)
    def _():
        st_ref[...] = jnp.zeros_like(st_ref)

    c = HG_CHUNK
    nt = (((1,), (1,)), ((), ()))
    row = lax.broadcasted_iota(jnp.int32, (c, c), 0)
    col = lax.broadcasted_iota(jnp.int32, (c, c), 1)
    tril = (row >= col).astype(BF16)
    row1 = lax.broadcasted_iota(jnp.int32, (c, 1), 0)
    half_pos = row1 % HG_HALF
    second_half = row1 % HG_SUB >= HG_HALF
    same_sub = row // HG_SUB == col // HG_SUB

    w = HG_HEADS * HG_DK
    head_lanes = [slice(h * HG_DK, (h + 1) * HG_DK) for h in range(HG_HEADS)]

    def per_head(fn):
        return jnp.concatenate([fn(h, head_lanes[h]) for h in range(HG_HEADS)], axis=1)

    def shift_in_half(x, lag):
        return pltpu.roll(x.reshape(c // HG_HALF, HG_HALF, w), lag, 1).reshape(c, w)

    def chunk(ci, carry):
        r0 = pl.multiple_of(ci * c, c)
        q = q_ref[pl.ds(r0, c), :].astype(F32)
        z = f_ref[pl.ds(r0, c), :].astype(F32)
        v = i_ref[pl.ds(r0, c), :].astype(F32)
        g = g_ref[pl.ds(r0, c), :].astype(F32)
        k = (1.0 - lb_ref[...]) * jax.nn.sigmoid(-z)
        b = _cumsum_rows(tril, jnp.log1p(-k))
        b_last = b[c - 1:c, :]
        vb = v.astype(BF16)
        qe = (q * jnp.exp(b)).astype(BF16)
        o = per_head(lambda h, hl: lax.dot_general(qe[:, hl], st_ref[h].astype(BF16), nt,
                                                  preferred_element_type=F32))
        att_rows = [[jnp.zeros((HG_SUB, c), F32)] * HG_HEADS]
        for sb in range(1, c // HG_SUB):
            lo = sb * HG_SUB
            ref_b = b[lo - 1:lo, :]
            qt = (q[lo:lo + HG_SUB, :] * jnp.exp(b[lo:lo + HG_SUB, :] - ref_b)).astype(BF16)
            kt = (k * jnp.exp(jnp.where(row1 < lo, ref_b - b, -jnp.inf))).astype(BF16)
            att_rows.append([lax.dot_general(qt[:, hl], kt[:, hl], nt, preferred_element_type=F32)
                             for hl in head_lanes])
        mid = jnp.concatenate(
            [jnp.broadcast_to(b[lo + HG_HALF - 1:lo + HG_HALF, :], (HG_SUB, w)) for lo in range(0, c, HG_SUB)],
            axis=0)
        q2 = (q * jnp.exp(jnp.where(second_half, b - mid, -jnp.inf))).astype(BF16)
        k2 = (k * jnp.exp(jnp.where(second_half, -jnp.inf, mid - b))).astype(BF16)
        att_half = [lax.dot_general(q2[:, hl], k2[:, hl], nt, preferred_element_type=F32) for hl in head_lanes]

        def intra(h, hl):
            att = jnp.concatenate([rows[h] for rows in att_rows], axis=0) + jnp.where(same_sub, att_half[h], 0.0)
            return jnp.dot(att.astype(BF16), vb[:, hl], preferred_element_type=F32)

        o = o + per_head(intra)
        for lag in range(HG_HALF):
            if lag == 0:
                ks, bs, vs = k, b, v
            else:
                ks, bs, vs = shift_in_half(k, lag), shift_in_half(b, lag), shift_in_half(v, lag)
            valid = half_pos >= lag
            prod = q * ks * jnp.exp(jnp.where(valid, b - bs, 0.0))
            a = per_head(lambda h, hl: jnp.broadcast_to(
                jnp.sum(prod[:, hl], axis=1, keepdims=True), (c, HG_DK)))
            o = o + jnp.where(valid, a, 0.0) * vs
        khat = (k * jnp.exp(b_last - b)).astype(BF16)
        decay = jnp.exp(b_last)
        for h, hl in enumerate(head_lanes):
            st_ref[h] = st_ref[h] * decay[:, hl] + lax.dot_general(
                vb[:, hl], khat[:, hl], (((0,), (0,)), ((), ())), preferred_element_type=F32)
        sq = o * o
        ms = per_head(lambda h, hl: jnp.broadcast_to(jnp.mean(sq[:, hl], -1, keepdims=True), (c, HG_DK)))
        o = o * lax.rsqrt(ms + 1e-6)
        o_ref[pl.ds(r0, c), :] = (o * nw_ref[...] * (g * jax.nn.sigmoid(g))).astype(o_ref.dtype)
        return carry

    lax.fori_loop(0, n_chunks, chunk, 0)


def _hgrn(proj, lb, norm_w, ts):
    bsz, s, _ = proj.shape
    col0 = COL_HG // HG_WIDTH

    def col_spec(n):
        return pl.BlockSpec((None, ts, HG_WIDTH), lambda b, t: (b, t, col0 + n))

    head_spec = pl.BlockSpec((1, HG_WIDTH), lambda b, t: (0, 0))
    return pl.pallas_call(
        functools.partial(_hgrn_kernel, n_chunks=ts // HG_CHUNK),
        grid=(bsz, s // ts),
        in_specs=[col_spec(0), col_spec(1), col_spec(2), col_spec(3), head_spec, head_spec],
        out_specs=pl.BlockSpec((None, ts, HG_WIDTH), lambda b, t: (b, t, 0)),
        out_shape=jax.ShapeDtypeStruct((bsz, s, HG_WIDTH), BF16),
        scratch_shapes=[pltpu.VMEM((HG_HEADS, HG_DK, HG_DK), F32)],
        compiler_params=_cparams(("parallel", "arbitrary")),
    )(proj, proj, proj, proj, lb, norm_w)


def _compress_kernel(a_ref, pos_ref, w1_ref, w2_ref, o_ref):
    half = CMP_STRIDE * NSA_DH
    a = a_ref[...].astype(F32)
    n = a.shape[0]
    a1 = (a + pos_ref[:, :half]).astype(BF16)
    a2 = (a + pos_ref[:, half:]).astype(BF16)
    y1 = jnp.dot(a1, w1_ref[:half, :], preferred_element_type=F32)
    y2 = jnp.dot(a2, w1_ref[half:, :], preferred_element_type=F32)
    hid = jax.nn.gelu(y1 + pltpu.roll(y2, n - 1, 0))
    o_ref[...] = jnp.dot(hid.astype(BF16), w2_ref[...], preferred_element_type=F32).astype(o_ref.dtype)


def _compress(a, pos, w1, w2):
    _, bsz, g, n, width = a.shape
    return pl.pallas_call(
        _compress_kernel,
        grid=(2, bsz, g),
        in_specs=[pl.BlockSpec((None, None, None, n, width), lambda c, b, h: (c, b, h, 0, 0)),
                  pl.BlockSpec((None, 1, 2 * width), lambda c, b, h: (c, 0, 0)),
                  pl.BlockSpec((None, 2 * width, CMP_HIDDEN), lambda c, b, h: (c, 0, 0)),
                  pl.BlockSpec((None, CMP_HIDDEN, NSA_DH), lambda c, b, h: (c, 0, 0))],
        out_specs=pl.BlockSpec((None, None, None, n, NSA_DH), lambda c, b, h: (c, b, h, 0, 0)),
        out_shape=jax.ShapeDtypeStruct((2, bsz, g, n, NSA_DH), BF16),
        compiler_params=_cparams(("parallel", "parallel", "parallel")),
    )(a, pos, w1, w2)


def _bucket_of(dist):
    n = jnp.maximum(dist, 0)
    exact = REL_BUCKETS // 2
    large = exact + (jnp.log(jnp.maximum(n, exact).astype(F32) / exact)
                     / math.log(REL_MAX_DIST / exact) * (REL_BUCKETS - exact)).astype(jnp.int32)
    return jnp.where(n < exact, n, jnp.minimum(large, REL_BUCKETS - 1))


def _bias_kernel(tab_ref, o_ref, *, dist_fn):
    rows, cols = o_ref.shape[2], o_ref.shape[3]
    row = lax.broadcasted_iota(jnp.int32, (rows, cols), 0)
    col = lax.broadcasted_iota(jnp.int32, (rows, cols), 1)
    bucket = _bucket_of(dist_fn(pl.program_id(0), row, col))
    for h in range(NSA_HEADS):
        tab = jnp.broadcast_to(tab_ref[h:h + 1, :] * LOG2E, (rows, LANES))
        for c0 in range(0, cols, LANES):
            o_ref[h, 0, :, c0:c0 + LANES] = jnp.take_along_axis(tab, bucket[:, c0:c0 + LANES], axis=1)


def _bias_tiles(rel_bias, n_tiles, rows, cols, dist_fn):
    tab = jnp.pad(rel_bias.T, ((0, 0), (0, LANES - REL_BUCKETS)))
    return pl.pallas_call(
        functools.partial(_bias_kernel, dist_fn=dist_fn),
        grid=(n_tiles,),
        in_specs=[pl.BlockSpec((NSA_HEADS, LANES), lambda t: (0, 0))],
        out_specs=pl.BlockSpec((NSA_HEADS, 1, rows, cols), lambda t: (0, t, 0, 0)),
        out_shape=jax.ShapeDtypeStruct((NSA_HEADS, n_tiles, rows, cols), F32),
        compiler_params=_cparams(("parallel",)),
    )(tab)


def _cmp_kernel(q_ref, k_ref, v_ref, bias_ref, c2s_ref, gl_ref, o_ref, pen_ref, *, tq):
    p = NSA_HPG
    t0 = pl.program_id(1) * tq
    bsz, ncp = k_ref.shape[0], k_ref.shape[1]
    bias = bias_ref[:, 0]
    tpos = t0 + lax.broadcasted_iota(jnp.int32, (tq, ncp), 0)
    cend = lax.broadcasted_iota(jnp.int32, (tq, ncp), 1) * CMP_STRIDE + (CMP_LEN - 1)
    mask = (tpos >= cend)[None]
    lane = lax.broadcasted_iota(jnp.int32, (tq, LANES), 1)
    cur = (t0 + lax.broadcasted_iota(jnp.int32, (tq, LANES), 0)) // SEL_BLOCK
    forced = (lane == 0) | (lane == cur) | (lane == cur - 1)
    visible = lane <= cur
    batch = range(bsz)
    nt = (((1,), (1,)), ((), ()))
    lgs = [lax.dot_general(q_ref[b].reshape(p * tq, NSA_DH), k_ref[b], nt, preferred_element_type=F32)
           for b in batch]
    lgs = [jnp.where(mask, lg.reshape(p, tq, ncp) + bias, NEG_BIG) for lg in lgs]
    es = [jnp.where(mask, jnp.exp2(lg - jnp.max(lg, -1, keepdims=True)), 0.0) for lg in lgs]
    pcs = [e / jnp.maximum(jnp.sum(e, -1, keepdims=True), 1e-30) for e in es]
    os = [jnp.dot(pcs[b].reshape(p * tq, ncp).astype(BF16), v_ref[b], preferred_element_type=F32) for b in batch]
    imps = [sum(jnp.dot(part, c2s_ref[...], preferred_element_type=F32) for part in _split3(jnp.sum(pc, axis=0)))
            for pc in pcs]
    for b in batch:
        o_ref[b] = os[b].reshape(p, tq, NSA_DH) * jax.nn.sigmoid(gl_ref[b, :, :, 0:1].astype(F32))
    scores = [jnp.where(forced, FORCE_SCORE, jnp.where(visible, imp, NEG_BIG)) for imp in imps]
    members = [jnp.zeros((tq, LANES), F32)] * bsz
    for _ in range(N_SEL):
        hits = [lane == jnp.argmax(score, axis=-1, keepdims=True) for score in scores]
        members = [jnp.where(hit, 1.0, member) for hit, member in zip(hits, members)]
        scores = [jnp.where(hit, -jnp.inf, score) for hit, score in zip(hits, scores)]
    for b in batch:
        pen_ref[b] = jnp.where((members[b] > 0.5) & visible, 0.0, NEG_BIG).astype(pen_ref.dtype)


def _cmp_branch(q5, kc, vc, bias_c, c2s, gl5, tq):
    bsz, g, p, s, dh = q5.shape
    ncp = kc.shape[2]
    qmap = lambda h, i: (0, h, 0, i, 0)
    return pl.pallas_call(
        functools.partial(_cmp_kernel, tq=tq),
        grid=(g, s // tq),
        in_specs=[pl.BlockSpec((bsz, None, p, tq, dh), qmap),
                  pl.BlockSpec((bsz, None, ncp, dh), lambda h, i: (0, h, 0, 0)),
                  pl.BlockSpec((bsz, None, ncp, dh), lambda h, i: (0, h, 0, 0)),
                  pl.BlockSpec((p, 1, tq, ncp), lambda h, i: (h, i, 0, 0)),
                  pl.BlockSpec((ncp, LANES), lambda h, i: (0, 0)),
                  pl.BlockSpec((bsz, None, p, tq, 3), qmap)],
        out_specs=[pl.BlockSpec((bsz, None, p, tq, dh), qmap),
                   pl.BlockSpec((bsz, None, tq, LANES), lambda h, i: (0, h, i, 0))],
        out_shape=[jax.ShapeDtypeStruct((bsz, g, p, s, dh), F32),
                   jax.ShapeDtypeStruct((bsz, g, s, LANES), BF16)],
        compiler_params=_cparams(("parallel", "parallel")),
    )(q5, kc, vc, bias_c, c2s, gl5)


def _win_kernel(*refs, tq, n_back):
    nk = n_back + 1
    q_ref, k_refs, v_refs, b_refs = refs[0], refs[1:1 + nk], refs[1 + nk:1 + 2 * nk], refs[1 + 2 * nk:1 + 3 * nk]
    gl_ref, acc_ref, o_ref = refs[1 + 3 * nk:]
    p = NSA_HPG
    bsz = q_ref.shape[0]
    t0 = pl.program_id(1) * tq
    row = lax.broadcasted_iota(jnp.int32, (tq, tq), 0)
    col = lax.broadcasted_iota(jnp.int32, (tq, tq), 1)
    masks = [((row - col + d * tq >= 0) & (row - col + d * tq < WINDOW) & (t0 - d * tq >= 0))[None]
             for d in range(nk)]
    biases = [b_ref[:, 0] for b_ref in b_refs]
    nt = (((1,), (1,)), ((), ()))
    batch = range(bsz)
    qs = [q_ref[b].reshape(p * tq, NSA_DH) for b in batch]
    lgs = [[lax.dot_general(qs[b], k_ref[b], nt, preferred_element_type=F32).reshape(p, tq, tq)
            for k_ref in k_refs] for b in batch]
    lgs = [[jnp.where(mask, lg + bias, NEG_BIG) for lg, mask, bias in zip(lgs[b], masks, biases)] for b in batch]
    ms = [functools.reduce(jnp.maximum, [jnp.max(lg, -1, keepdims=True) for lg in lgs[b]]) for b in batch]
    es = [[jnp.exp2(lg - ms[b]).reshape(p * tq, tq).astype(BF16) for lg in lgs[b]] for b in batch]
    os = [sum(jnp.dot(e, v_ref[b], preferred_element_type=F32) for e, v_ref in zip(es[b], v_refs)) for b in batch]
    for b in batch:
        o = (os[b] / pltpu.roll(os[b], NSA_DH, 1))[:, :NSA_DH].reshape(p, tq, NSA_DH)
        o_ref[b] = acc_ref[b] + o * jax.nn.sigmoid(gl_ref[b, :, :, 2:3].astype(F32))


def _win_branch(q5, kw, vw_aug, bias_t, gl5, acc, tq):
    bsz, g, p, s, dh = q5.shape
    n_back = WINDOW // tq
    back = [lambda h, i, d=d: (0, h, jnp.maximum(i - d, 0), 0) for d in range(n_back + 1)]
    qmap = lambda h, i: (0, h, 0, i, 0)
    in_specs = ([pl.BlockSpec((bsz, None, p, tq, dh), qmap)]
                + [pl.BlockSpec((bsz, None, tq, dh), m) for m in back]
                + [pl.BlockSpec((bsz, None, tq, LANES), m) for m in back]
                + [pl.BlockSpec((p, 1, tq, tq), lambda h, i, d=d: (h, d, 0, 0)) for d in range(n_back + 1)]
                + [pl.BlockSpec((bsz, None, p, tq, 3), qmap), pl.BlockSpec((bsz, None, p, tq, dh), qmap)])
    return pl.pallas_call(
        functools.partial(_win_kernel, tq=tq, n_back=n_back),
        grid=(g, s // tq),
        in_specs=in_specs,
        out_specs=pl.BlockSpec((bsz, None, p, tq, dh), qmap),
        out_shape=jax.ShapeDtypeStruct((bsz, g, p, s, dh), F32),
        input_output_aliases={len(in_specs) - 1: 0},
        compiler_params=_cparams(("parallel", "parallel")),
    )(q5, *([kw] * (n_back + 1)), *([vw_aug] * (n_back + 1)), *([bias_t] * (n_back + 1)), gl5, acc)


def _sel_kernel(it_ref, jt_ref, q_ref, k_ref, v_ref, bias_ref, gl_ref, acc_ref, o_ref, m_sc, a_sc,
                *, tq, tk, bsz):
    p = NSA_HPG
    step = pl.program_id(1)
    i = it_ref[step]
    j = jt_ref[step]
    t0 = i * tq
    s0 = j * tk
    last = s0 + tk > t0

    @pl.when(j == 0)
    def _():
        m_sc[...] = jnp.full_like(m_sc, NEG_BIG)
        a_sc[...] = jnp.zeros_like(a_sc)

    def accumulate(on_diagonal):
        bias = bias_ref[:, 0]
        if on_diagonal:
            causal = (t0 + lax.broadcasted_iota(jnp.int32, (tq, tk), 0)
                      >= s0 + lax.broadcasted_iota(jnp.int32, (tq, tk), 1))[None]
        for b in range(bsz):
            q = q_ref[b].reshape(p * tq, 2 * LANES)
            lg = lax.dot_general(q, k_ref[b], (((1,), (1,)), ((), ())), preferred_element_type=F32)
            lg = lg.reshape(p, tq, tk) + bias
            if on_diagonal:
                lg = jnp.where(causal, lg, NEG_BIG)
            lg = lg.reshape(p * tq, tk)
            m_old = m_sc[b]
            m_new = jnp.maximum(m_old, jnp.max(lg, -1, keepdims=True))
            alpha = jnp.exp2(m_old - m_new)
            e = jnp.exp2(lg - jnp.concatenate([m_new] * (tk // LANES), axis=1))
            a_sc[b] = alpha * a_sc[b] + jnp.dot(e.astype(BF16), v_ref[b], preferred_element_type=F32)
            m_sc[b] = m_new

    @pl.when(jnp.logical_not(last))
    def _():
        accumulate(False)

    @pl.when(last)
    def _():
        accumulate(True)
        gate = jax.nn.sigmoid(gl_ref[:, :, :, 1:2].astype(F32))
        a = a_sc[...]
        o = (a / pltpu.roll(a, NSA_DH, 2))[:, :, :NSA_DH].reshape(bsz, p, tq, NSA_DH)
        o_ref[...] = (acc_ref[...] + o * gate).astype(o_ref.dtype)


def _sel_branch(q_aug, k_aug, v_aug, bias_t, gl5, acc, tq, tk):
    bsz, g, p, s, wq = q_aug.shape
    dh = acc.shape[-1]
    r = tk // tq
    pairs = [(i, j) for i in range(s // tq) for j in range(i // r + 1)]
    it = jnp.asarray([ij[0] for ij in pairs], jnp.int32)
    jt = jnp.asarray([ij[1] for ij in pairs], jnp.int32)
    nd = bias_t.shape[1] - 1
    qmap = lambda h, t, it, jt: (0, h, 0, it[t], 0)
    kmap = lambda h, t, it, jt: (0, h, jt[t], 0)
    grid_spec = pltpu.PrefetchScalarGridSpec(
        num_scalar_prefetch=2,
        grid=(g, len(pairs)),
        in_specs=[pl.BlockSpec((bsz, None, p, tq, wq), qmap),
                  pl.BlockSpec((bsz, None, tk, wq), kmap),
                  pl.BlockSpec((bsz, None, tk, LANES), kmap),
                  pl.BlockSpec((p, 1, tq, tk),
                               lambda h, t, it, jt: (h, jnp.minimum(it[t] - r * jt[t], nd), 0, 0)),
                  pl.BlockSpec((bsz, None, p, tq, 3), qmap),
                  pl.BlockSpec((bsz, None, p, tq, dh), qmap)],
        out_specs=pl.BlockSpec((bsz, None, p, tq, dh), qmap),
        scratch_shapes=[pltpu.VMEM((bsz, p * tq, LANES), F32),
                        pltpu.VMEM((bsz, p * tq, LANES), F32)],
    )
    return pl.pallas_call(
        functools.partial(_sel_kernel, tq=tq, tk=tk, bsz=bsz),
        grid_spec=grid_spec,
        out_shape=jax.ShapeDtypeStruct((bsz, g, p, s, dh), BF16),
        compiler_params=_cparams(("parallel", "arbitrary")),
    )(it, jt, q_aug, k_aug, v_aug, bias_t, gl5, acc)


def _merge_kernel(x_ref, oa_ref, ob_ref, ga_ref, gb_ref, wa_ref, wb_ref, wo_ref, g_ref, b_ref, o_ref):
    ya = jnp.dot(oa_ref[...].astype(BF16), wa_ref[...], preferred_element_type=F32)
    yb = jnp.dot(ob_ref[...].astype(BF16), wb_ref[...], preferred_element_type=F32)
    y = jax.nn.sigmoid(ga_ref[...].astype(F32)) * ya + jax.nn.sigmoid(gb_ref[...].astype(F32)) * yb
    mix = jnp.dot(y.astype(BF16), wo_ref[...], preferred_element_type=F32)
    o_ref[...] = _layer_norm(DN_ALPHA * x_ref[...] + mix, g_ref[...], b_ref[...])


def _merge(x, o_a, o_b, proj, wa, wb, wo, g, b, tm):
    t, d = x.shape
    nga = COL_GA // d
    row = lambda i: (i, 0)
    const = lambda i: (0, 0)
    return pl.pallas_call(
        _merge_kernel,
        grid=(t // tm,),
        in_specs=[pl.BlockSpec((tm, d), row),
                  pl.BlockSpec((tm, HG_WIDTH), row),
                  pl.BlockSpec((tm, NSA_WIDTH), row),
                  pl.BlockSpec((tm, d), lambda i: (i, nga)),
                  pl.BlockSpec((tm, d), lambda i: (i, nga + 1)),
                  pl.BlockSpec((HG_WIDTH, d), const),
                  pl.BlockSpec((NSA_WIDTH, d), const),
                  pl.BlockSpec((d, d), const),
                  pl.BlockSpec((1, d), const),
                  pl.BlockSpec((1, d), const)],
        out_specs=pl.BlockSpec((tm, d), row),
        out_shape=jax.ShapeDtypeStruct((t, d), F32),
        compiler_params=_cparams(("parallel",)),
    )(x, o_a, o_b, proj, proj, wa, wb, wo, g, b)


def _swiglu_step(xb, wg_ref, wu_ref, wd_ref):
    hg = jnp.dot(xb, wg_ref[...], preferred_element_type=F32)
    hu = jnp.dot(xb, wu_ref[...], preferred_element_type=F32)
    h = (hg * jax.nn.sigmoid(hg)) * hu
    return jnp.dot(h.astype(BF16), wd_ref[...], preferred_element_type=F32)


def _ffn_kernel(x_ref, wg_ref, wu_ref, wd_ref, g_ref, b_ref, o_ref, xb_ref, acc_ref):
    j = pl.program_id(1)

    @pl.when(j == 0)
    def _():
        xb_ref[...] = x_ref[...].astype(BF16)
        acc_ref[...] = jnp.zeros_like(acc_ref)

    acc_ref[...] += _swiglu_step(xb_ref[...], wg_ref, wu_ref, wd_ref)

    @pl.when(j == pl.num_programs(1) - 1)
    def _():
        o_ref[...] = _layer_norm(DN_ALPHA * x_ref[...] + acc_ref[...], g_ref[...], b_ref[...])


def _ffn(x, wg, wu, wd, g, b, tm, tf):
    t, d = x.shape
    f = wg.shape[1]
    return pl.pallas_call(
        _ffn_kernel,
        grid=(t // tm, f // tf),
        in_specs=[pl.BlockSpec((tm, d), lambda i, j: (i, 0)),
                  pl.BlockSpec((d, tf), lambda i, j: (0, j)),
                  pl.BlockSpec((d, tf), lambda i, j: (0, j)),
                  pl.BlockSpec((tf, d), lambda i, j: (j, 0)),
                  pl.BlockSpec((1, d), lambda i, j: (0, 0)),
                  pl.BlockSpec((1, d), lambda i, j: (0, 0))],
        out_specs=pl.BlockSpec((tm, d), lambda i, j: (i, 0)),
        out_shape=jax.ShapeDtypeStruct((t, d), F32),
        scratch_shapes=[pltpu.VMEM((tm, d), BF16), pltpu.VMEM((tm, d), F32)],
        compiler_params=_cparams(("parallel", "arbitrary")),
    )(x, wg, wu, wd, g, b)


def _router_kernel(x_ref, w_ref, o_ref):
    logits = jnp.dot(x_ref[...], w_ref[...], preferred_element_type=F32, precision=lax.Precision.HIGHEST)
    lane = lax.broadcasted_iota(jnp.int32, logits.shape, 1).astype(F32)
    logits = jnp.where(lane < N_EXPERTS, logits, -jnp.inf)
    v1 = jnp.max(logits, -1, keepdims=True)
    e1 = jnp.min(jnp.where(logits == v1, lane, float(LANES)), -1, keepdims=True)
    rest = jnp.where(lane == e1, -jnp.inf, logits)
    v2 = jnp.max(rest, -1, keepdims=True)
    e2 = jnp.min(jnp.where(rest == v2, lane, float(LANES)), -1, keepdims=True)
    x2 = jnp.exp(v2 - v1)
    den = 1.0 + x2
    o_ref[...] = jnp.where(lane == 0, e1, jnp.where(lane == 1, e2, jnp.where(
        lane == 2, 1.0 / den, jnp.where(lane == 3, x2 / den, 0.0))))


def _router(x, w, tm):
    t, d = x.shape
    return pl.pallas_call(
        _router_kernel,
        grid=(t // tm,),
        in_specs=[pl.BlockSpec((tm, d), lambda i: (i, 0)), pl.BlockSpec((d, LANES), lambda i: (0, 0))],
        out_specs=pl.BlockSpec((tm, LANES), lambda i: (i, 0)),
        out_shape=jax.ShapeDtypeStruct((t, LANES), F32),
        compiler_params=_cparams(("parallel",)),
    )(x, w)


def _expert_kernel(be_ref, x_ref, wg_ref, wu_ref, wd_ref, o_ref, acc_ref):
    j = pl.program_id(1)

    @pl.when(j == 0)
    def _():
        acc_ref[...] = jnp.zeros_like(acc_ref)

    xb = x_ref[...]
    hg = jnp.dot(xb, wg_ref[...].astype(BF16), preferred_element_type=F32)
    hu = jnp.dot(xb, wu_ref[...].astype(BF16), preferred_element_type=F32)
    h = (hg * jax.nn.sigmoid(hg)) * hu
    acc_ref[...] += jnp.dot(h.astype(BF16), wd_ref[...].astype(BF16), preferred_element_type=F32)

    @pl.when(j == pl.num_programs(1) - 1)
    def _():
        o_ref[...] = acc_ref[...].astype(o_ref.dtype)


def _experts(blk_e, xs, wg, wu, wd, tf):
    rows, d = xs.shape
    f = wg.shape[2]
    tm = MOE_ROW_BLOCK
    grid_spec = pltpu.PrefetchScalarGridSpec(
        num_scalar_prefetch=1,
        grid=(rows // tm, f // tf),
        in_specs=[pl.BlockSpec((tm, d), lambda i, j, be: (i, 0)),
                  pl.BlockSpec((None, d, tf), lambda i, j, be: (be[i], 0, j)),
                  pl.BlockSpec((None, d, tf), lambda i, j, be: (be[i], 0, j)),
                  pl.BlockSpec((None, tf, d), lambda i, j, be: (be[i], j, 0))],
        out_specs=pl.BlockSpec((tm, d), lambda i, j, be: (i, 0)),
        scratch_shapes=[pltpu.VMEM((tm, d), F32)],
    )
    return pl.pallas_call(
        _expert_kernel,
        grid_spec=grid_spec,
        out_shape=jax.ShapeDtypeStruct((rows, d), BF16),
        compiler_params=_cparams(("parallel", "arbitrary")),
    )(blk_e, xs, wg, wu, wd)


def _combine_kernel(x_ref, y1_ref, y2_ref, gt_ref, g_ref, b_ref, o_ref):
    f = y1_ref[...].astype(F32) * gt_ref[:, 2:3] + y2_ref[...].astype(F32) * gt_ref[:, 3:4]
    o_ref[...] = _layer_norm(DN_ALPHA * x_ref[...] + f, g_ref[...], b_ref[...])


def _combine(x, yk, route, g, b, tm):
    t, d = x.shape
    row = lambda i: (i, 0)
    const = lambda i: (0, 0)
    return pl.pallas_call(
        _combine_kernel,
        grid=(t // tm,),
        in_specs=[pl.BlockSpec((tm, d), row), pl.BlockSpec((tm, d), row), pl.BlockSpec((tm, d), lambda i: (i, 1)),
                  pl.BlockSpec((tm, LANES), row), pl.BlockSpec((1, d), const), pl.BlockSpec((1, d), const)],
        out_specs=pl.BlockSpec((tm, d), row),
        out_shape=jax.ShapeDtypeStruct((t, d), F32),
        compiler_params=_cparams(("parallel",)),
    )(x, yk, yk, route, g, b)


def _moe(x, w_router, wg, wu, wd, g, b):
    t, d = x.shape
    tk_ = t * TOP_K
    route = _router(x, jnp.pad(w_router, ((0, 0), (0, LANES - N_EXPERTS))), 512)
    flat_e = route[:, :TOP_K].astype(jnp.int32).reshape(-1)
    onehot = (flat_e[:, None] == jnp.arange(N_EXPERTS)[None, :]).astype(jnp.int32)
    csum = jnp.cumsum(onehot, axis=0)
    counts = csum[-1]
    rank = jnp.sum(onehot * csum, axis=1) - 1
    padded = (counts + MOE_ROW_BLOCK - 1) // MOE_ROW_BLOCK * MOE_ROW_BLOCK
    pend = jnp.cumsum(padded)
    dest = (pend - padded)[flat_e] + rank
    n_blocks = -(-(tk_ + N_EXPERTS * (MOE_ROW_BLOCK - 1)) // MOE_ROW_BLOCK)
    n_rows = n_blocks * MOE_ROW_BLOCK
    row_tok = jnp.zeros((n_rows,), jnp.int32).at[dest].set(jnp.arange(tk_, dtype=jnp.int32) // TOP_K)
    blk_e = jnp.minimum(jnp.searchsorted(pend, jnp.arange(n_blocks) * MOE_ROW_BLOCK, side='right'),
                        N_EXPERTS - 1).astype(jnp.int32)
    xs = x.astype(BF16)[row_tok]
    ys = _experts(blk_e, xs, wg, wu, wd, 512)
    return _combine(x, ys[dest].reshape(t, TOP_K * d), route, g, b, 512)


def _pack_w_in(w_in):
    offs = np.concatenate([[0], np.cumsum(IN_SIZES)])
    seg = [w_in[:, offs[j]:offs[j + 1]] for j in range(len(IN_SIZES))]
    seg[4] = seg[4] * (NSA_DH ** -0.5 * LOG2E)
    seg[11] = jnp.pad(seg[11], ((0, 0), (0, LANES - 3 * NSA_HEADS)))
    seg = seg[12:14] + seg[0:12]
    packed = jnp.concatenate(seg + [jnp.zeros((w_in.shape[0], PROJ_WP - PROJ_W), w_in.dtype)], axis=1)
    return packed.astype(BF16)


def _cmp_to_sel(n_cmp_pad, n_cmp):
    cs = np.arange(n_cmp_pad)[:, None] * CMP_STRIDE
    ss = np.arange(LANES)[None, :] * SEL_BLOCK
    overlap = np.clip(np.minimum(cs + CMP_LEN, ss + SEL_BLOCK) - np.maximum(cs, ss), 0, None) / CMP_LEN
    overlap[n_cmp:] = 0.0
    return jnp.asarray(overlap, BF16)


def _token_mixer(x, w_in_p, lb, hg_norm_w, cmp_pos, cmp_w1, cmp_w2, bias_c, bias_t, wa, wb, wo, ln_g, ln_b):
    bsz, s, d = x.shape
    g, p, dh = NSA_GROUPS, NSA_HPG, NSA_DH
    xf = x.reshape(bsz * s, d)
    proj = _project(xf, w_in_p, 1024, PROJ_TN).reshape(bsz, s, PROJ_WP)
    o_a = _hgrn(proj, lb, hg_norm_w, 512)

    def heads(c0, width):
        return proj[:, :, c0:c0 + width]

    q5 = heads(COL_NQ, NSA_WIDTH).astype(BF16).reshape(bsz, s, g, p, dh).transpose(0, 2, 3, 1, 4)
    kv = heads(COL_KV, 6 * KV_WIDTH).reshape(bsz, s, 6, g, dh)
    n16 = s // CMP_STRIDE
    a = kv[:, :, 0:2].reshape(bsz, n16, CMP_STRIDE, 2, g, dh).transpose(3, 0, 4, 1, 2, 5)
    a = a.reshape(2, bsz, g, n16, CMP_STRIDE * dh)
    kvc = _compress(a, cmp_pos.reshape(2, 1, CMP_LEN * dh), cmp_w1.astype(BF16), cmp_w2.astype(BF16))
    kvh = kv[:, :, 2:6].astype(BF16).transpose(2, 0, 3, 1, 4)
    gl5 = heads(COL_NG, 3 * NSA_HEADS).reshape(bsz, s, g, p, 3).transpose(0, 2, 3, 1, 4)
    n_cmp = (s - CMP_LEN) // CMP_STRIDE + 1
    acc, pen = _cmp_branch(q5, kvc[0], kvc[1], bias_c, _cmp_to_sel(n16, n_cmp), gl5, ATT_TQ)
    ones = jnp.ones((bsz, g, s, LANES - dh), BF16)
    acc = _win_branch(q5, kvh[2], jnp.concatenate([kvh[3], ones], axis=-1), bias_t, gl5, acc, ATT_TQ)
    q_aug = jnp.concatenate([jnp.broadcast_to(pen[:, :, None], (bsz, g, p, s, LANES)), q5,
                             jnp.zeros((bsz, g, p, s, LANES - dh), BF16)], axis=-1)
    block_of_key = (np.arange(s)[:, None] // SEL_BLOCK == np.arange(LANES)[None, :])
    k_aug = jnp.concatenate([jnp.broadcast_to(jnp.asarray(block_of_key, BF16), (bsz, g, s, LANES)), kvh[0],
                             jnp.zeros((bsz, g, s, LANES - dh), BF16)], axis=-1)
    v_aug = jnp.concatenate([kvh[1], ones], axis=-1)
    o_b = _sel_branch(q_aug, k_aug, v_aug, bias_t, gl5, acc, ATT_TQ, ATT_TK)
    o_b = o_b.transpose(0, 3, 1, 2, 4).reshape(bsz * s, NSA_WIDTH)
    return _merge(xf, o_a.reshape(bsz * s, HG_WIDTH), o_b, proj.reshape(bsz * s, PROJ_WP),
                  wa, wb, wo, ln_g, ln_b, 512)


def kernel(x, w_in, hg_lb_logits, hg_norm_w, cmp_pos, cmp_w1, cmp_w2, rel_bias, w_branch_a, w_branch_b,
           w_out, ln1_g, ln1_b, ln2_g, ln2_b, ffn_w_gate, ffn_w_up, ffn_w_down, moe_router, moe_w_gate,
           moe_w_up, moe_w_down):
    bsz, s, d = x.shape
    depth = w_in.shape[0]
    p_lb = jax.nn.softmax(hg_lb_logits.astype(F32), axis=0)
    lbs = jnp.cumsum(p_lb, axis=0) - p_lb[0]
    n16 = s // CMP_STRIDE
    bias_c = _bias_tiles(rel_bias, s // ATT_TQ, ATT_TQ, n16,
                         lambda t, r, c: t * ATT_TQ + r - (c * CMP_STRIDE + CMP_LEN - 1))
    bias_t = _bias_tiles(rel_bias, BIAS_ND + 1, ATT_TQ, ATT_TK, lambda t, r, c: t * ATT_TQ + r - c)
    f_pad = -(-D_FF // LANES) * LANES - D_FF
    xf = x.reshape(bsz * s, d)
    for l in range(depth):
        xf = _token_mixer(xf.reshape(bsz, s, d), _pack_w_in(w_in[l]), lbs[l][None], hg_norm_w[l][None],
                          cmp_pos[l], cmp_w1[l], cmp_w2[l], bias_c, bias_t,
                          w_branch_a[l].astype(BF16), w_branch_b[l].astype(BF16), w_out[l].astype(BF16),
                          ln1_g[l][None], ln1_b[l][None])
        if l % 2 == 0:
            wg = jnp.pad(ffn_w_gate[l // 2], ((0, 0), (0, f_pad))).astype(BF16)
            wu = jnp.pad(ffn_w_up[l // 2], ((0, 0), (0, f_pad))).astype(BF16)
            wd = jnp.pad(ffn_w_down[l // 2], ((0, f_pad), (0, 0))).astype(BF16)
            xf = _ffn(xf, wg, wu, wd, ln2_g[l][None], ln2_b[l][None], 512, wg.shape[1] // 2)
        else:
            xf = _moe(xf, moe_router[l // 2], moe_w_gate[l // 2], moe_w_up[l // 2], moe_w_down[l // 2],
                      ln2_g[l][None], ln2_b[l][None])
    return xf.reshape(bsz, s, d)
```

```python
import functools
import math

import jax
import jax.numpy as jnp
import numpy as np
from jax import lax
from jax.experimental import pallas as pl
from jax.experimental.pallas import tpu as pltpu

F32 = jnp.float32
BF16 = jnp.bfloat16

D_MODEL = 1024
DEPTH = 2
HG_HEADS = 4
HG_DK = 128
HG_WIDTH = HG_HEADS * HG_DK
HG_CHUNK = 64
HG_SUB = 16
HG_HALF = HG_SUB // 2
NSA_HEADS = 8
NSA_GROUPS = 2
NSA_HPG = NSA_HEADS // NSA_GROUPS
NSA_DH = 64
NSA_WIDTH = NSA_HEADS * NSA_DH
KV_WIDTH = NSA_GROUPS * NSA_DH
CMP_LEN = 32
CMP_STRIDE = 16
CMP_HIDDEN = 2 * NSA_DH
SEL_BLOCK = 64
N_SEL = 16
WINDOW = 512
FORCE_SCORE = 1e9
NEG_BIG = -1e30
REL_BUCKETS = 32
REL_MAX_DIST = 2048
D_FF = 2752
N_EXPERTS = 8
TOP_K = 2
D_FF_EXPERT = 3584
MOE_ROW_BLOCK = 1024
DN_ALPHA = (2 * DEPTH) ** 0.25
LN_EPS = 1e-5
IN_SIZES = (HG_WIDTH, HG_WIDTH, HG_WIDTH, HG_WIDTH, NSA_WIDTH,
            KV_WIDTH, KV_WIDTH, KV_WIDTH, KV_WIDTH, KV_WIDTH, KV_WIDTH,
            3 * NSA_HEADS, D_MODEL, D_MODEL)

LANES = 128
LOG2E = 1.0 / math.log(2.0)
COL_GA = 0
COL_HG = 2 * D_MODEL
COL_NQ = COL_HG + 4 * HG_WIDTH
COL_KV = COL_NQ + NSA_WIDTH
COL_NG = COL_KV + 6 * KV_WIDTH
PROJ_W = COL_NG + LANES
PROJ_TN = 512
PROJ_WP = -(-PROJ_W // PROJ_TN) * PROJ_TN

ATT_TQ = 256
ATT_TK = 512
BIAS_ND = -(-(REL_MAX_DIST + ATT_TK) // ATT_TQ)
VMEM_LIMIT = 48 * 1024 * 1024


def _cparams(sem):
    return pltpu.CompilerParams(dimension_semantics=sem, vmem_limit_bytes=VMEM_LIMIT)


def _proj_kernel(x_ref, w_ref, o_ref, xb_ref):
    @pl.when(pl.program_id(1) == 0)
    def _():
        xb_ref[...] = x_ref[...].astype(BF16)

    o_ref[...] = jnp.dot(xb_ref[...], w_ref[...], preferred_element_type=F32).astype(o_ref.dtype)


def _project(x, w, tm, tn):
    m, k = x.shape
    n = w.shape[1]
    return pl.pallas_call(
        _proj_kernel,
        grid=(m // tm, n // tn),
        in_specs=[pl.BlockSpec((tm, k), lambda i, j: (i, 0)),
                  pl.BlockSpec((k, tn), lambda i, j: (0, j))],
        out_specs=pl.BlockSpec((tm, tn), lambda i, j: (i, j)),
        out_shape=jax.ShapeDtypeStruct((m, n), BF16),
        scratch_shapes=[pltpu.VMEM((tm, k), BF16)],
        compiler_params=_cparams(("parallel", "arbitrary")),
    )(x, w)


def _layer_norm(y, g, b):
    mu = jnp.mean(y, -1, keepdims=True)
    yc = y - mu
    var = jnp.mean(yc * yc, -1, keepdims=True)
    return yc * lax.rsqrt(var + LN_EPS) * g + b


def _split3(x):
    parts = []
    for _ in range(3):
        part = x.astype(BF16)
        parts.append(part)
        x = x - part.astype(F32)
    return parts


def _cumsum_rows(tril, x):
    return sum(jnp.dot(tril, part, preferred_element_type=F32) for part in _split3(x))


def _hgrn_kernel(q_ref, f_ref, i_ref, g_ref, lb_ref, nw_ref, o_ref, st_ref, *, n_chunks):
    @pl.when(pl.program_id(1) == 0)
    def _():
        st_ref[...] = jnp.zeros_like(st_ref)

    c = HG_CHUNK
    nt = (((1,), (1,)), ((), ()))
    row = lax.broadcasted_iota(jnp.int32, (c, c), 0)
    col = lax.broadcasted_iota(jnp.int32, (c, c), 1)
    tril = (row >= col).astype(BF16)
    row1 = lax.broadcasted_iota(jnp.int32, (c, 1), 0)
    half_pos = row1 % HG_HALF
    second_half = row1 % HG_SUB >= HG_HALF
    same_sub = row // HG_SUB == col // HG_SUB

    w = HG_HEADS * HG_DK
    head_lanes = [slice(h * HG_DK, (h + 1) * HG_DK) for h in range(HG_HEADS)]

    def per_head(fn):
        return jnp.concatenate([fn(h, head_lanes[h]) for h in range(HG_HEADS)], axis=1)

    def shift_in_half(x, lag):
        return pltpu.roll(x.reshape(c // HG_HALF, HG_HALF, w), lag, 1).reshape(c, w)

    def chunk(ci, carry):
        r0 = pl.multiple_of(ci * c, c)
        q = q_ref[pl.ds(r0, c), :].astype(F32)
        z = f_ref[pl.ds(r0, c), :].astype(F32)
        v = i_ref[pl.ds(r0, c), :].astype(F32)
        g = g_ref[pl.ds(r0, c), :].astype(F32)
        k = (1.0 - lb_ref[...]) * jax.nn.sigmoid(-z)
        b = _cumsum_rows(tril, jnp.log1p(-k))
        b_last = b[c - 1:c, :]
        vb = v.astype(BF16)
        qe = (q * jnp.exp(b)).astype(BF16)
        o = per_head(lambda h, hl: lax.dot_general(qe[:, hl], st_ref[h].astype(BF16), nt,
                                                  preferred_element_type=F32))
        att_rows = [[jnp.zeros((HG_SUB, c), F32)] * HG_HEADS]
        for sb in range(1, c // HG_SUB):
            lo = sb * HG_SUB
            ref_b = b[lo - 1:lo, :]
            qt = (q[lo:lo + HG_SUB, :] * jnp.exp(b[lo:lo + HG_SUB, :] - ref_b)).astype(BF16)
            kt = (k * jnp.exp(jnp.where(row1 < lo, ref_b - b, -jnp.inf))).astype(BF16)
            att_rows.append([lax.dot_general(qt[:, hl], kt[:, hl], nt, preferred_element_type=F32)
                             for hl in head_lanes])
        mid = jnp.concatenate(
            [jnp.broadcast_to(b[lo + HG_HALF - 1:lo + HG_HALF, :], (HG_SUB, w)) for lo in range(0, c, HG_SUB)],
            axis=0)
        q2 = (q * jnp.exp(jnp.where(second_half, b - mid, -jnp.inf))).astype(BF16)
        k2 = (k * jnp.exp(jnp.where(second_half, -jnp.inf, mid - b))).astype(BF16)
        att_half = [lax.dot_general(q2[:, hl], k2[:, hl], nt, preferred_element_type=F32) for hl in head_lanes]

        def intra(h, hl):
            att = jnp.concatenate([rows[h] for rows in att_rows], axis=0) + jnp.where(same_sub, att_half[h], 0.0)
            return jnp.dot(att.astype(BF16), vb[:, hl], preferred_element_type=F32)

        o = o + per_head(intra)
        for lag in range(HG_HALF):
            if lag == 0:
                ks, bs, vs = k, b, v
            else:
                ks, bs, vs = shift_in_half(k, lag), shift_in_half(b, lag), shift_in_half(v, lag)
            valid = half_pos >= lag
            prod = q * ks * jnp.exp(jnp.where(valid, b - bs, 0.0))
            a = per_head(lambda h, hl: jnp.broadcast_to(
                jnp.sum(prod[:, hl], axis=1, keepdims=True), (c, HG_DK)))
            o = o + jnp.where(valid, a, 0.0) * vs
        khat = (k * jnp.exp(b_last - b)).astype(BF16)
        decay = jnp.exp(b_last)
        for h, hl in enumerate(head_lanes):
            st_ref[h] = st_ref[h] * decay[:, hl] + lax.dot_general(
                vb[:, hl], khat[:, hl], (((0,), (0,)), ((), ())), preferred_element_type=F32)
        sq = o * o
        ms = per_head(lambda h, hl: jnp.broadcast_to(jnp.mean(sq[:, hl], -1, keepdims=True), (c, HG_DK)))
        o = o * lax.rsqrt(ms + 1e-6)
        o_ref[pl.ds(r0, c), :] = (o * nw_ref[...] * (g * jax.nn.sigmoid(g))).astype(o_ref.dtype)
        return carry

    lax.fori_loop(0, n_chunks, chunk, 0)


def _hgrn(proj, lb, norm_w, ts):
    bsz, s, _ = proj.shape
    col0 = COL_HG // HG_WIDTH

    def col_spec(n):
        return pl.BlockSpec((None, ts, HG_WIDTH), lambda b, t: (b, t, col0 + n))

    head_spec = pl.BlockSpec((1, HG_WIDTH), lambda b, t: (0, 0))
    return pl.pallas_call(
        functools.partial(_hgrn_kernel, n_chunks=ts // HG_CHUNK),
        grid=(bsz, s // ts),
        in_specs=[col_spec(0), col_spec(1), col_spec(2), col_spec(3), head_spec, head_spec],
        out_specs=pl.BlockSpec((None, ts, HG_WIDTH), lambda b, t: (b, t, 0)),
        out_shape=jax.ShapeDtypeStruct((bsz, s, HG_WIDTH), BF16),
        scratch_shapes=[pltpu.VMEM((HG_HEADS, HG_DK, HG_DK), F32)],
        compiler_params=_cparams(("parallel", "arbitrary")),
    )(proj, proj, proj, proj, lb, norm_w)


def _compress_kernel(a_ref, pos_ref, w1_ref, w2_ref, o_ref):
    half = CMP_STRIDE * NSA_DH
    a = a_ref[...].astype(F32)
    n = a.shape[0]
    a1 = (a + pos_ref[:, :half]).astype(BF16)
    a2 = (a + pos_ref[:, half:]).astype(BF16)
    y1 = jnp.dot(a1, w1_ref[:half, :], preferred_element_type=F32)
    y2 = jnp.dot(a2, w1_ref[half:, :], preferred_element_type=F32)
    hid = jax.nn.gelu(y1 + pltpu.roll(y2, n - 1, 0))
    o_ref[...] = jnp.dot(hid.astype(BF16), w2_ref[...], preferred_element_type=F32).astype(o_ref.dtype)


def _compress(a, pos, w1, w2):
    _, bsz, g, n, width = a.shape
    return pl.pallas_call(
        _compress_kernel,
        grid=(2, bsz, g),
        in_specs=[pl.BlockSpec((None, None, None, n, width), lambda c, b, h: (c, b, h, 0, 0)),
                  pl.BlockSpec((None, 1, 2 * width), lambda c, b, h: (c, 0, 0)),
                  pl.BlockSpec((None, 2 * width, CMP_HIDDEN), lambda c, b, h: (c, 0, 0)),
                  pl.BlockSpec((None, CMP_HIDDEN, NSA_DH), lambda c, b, h: (c, 0, 0))],
        out_specs=pl.BlockSpec((None, None, None, n, NSA_DH), lambda c, b, h: (c, b, h, 0, 0)),
        out_shape=jax.ShapeDtypeStruct((2, bsz, g, n, NSA_DH), BF16),
        compiler_params=_cparams(("parallel", "parallel", "parallel")),
    )(a, pos, w1, w2)


def _bucket_of(dist):
    n = jnp.maximum(dist, 0)
    exact = REL_BUCKETS // 2
    large = exact + (jnp.log(jnp.maximum(n, exact).astype(F32) / exact)
                     / math.log(REL_MAX_DIST / exact) * (REL_BUCKETS - exact)).astype(jnp.int32)
    return jnp.where(n < exact, n, jnp.minimum(large, REL_BUCKETS - 1))


def _bias_kernel(tab_ref, o_ref, *, dist_fn):
    rows, cols = o_ref.shape[2], o_ref.shape[3]
    row = lax.broadcasted_iota(jnp.int32, (rows, cols), 0)
    col = lax.broadcasted_iota(jnp.int32, (rows, cols), 1)
    bucket = _bucket_of(dist_fn(pl.program_id(0), row, col))
    for h in range(NSA_HEADS):
        tab = jnp.broadcast_to(tab_ref[h:h + 1, :] * LOG2E, (rows, LANES))
        for c0 in range(0, cols, LANES):
            o_ref[h, 0, :, c0:c0 + LANES] = jnp.take_along_axis(tab, bucket[:, c0:c0 + LANES], axis=1)


def _bias_tiles(rel_bias, n_tiles, rows, cols, dist_fn):
    tab = jnp.pad(rel_bias.T, ((0, 0), (0, LANES - REL_BUCKETS)))
    return pl.pallas_call(
        functools.partial(_bias_kernel, dist_fn=dist_fn),
        grid=(n_tiles,),
        in_specs=[pl.BlockSpec((NSA_HEADS, LANES), lambda t: (0, 0))],
        out_specs=pl.BlockSpec((NSA_HEADS, 1, rows, cols), lambda t: (0, t, 0, 0)),
        out_shape=jax.ShapeDtypeStruct((NSA_HEADS, n_tiles, rows, cols), F32),
        compiler_params=_cparams(("parallel",)),
    )(tab)


def _cmp_kernel(q_ref, k_ref, v_ref, bias_ref, c2s_ref, gl_ref, o_ref, pen_ref, *, tq):
    p = NSA_HPG
    t0 = pl.program_id(1) * tq
    bsz, ncp = k_ref.shape[0], k_ref.shape[1]
    bias = bias_ref[:, 0]
    tpos = t0 + lax.broadcasted_iota(jnp.int32, (tq, ncp), 0)
    cend = lax.broadcasted_iota(jnp.int32, (tq, ncp), 1) * CMP_STRIDE + (CMP_LEN - 1)
    mask = (tpos >= cend)[None]
    lane = lax.broadcasted_iota(jnp.int32, (tq, LANES), 1)
    cur = (t0 + lax.broadcasted_iota(jnp.int32, (tq, LANES), 0)) // SEL_BLOCK
    forced = (lane == 0) | (lane == cur) | (lane == cur - 1)
    visible = lane <= cur
    batch = range(bsz)
    nt = (((1,), (1,)), ((), ()))
    lgs = [lax.dot_general(q_ref[b].reshape(p * tq, NSA_DH), k_ref[b], nt, preferred_element_type=F32)
           for b in batch]
    lgs = [jnp.where(mask, lg.reshape(p, tq, ncp) + bias, NEG_BIG) for lg in lgs]
    es = [jnp.where(mask, jnp.exp2(lg - jnp.max(lg, -1, keepdims=True)), 0.0) for lg in lgs]
    pcs = [e / jnp.maximum(jnp.sum(e, -1, keepdims=True), 1e-30) for e in es]
    os = [jnp.dot(pcs[b].reshape(p * tq, ncp).astype(BF16), v_ref[b], preferred_element_type=F32) for b in batch]
    imps = [sum(jnp.dot(part, c2s_ref[...], preferred_element_type=F32) for part in _split3(jnp.sum(pc, axis=0)))
            for pc in pcs]
    for b in batch:
        o_ref[b] = os[b].reshape(p, tq, NSA_DH) * jax.nn.sigmoid(gl_ref[b, :, :, 0:1].astype(F32))
    scores = [jnp.where(forced, FORCE_SCORE, jnp.where(visible, imp, NEG_BIG)) for imp in imps]
    members = [jnp.zeros((tq, LANES), F32)] * bsz
    for _ in range(N_SEL):
        hits = [lane == jnp.argmax(score, axis=-1, keepdims=True) for score in scores]
        members = [jnp.where(hit, 1.0, member) for hit, member in zip(hits, members)]
        scores = [jnp.where(hit, -jnp.inf, score) for hit, score in zip(hits, scores)]
    for b in batch:
        pen_ref[b] = jnp.where((members[b] > 0.5) & visible, 0.0, NEG_BIG).astype(pen_ref.dtype)


def _cmp_branch(q5, kc, vc, bias_c, c2s, gl5, tq):
    bsz, g, p, s, dh = q5.shape
    ncp = kc.shape[2]
    qmap = lambda h, i: (0, h, 0, i, 0)
    return pl.pallas_call(
        functools.partial(_cmp_kernel, tq=tq),
        grid=(g, s // tq),
        in_specs=[pl.BlockSpec((bsz, None, p, tq, dh), qmap),
                  pl.BlockSpec((bsz, None, ncp, dh), lambda h, i: (0, h, 0, 0)),
                  pl.BlockSpec((bsz, None, ncp, dh), lambda h, i: (0, h, 0, 0)),
                  pl.BlockSpec((p, 1, tq, ncp), lambda h, i: (h, i, 0, 0)),
                  pl.BlockSpec((ncp, LANES), lambda h, i: (0, 0)),
                  pl.BlockSpec((bsz, None, p, tq, 3), qmap)],
        out_specs=[pl.BlockSpec((bsz, None, p, tq, dh), qmap),
                   pl.BlockSpec((bsz, None, tq, LANES), lambda h, i: (0, h, i, 0))],
        out_shape=[jax.ShapeDtypeStruct((bsz, g, p, s, dh), F32),
                   jax.ShapeDtypeStruct((bsz, g, s, LANES), BF16)],
        compiler_params=_cparams(("parallel", "parallel")),
    )(q5, kc, vc, bias_c, c2s, gl5)


def _win_kernel(*refs, tq, n_back):
    nk = n_back + 1
    q_ref, k_refs, v_refs, b_refs = refs[0], refs[1:1 + nk], refs[1 + nk:1 + 2 * nk], refs[1 + 2 * nk:1 + 3 * nk]
    gl_ref, acc_ref, o_ref = refs[1 + 3 * nk:]
    p = NSA_HPG
    bsz = q_ref.shape[0]
    t0 = pl.program_id(1) * tq
    row = lax.broadcasted_iota(jnp.int32, (tq, tq), 0)
    col = lax.broadcasted_iota(jnp.int32, (tq, tq), 1)
    masks = [((row - col + d * tq >= 0) & (row - col + d * tq < WINDOW) & (t0 - d * tq >= 0))[None]
             for d in range(nk)]
    biases = [b_ref[:, 0] for b_ref in b_refs]
    nt = (((1,), (1,)), ((), ()))
    batch = range(bsz)
    qs = [q_ref[b].reshape(p * tq, NSA_DH) for b in batch]
    lgs = [[lax.dot_general(qs[b], k_ref[b], nt, preferred_element_type=F32).reshape(p, tq, tq)
            for k_ref in k_refs] for b in batch]
    lgs = [[jnp.where(mask, lg + bias, NEG_BIG) for lg, mask, bias in zip(lgs[b], masks, biases)] for b in batch]
    ms = [functools.reduce(jnp.maximum, [jnp.max(lg, -1, keepdims=True) for lg in lgs[b]]) for b in batch]
    es = [[jnp.exp2(lg - ms[b]).reshape(p * tq, tq).astype(BF16) for lg in lgs[b]] for b in batch]
    os = [sum(jnp.dot(e, v_ref[b], preferred_element_type=F32) for e, v_ref in zip(es[b], v_refs)) for b in batch]
    for b in batch:
        o = (os[b] / pltpu.roll(os[b], NSA_DH, 1))[:, :NSA_DH].reshape(p, tq, NSA_DH)
        o_ref[b] = acc_ref[b] + o * jax.nn.sigmoid(gl_ref[b, :, :, 2:3].astype(F32))


def _win_branch(q5, kw, vw_aug, bias_t, gl5, acc, tq):
    bsz, g, p, s, dh = q5.shape
    n_back = WINDOW // tq
    back = [lambda h, i, d=d: (0, h, jnp.maximum(i - d, 0), 0) for d in range(n_back + 1)]
    qmap = lambda h, i: (0, h, 0, i, 0)
    in_specs = ([pl.BlockSpec((bsz, None, p, tq, dh), qmap)]
                + [pl.BlockSpec((bsz, None, tq, dh), m) for m in back]
                + [pl.BlockSpec((bsz, None, tq, LANES), m) for m in back]
                + [pl.BlockSpec((p, 1, tq, tq), lambda h, i, d=d: (h, d, 0, 0)) for d in range(n_back + 1)]
                + [pl.BlockSpec((bsz, None, p, tq, 3), qmap), pl.BlockSpec((bsz, None, p, tq, dh), qmap)])
    return pl.pallas_call(
        functools.partial(_win_kernel, tq=tq, n_back=n_back),
        grid=(g, s // tq),
        in_specs=in_specs,
        out_specs=pl.BlockSpec((bsz, None, p, tq, dh), qmap),
        out_shape=jax.ShapeDtypeStruct((bsz, g, p, s, dh), F32),
        input_output_aliases={len(in_specs) - 1: 0},
        compiler_params=_cparams(("parallel", "parallel")),
    )(q5, *([kw] * (n_back + 1)), *([vw_aug] * (n_back + 1)), *([bias_t] * (n_back + 1)), gl5, acc)


def _sel_kernel(it_ref, jt_ref, q_ref, k_ref, v_ref, bias_ref, gl_ref, acc_ref, o_ref, m_sc, a_sc,
                *, tq, tk, bsz, n_bias_tiles):
    p = NSA_HPG
    nq = p * tq
    step = pl.program_id(1)
    i = it_ref[step]
    j = jt_ref[step]
    t0 = i * tq
    s0 = j * tk
    last = s0 + tk > t0

    @pl.when(j == 0)
    def _():
        m_sc[...] = jnp.full_like(m_sc, NEG_BIG)
        a_sc[...] = jnp.zeros_like(a_sc)

    def accumulate(kind):
        batch = range(bsz)
        nt = (((1,), (1,)), ((), ()))
        lgs = [lax.dot_general(k_ref[b], q_ref[b].reshape(nq, 2 * LANES), nt, preferred_element_type=F32)
               for b in batch]
        if kind == "far":
            shift = jnp.concatenate([bias_ref[h, 0, 0:1, :] for h in range(p)], axis=1)
        else:
            bias = jnp.concatenate([bias_ref[h, 0] for h in range(p)], axis=1)
            lgs = [lg + bias for lg in lgs]
            shift = 0.0
        if kind == "diagonal":
            causal = (s0 + lax.broadcasted_iota(jnp.int32, (tk, tq), 0)
                      <= t0 + lax.broadcasted_iota(jnp.int32, (tk, tq), 1))
            causal = jnp.concatenate([causal] * p, axis=1)
            lgs = [jnp.where(causal, lg, NEG_BIG) for lg in lgs]
        m_olds = [m_sc[b] for b in batch]
        m_news = [jnp.maximum(m_olds[b], jnp.max(lgs[b], 0, keepdims=True) + shift) for b in batch]
        es = [jnp.exp2(lgs[b] - (m_news[b][0:1, :] - shift)).astype(BF16) for b in batch]
        pvs = [jnp.dot(v_ref[b], es[b], preferred_element_type=F32) for b in batch]
        for b in batch:
            alpha = jnp.exp2(m_olds[b] - m_news[b])
            a_sc[b] = jnp.concatenate([alpha] * (LANES // 8), axis=0) * a_sc[b] + pvs[b]
            m_sc[b] = m_news[b]

    far = t0 - s0 >= n_bias_tiles * tq

    @pl.when(far)
    def _():
        accumulate("far")

    @pl.when(jnp.logical_not(far) & jnp.logical_not(last))
    def _():
        accumulate("near")

    @pl.when(last)
    def _():
        accumulate("diagonal")
        gate = jax.nn.sigmoid(gl_ref[:, :, :, 1:2].astype(F32))
        a = jnp.stack([a_sc[b].T for b in range(bsz)])
        o = (a / pltpu.roll(a, NSA_DH, 2))[:, :, :NSA_DH].reshape(bsz, p, tq, NSA_DH)
        o_ref[...] = (acc_ref[...] + o * gate).astype(o_ref.dtype)


def _sel_branch(q_aug, k_aug, v_aug, bias_t, gl5, acc, tq, tk):
    bsz, g, p, s, wq = q_aug.shape
    dh = acc.shape[-1]
    r = tk // tq
    pairs = [(i, j) for i in range(s // tq) for j in range(i // r + 1)]
    it = jnp.asarray([ij[0] for ij in pairs], jnp.int32)
    jt = jnp.asarray([ij[1] for ij in pairs], jnp.int32)
    nd = bias_t.shape[1] - 1
    qmap = lambda h, t, it, jt: (0, h, 0, it[t], 0)
    kmap = lambda h, t, it, jt: (0, h, jt[t], 0)
    grid_spec = pltpu.PrefetchScalarGridSpec(
        num_scalar_prefetch=2,
        grid=(g, len(pairs)),
        in_specs=[pl.BlockSpec((bsz, None, p, tq, wq), qmap),
                  pl.BlockSpec((bsz, None, tk, wq), kmap),
                  pl.BlockSpec((bsz, None, LANES, tk), lambda h, t, it, jt: (0, h, 0, jt[t])),
                  pl.BlockSpec((p, 1, tk, tq),
                               lambda h, t, it, jt: (h, jnp.minimum(it[t] - r * jt[t], nd), 0, 0)),
                  pl.BlockSpec((bsz, None, p, tq, 3), qmap),
                  pl.BlockSpec((bsz, None, p, tq, dh), qmap)],
        out_specs=pl.BlockSpec((bsz, None, p, tq, dh), qmap),
        scratch_shapes=[pltpu.VMEM((bsz, 8, p * tq), F32),
                        pltpu.VMEM((bsz, LANES, p * tq), F32)],
    )
    return pl.pallas_call(
        functools.partial(_sel_kernel, tq=tq, tk=tk, bsz=bsz, n_bias_tiles=nd),
        grid_spec=grid_spec,
        out_shape=jax.ShapeDtypeStruct((bsz, g, p, s, dh), BF16),
        compiler_params=_cparams(("parallel", "arbitrary")),
    )(it, jt, q_aug, k_aug, v_aug, bias_t, gl5, acc)


def _merge_kernel(x_ref, oa_ref, ob_ref, ga_ref, gb_ref, wa_ref, wb_ref, wo_ref, g_ref, b_ref, o_ref):
    ya = jnp.dot(oa_ref[...].astype(BF16), wa_ref[...], preferred_element_type=F32)
    yb = jnp.dot(ob_ref[...].astype(BF16), wb_ref[...], preferred_element_type=F32)
    y = jax.nn.sigmoid(ga_ref[...].astype(F32)) * ya + jax.nn.sigmoid(gb_ref[...].astype(F32)) * yb
    mix = jnp.dot(y.astype(BF16), wo_ref[...], preferred_element_type=F32)
    o_ref[...] = _layer_norm(DN_ALPHA * x_ref[...] + mix, g_ref[...], b_ref[...])


def _merge(x, o_a, o_b, proj, wa, wb, wo, g, b, tm):
    t, d = x.shape
    nga = COL_GA // d
    row = lambda i: (i, 0)
    const = lambda i: (0, 0)
    return pl.pallas_call(
        _merge_kernel,
        grid=(t // tm,),
        in_specs=[pl.BlockSpec((tm, d), row),
                  pl.BlockSpec((tm, HG_WIDTH), row),
                  pl.BlockSpec((tm, NSA_WIDTH), row),
                  pl.BlockSpec((tm, d), lambda i: (i, nga)),
                  pl.BlockSpec((tm, d), lambda i: (i, nga + 1)),
                  pl.BlockSpec((HG_WIDTH, d), const),
                  pl.BlockSpec((NSA_WIDTH, d), const),
                  pl.BlockSpec((d, d), const),
                  pl.BlockSpec((1, d), const),
                  pl.BlockSpec((1, d), const)],
        out_specs=pl.BlockSpec((tm, d), row),
        out_shape=jax.ShapeDtypeStruct((t, d), F32),
        compiler_params=_cparams(("parallel",)),
    )(x, o_a, o_b, proj, proj, wa, wb, wo, g, b)


def _swiglu_step(xb, wg_ref, wu_ref, wd_ref):
    hg = jnp.dot(xb, wg_ref[...], preferred_element_type=F32)
    hu = jnp.dot(xb, wu_ref[...], preferred_element_type=F32)
    h = (hg * jax.nn.sigmoid(hg)) * hu
    return jnp.dot(h.astype(BF16), wd_ref[...], preferred_element_type=F32)


def _ffn_kernel(x_ref, wg_ref, wu_ref, wd_ref, g_ref, b_ref, o_ref, xb_ref, acc_ref):
    j = pl.program_id(1)

    @pl.when(j == 0)
    def _():
        xb_ref[...] = x_ref[...].astype(BF16)
        acc_ref[...] = jnp.zeros_like(acc_ref)

    acc_ref[...] += _swiglu_step(xb_ref[...], wg_ref, wu_ref, wd_ref)

    @pl.when(j == pl.num_programs(1) - 1)
    def _():
        o_ref[...] = _layer_norm(DN_ALPHA * x_ref[...] + acc_ref[...], g_ref[...], b_ref[...])


def _ffn(x, wg, wu, wd, g, b, tm, tf):
    t, d = x.shape
    f = wg.shape[1]
    return pl.pallas_call(
        _ffn_kernel,
        grid=(t // tm, f // tf),
        in_specs=[pl.BlockSpec((tm, d), lambda i, j: (i, 0)),
                  pl.BlockSpec((d, tf), lambda i, j: (0, j)),
                  pl.BlockSpec((d, tf), lambda i, j: (0, j)),
                  pl.BlockSpec((tf, d), lambda i, j: (j, 0)),
                  pl.BlockSpec((1, d), lambda i, j: (0, 0)),
                  pl.BlockSpec((1, d), lambda i, j: (0, 0))],
        out_specs=pl.BlockSpec((tm, d), lambda i, j: (i, 0)),
        out_shape=jax.ShapeDtypeStruct((t, d), F32),
        scratch_shapes=[pltpu.VMEM((tm, d), BF16), pltpu.VMEM((tm, d), F32)],
        compiler_params=_cparams(("parallel", "arbitrary")),
    )(x, wg, wu, wd, g, b)


def _router_kernel(x_ref, w_ref, o_ref):
    logits = jnp.dot(x_ref[...], w_ref[...], preferred_element_type=F32, precision=lax.Precision.HIGHEST)
    lane = lax.broadcasted_iota(jnp.int32, logits.shape, 1).astype(F32)
    logits = jnp.where(lane < N_EXPERTS, logits, -jnp.inf)
    v1 = jnp.max(logits, -1, keepdims=True)
    e1 = jnp.min(jnp.where(logits == v1, lane, float(LANES)), -1, keepdims=True)
    rest = jnp.where(lane == e1, -jnp.inf, logits)
    v2 = jnp.max(rest, -1, keepdims=True)
    e2 = jnp.min(jnp.where(rest == v2, lane, float(LANES)), -1, keepdims=True)
    x2 = jnp.exp(v2 - v1)
    den = 1.0 + x2
    o_ref[...] = jnp.where(lane == 0, e1, jnp.where(lane == 1, e2, jnp.where(
        lane == 2, 1.0 / den, jnp.where(lane == 3, x2 / den, 0.0))))


def _router(x, w, tm):
    t, d = x.shape
    return pl.pallas_call(
        _router_kernel,
        grid=(t // tm,),
        in_specs=[pl.BlockSpec((tm, d), lambda i: (i, 0)), pl.BlockSpec((d, LANES), lambda i: (0, 0))],
        out_specs=pl.BlockSpec((tm, LANES), lambda i: (i, 0)),
        out_shape=jax.ShapeDtypeStruct((t, LANES), F32),
        compiler_params=_cparams(("parallel",)),
    )(x, w)


def _expert_kernel(be_ref, x_ref, wg_ref, wu_ref, wd_ref, o_ref, acc_ref):
    j = pl.program_id(1)

    @pl.when(j == 0)
    def _():
        acc_ref[...] = jnp.zeros_like(acc_ref)

    xb = x_ref[...].astype(BF16)
    hg = jnp.dot(xb, wg_ref[...].astype(BF16), preferred_element_type=F32)
    hu = jnp.dot(xb, wu_ref[...].astype(BF16), preferred_element_type=F32)
    h = (hg * jax.nn.sigmoid(hg)) * hu
    acc_ref[...] += jnp.dot(h.astype(BF16), wd_ref[...].astype(BF16), preferred_element_type=F32)

    @pl.when(j == pl.num_programs(1) - 1)
    def _():
        o_ref[...] = acc_ref[...].astype(o_ref.dtype)


def _experts(blk_e, xs, wg, wu, wd, tf):
    rows, d = xs.shape
    f = wg.shape[2]
    tm = MOE_ROW_BLOCK
    grid_spec = pltpu.PrefetchScalarGridSpec(
        num_scalar_prefetch=1,
        grid=(rows // tm, f // tf),
        in_specs=[pl.BlockSpec((tm, d), lambda i, j, be: (i, 0)),
                  pl.BlockSpec((None, d, tf), lambda i, j, be: (be[i], 0, j)),
                  pl.BlockSpec((None, d, tf), lambda i, j, be: (be[i], 0, j)),
                  pl.BlockSpec((None, tf, d), lambda i, j, be: (be[i], j, 0))],
        out_specs=pl.BlockSpec((tm, d), lambda i, j, be: (i, 0)),
        scratch_shapes=[pltpu.VMEM((tm, d), F32)],
    )
    return pl.pallas_call(
        _expert_kernel,
        grid_spec=grid_spec,
        out_shape=jax.ShapeDtypeStruct((rows, d), F32),
        compiler_params=_cparams(("parallel", "arbitrary")),
    )(blk_e, xs, wg, wu, wd)


def _combine_kernel(x_ref, y1_ref, y2_ref, gt_ref, g_ref, b_ref, o_ref):
    f = y1_ref[...].astype(F32) * gt_ref[:, 2:3] + y2_ref[...].astype(F32) * gt_ref[:, 3:4]
    o_ref[...] = _layer_norm(DN_ALPHA * x_ref[...] + f, g_ref[...], b_ref[...])


def _combine(x, yk, route, g, b, tm):
    t, d = x.shape
    row = lambda i: (i, 0)
    const = lambda i: (0, 0)
    return pl.pallas_call(
        _combine_kernel,
        grid=(t // tm,),
        in_specs=[pl.BlockSpec((tm, d), row), pl.BlockSpec((tm, d), row), pl.BlockSpec((tm, d), lambda i: (i, 1)),
                  pl.BlockSpec((tm, LANES), row), pl.BlockSpec((1, d), const), pl.BlockSpec((1, d), const)],
        out_specs=pl.BlockSpec((tm, d), row),
        out_shape=jax.ShapeDtypeStruct((t, d), F32),
        compiler_params=_cparams(("parallel",)),
    )(x, yk, yk, route, g, b)


def _moe(x, w_router, wg, wu, wd, g, b):
    t, d = x.shape
    tk_ = t * TOP_K
    route = _router(x, jnp.pad(w_router, ((0, 0), (0, LANES - N_EXPERTS))), 512)
    flat_e = route[:, :TOP_K].astype(jnp.int32).reshape(-1)
    onehot = (flat_e[:, None] == jnp.arange(N_EXPERTS)[None, :]).astype(jnp.int32)
    csum = jnp.cumsum(onehot, axis=0)
    counts = csum[-1]
    rank = jnp.sum(onehot * csum, axis=1) - 1
    padded = (counts + MOE_ROW_BLOCK - 1) // MOE_ROW_BLOCK * MOE_ROW_BLOCK
    pend = jnp.cumsum(padded)
    dest = (pend - padded)[flat_e] + rank
    n_blocks = -(-(tk_ + N_EXPERTS * (MOE_ROW_BLOCK - 1)) // MOE_ROW_BLOCK)
    n_rows = n_blocks * MOE_ROW_BLOCK
    row_tok = jnp.zeros((n_rows,), jnp.int32).at[dest].set(jnp.arange(tk_, dtype=jnp.int32) // TOP_K)
    blk_e = jnp.minimum(jnp.searchsorted(pend, jnp.arange(n_blocks) * MOE_ROW_BLOCK, side='right'),
                        N_EXPERTS - 1).astype(jnp.int32)
    xs = x[row_tok]
    ys = _experts(blk_e, xs, wg, wu, wd, 512)
    return _combine(x, ys[dest].reshape(t, TOP_K * d), route, g, b, 512)


def _pack_w_in(w_in):
    offs = np.concatenate([[0], np.cumsum(IN_SIZES)])
    seg = [w_in[:, offs[j]:offs[j + 1]] for j in range(len(IN_SIZES))]
    seg[4] = seg[4] * (NSA_DH ** -0.5 * LOG2E)
    seg[11] = jnp.pad(seg[11], ((0, 0), (0, LANES - 3 * NSA_HEADS)))
    seg = seg[12:14] + seg[0:12]
    packed = jnp.concatenate(seg + [jnp.zeros((w_in.shape[0], PROJ_WP - PROJ_W), w_in.dtype)], axis=1)
    return packed.astype(BF16)


def _cmp_to_sel(n_cmp_pad, n_cmp):
    cs = np.arange(n_cmp_pad)[:, None] * CMP_STRIDE
    ss = np.arange(LANES)[None, :] * SEL_BLOCK
    overlap = np.clip(np.minimum(cs + CMP_LEN, ss + SEL_BLOCK) - np.maximum(cs, ss), 0, None) / CMP_LEN
    overlap[n_cmp:] = 0.0
    return jnp.asarray(overlap, BF16)


def _token_mixer(x, w_in_p, lb, hg_norm_w, cmp_pos, cmp_w1, cmp_w2, bias_c, bias_t, bias_tt, wa, wb, wo, ln_g,
                 ln_b):
    bsz, s, d = x.shape
    g, p, dh = NSA_GROUPS, NSA_HPG, NSA_DH
    xf = x.reshape(bsz * s, d)
    proj = _project(xf, w_in_p, 1024, PROJ_TN).reshape(bsz, s, PROJ_WP)
    o_a = _hgrn(proj, lb, hg_norm_w, 512)

    def heads(c0, width):
        return proj[:, :, c0:c0 + width]

    q5 = heads(COL_NQ, NSA_WIDTH).astype(BF16).reshape(bsz, s, g, p, dh).transpose(0, 2, 3, 1, 4)
    kv = heads(COL_KV, 6 * KV_WIDTH).reshape(bsz, s, 6, g, dh)
    n16 = s // CMP_STRIDE
    a = kv[:, :, 0:2].reshape(bsz, n16, CMP_STRIDE, 2, g, dh).transpose(3, 0, 4, 1, 2, 5)
    a = a.reshape(2, bsz, g, n16, CMP_STRIDE * dh)
    kvc = _compress(a, cmp_pos.reshape(2, 1, CMP_LEN * dh), cmp_w1.astype(BF16), cmp_w2.astype(BF16))
    kvh = kv[:, :, 2:6].astype(BF16).transpose(2, 0, 3, 1, 4)
    gl5 = heads(COL_NG, 3 * NSA_HEADS).reshape(bsz, s, g, p, 3).transpose(0, 2, 3, 1, 4)
    n_cmp = (s - CMP_LEN) // CMP_STRIDE + 1
    acc, pen = _cmp_branch(q5, kvc[0], kvc[1], bias_c, _cmp_to_sel(n16, n_cmp), gl5, ATT_TQ)
    ones = jnp.ones((bsz, g, s, LANES - dh), BF16)
    acc = _win_branch(q5, kvh[2], jnp.concatenate([kvh[3], ones], axis=-1), bias_t, gl5, acc, ATT_TQ)
    q_aug = jnp.concatenate([jnp.broadcast_to(pen[:, :, None], (bsz, g, p, s, LANES)), q5,
                             jnp.zeros((bsz, g, p, s, LANES - dh), BF16)], axis=-1)
    block_of_key = (np.arange(s)[:, None] // SEL_BLOCK == np.arange(LANES)[None, :])
    k_aug = jnp.concatenate([jnp.broadcast_to(jnp.asarray(block_of_key, BF16), (bsz, g, s, LANES)), kvh[0],
                             jnp.zeros((bsz, g, s, LANES - dh), BF16)], axis=-1)
    v_aug_t = jnp.concatenate([kvh[1].transpose(0, 1, 3, 2), jnp.ones((bsz, g, LANES - dh, s), BF16)], axis=2)
    o_b = _sel_branch(q_aug, k_aug, v_aug_t, bias_tt, gl5, acc, ATT_TQ, ATT_TK)
    o_b = o_b.transpose(0, 3, 1, 2, 4).reshape(bsz * s, NSA_WIDTH)
    return _merge(xf, o_a.reshape(bsz * s, HG_WIDTH), o_b, proj.reshape(bsz * s, PROJ_WP),
                  wa, wb, wo, ln_g, ln_b, 512)


def kernel(x, w_in, hg_lb_logits, hg_norm_w, cmp_pos, cmp_w1, cmp_w2, rel_bias, w_branch_a, w_branch_b,
           w_out, ln1_g, ln1_b, ln2_g, ln2_b, ffn_w_gate, ffn_w_up, ffn_w_down, moe_router, moe_w_gate,
           moe_w_up, moe_w_down):
    bsz, s, d = x.shape
    depth = w_in.shape[0]
    p_lb = jax.nn.softmax(hg_lb_logits.astype(F32), axis=0)
    lbs = jnp.cumsum(p_lb, axis=0) - p_lb[0]
    n16 = s // CMP_STRIDE
    bias_c = _bias_tiles(rel_bias, s // ATT_TQ, ATT_TQ, n16,
                         lambda t, r, c: t * ATT_TQ + r - (c * CMP_STRIDE + CMP_LEN - 1))
    bias_t = _bias_tiles(rel_bias, BIAS_ND + 1, ATT_TQ, ATT_TK, lambda t, r, c: t * ATT_TQ + r - c)
    bias_tt = _bias_tiles(rel_bias, BIAS_ND + 1, ATT_TK, ATT_TQ, lambda t, r, c: t * ATT_TQ + c - r)
    f_pad = -(-D_FF // LANES) * LANES - D_FF
    xf = x.reshape(bsz * s, d)
    for l in range(depth):
        xf = _token_mixer(xf.reshape(bsz, s, d), _pack_w_in(w_in[l]), lbs[l][None], hg_norm_w[l][None],
                          cmp_pos[l], cmp_w1[l], cmp_w2[l], bias_c, bias_t, bias_tt,
                          w_branch_a[l].astype(BF16), w_branch_b[l].astype(BF16), w_out[l].astype(BF16),
                          ln1_g[l][None], ln1_b[l][None])
        if l % 2 == 0:
            wg = jnp.pad(ffn_w_gate[l // 2], ((0, 0), (0, f_pad))).astype(BF16)
            wu = jnp.pad(ffn_w_up[l // 2], ((0, 0), (0, f_pad))).astype(BF16)
            wd = jnp.pad(ffn_w_down[l // 2], ((0, f_pad), (0, 0))).astype(BF16)
            xf = _ffn(xf, wg, wu, wd, ln2_g[l][None], ln2_b[l][None], 512, wg.shape[1] // 2)
        else:
            xf = _moe(xf, moe_router[l // 2], moe_w_gate[l // 2], moe_w_up[l // 2], moe_w_down[l // 2],
                      ln2_g[l][None], ln2_b[l][None])
    return xf.reshape(bsz, s, d)
```

```python
import functools
import math

import jax
import jax.numpy as jnp
import numpy as np
from jax import lax
from jax.experimental import pallas as pl
from jax.experimental.pallas import tpu as pltpu

F32 = jnp.float32
BF16 = jnp.bfloat16

D_MODEL = 1024
DEPTH = 2
HG_HEADS = 4
HG_DK = 128
HG_WIDTH = HG_HEADS * HG_DK
HG_CHUNK = 64
HG_SUB = 16
HG_HALF = HG_SUB // 2
NSA_HEADS = 8
NSA_GROUPS = 2
NSA_HPG = NSA_HEADS // NSA_GROUPS
NSA_DH = 64
NSA_WIDTH = NSA_HEADS * NSA_DH
KV_WIDTH = NSA_GROUPS * NSA_DH
CMP_LEN = 32
CMP_STRIDE = 16
CMP_HIDDEN = 2 * NSA_DH
SEL_BLOCK = 64
N_SEL = 16
WINDOW = 512
FORCE_SCORE = 1e9
NEG_BIG = -1e30
REL_BUCKETS = 32
REL_MAX_DIST = 2048
D_FF = 2752
N_EXPERTS = 8
TOP_K = 2
D_FF_EXPERT = 3584
MOE_ROW_BLOCK = 1024
DN_ALPHA = (2 * DEPTH) ** 0.25
LN_EPS = 1e-5
IN_SIZES = (HG_WIDTH, HG_WIDTH, HG_WIDTH, HG_WIDTH, NSA_WIDTH,
            KV_WIDTH, KV_WIDTH, KV_WIDTH, KV_WIDTH, KV_WIDTH, KV_WIDTH,
            3 * NSA_HEADS, D_MODEL, D_MODEL)

LANES = 128
LOG2E = 1.0 / math.log(2.0)
COL_GA = 0
COL_HG = 2 * D_MODEL
COL_NQ = COL_HG + 4 * HG_WIDTH
COL_KV = COL_NQ + NSA_WIDTH
COL_NG = COL_KV + 6 * KV_WIDTH
PROJ_W = COL_NG + LANES
PROJ_TN = 512
PROJ_WP = -(-PROJ_W // PROJ_TN) * PROJ_TN

ATT_TQ = 256
ATT_TK = 512
BIAS_ND = -(-(REL_MAX_DIST + ATT_TK) // ATT_TQ)
VMEM_LIMIT = 48 * 1024 * 1024


def _cparams(sem):
    return pltpu.CompilerParams(dimension_semantics=sem, vmem_limit_bytes=VMEM_LIMIT)


def _proj_kernel(x_ref, w_ref, o_ref, xb_ref):
    @pl.when(pl.program_id(1) == 0)
    def _():
        xb_ref[...] = x_ref[...].astype(BF16)

    o_ref[...] = jnp.dot(xb_ref[...], w_ref[...], preferred_element_type=F32).astype(o_ref.dtype)


def _project(x, w, tm, tn):
    m, k = x.shape
    n = w.shape[1]
    return pl.pallas_call(
        _proj_kernel,
        grid=(m // tm, n // tn),
        in_specs=[pl.BlockSpec((tm, k), lambda i, j: (i, 0)),
                  pl.BlockSpec((k, tn), lambda i, j: (0, j))],
        out_specs=pl.BlockSpec((tm, tn), lambda i, j: (i, j)),
        out_shape=jax.ShapeDtypeStruct((m, n), BF16),
        scratch_shapes=[pltpu.VMEM((tm, k), BF16)],
        compiler_params=_cparams(("parallel", "arbitrary")),
    )(x, w)


def _layer_norm(y, g, b):
    mu = jnp.mean(y, -1, keepdims=True)
    yc = y - mu
    var = jnp.mean(yc * yc, -1, keepdims=True)
    return yc * lax.rsqrt(var + LN_EPS) * g + b


def _split3(x):
    parts = []
    for _ in range(3):
        part = x.astype(BF16)
        parts.append(part)
        x = x - part.astype(F32)
    return parts


def _cumsum_rows(tril, x):
    return sum(jnp.dot(tril, part, preferred_element_type=F32) for part in _split3(x))


def _hgrn_kernel(q_ref, f_ref, i_ref, g_ref, lb_ref, nw_ref, o_ref, st_ref, *, n_chunks):
    @pl.when(pl.program_id(1) == 0)
    def _():
        st_ref[...] = jnp.zeros_like(st_ref)

    c = HG_CHUNK
    nt = (((1,), (1,)), ((), ()))
    row = lax.broadcasted_iota(jnp.int32, (c, c), 0)
    col = lax.broadcasted_iota(jnp.int32, (c, c), 1)
    tril = (row >= col).astype(BF16)
    row1 = lax.broadcasted_iota(jnp.int32, (c, 1), 0)
    half_pos = row1 % HG_HALF
    second_half = row1 % HG_SUB >= HG_HALF
    same_sub = row // HG_SUB == col // HG_SUB

    w = HG_HEADS * HG_DK
    head_lanes = [slice(h * HG_DK, (h + 1) * HG_DK) for h in range(HG_HEADS)]

    def per_head(fn):
        return jnp.concatenate([fn(h, head_lanes[h]) for h in range(HG_HEADS)], axis=1)

    def shift_in_half(x, lag):
        return pltpu.roll(x.reshape(c // HG_HALF, HG_HALF, w), lag, 1).reshape(c, w)

    def chunk(ci, carry):
        r0 = pl.multiple_of(ci * c, c)
        q = q_ref[pl.ds(r0, c), :].astype(F32)
        z = f_ref[pl.ds(r0, c), :].astype(F32)
        v = i_ref[pl.ds(r0, c), :].astype(F32)
        g = g_ref[pl.ds(r0, c), :].astype(F32)
        k = (1.0 - lb_ref[...]) * jax.nn.sigmoid(-z)
        b = _cumsum_rows(tril, jnp.log1p(-k))
        b_last = b[c - 1:c, :]
        vb = v.astype(BF16)
        qe = (q * jnp.exp(b)).astype(BF16)
        o = per_head(lambda h, hl: lax.dot_general(qe[:, hl], st_ref[h].astype(BF16), nt,
                                                  preferred_element_type=F32))
        att_rows = [[jnp.zeros((HG_SUB, c), F32)] * HG_HEADS]
        for sb in range(1, c // HG_SUB):
            lo = sb * HG_SUB
            ref_b = b[lo - 1:lo, :]
            qt = (q[lo:lo + HG_SUB, :] * jnp.exp(b[lo:lo + HG_SUB, :] - ref_b)).astype(BF16)
            kt = (k * jnp.exp(jnp.where(row1 < lo, ref_b - b, -jnp.inf))).astype(BF16)
            att_rows.append([lax.dot_general(qt[:, hl], kt[:, hl], nt, preferred_element_type=F32)
                             for hl in head_lanes])
        mid = jnp.concatenate(
            [jnp.broadcast_to(b[lo + HG_HALF - 1:lo + HG_HALF, :], (HG_SUB, w)) for lo in range(0, c, HG_SUB)],
            axis=0)
        q2 = (q * jnp.exp(jnp.where(second_half, b - mid, -jnp.inf))).astype(BF16)
        k2 = (k * jnp.exp(jnp.where(second_half, -jnp.inf, mid - b))).astype(BF16)
        att_half = [lax.dot_general(q2[:, hl], k2[:, hl], nt, preferred_element_type=F32) for hl in head_lanes]

        def intra(h, hl):
            att = jnp.concatenate([rows[h] for rows in att_rows], axis=0) + jnp.where(same_sub, att_half[h], 0.0)
            return jnp.dot(att.astype(BF16), vb[:, hl], preferred_element_type=F32)

        o = o + per_head(intra)
        for lag in range(HG_HALF):
            if lag == 0:
                ks, bs, vs = k, b, v
            else:
                ks, bs, vs = shift_in_half(k, lag), shift_in_half(b, lag), shift_in_half(v, lag)
            valid = half_pos >= lag
            prod = q * ks * jnp.exp(jnp.where(valid, b - bs, 0.0))
            a = per_head(lambda h, hl: jnp.broadcast_to(
                jnp.sum(prod[:, hl], axis=1, keepdims=True), (c, HG_DK)))
            o = o + jnp.where(valid, a, 0.0) * vs
        khat = (k * jnp.exp(b_last - b)).astype(BF16)
        decay = jnp.exp(b_last)
        for h, hl in enumerate(head_lanes):
            st_ref[h] = st_ref[h] * decay[:, hl] + lax.dot_general(
                vb[:, hl], khat[:, hl], (((0,), (0,)), ((), ())), preferred_element_type=F32)
        sq = o * o
        ms = per_head(lambda h, hl: jnp.broadcast_to(jnp.mean(sq[:, hl], -1, keepdims=True), (c, HG_DK)))
        o = o * lax.rsqrt(ms + 1e-6)
        o_ref[pl.ds(r0, c), :] = (o * nw_ref[...] * (g * jax.nn.sigmoid(g))).astype(o_ref.dtype)
        return carry

    lax.fori_loop(0, n_chunks, chunk, 0)


def _hgrn(proj, lb, norm_w, ts):
    bsz, s, _ = proj.shape
    col0 = COL_HG // HG_WIDTH

    def col_spec(n):
        return pl.BlockSpec((None, ts, HG_WIDTH), lambda b, t: (b, t, col0 + n))

    head_spec = pl.BlockSpec((1, HG_WIDTH), lambda b, t: (0, 0))
    return pl.pallas_call(
        functools.partial(_hgrn_kernel, n_chunks=ts // HG_CHUNK),
        grid=(bsz, s // ts),
        in_specs=[col_spec(0), col_spec(1), col_spec(2), col_spec(3), head_spec, head_spec],
        out_specs=pl.BlockSpec((None, ts, HG_WIDTH), lambda b, t: (b, t, 0)),
        out_shape=jax.ShapeDtypeStruct((bsz, s, HG_WIDTH), BF16),
        scratch_shapes=[pltpu.VMEM((HG_HEADS, HG_DK, HG_DK), F32)],
        compiler_params=_cparams(("parallel", "arbitrary")),
    )(proj, proj, proj, proj, lb, norm_w)


def _compress_kernel(a_ref, pos_ref, w1_ref, w2_ref, o_ref):
    half = CMP_STRIDE * NSA_DH
    a = a_ref[...].astype(F32)
    n = a.shape[0]
    a1 = (a + pos_ref[:, :half]).astype(BF16)
    a2 = (a + pos_ref[:, half:]).astype(BF16)
    y1 = jnp.dot(a1, w1_ref[:half, :], preferred_element_type=F32)
    y2 = jnp.dot(a2, w1_ref[half:, :], preferred_element_type=F32)
    hid = jax.nn.gelu(y1 + pltpu.roll(y2, n - 1, 0))
    o_ref[...] = jnp.dot(hid.astype(BF16), w2_ref[...], preferred_element_type=F32).astype(o_ref.dtype)


def _compress(a, pos, w1, w2):
    _, bsz, g, n, width = a.shape
    return pl.pallas_call(
        _compress_kernel,
        grid=(2, bsz, g),
        in_specs=[pl.BlockSpec((None, None, None, n, width), lambda c, b, h: (c, b, h, 0, 0)),
                  pl.BlockSpec((None, 1, 2 * width), lambda c, b, h: (c, 0, 0)),
                  pl.BlockSpec((None, 2 * width, CMP_HIDDEN), lambda c, b, h: (c, 0, 0)),
                  pl.BlockSpec((None, CMP_HIDDEN, NSA_DH), lambda c, b, h: (c, 0, 0))],
        out_specs=pl.BlockSpec((None, None, None, n, NSA_DH), lambda c, b, h: (c, b, h, 0, 0)),
        out_shape=jax.ShapeDtypeStruct((2, bsz, g, n, NSA_DH), BF16),
        compiler_params=_cparams(("parallel", "parallel", "parallel")),
    )(a, pos, w1, w2)


def _bucket_of(dist):
    n = jnp.maximum(dist, 0)
    exact = REL_BUCKETS // 2
    large = exact + (jnp.log(jnp.maximum(n, exact).astype(F32) / exact)
                     / math.log(REL_MAX_DIST / exact) * (REL_BUCKETS - exact)).astype(jnp.int32)
    return jnp.where(n < exact, n, jnp.minimum(large, REL_BUCKETS - 1))


def _bias_kernel(tab_ref, o_ref, *, dist_fn):
    rows, cols = o_ref.shape[2], o_ref.shape[3]
    row = lax.broadcasted_iota(jnp.int32, (rows, cols), 0)
    col = lax.broadcasted_iota(jnp.int32, (rows, cols), 1)
    bucket = _bucket_of(dist_fn(pl.program_id(0), row, col))
    for h in range(NSA_HEADS):
        tab = jnp.broadcast_to(tab_ref[h:h + 1, :] * LOG2E, (rows, LANES))
        for c0 in range(0, cols, LANES):
            o_ref[h, 0, :, c0:c0 + LANES] = jnp.take_along_axis(tab, bucket[:, c0:c0 + LANES], axis=1)


def _bias_tiles(rel_bias, n_tiles, rows, cols, dist_fn):
    tab = jnp.pad(rel_bias.T, ((0, 0), (0, LANES - REL_BUCKETS)))
    return pl.pallas_call(
        functools.partial(_bias_kernel, dist_fn=dist_fn),
        grid=(n_tiles,),
        in_specs=[pl.BlockSpec((NSA_HEADS, LANES), lambda t: (0, 0))],
        out_specs=pl.BlockSpec((NSA_HEADS, 1, rows, cols), lambda t: (0, t, 0, 0)),
        out_shape=jax.ShapeDtypeStruct((NSA_HEADS, n_tiles, rows, cols), F32),
        compiler_params=_cparams(("parallel",)),
    )(tab)


def _cmp_kernel(q_ref, k_ref, v_ref, bias_ref, c2s_ref, gl_ref, o_ref, pen_ref, *, tq):
    p = NSA_HPG
    t0 = pl.program_id(1) * tq
    bsz, ncp = k_ref.shape[0], k_ref.shape[1]
    bias = bias_ref[:, 0]
    tpos = t0 + lax.broadcasted_iota(jnp.int32, (tq, ncp), 0)
    cend = lax.broadcasted_iota(jnp.int32, (tq, ncp), 1) * CMP_STRIDE + (CMP_LEN - 1)
    mask = (tpos >= cend)[None]
    lane = lax.broadcasted_iota(jnp.int32, (tq, LANES), 1)
    cur = (t0 + lax.broadcasted_iota(jnp.int32, (tq, LANES), 0)) // SEL_BLOCK
    forced = (lane == 0) | (lane == cur) | (lane == cur - 1)
    visible = lane <= cur
    batch = range(bsz)
    nt = (((1,), (1,)), ((), ()))
    lgs = [lax.dot_general(q_ref[b].reshape(p * tq, NSA_DH), k_ref[b], nt, preferred_element_type=F32)
           for b in batch]
    lgs = [jnp.where(mask, lg.reshape(p, tq, ncp) + bias, NEG_BIG) for lg in lgs]
    es = [jnp.where(mask, jnp.exp2(lg - jnp.max(lg, -1, keepdims=True)), 0.0) for lg in lgs]
    pcs = [e / jnp.maximum(jnp.sum(e, -1, keepdims=True), 1e-30) for e in es]
    os = [jnp.dot(pcs[b].reshape(p * tq, ncp).astype(BF16), v_ref[b], preferred_element_type=F32) for b in batch]
    imps = [sum(jnp.dot(part, c2s_ref[...], preferred_element_type=F32) for part in _split3(jnp.sum(pc, axis=0)))
            for pc in pcs]
    for b in batch:
        o_ref[b] = os[b].reshape(p, tq, NSA_DH) * jax.nn.sigmoid(gl_ref[b, :, :, 0:1].astype(F32))
    scores = [jnp.where(forced, FORCE_SCORE, jnp.where(visible, imp, NEG_BIG)) for imp in imps]
    members = [jnp.zeros((tq, LANES), F32)] * bsz
    for _ in range(N_SEL):
        hits = [lane == jnp.argmax(score, axis=-1, keepdims=True) for score in scores]
        members = [jnp.where(hit, 1.0, member) for hit, member in zip(hits, members)]
        scores = [jnp.where(hit, -jnp.inf, score) for hit, score in zip(hits, scores)]
    for b in batch:
        pen_ref[b] = jnp.where((members[b] > 0.5) & visible, 0.0, NEG_BIG).astype(pen_ref.dtype)


def _cmp_branch(q5, kc, vc, bias_c, c2s, gl5, tq):
    bsz, g, p, s, dh = q5.shape
    ncp = kc.shape[2]
    qmap = lambda h, i: (0, h, 0, i, 0)
    return pl.pallas_call(
        functools.partial(_cmp_kernel, tq=tq),
        grid=(g, s // tq),
        in_specs=[pl.BlockSpec((bsz, None, p, tq, dh), qmap),
                  pl.BlockSpec((bsz, None, ncp, dh), lambda h, i: (0, h, 0, 0)),
                  pl.BlockSpec((bsz, None, ncp, dh), lambda h, i: (0, h, 0, 0)),
                  pl.BlockSpec((p, 1, tq, ncp), lambda h, i: (h, i, 0, 0)),
                  pl.BlockSpec((ncp, LANES), lambda h, i: (0, 0)),
                  pl.BlockSpec((bsz, None, p, tq, 3), qmap)],
        out_specs=[pl.BlockSpec((bsz, None, p, tq, dh), qmap),
                   pl.BlockSpec((bsz, None, tq, LANES), lambda h, i: (0, h, i, 0))],
        out_shape=[jax.ShapeDtypeStruct((bsz, g, p, s, dh), F32),
                   jax.ShapeDtypeStruct((bsz, g, s, LANES), BF16)],
        compiler_params=_cparams(("parallel", "parallel")),
    )(q5, kc, vc, bias_c, c2s, gl5)


def _win_kernel(*refs, tq, n_back):
    nk = n_back + 1
    q_ref, k_refs, v_refs, b_refs = refs[0], refs[1:1 + nk], refs[1 + nk:1 + 2 * nk], refs[1 + 2 * nk:1 + 3 * nk]
    gl_ref, acc_ref, o_ref = refs[1 + 3 * nk:]
    p = NSA_HPG
    bsz = q_ref.shape[0]
    t0 = pl.program_id(1) * tq
    row = lax.broadcasted_iota(jnp.int32, (tq, tq), 0)
    col = lax.broadcasted_iota(jnp.int32, (tq, tq), 1)
    masks = [((row - col + d * tq >= 0) & (row - col + d * tq < WINDOW) & (t0 - d * tq >= 0))[None]
             for d in range(nk)]
    biases = [b_ref[:, 0] for b_ref in b_refs]
    nt = (((1,), (1,)), ((), ()))
    batch = range(bsz)
    qs = [q_ref[b].reshape(p * tq, NSA_DH) for b in batch]
    lgs = [[lax.dot_general(qs[b], k_ref[b], nt, preferred_element_type=F32).reshape(p, tq, tq)
            for k_ref in k_refs] for b in batch]
    lgs = [[jnp.where(mask, lg + bias, NEG_BIG) for lg, mask, bias in zip(lgs[b], masks, biases)] for b in batch]
    ms = [functools.reduce(jnp.maximum, [jnp.max(lg, -1, keepdims=True) for lg in lgs[b]]) for b in batch]
    es = [[jnp.exp2(lg - ms[b]).reshape(p * tq, tq).astype(BF16) for lg in lgs[b]] for b in batch]
    os = [sum(jnp.dot(e, v_ref[b], preferred_element_type=F32) for e, v_ref in zip(es[b], v_refs)) for b in batch]
    for b in batch:
        o = (os[b] / pltpu.roll(os[b], NSA_DH, 1))[:, :NSA_DH].reshape(p, tq, NSA_DH)
        o_ref[b] = acc_ref[b] + o * jax.nn.sigmoid(gl_ref[b, :, :, 2:3].astype(F32))


def _win_branch(q5, kw, vw_aug, bias_t, gl5, acc, tq):
    bsz, g, p, s, dh = q5.shape
    n_back = WINDOW // tq
    back = [lambda h, i, d=d: (0, h, jnp.maximum(i - d, 0), 0) for d in range(n_back + 1)]
    qmap = lambda h, i: (0, h, 0, i, 0)
    in_specs = ([pl.BlockSpec((bsz, None, p, tq, dh), qmap)]
                + [pl.BlockSpec((bsz, None, tq, dh), m) for m in back]
                + [pl.BlockSpec((bsz, None, tq, LANES), m) for m in back]
                + [pl.BlockSpec((p, 1, tq, tq), lambda h, i, d=d: (h, d, 0, 0)) for d in range(n_back + 1)]
                + [pl.BlockSpec((bsz, None, p, tq, 3), qmap), pl.BlockSpec((bsz, None, p, tq, dh), qmap)])
    return pl.pallas_call(
        functools.partial(_win_kernel, tq=tq, n_back=n_back),
        grid=(g, s // tq),
        in_specs=in_specs,
        out_specs=pl.BlockSpec((bsz, None, p, tq, dh), qmap),
        out_shape=jax.ShapeDtypeStruct((bsz, g, p, s, dh), F32),
        input_output_aliases={len(in_specs) - 1: 0},
        compiler_params=_cparams(("parallel", "parallel")),
    )(q5, *([kw] * (n_back + 1)), *([vw_aug] * (n_back + 1)), *([bias_t] * (n_back + 1)), gl5, acc)


def _sel_kernel(it_ref, jt_ref, q_ref, k_ref, v_ref, bias_ref, gl_ref, acc_ref, o_ref, m_sc, a_sc,
                *, tq, tk, bsz):
    p = NSA_HPG
    nq = p * tq
    step = pl.program_id(1)
    i = it_ref[step]
    j = jt_ref[step]
    t0 = i * tq
    s0 = j * tk
    last = s0 + tk > t0

    @pl.when(j == 0)
    def _():
        m_sc[...] = jnp.full_like(m_sc, NEG_BIG)
        a_sc[...] = jnp.zeros_like(a_sc)

    def accumulate(on_diagonal):
        nt = (((1,), (1,)), ((), ()))
        bias = bias_ref[:, 0].reshape(nq, tk)
        if on_diagonal:
            causal = (t0 + lax.broadcasted_iota(jnp.int32, (tq, tk), 0)
                      >= s0 + lax.broadcasted_iota(jnp.int32, (tq, tk), 1))
            causal = jnp.concatenate([causal] * p, axis=0)
        for b in range(bsz):
            lg = lax.dot_general(q_ref[b].reshape(nq, 2 * LANES), k_ref[b], nt, preferred_element_type=F32) + bias
            if on_diagonal:
                lg = jnp.where(causal, lg, NEG_BIG)
            m_old = m_sc[b]
            m_new = jnp.maximum(m_old, jnp.max(lg, -1, keepdims=True))
            e = jnp.exp2(lg - jnp.concatenate([m_new] * (tk // LANES), axis=1))
            a_sc[b] = (jnp.exp2(m_old - m_new) * a_sc[b]
                       + jnp.dot(e.astype(BF16), v_ref[b], preferred_element_type=F32))
            m_sc[b] = m_new

    @pl.when(jnp.logical_not(last))
    def _():
        accumulate(False)

    @pl.when(last)
    def _():
        accumulate(True)
        gate = jax.nn.sigmoid(gl_ref[:, :, :, 1:2].astype(F32))
        a = a_sc[...]
        o = (a / pltpu.roll(a, NSA_DH, 2))[:, :, :NSA_DH].reshape(bsz, p, tq, NSA_DH)
        o_ref[...] = (acc_ref[...] + o * gate).astype(o_ref.dtype)


def _sel_branch(q_aug, k_aug, v_aug, bias_t, gl5, acc, tq, tk):
    bsz, g, p, s, wq = q_aug.shape
    dh = acc.shape[-1]
    r = tk // tq
    pairs = [(i, j) for i in range(s // tq) for j in range(i // r + 1)]
    it = jnp.asarray([ij[0] for ij in pairs], jnp.int32)
    jt = jnp.asarray([ij[1] for ij in pairs], jnp.int32)
    nd = bias_t.shape[1] - 1
    qmap = lambda h, t, it, jt: (0, h, 0, it[t], 0)
    kmap = lambda h, t, it, jt: (0, h, jt[t], 0)
    grid_spec = pltpu.PrefetchScalarGridSpec(
        num_scalar_prefetch=2,
        grid=(g, len(pairs)),
        in_specs=[pl.BlockSpec((bsz, None, p, tq, wq), qmap),
                  pl.BlockSpec((bsz, None, tk, wq), kmap),
                  pl.BlockSpec((bsz, None, tk, LANES), kmap),
                  pl.BlockSpec((p, 1, tq, tk),
                               lambda h, t, it, jt: (h, jnp.minimum(it[t] - r * jt[t], nd), 0, 0)),
                  pl.BlockSpec((bsz, None, p, tq, 3), qmap),
                  pl.BlockSpec((bsz, None, p, tq, dh), qmap)],
        out_specs=pl.BlockSpec((bsz, None, p, tq, dh), qmap),
        scratch_shapes=[pltpu.VMEM((bsz, p * tq, LANES), F32),
                        pltpu.VMEM((bsz, p * tq, LANES), F32)],
    )
    return pl.pallas_call(
        functools.partial(_sel_kernel, tq=tq, tk=tk, bsz=bsz),
        grid_spec=grid_spec,
        out_shape=jax.ShapeDtypeStruct((bsz, g, p, s, dh), BF16),
        compiler_params=_cparams(("parallel", "arbitrary")),
    )(it, jt, q_aug, k_aug, v_aug, bias_t, gl5, acc)


def _merge_kernel(x_ref, oa_ref, ob_ref, ga_ref, gb_ref, wa_ref, wb_ref, wo_ref, g_ref, b_ref, o_ref):
    ya = jnp.dot(oa_ref[...].astype(BF16), wa_ref[...], preferred_element_type=F32)
    yb = jnp.dot(ob_ref[...].astype(BF16), wb_ref[...], preferred_element_type=F32)
    y = jax.nn.sigmoid(ga_ref[...].astype(F32)) * ya + jax.nn.sigmoid(gb_ref[...].astype(F32)) * yb
    mix = jnp.dot(y.astype(BF16), wo_ref[...], preferred_element_type=F32)
    o_ref[...] = _layer_norm(DN_ALPHA * x_ref[...] + mix, g_ref[...], b_ref[...])


def _merge(x, o_a, o_b, proj, wa, wb, wo, g, b, tm):
    t, d = x.shape
    nga = COL_GA // d
    row = lambda i: (i, 0)
    const = lambda i: (0, 0)
    return pl.pallas_call(
        _merge_kernel,
        grid=(t // tm,),
        in_specs=[pl.BlockSpec((tm, d), row),
                  pl.BlockSpec((tm, HG_WIDTH), row),
                  pl.BlockSpec((tm, NSA_WIDTH), row),
                  pl.BlockSpec((tm, d), lambda i: (i, nga)),
                  pl.BlockSpec((tm, d), lambda i: (i, nga + 1)),
                  pl.BlockSpec((HG_WIDTH, d), const),
                  pl.BlockSpec((NSA_WIDTH, d), const),
                  pl.BlockSpec((d, d), const),
                  pl.BlockSpec((1, d), const),
                  pl.BlockSpec((1, d), const)],
        out_specs=pl.BlockSpec((tm, d), row),
        out_shape=jax.ShapeDtypeStruct((t, d), F32),
        compiler_params=_cparams(("parallel",)),
    )(x, o_a, o_b, proj, proj, wa, wb, wo, g, b)


def _swiglu_step(xb, wg_ref, wu_ref, wd_ref):
    hg = jnp.dot(xb, wg_ref[...], preferred_element_type=F32)
    hu = jnp.dot(xb, wu_ref[...], preferred_element_type=F32)
    h = (hg * jax.nn.sigmoid(hg)) * hu
    return jnp.dot(h.astype(BF16), wd_ref[...], preferred_element_type=F32)


def _ffn_kernel(x_ref, wg_ref, wu_ref, wd_ref, g_ref, b_ref, o_ref, xb_ref, acc_ref):
    j = pl.program_id(1)

    @pl.when(j == 0)
    def _():
        xb_ref[...] = x_ref[...].astype(BF16)
        acc_ref[...] = jnp.zeros_like(acc_ref)

    acc_ref[...] += _swiglu_step(xb_ref[...], wg_ref, wu_ref, wd_ref)

    @pl.when(j == pl.num_programs(1) - 1)
    def _():
        o_ref[...] = _layer_norm(DN_ALPHA * x_ref[...] + acc_ref[...], g_ref[...], b_ref[...])


def _ffn(x, wg, wu, wd, g, b, tm, tf):
    t, d = x.shape
    f = wg.shape[1]
    return pl.pallas_call(
        _ffn_kernel,
        grid=(t // tm, f // tf),
        in_specs=[pl.BlockSpec((tm, d), lambda i, j: (i, 0)),
                  pl.BlockSpec((d, tf), lambda i, j: (0, j)),
                  pl.BlockSpec((d, tf), lambda i, j: (0, j)),
                  pl.BlockSpec((tf, d), lambda i, j: (j, 0)),
                  pl.BlockSpec((1, d), lambda i, j: (0, 0)),
                  pl.BlockSpec((1, d), lambda i, j: (0, 0))],
        out_specs=pl.BlockSpec((tm, d), lambda i, j: (i, 0)),
        out_shape=jax.ShapeDtypeStruct((t, d), F32),
        scratch_shapes=[pltpu.VMEM((tm, d), BF16), pltpu.VMEM((tm, d), F32)],
        compiler_params=_cparams(("parallel", "arbitrary")),
    )(x, wg, wu, wd, g, b)


def _router_kernel(x_ref, w_ref, o_ref):
    logits = jnp.dot(x_ref[...], w_ref[...], preferred_element_type=F32, precision=lax.Precision.HIGHEST)
    lane = lax.broadcasted_iota(jnp.int32, logits.shape, 1).astype(F32)
    logits = jnp.where(lane < N_EXPERTS, logits, -jnp.inf)
    v1 = jnp.max(logits, -1, keepdims=True)
    e1 = jnp.min(jnp.where(logits == v1, lane, float(LANES)), -1, keepdims=True)
    rest = jnp.where(lane == e1, -jnp.inf, logits)
    v2 = jnp.max(rest, -1, keepdims=True)
    e2 = jnp.min(jnp.where(rest == v2, lane, float(LANES)), -1, keepdims=True)
    x2 = jnp.exp(v2 - v1)
    den = 1.0 + x2
    o_ref[...] = jnp.where(lane == 0, e1, jnp.where(lane == 1, e2, jnp.where(
        lane == 2, 1.0 / den, jnp.where(lane == 3, x2 / den, 0.0))))


def _router(x, w, tm):
    t, d = x.shape
    return pl.pallas_call(
        _router_kernel,
        grid=(t // tm,),
        in_specs=[pl.BlockSpec((tm, d), lambda i: (i, 0)), pl.BlockSpec((d, LANES), lambda i: (0, 0))],
        out_specs=pl.BlockSpec((tm, LANES), lambda i: (i, 0)),
        out_shape=jax.ShapeDtypeStruct((t, LANES), F32),
        compiler_params=_cparams(("parallel",)),
    )(x, w)


def _expert_kernel(be_ref, x_ref, wg_ref, wu_ref, wd_ref, o_ref, acc_ref):
    j = pl.program_id(1)

    @pl.when(j == 0)
    def _():
        acc_ref[...] = jnp.zeros_like(acc_ref)

    xb = x_ref[...].astype(BF16)
    hg = jnp.dot(xb, wg_ref[...].astype(BF16), preferred_element_type=F32)
    hu = jnp.dot(xb, wu_ref[...].astype(BF16), preferred_element_type=F32)
    h = (hg * jax.nn.sigmoid(hg)) * hu
    acc_ref[...] += jnp.dot(h.astype(BF16), wd_ref[...].astype(BF16), preferred_element_type=F32)

    @pl.when(j == pl.num_programs(1) - 1)
    def _():
        o_ref[...] = acc_ref[...].astype(o_ref.dtype)


def _experts(blk_e, xs, wg, wu, wd, tf):
    rows, d = xs.shape
    f = wg.shape[2]
    tm = MOE_ROW_BLOCK
    grid_spec = pltpu.PrefetchScalarGridSpec(
        num_scalar_prefetch=1,
        grid=(rows // tm, f // tf),
        in_specs=[pl.BlockSpec((tm, d), lambda i, j, be: (i, 0)),
                  pl.BlockSpec((None, d, tf), lambda i, j, be: (be[i], 0, j)),
                  pl.BlockSpec((None, d, tf), lambda i, j, be: (be[i], 0, j)),
                  pl.BlockSpec((None, tf, d), lambda i, j, be: (be[i], j, 0))],
        out_specs=pl.BlockSpec((tm, d), lambda i, j, be: (i, 0)),
        scratch_shapes=[pltpu.VMEM((tm, d), F32)],
    )
    return pl.pallas_call(
        _expert_kernel,
        grid_spec=grid_spec,
        out_shape=jax.ShapeDtypeStruct((rows, d), F32),
        compiler_params=_cparams(("parallel", "arbitrary")),
    )(blk_e, xs, wg, wu, wd)


def _combine_kernel(x_ref, y1_ref, y2_ref, gt_ref, g_ref, b_ref, o_ref):
    f = y1_ref[...].astype(F32) * gt_ref[:, 2:3] + y2_ref[...].astype(F32) * gt_ref[:, 3:4]
    o_ref[...] = _layer_norm(DN_ALPHA * x_ref[...] + f, g_ref[...], b_ref[...])


def _combine(x, y1, y2, route, g, b, tm):
    t, d = x.shape
    row = lambda i: (i, 0)
    const = lambda i: (0, 0)
    return pl.pallas_call(
        _combine_kernel,
        grid=(t // tm,),
        in_specs=[pl.BlockSpec((tm, d), row), pl.BlockSpec((tm, d), row), pl.BlockSpec((tm, d), row),
                  pl.BlockSpec((tm, LANES), row), pl.BlockSpec((1, d), const), pl.BlockSpec((1, d), const)],
        out_specs=pl.BlockSpec((tm, d), row),
        out_shape=jax.ShapeDtypeStruct((t, d), F32),
        compiler_params=_cparams(("parallel",)),
    )(x, y1, y2, route, g, b)


def _moe(x, w_router, wg, wu, wd, g, b):
    t, d = x.shape
    tk_ = t * TOP_K
    route = _router(x, jnp.pad(w_router, ((0, 0), (0, LANES - N_EXPERTS))), 512)
    flat_e = route[:, :TOP_K].astype(jnp.int32).reshape(-1)
    onehot = (flat_e[:, None] == jnp.arange(N_EXPERTS)[None, :]).astype(jnp.int32)
    csum = jnp.cumsum(onehot, axis=0)
    counts = csum[-1]
    rank = jnp.sum(onehot * csum, axis=1) - 1
    padded = (counts + MOE_ROW_BLOCK - 1) // MOE_ROW_BLOCK * MOE_ROW_BLOCK
    pend = jnp.cumsum(padded)
    dest = (pend - padded)[flat_e] + rank
    n_blocks = -(-(tk_ + N_EXPERTS * (MOE_ROW_BLOCK - 1)) // MOE_ROW_BLOCK)
    n_rows = n_blocks * MOE_ROW_BLOCK
    row_tok = (jnp.arange(n_rows, dtype=jnp.int32) % t).at[dest].set(jnp.arange(tk_, dtype=jnp.int32) // TOP_K)
    blk_e = jnp.minimum(jnp.searchsorted(pend, jnp.arange(n_blocks) * MOE_ROW_BLOCK, side='right'),
                        N_EXPERTS - 1).astype(jnp.int32)
    xs = x[row_tok]
    ys = _experts(blk_e, xs, wg, wu, wd, 512)
    dest2 = dest.reshape(t, TOP_K)
    return _combine(x, ys[dest2[:, 0]], ys[dest2[:, 1]], route, g, b, 512)


def _pack_w_in(w_in):
    offs = np.concatenate([[0], np.cumsum(IN_SIZES)])
    seg = [w_in[:, offs[j]:offs[j + 1]] for j in range(len(IN_SIZES))]
    seg[4] = seg[4] * (NSA_DH ** -0.5 * LOG2E)
    seg[11] = jnp.pad(seg[11], ((0, 0), (0, LANES - 3 * NSA_HEADS)))
    seg = seg[12:14] + seg[0:12]
    packed = jnp.concatenate(seg + [jnp.zeros((w_in.shape[0], PROJ_WP - PROJ_W), w_in.dtype)], axis=1)
    return packed.astype(BF16)


def _cmp_to_sel(n_cmp_pad, n_cmp):
    cs = np.arange(n_cmp_pad)[:, None] * CMP_STRIDE
    ss = np.arange(LANES)[None, :] * SEL_BLOCK
    overlap = np.clip(np.minimum(cs + CMP_LEN, ss + SEL_BLOCK) - np.maximum(cs, ss), 0, None) / CMP_LEN
    overlap[n_cmp:] = 0.0
    return jnp.asarray(overlap, BF16)


def _token_mixer(x, w_in_p, lb, hg_norm_w, cmp_pos, cmp_w1, cmp_w2, bias_c, bias_t, wa, wb, wo, ln_g, ln_b):
    bsz, s, d = x.shape
    g, p, dh = NSA_GROUPS, NSA_HPG, NSA_DH
    xf = x.reshape(bsz * s, d)
    proj = _project(xf, w_in_p, 1024, PROJ_TN).reshape(bsz, s, PROJ_WP)
    o_a = _hgrn(proj, lb, hg_norm_w, 512)

    def heads(c0, width):
        return proj[:, :, c0:c0 + width]

    q5 = heads(COL_NQ, NSA_WIDTH).astype(BF16).reshape(bsz, s, g, p, dh).transpose(0, 2, 3, 1, 4)
    kv = heads(COL_KV, 6 * KV_WIDTH).reshape(bsz, s, 6, g, dh)
    n16 = s // CMP_STRIDE
    a = kv[:, :, 0:2].reshape(bsz, n16, CMP_STRIDE, 2, g, dh).transpose(3, 0, 4, 1, 2, 5)
    a = a.reshape(2, bsz, g, n16, CMP_STRIDE * dh)
    kvc = _compress(a, cmp_pos.reshape(2, 1, CMP_LEN * dh), cmp_w1.astype(BF16), cmp_w2.astype(BF16))
    kvh = kv[:, :, 2:6].astype(BF16).transpose(2, 0, 3, 1, 4)
    gl5 = heads(COL_NG, 3 * NSA_HEADS).reshape(bsz, s, g, p, 3).transpose(0, 2, 3, 1, 4)
    n_cmp = (s - CMP_LEN) // CMP_STRIDE + 1
    acc, pen = _cmp_branch(q5, kvc[0], kvc[1], bias_c, _cmp_to_sel(n16, n_cmp), gl5, ATT_TQ)
    ones = jnp.ones((bsz, g, s, LANES - dh), BF16)
    acc = _win_branch(q5, kvh[2], jnp.concatenate([kvh[3], ones], axis=-1), bias_t, gl5, acc, ATT_TQ)
    q_aug = jnp.concatenate([jnp.broadcast_to(pen[:, :, None], (bsz, g, p, s, LANES)), q5,
                             jnp.zeros((bsz, g, p, s, LANES - dh), BF16)], axis=-1)
    block_of_key = (np.arange(s)[:, None] // SEL_BLOCK == np.arange(LANES)[None, :])
    k_aug = jnp.concatenate([jnp.broadcast_to(jnp.asarray(block_of_key, BF16), (bsz, g, s, LANES)), kvh[0],
                             jnp.zeros((bsz, g, s, LANES - dh), BF16)], axis=-1)
    v_aug = jnp.concatenate([kvh[1], ones], axis=-1)
    o_b = _sel_branch(q_aug, k_aug, v_aug, bias_t, gl5, acc, ATT_TQ, ATT_TK)
    o_b = o_b.transpose(0, 3, 1, 2, 4).reshape(bsz * s, NSA_WIDTH)
    return _merge(xf, o_a.reshape(bsz * s, HG_WIDTH), o_b, proj.reshape(bsz * s, PROJ_WP),
                  wa, wb, wo, ln_g, ln_b, 512)


def kernel(x, w_in, hg_lb_logits, hg_norm_w, cmp_pos, cmp_w1, cmp_w2, rel_bias, w_branch_a, w_branch_b,
           w_out, ln1_g, ln1_b, ln2_g, ln2_b, ffn_w_gate, ffn_w_up, ffn_w_down, moe_router, moe_w_gate,
           moe_w_up, moe_w_down):
    bsz, s, d = x.shape
    depth = w_in.shape[0]
    p_lb = jax.nn.softmax(hg_lb_logits.astype(F32), axis=0)
    lbs = jnp.cumsum(p_lb, axis=0) - p_lb[0]
    n16 = s // CMP_STRIDE
    bias_c = _bias_tiles(rel_bias, s // ATT_TQ, ATT_TQ, n16,
                         lambda t, r, c: t * ATT_TQ + r - (c * CMP_STRIDE + CMP_LEN - 1))
    bias_t = _bias_tiles(rel_bias, BIAS_ND + 1, ATT_TQ, ATT_TK, lambda t, r, c: t * ATT_TQ + r - c)
    f_pad = -(-D_FF // LANES) * LANES - D_FF
    xf = x.reshape(bsz * s, d)
    for l in range(depth):
        xf = _token_mixer(xf.reshape(bsz, s, d), _pack_w_in(w_in[l]), lbs[l][None], hg_norm_w[l][None],
                          cmp_pos[l], cmp_w1[l], cmp_w2[l], bias_c, bias_t,
                          w_branch_a[l].astype(BF16), w_branch_b[l].astype(BF16), w_out[l].astype(BF16),
                          ln1_g[l][None], ln1_b[l][None])
        if l % 2 == 0:
            wg = jnp.pad(ffn_w_gate[l // 2], ((0, 0), (0, f_pad))).astype(BF16)
            wu = jnp.pad(ffn_w_up[l // 2], ((0, 0), (0, f_pad))).astype(BF16)
            wd = jnp.pad(ffn_w_down[l // 2], ((0, f_pad), (0, 0))).astype(BF16)
            xf = _ffn(xf, wg, wu, wd, ln2_g[l][None], ln2_b[l][None], 512, wg.shape[1] // 2)
        else:
            xf = _moe(xf, moe_router[l // 2], moe_w_gate[l // 2], moe_w_up[l // 2], moe_w_down[l // 2],
                      ln2_g[l][None], ln2_b[l][None])
    return xf.reshape(bsz, s, d)
```

```python
import functools
import math

import jax
import jax.numpy as jnp
import numpy as np
from jax import lax
from jax.experimental import pallas as pl
from jax.experimental.pallas import tpu as pltpu

F32 = jnp.float32
BF16 = jnp.bfloat16

D_MODEL = 1024
DEPTH = 2
HG_HEADS = 4
HG_DK = 128
HG_WIDTH = HG_HEADS * HG_DK
HG_CHUNK = 64
HG_SUB = 16
HG_HALF = HG_SUB // 2
NSA_HEADS = 8
NSA_GROUPS = 2
NSA_HPG = NSA_HEADS // NSA_GROUPS
NSA_DH = 64
NSA_WIDTH = NSA_HEADS * NSA_DH
KV_WIDTH = NSA_GROUPS * NSA_DH
CMP_LEN = 32
CMP_STRIDE = 16
CMP_HIDDEN = 2 * NSA_DH
SEL_BLOCK = 64
N_SEL = 16
WINDOW = 512
FORCE_SCORE = 1e9
NEG_BIG = -1e30
REL_BUCKETS = 32
REL_MAX_DIST = 2048
D_FF = 2752
N_EXPERTS = 8
TOP_K = 2
D_FF_EXPERT = 3584
MOE_ROW_BLOCK = 1024
DN_ALPHA = (2 * DEPTH) ** 0.25
LN_EPS = 1e-5
IN_SIZES = (HG_WIDTH, HG_WIDTH, HG_WIDTH, HG_WIDTH, NSA_WIDTH,
            KV_WIDTH, KV_WIDTH, KV_WIDTH, KV_WIDTH, KV_WIDTH, KV_WIDTH,
            3 * NSA_HEADS, D_MODEL, D_MODEL)

LANES = 128
LOG2E = 1.0 / math.log(2.0)
COL_GA = 0
COL_HG = 2 * D_MODEL
COL_NQ = COL_HG + 4 * HG_WIDTH
COL_KV = COL_NQ + NSA_WIDTH
COL_NG = COL_KV + 6 * KV_WIDTH
PROJ_W = COL_NG + LANES
PROJ_TN = 512
PROJ_WP = -(-PROJ_W // PROJ_TN) * PROJ_TN

ATT_TQ = 256
ATT_TK = 512
BIAS_ND = -(-(REL_MAX_DIST + ATT_TK) // ATT_TQ)
VMEM_LIMIT = 48 * 1024 * 1024


def _cparams(sem):
    return pltpu.CompilerParams(dimension_semantics=sem, vmem_limit_bytes=VMEM_LIMIT)


def _proj_kernel(x_ref, w_ref, o_ref, xb_ref):
    @pl.when(pl.program_id(1) == 0)
    def _():
        xb_ref[...] = x_ref[...].astype(BF16)

    o_ref[...] = jnp.dot(xb_ref[...], w_ref[...], preferred_element_type=F32).astype(o_ref.dtype)


def _project(x, w, tm, tn):
    m, k = x.shape
    n = w.shape[1]
    return pl.pallas_call(
        _proj_kernel,
        grid=(m // tm, n // tn),
        in_specs=[pl.BlockSpec((tm, k), lambda i, j: (i, 0)),
                  pl.BlockSpec((k, tn), lambda i, j: (0, j))],
        out_specs=pl.BlockSpec((tm, tn), lambda i, j: (i, j)),
        out_shape=jax.ShapeDtypeStruct((m, n), BF16),
        scratch_shapes=[pltpu.VMEM((tm, k), BF16)],
        compiler_params=_cparams(("parallel", "arbitrary")),
    )(x, w)


def _layer_norm(y, g, b):
    mu = jnp.mean(y, -1, keepdims=True)
    yc = y - mu
    var = jnp.mean(yc * yc, -1, keepdims=True)
    return yc * lax.rsqrt(var + LN_EPS) * g + b


def _split3(x):
    parts = []
    for _ in range(3):
        part = x.astype(BF16)
        parts.append(part)
        x = x - part.astype(F32)
    return parts


def _cumsum_rows(tril, x):
    return sum(jnp.dot(tril, part, preferred_element_type=F32) for part in _split3(x))


def _hgrn_kernel(q_ref, f_ref, i_ref, g_ref, lb_ref, nw_ref, o_ref, st_ref, *, n_chunks):
    @pl.when(pl.program_id(1) == 0)
    def _():
        st_ref[...] = jnp.zeros_like(st_ref)

    c = HG_CHUNK
    nt = (((1,), (1,)), ((), ()))
    row = lax.broadcasted_iota(jnp.int32, (c, c), 0)
    col = lax.broadcasted_iota(jnp.int32, (c, c), 1)
    tril = (row >= col).astype(BF16)
    row1 = lax.broadcasted_iota(jnp.int32, (c, 1), 0)
    half_pos = row1 % HG_HALF
    second_half = row1 % HG_SUB >= HG_HALF
    same_sub = row // HG_SUB == col // HG_SUB

    w = HG_HEADS * HG_DK
    head_lanes = [slice(h * HG_DK, (h + 1) * HG_DK) for h in range(HG_HEADS)]

    def per_head(fn):
        return jnp.concatenate([fn(h, head_lanes[h]) for h in range(HG_HEADS)], axis=1)

    def shift_in_half(x, lag):
        return pltpu.roll(x.reshape(c // HG_HALF, HG_HALF, w), lag, 1).reshape(c, w)

    def chunk(ci, carry):
        r0 = pl.multiple_of(ci * c, c)
        q = q_ref[pl.ds(r0, c), :].astype(F32)
        z = f_ref[pl.ds(r0, c), :].astype(F32)
        v = i_ref[pl.ds(r0, c), :].astype(F32)
        g = g_ref[pl.ds(r0, c), :].astype(F32)
        k = (1.0 - lb_ref[...]) * jax.nn.sigmoid(-z)
        b = _cumsum_rows(tril, jnp.log1p(-k))
        b_last = b[c - 1:c, :]
        vb = v.astype(BF16)
        qe = (q * jnp.exp(b)).astype(BF16)
        o = per_head(lambda h, hl: lax.dot_general(qe[:, hl], st_ref[h].astype(BF16), nt,
                                                  preferred_element_type=F32))
        att_rows = [[jnp.zeros((HG_SUB, c), F32)] * HG_HEADS]
        for sb in range(1, c // HG_SUB):
            lo = sb * HG_SUB
            ref_b = b[lo - 1:lo, :]
            qt = (q[lo:lo + HG_SUB, :] * jnp.exp(b[lo:lo + HG_SUB, :] - ref_b)).astype(BF16)
            kt = (k * jnp.exp(jnp.where(row1 < lo, ref_b - b, -jnp.inf))).astype(BF16)
            att_rows.append([lax.dot_general(qt[:, hl], kt[:, hl], nt, preferred_element_type=F32)
                             for hl in head_lanes])
        mid = jnp.concatenate(
            [jnp.broadcast_to(b[lo + HG_HALF - 1:lo + HG_HALF, :], (HG_SUB, w)) for lo in range(0, c, HG_SUB)],
            axis=0)
        q2 = (q * jnp.exp(jnp.where(second_half, b - mid, -jnp.inf))).astype(BF16)
        k2 = (k * jnp.exp(jnp.where(second_half, -jnp.inf, mid - b))).astype(BF16)
        att_half = [lax.dot_general(q2[:, hl], k2[:, hl], nt, preferred_element_type=F32) for hl in head_lanes]

        def intra(h, hl):
            att = jnp.concatenate([rows[h] for rows in att_rows], axis=0) + jnp.where(same_sub, att_half[h], 0.0)
            return jnp.dot(att.astype(BF16), vb[:, hl], preferred_element_type=F32)

        o = o + per_head(intra)
        for lag in range(HG_HALF):
            if lag == 0:
                ks, bs, vs = k, b, v
            else:
                ks, bs, vs = shift_in_half(k, lag), shift_in_half(b, lag), shift_in_half(v, lag)
            valid = half_pos >= lag
            prod = q * ks * jnp.exp(jnp.where(valid, b - bs, 0.0))
            a = per_head(lambda h, hl: jnp.broadcast_to(
                jnp.sum(prod[:, hl], axis=1, keepdims=True), (c, HG_DK)))
            o = o + jnp.where(valid, a, 0.0) * vs
        khat = (k * jnp.exp(b_last - b)).astype(BF16)
        decay = jnp.exp(b_last)
        for h, hl in enumerate(head_lanes):
            st_ref[h] = st_ref[h] * decay[:, hl] + lax.dot_general(
                vb[:, hl], khat[:, hl], (((0,), (0,)), ((), ())), preferred_element_type=F32)
        sq = o * o
        ms = per_head(lambda h, hl: jnp.broadcast_to(jnp.mean(sq[:, hl], -1, keepdims=True), (c, HG_DK)))
        o = o * lax.rsqrt(ms + 1e-6)
        o_ref[pl.ds(r0, c), :] = (o * nw_ref[...] * (g * jax.nn.sigmoid(g))).astype(o_ref.dtype)
        return carry

    lax.fori_loop(0, n_chunks, chunk, 0)


def _hgrn(proj, lb, norm_w, ts):
    bsz, s, _ = proj.shape
    col0 = COL_HG // HG_WIDTH

    def col_spec(n):
        return pl.BlockSpec((None, ts, HG_WIDTH), lambda b, t: (b, t, col0 + n))

    head_spec = pl.BlockSpec((1, HG_WIDTH), lambda b, t: (0, 0))
    return pl.pallas_call(
        functools.partial(_hgrn_kernel, n_chunks=ts // HG_CHUNK),
        grid=(bsz, s // ts),
        in_specs=[col_spec(0), col_spec(1), col_spec(2), col_spec(3), head_spec, head_spec],
        out_specs=pl.BlockSpec((None, ts, HG_WIDTH), lambda b, t: (b, t, 0)),
        out_shape=jax.ShapeDtypeStruct((bsz, s, HG_WIDTH), BF16),
        scratch_shapes=[pltpu.VMEM((HG_HEADS, HG_DK, HG_DK), F32)],
        compiler_params=_cparams(("parallel", "arbitrary")),
    )(proj, proj, proj, proj, lb, norm_w)


def _compress_kernel(a_ref, pos_ref, w1_ref, w2_ref, o_ref):
    half = CMP_STRIDE * NSA_DH
    a = a_ref[...].astype(F32)
    n = a.shape[0]
    a1 = (a + pos_ref[:, :half]).astype(BF16)
    a2 = (a + pos_ref[:, half:]).astype(BF16)
    y1 = jnp.dot(a1, w1_ref[:half, :], preferred_element_type=F32)
    y2 = jnp.dot(a2, w1_ref[half:, :], preferred_element_type=F32)
    hid = jax.nn.gelu(y1 + pltpu.roll(y2, n - 1, 0))
    o_ref[...] = jnp.dot(hid.astype(BF16), w2_ref[...], preferred_element_type=F32).astype(o_ref.dtype)


def _compress(a, pos, w1, w2):
    _, bsz, g, n, width = a.shape
    return pl.pallas_call(
        _compress_kernel,
        grid=(2, bsz, g),
        in_specs=[pl.BlockSpec((None, None, None, n, width), lambda c, b, h: (c, b, h, 0, 0)),
                  pl.BlockSpec((None, 1, 2 * width), lambda c, b, h: (c, 0, 0)),
                  pl.BlockSpec((None, 2 * width, CMP_HIDDEN), lambda c, b, h: (c, 0, 0)),
                  pl.BlockSpec((None, CMP_HIDDEN, NSA_DH), lambda c, b, h: (c, 0, 0))],
        out_specs=pl.BlockSpec((None, None, None, n, NSA_DH), lambda c, b, h: (c, b, h, 0, 0)),
        out_shape=jax.ShapeDtypeStruct((2, bsz, g, n, NSA_DH), BF16),
        compiler_params=_cparams(("parallel", "parallel", "parallel")),
    )(a, pos, w1, w2)


def _bucket_of(dist):
    n = jnp.maximum(dist, 0)
    exact = REL_BUCKETS // 2
    large = exact + (jnp.log(jnp.maximum(n, exact).astype(F32) / exact)
                     / math.log(REL_MAX_DIST / exact) * (REL_BUCKETS - exact)).astype(jnp.int32)
    return jnp.where(n < exact, n, jnp.minimum(large, REL_BUCKETS - 1))


def _bias_kernel(tab_ref, o_ref, *, dist_fn):
    rows, cols = o_ref.shape[2], o_ref.shape[3]
    row = lax.broadcasted_iota(jnp.int32, (rows, cols), 0)
    col = lax.broadcasted_iota(jnp.int32, (rows, cols), 1)
    bucket = _bucket_of(dist_fn(pl.program_id(0), row, col))
    for h in range(NSA_HEADS):
        tab = jnp.broadcast_to(tab_ref[h:h + 1, :] * LOG2E, (rows, LANES))
        for c0 in range(0, cols, LANES):
            o_ref[h, 0, :, c0:c0 + LANES] = jnp.take_along_axis(tab, bucket[:, c0:c0 + LANES], axis=1)


def _bias_tiles(rel_bias, n_tiles, rows, cols, dist_fn):
    tab = jnp.pad(rel_bias.T, ((0, 0), (0, LANES - REL_BUCKETS)))
    return pl.pallas_call(
        functools.partial(_bias_kernel, dist_fn=dist_fn),
        grid=(n_tiles,),
        in_specs=[pl.BlockSpec((NSA_HEADS, LANES), lambda t: (0, 0))],
        out_specs=pl.BlockSpec((NSA_HEADS, 1, rows, cols), lambda t: (0, t, 0, 0)),
        out_shape=jax.ShapeDtypeStruct((NSA_HEADS, n_tiles, rows, cols), F32),
        compiler_params=_cparams(("parallel",)),
    )(tab)


def _stack_heads(qb):
    tq = qb.shape[0]
    half = lax.broadcasted_iota(jnp.int32, (tq, LANES), 1) // NSA_DH
    return jnp.concatenate(
        [jnp.where(half == hp % 2, qb[:, LANES * (hp // 2):LANES * (hp // 2 + 1)], jnp.zeros((tq, LANES), qb.dtype))
         for hp in range(NSA_HPG)], axis=0)


def _cmp_kernel(q_ref, k_ref, v_ref, bias_ref, c2s_ref, gl_ref, o_ref, qa_ref, *, tq):
    p = NSA_HPG
    t0 = pl.program_id(1) * tq
    bsz, ncp = k_ref.shape[0], k_ref.shape[1]
    bias = bias_ref[:, 0]
    tpos = t0 + lax.broadcasted_iota(jnp.int32, (tq, ncp), 0)
    cend = lax.broadcasted_iota(jnp.int32, (tq, ncp), 1) * CMP_STRIDE + (CMP_LEN - 1)
    mask = (tpos >= cend)[None]
    lane = lax.broadcasted_iota(jnp.int32, (tq, LANES), 1)
    cur = (t0 + lax.broadcasted_iota(jnp.int32, (tq, LANES), 0)) // SEL_BLOCK
    forced = (lane == 0) | (lane == cur) | (lane == cur - 1)
    visible = lane <= cur
    batch = range(bsz)
    nt = (((1,), (1,)), ((), ()))
    qs = [_stack_heads(q_ref[b]) for b in batch]
    lgs = [lax.dot_general(qs[b], k_ref[b], nt, preferred_element_type=F32) for b in batch]
    lgs = [jnp.where(mask, lg.reshape(p, tq, ncp) + bias, NEG_BIG) for lg in lgs]
    es = [jnp.where(mask, jnp.exp2(lg - jnp.max(lg, -1, keepdims=True)), 0.0) for lg in lgs]
    pcs = [e / jnp.maximum(jnp.sum(e, -1, keepdims=True), 1e-30) for e in es]
    os = [jnp.dot(pcs[b].reshape(p * tq, ncp).astype(BF16), v_ref[b], preferred_element_type=F32) for b in batch]
    imps = [sum(jnp.dot(part, c2s_ref[...], preferred_element_type=F32) for part in _split3(jnp.sum(pc, axis=0)))
            for pc in pcs]
    for b in batch:
        o_ref[b] = os[b].reshape(p, tq, NSA_DH) * jax.nn.sigmoid(gl_ref[b, :, :, 0:1].astype(F32))
    scores = [jnp.where(forced, FORCE_SCORE, jnp.where(visible, imp, NEG_BIG)) for imp in imps]
    members = [jnp.zeros((tq, LANES), F32)] * bsz
    for _ in range(N_SEL):
        hits = [lane == jnp.argmax(score, axis=-1, keepdims=True) for score in scores]
        members = [jnp.where(hit, 1.0, member) for hit, member in zip(hits, members)]
        scores = [jnp.where(hit, -jnp.inf, score) for hit, score in zip(hits, scores)]
    for b in batch:
        pen = jnp.where((members[b] > 0.5) & visible, 0.0, NEG_BIG).astype(qa_ref.dtype)
        qa_ref[b] = jnp.concatenate([jnp.concatenate([pen] * p, axis=0), qs[b]], axis=1).reshape(p, tq, 2 * LANES)


def _q_spec(bsz, tq):
    width = NSA_HPG * NSA_DH
    return pl.BlockSpec((bsz, tq, width), lambda h, i: (0, i, COL_NQ // width + h))


def _cmp_branch(proj, kc2, vc, bias_c, c2s, gl5, tq):
    bsz, s, _ = proj.shape
    g, p, dh = NSA_GROUPS, NSA_HPG, NSA_DH
    ncp = kc2.shape[2]
    qmap = lambda h, i: (0, h, 0, i, 0)
    return pl.pallas_call(
        functools.partial(_cmp_kernel, tq=tq),
        grid=(g, s // tq),
        in_specs=[_q_spec(bsz, tq),
                  pl.BlockSpec((bsz, None, ncp, LANES), lambda h, i: (0, h, 0, 0)),
                  pl.BlockSpec((bsz, None, ncp, dh), lambda h, i: (0, h, 0, 0)),
                  pl.BlockSpec((p, 1, tq, ncp), lambda h, i: (h, i, 0, 0)),
                  pl.BlockSpec((ncp, LANES), lambda h, i: (0, 0)),
                  pl.BlockSpec((bsz, None, p, tq, 3), qmap)],
        out_specs=[pl.BlockSpec((bsz, None, p, tq, dh), qmap),
                   pl.BlockSpec((bsz, None, p, tq, 2 * LANES), qmap)],
        out_shape=[jax.ShapeDtypeStruct((bsz, g, p, s, dh), F32),
                   jax.ShapeDtypeStruct((bsz, g, p, s, 2 * LANES), BF16)],
        compiler_params=_cparams(("parallel", "parallel")),
    )(proj, kc2, vc, bias_c, c2s, gl5)


def _win_kernel(*refs, tq, n_back):
    nk = n_back + 1
    q_ref, k_refs, v_refs, b_refs = refs[0], refs[1:1 + nk], refs[1 + nk:1 + 2 * nk], refs[1 + 2 * nk:1 + 3 * nk]
    gl_ref, acc_ref, o_ref = refs[1 + 3 * nk:]
    p = NSA_HPG
    bsz = q_ref.shape[0]
    t0 = pl.program_id(1) * tq
    row = lax.broadcasted_iota(jnp.int32, (tq, tq), 0)
    col = lax.broadcasted_iota(jnp.int32, (tq, tq), 1)
    masks = [((row - col + d * tq >= 0) & (row - col + d * tq < WINDOW) & (t0 - d * tq >= 0))[None]
             for d in range(nk)]
    biases = [b_ref[:, 0] for b_ref in b_refs]
    nt = (((1,), (1,)), ((), ()))
    batch = range(bsz)
    qs = [_stack_heads(q_ref[b]) for b in batch]
    lgs = [[lax.dot_general(qs[b], k_ref[b], nt, preferred_element_type=F32).reshape(p, tq, tq)
            for k_ref in k_refs] for b in batch]
    lgs = [[jnp.where(mask, lg + bias, NEG_BIG) for lg, mask, bias in zip(lgs[b], masks, biases)] for b in batch]
    ms = [functools.reduce(jnp.maximum, [jnp.max(lg, -1, keepdims=True) for lg in lgs[b]]) for b in batch]
    es = [[jnp.exp2(lg - ms[b]).reshape(p * tq, tq).astype(BF16) for lg in lgs[b]] for b in batch]
    os = [sum(jnp.dot(e, v_ref[b], preferred_element_type=F32) for e, v_ref in zip(es[b], v_refs)) for b in batch]
    for b in batch:
        o = (os[b] / pltpu.roll(os[b], NSA_DH, 1))[:, :NSA_DH].reshape(p, tq, NSA_DH)
        o_ref[b] = acc_ref[b] + o * jax.nn.sigmoid(gl_ref[b, :, :, 2:3].astype(F32))


def _win_branch(proj, kw2, vw_aug, bias_t, gl5, acc, tq):
    bsz, g, p, s, dh = acc.shape
    n_back = WINDOW // tq
    back = [lambda h, i, d=d: (0, h, jnp.maximum(i - d, 0), 0) for d in range(n_back + 1)]
    qmap = lambda h, i: (0, h, 0, i, 0)
    in_specs = ([_q_spec(bsz, tq)]
                + [pl.BlockSpec((bsz, None, tq, LANES), m) for m in back]
                + [pl.BlockSpec((bsz, None, tq, LANES), m) for m in back]
                + [pl.BlockSpec((p, 1, tq, tq), lambda h, i, d=d: (h, d, 0, 0)) for d in range(n_back + 1)]
                + [pl.BlockSpec((bsz, None, p, tq, 3), qmap), pl.BlockSpec((bsz, None, p, tq, dh), qmap)])
    return pl.pallas_call(
        functools.partial(_win_kernel, tq=tq, n_back=n_back),
        grid=(g, s // tq),
        in_specs=in_specs,
        out_specs=pl.BlockSpec((bsz, None, p, tq, dh), qmap),
        out_shape=jax.ShapeDtypeStruct((bsz, g, p, s, dh), F32),
        input_output_aliases={len(in_specs) - 1: 0},
        compiler_params=_cparams(("parallel", "parallel")),
    )(proj, *([kw2] * (n_back + 1)), *([vw_aug] * (n_back + 1)), *([bias_t] * (n_back + 1)), gl5, acc)


def _sel_kernel(it_ref, jt_ref, q_ref, k_ref, v_ref, bias_ref, gl_ref, acc_ref, o_ref, m_sc, a_sc,
                *, tq, tk, bsz):
    p = NSA_HPG
    nq = p * tq
    step = pl.program_id(1)
    i = it_ref[step]
    j = jt_ref[step]
    t0 = i * tq
    s0 = j * tk
    last = s0 + tk > t0

    @pl.when(j == 0)
    def _():
        m_sc[...] = jnp.full_like(m_sc, NEG_BIG)
        a_sc[...] = jnp.zeros_like(a_sc)

    def accumulate(on_diagonal):
        nt = (((1,), (1,)), ((), ()))
        bias = bias_ref[:, 0].reshape(nq, tk)
        if on_diagonal:
            causal = (t0 + lax.broadcasted_iota(jnp.int32, (tq, tk), 0)
                      >= s0 + lax.broadcasted_iota(jnp.int32, (tq, tk), 1))
            causal = jnp.concatenate([causal] * p, axis=0)
        for b in range(bsz):
            lg = lax.dot_general(q_ref[b].reshape(nq, 2 * LANES), k_ref[b], nt, preferred_element_type=F32) + bias
            if on_diagonal:
                lg = jnp.where(causal, lg, NEG_BIG)
            m_old = m_sc[b]
            m_new = jnp.maximum(m_old, jnp.max(lg, -1, keepdims=True))
            e = jnp.exp2(lg - jnp.concatenate([m_new] * (tk // LANES), axis=1))
            a_sc[b] = (jnp.exp2(m_old - m_new) * a_sc[b]
                       + jnp.dot(e.astype(BF16), v_ref[b], preferred_element_type=F32))
            m_sc[b] = m_new

    @pl.when(jnp.logical_not(last))
    def _():
        accumulate(False)

    @pl.when(last)
    def _():
        accumulate(True)
        gate = jax.nn.sigmoid(gl_ref[:, :, :, 1:2].astype(F32))
        a = a_sc[...]
        o = (a / pltpu.roll(a, NSA_DH, 2))[:, :, :NSA_DH].reshape(bsz, p, tq, NSA_DH)
        o_ref[...] = (acc_ref[...] + o * gate).astype(o_ref.dtype)


def _sel_branch(q_aug, k_aug, v_aug, bias_t, gl5, acc, tq, tk):
    bsz, g, p, s, wq = q_aug.shape
    dh = acc.shape[-1]
    r = tk // tq
    pairs = [(i, j) for i in range(s // tq) for j in range(i // r + 1)]
    it = jnp.asarray([ij[0] for ij in pairs], jnp.int32)
    jt = jnp.asarray([ij[1] for ij in pairs], jnp.int32)
    nd = bias_t.shape[1] - 1
    qmap = lambda h, t, it, jt: (0, h, 0, it[t], 0)
    kmap = lambda h, t, it, jt: (0, h, jt[t], 0)
    grid_spec = pltpu.PrefetchScalarGridSpec(
        num_scalar_prefetch=2,
        grid=(g, len(pairs)),
        in_specs=[pl.BlockSpec((bsz, None, p, tq, wq), qmap),
                  pl.BlockSpec((bsz, None, tk, wq), kmap),
                  pl.BlockSpec((bsz, None, tk, LANES), kmap),
                  pl.BlockSpec((p, 1, tq, tk),
                               lambda h, t, it, jt: (h, jnp.minimum(it[t] - r * jt[t], nd), 0, 0)),
                  pl.BlockSpec((bsz, None, p, tq, 3), qmap),
                  pl.BlockSpec((bsz, None, p, tq, dh), qmap)],
        out_specs=pl.BlockSpec((bsz, None, p, tq, dh), qmap),
        scratch_shapes=[pltpu.VMEM((bsz, p * tq, LANES), F32),
                        pltpu.VMEM((bsz, p * tq, LANES), F32)],
    )
    return pl.pallas_call(
        functools.partial(_sel_kernel, tq=tq, tk=tk, bsz=bsz),
        grid_spec=grid_spec,
        out_shape=jax.ShapeDtypeStruct((bsz, g, p, s, dh), BF16),
        compiler_params=_cparams(("parallel", "arbitrary")),
    )(it, jt, q_aug, k_aug, v_aug, bias_t, gl5, acc)


def _merge_kernel(x_ref, oa_ref, ob_ref, ga_ref, gb_ref, wa_ref, wb_ref, wo_ref, g_ref, b_ref, o_ref):
    ya = jnp.dot(oa_ref[...].astype(BF16), wa_ref[...], preferred_element_type=F32)
    yb = jnp.dot(ob_ref[...].astype(BF16), wb_ref[...], preferred_element_type=F32)
    y = jax.nn.sigmoid(ga_ref[...].astype(F32)) * ya + jax.nn.sigmoid(gb_ref[...].astype(F32)) * yb
    mix = jnp.dot(y.astype(BF16), wo_ref[...], preferred_element_type=F32)
    o_ref[...] = _layer_norm(DN_ALPHA * x_ref[...] + mix, g_ref[...], b_ref[...])


def _merge(x, o_a, o_b, proj, wa, wb, wo, g, b, tm):
    t, d = x.shape
    nga = COL_GA // d
    row = lambda i: (i, 0)
    const = lambda i: (0, 0)
    return pl.pallas_call(
        _merge_kernel,
        grid=(t // tm,),
        in_specs=[pl.BlockSpec((tm, d), row),
                  pl.BlockSpec((tm, HG_WIDTH), row),
                  pl.BlockSpec((tm, NSA_WIDTH), row),
                  pl.BlockSpec((tm, d), lambda i: (i, nga)),
                  pl.BlockSpec((tm, d), lambda i: (i, nga + 1)),
                  pl.BlockSpec((HG_WIDTH, d), const),
                  pl.BlockSpec((NSA_WIDTH, d), const),
                  pl.BlockSpec((d, d), const),
                  pl.BlockSpec((1, d), const),
                  pl.BlockSpec((1, d), const)],
        out_specs=pl.BlockSpec((tm, d), row),
        out_shape=jax.ShapeDtypeStruct((t, d), F32),
        compiler_params=_cparams(("parallel",)),
    )(x, o_a, o_b, proj, proj, wa, wb, wo, g, b)


def _swiglu_step(xb, wg_ref, wu_ref, wd_ref):
    hg = jnp.dot(xb, wg_ref[...], preferred_element_type=F32)
    hu = jnp.dot(xb, wu_ref[...], preferred_element_type=F32)
    h = (hg * jax.nn.sigmoid(hg)) * hu
    return jnp.dot(h.astype(BF16), wd_ref[...], preferred_element_type=F32)


def _ffn_kernel(x_ref, wg_ref, wu_ref, wd_ref, g_ref, b_ref, o_ref):
    x = x_ref[...]
    f = _swiglu_step(x.astype(BF16), wg_ref, wu_ref, wd_ref)
    o_ref[...] = _layer_norm(DN_ALPHA * x + f, g_ref[...], b_ref[...])


def _ffn(x, wg, wu, wd, g, b, tm):
    t, d = x.shape
    f = wg.shape[1]
    const = lambda i: (0, 0)
    once = pl.Buffered(1)
    return pl.pallas_call(
        _ffn_kernel,
        grid=(t // tm,),
        in_specs=[pl.BlockSpec((tm, d), lambda i: (i, 0)),
                  pl.BlockSpec((d, f), const, pipeline_mode=once),
                  pl.BlockSpec((d, f), const, pipeline_mode=once),
                  pl.BlockSpec((f, d), const, pipeline_mode=once),
                  pl.BlockSpec((1, d), const),
                  pl.BlockSpec((1, d), const)],
        out_specs=pl.BlockSpec((tm, d), lambda i: (i, 0)),
        out_shape=jax.ShapeDtypeStruct((t, d), F32),
        compiler_params=_cparams(("parallel",)),
    )(x, wg, wu, wd, g, b)


def _router_kernel(x_ref, w_ref, o_ref):
    logits = jnp.dot(x_ref[...], w_ref[...], preferred_element_type=F32, precision=lax.Precision.HIGHEST)
    lane = lax.broadcasted_iota(jnp.int32, logits.shape, 1).astype(F32)
    logits = jnp.where(lane < N_EXPERTS, logits, -jnp.inf)
    v1 = jnp.max(logits, -1, keepdims=True)
    e1 = jnp.min(jnp.where(logits == v1, lane, float(LANES)), -1, keepdims=True)
    rest = jnp.where(lane == e1, -jnp.inf, logits)
    v2 = jnp.max(rest, -1, keepdims=True)
    e2 = jnp.min(jnp.where(rest == v2, lane, float(LANES)), -1, keepdims=True)
    x2 = jnp.exp(v2 - v1)
    den = 1.0 + x2
    o_ref[...] = jnp.where(lane == 0, e1, jnp.where(lane == 1, e2, jnp.where(
        lane == 2, 1.0 / den, jnp.where(lane == 3, x2 / den, 0.0))))


def _router(x, w, tm):
    t, d = x.shape
    return pl.pallas_call(
        _router_kernel,
        grid=(t // tm,),
        in_specs=[pl.BlockSpec((tm, d), lambda i: (i, 0)), pl.BlockSpec((d, LANES), lambda i: (0, 0))],
        out_specs=pl.BlockSpec((tm, LANES), lambda i: (i, 0)),
        out_shape=jax.ShapeDtypeStruct((t, LANES), F32),
        compiler_params=_cparams(("parallel",)),
    )(x, w)


def _expert_kernel(be_ref, x_ref, wg_ref, wu_ref, wd_ref, o_ref, acc_ref):
    j = pl.program_id(1)

    @pl.when(j == 0)
    def _():
        acc_ref[...] = jnp.zeros_like(acc_ref)

    xb = x_ref[...].astype(BF16)
    hg = jnp.dot(xb, wg_ref[...].astype(BF16), preferred_element_type=F32)
    hu = jnp.dot(xb, wu_ref[...].astype(BF16), preferred_element_type=F32)
    h = (hg * jax.nn.sigmoid(hg)) * hu
    acc_ref[...] += jnp.dot(h.astype(BF16), wd_ref[...].astype(BF16), preferred_element_type=F32)

    @pl.when(j == pl.num_programs(1) - 1)
    def _():
        o_ref[...] = acc_ref[...].astype(o_ref.dtype)


def _experts(blk_e, xs, wg, wu, wd, tf):
    rows, d = xs.shape
    f = wg.shape[2]
    tm = MOE_ROW_BLOCK
    grid_spec = pltpu.PrefetchScalarGridSpec(
        num_scalar_prefetch=1,
        grid=(rows // tm, f // tf),
        in_specs=[pl.BlockSpec((tm, d), lambda i, j, be: (i, 0)),
                  pl.BlockSpec((None, d, tf), lambda i, j, be: (be[i], 0, j)),
                  pl.BlockSpec((None, d, tf), lambda i, j, be: (be[i], 0, j)),
                  pl.BlockSpec((None, tf, d), lambda i, j, be: (be[i], j, 0))],
        out_specs=pl.BlockSpec((tm, d), lambda i, j, be: (i, 0)),
        scratch_shapes=[pltpu.VMEM((tm, d), F32)],
    )
    return pl.pallas_call(
        _expert_kernel,
        grid_spec=grid_spec,
        out_shape=jax.ShapeDtypeStruct((rows, d), F32),
        compiler_params=_cparams(("parallel", "arbitrary")),
    )(blk_e, xs, wg, wu, wd)


def _combine_kernel(x_ref, y1_ref, y2_ref, gt_ref, g_ref, b_ref, o_ref):
    f = y1_ref[...].astype(F32) * gt_ref[:, 2:3] + y2_ref[...].astype(F32) * gt_ref[:, 3:4]
    o_ref[...] = _layer_norm(DN_ALPHA * x_ref[...] + f, g_ref[...], b_ref[...])


def _combine(x, y1, y2, route, g, b, tm):
    t, d = x.shape
    row = lambda i: (i, 0)
    const = lambda i: (0, 0)
    return pl.pallas_call(
        _combine_kernel,
        grid=(t // tm,),
        in_specs=[pl.BlockSpec((tm, d), row), pl.BlockSpec((tm, d), row), pl.BlockSpec((tm, d), row),
                  pl.BlockSpec((tm, LANES), row), pl.BlockSpec((1, d), const), pl.BlockSpec((1, d), const)],
        out_specs=pl.BlockSpec((tm, d), row),
        out_shape=jax.ShapeDtypeStruct((t, d), F32),
        compiler_params=_cparams(("parallel",)),
    )(x, y1, y2, route, g, b)


def _moe(x, w_router, wg, wu, wd, g, b):
    t, d = x.shape
    tk_ = t * TOP_K
    route = _router(x, jnp.pad(w_router, ((0, 0), (0, LANES - N_EXPERTS))), 512)
    flat_e = route[:, :TOP_K].astype(jnp.int32).reshape(-1)
    onehot = (flat_e[:, None] == jnp.arange(N_EXPERTS)[None, :]).astype(jnp.int32)
    csum = jnp.cumsum(onehot, axis=0)
    counts = csum[-1]
    rank = jnp.sum(onehot * csum, axis=1) - 1
    padded = (counts + MOE_ROW_BLOCK - 1) // MOE_ROW_BLOCK * MOE_ROW_BLOCK
    pend = jnp.cumsum(padded)
    dest = (pend - padded)[flat_e] + rank
    n_blocks = -(-(tk_ + N_EXPERTS * (MOE_ROW_BLOCK - 1)) // MOE_ROW_BLOCK)
    n_rows = n_blocks * MOE_ROW_BLOCK
    row_tok = (jnp.arange(n_rows, dtype=jnp.int32) % t).at[dest].set(jnp.arange(tk_, dtype=jnp.int32) // TOP_K)
    blk_e = jnp.minimum(jnp.searchsorted(pend, jnp.arange(n_blocks) * MOE_ROW_BLOCK, side='right'),
                        N_EXPERTS - 1).astype(jnp.int32)
    xs = x[row_tok]
    ys = _experts(blk_e, xs, wg, wu, wd, 512)
    dest2 = dest.reshape(t, TOP_K)
    return _combine(x, ys[dest2[:, 0]], ys[dest2[:, 1]], route, g, b, 512)


def _pack_w_in(w_in):
    offs = np.concatenate([[0], np.cumsum(IN_SIZES)])
    seg = [w_in[:, offs[j]:offs[j + 1]] for j in range(len(IN_SIZES))]
    seg[4] = seg[4] * (NSA_DH ** -0.5 * LOG2E)
    seg[11] = jnp.pad(seg[11], ((0, 0), (0, LANES - 3 * NSA_HEADS)))
    seg = seg[12:14] + seg[0:12]
    packed = jnp.concatenate(seg + [jnp.zeros((w_in.shape[0], PROJ_WP - PROJ_W), w_in.dtype)], axis=1)
    return packed.astype(BF16)


def _cmp_to_sel(n_cmp_pad, n_cmp):
    cs = np.arange(n_cmp_pad)[:, None] * CMP_STRIDE
    ss = np.arange(LANES)[None, :] * SEL_BLOCK
    overlap = np.clip(np.minimum(cs + CMP_LEN, ss + SEL_BLOCK) - np.maximum(cs, ss), 0, None) / CMP_LEN
    overlap[n_cmp:] = 0.0
    return jnp.asarray(overlap, BF16)


def _token_mixer(x, w_in_p, lb, hg_norm_w, cmp_pos, cmp_w1, cmp_w2, bias_c, bias_t, wa, wb, wo, ln_g, ln_b):
    bsz, s, d = x.shape
    g, p, dh = NSA_GROUPS, NSA_HPG, NSA_DH
    xf = x.reshape(bsz * s, d)
    proj = _project(xf, w_in_p, 1024, PROJ_WP // 2).reshape(bsz, s, PROJ_WP)
    o_a = _hgrn(proj, lb, hg_norm_w, 1024)

    def heads(c0, width):
        return proj[:, :, c0:c0 + width]

    kv = heads(COL_KV, 6 * KV_WIDTH).reshape(bsz, s, 6, g, dh)
    n16 = s // CMP_STRIDE
    a = kv[:, :, 0:2].reshape(bsz, n16, CMP_STRIDE, 2, g, dh).transpose(3, 0, 4, 1, 2, 5)
    a = a.reshape(2, bsz, g, n16, CMP_STRIDE * dh)
    kvc = _compress(a, cmp_pos.reshape(2, 1, CMP_LEN * dh), cmp_w1.astype(BF16), cmp_w2.astype(BF16))
    kvh = kv[:, :, 2:6].astype(BF16).transpose(2, 0, 3, 1, 4)
    gl5 = heads(COL_NG, 3 * NSA_HEADS).reshape(bsz, s, g, p, 3).transpose(0, 2, 3, 1, 4)
    n_cmp = (s - CMP_LEN) // CMP_STRIDE + 1
    acc, q_aug = _cmp_branch(proj, jnp.concatenate([kvc[0], kvc[0]], axis=-1), kvc[1], bias_c,
                             _cmp_to_sel(n16, n_cmp), gl5, ATT_TQ)
    ones = jnp.ones((bsz, g, s, LANES - dh), BF16)
    acc = _win_branch(proj, jnp.concatenate([kvh[2], kvh[2]], axis=-1), jnp.concatenate([kvh[3], ones], axis=-1),
                      bias_t, gl5, acc, ATT_TQ)
    block_of_key = (np.arange(s)[:, None] // SEL_BLOCK == np.arange(LANES)[None, :])
    k_aug = jnp.concatenate([jnp.broadcast_to(jnp.asarray(block_of_key, BF16), (bsz, g, s, LANES)), kvh[0],
                             kvh[0]], axis=-1)
    v_aug = jnp.concatenate([kvh[1], ones], axis=-1)
    o_b = _sel_branch(q_aug, k_aug, v_aug, bias_t, gl5, acc, ATT_TQ, ATT_TK)
    o_b = o_b.transpose(0, 3, 1, 2, 4).reshape(bsz * s, NSA_WIDTH)
    return _merge(xf, o_a.reshape(bsz * s, HG_WIDTH), o_b, proj.reshape(bsz * s, PROJ_WP),
                  wa, wb, wo, ln_g, ln_b, 512)


def kernel(x, w_in, hg_lb_logits, hg_norm_w, cmp_pos, cmp_w1, cmp_w2, rel_bias, w_branch_a, w_branch_b,
           w_out, ln1_g, ln1_b, ln2_g, ln2_b, ffn_w_gate, ffn_w_up, ffn_w_down, moe_router, moe_w_gate,
           moe_w_up, moe_w_down):
    bsz, s, d = x.shape
    depth = w_in.shape[0]
    p_lb = jax.nn.softmax(hg_lb_logits.astype(F32), axis=0)
    lbs = jnp.cumsum(p_lb, axis=0) - p_lb[0]
    n16 = s // CMP_STRIDE
    bias_c = _bias_tiles(rel_bias, s // ATT_TQ, ATT_TQ, n16,
                         lambda t, r, c: t * ATT_TQ + r - (c * CMP_STRIDE + CMP_LEN - 1))
    bias_t = _bias_tiles(rel_bias, BIAS_ND + 1, ATT_TQ, ATT_TK, lambda t, r, c: t * ATT_TQ + r - c)
    f_pad = -(-D_FF // LANES) * LANES - D_FF
    xf = x.reshape(bsz * s, d)
    for l in range(depth):
        xf = _token_mixer(xf.reshape(bsz, s, d), _pack_w_in(w_in[l]), lbs[l][None], hg_norm_w[l][None],
                          cmp_pos[l], cmp_w1[l], cmp_w2[l], bias_c, bias_t,
                          w_branch_a[l].astype(BF16), w_branch_b[l].astype(BF16), w_out[l].astype(BF16),
                          ln1_g[l][None], ln1_b[l][None])
        if l % 2 == 0:
            wg = jnp.pad(ffn_w_gate[l // 2], ((0, 0), (0, f_pad))).astype(BF16)
            wu = jnp.pad(ffn_w_up[l // 2], ((0, 0), (0, f_pad))).astype(BF16)
            wd = jnp.pad(ffn_w_down[l // 2], ((0, f_pad), (0, 0))).astype(BF16)
            xf = _ffn(xf, wg, wu, wd, ln2_g[l][None], ln2_b[l][None], 512)
        else:
            xf = _moe(xf, moe_router[l // 2], moe_w_gate[l // 2], moe_w_up[l // 2], moe_w_down[l // 2],
                      ln2_g[l][None], ln2_b[l][None])
    return xf.reshape(bsz, s, d)
```

```python
import functools
import math

import jax
import jax.numpy as jnp
import numpy as np
from jax import lax
from jax.experimental import pallas as pl
from jax.experimental.pallas import tpu as pltpu

F32 = jnp.float32
BF16 = jnp.bfloat16

D_MODEL = 1024
DEPTH = 2
HG_HEADS = 4
HG_DK = 128
HG_WIDTH = HG_HEADS * HG_DK
HG_CHUNK = 64
HG_SUB = 16
HG_HALF = HG_SUB // 2
NSA_HEADS = 8
NSA_GROUPS = 2
NSA_HPG = NSA_HEADS // NSA_GROUPS
NSA_DH = 64
NSA_WIDTH = NSA_HEADS * NSA_DH
KV_WIDTH = NSA_GROUPS * NSA_DH
CMP_LEN = 32
CMP_STRIDE = 16
CMP_HIDDEN = 2 * NSA_DH
SEL_BLOCK = 64
N_SEL = 16
WINDOW = 512
FORCE_SCORE = 1e9
NEG_BIG = -1e30
REL_BUCKETS = 32
REL_MAX_DIST = 2048
D_FF = 2752
N_EXPERTS = 8
TOP_K = 2
D_FF_EXPERT = 3584
MOE_ROW_BLOCK = 1024
DN_ALPHA = (2 * DEPTH) ** 0.25
LN_EPS = 1e-5
IN_SIZES = (HG_WIDTH, HG_WIDTH, HG_WIDTH, HG_WIDTH, NSA_WIDTH,
            KV_WIDTH, KV_WIDTH, KV_WIDTH, KV_WIDTH, KV_WIDTH, KV_WIDTH,
            3 * NSA_HEADS, D_MODEL, D_MODEL)

LANES = 128
LOG2E = 1.0 / math.log(2.0)
COL_GA = 0
COL_HG = 2 * D_MODEL
COL_NQ = COL_HG + 4 * HG_WIDTH
COL_KV = COL_NQ + NSA_WIDTH
COL_NG = COL_KV + 6 * KV_WIDTH
PROJ_W = COL_NG + NSA_GROUPS * LANES
PROJ_TN = 512
PROJ_WP = -(-PROJ_W // PROJ_TN) * PROJ_TN

ATT_TQ = 256
ATT_TK = 512
BIAS_ND = -(-(REL_MAX_DIST + ATT_TK) // ATT_TQ)
VMEM_LIMIT = 48 * 1024 * 1024


def _cparams(sem):
    return pltpu.CompilerParams(dimension_semantics=sem, vmem_limit_bytes=VMEM_LIMIT)


def _proj_kernel(x_ref, w_ref, o_ref, xb_ref):
    @pl.when(pl.program_id(1) == 0)
    def _():
        xb_ref[...] = x_ref[...].astype(BF16)

    o_ref[...] = jnp.dot(xb_ref[...], w_ref[...], preferred_element_type=F32).astype(o_ref.dtype)


def _project(x, w, tm, tn):
    m, k = x.shape
    n = w.shape[1]
    return pl.pallas_call(
        _proj_kernel,
        grid=(m // tm, n // tn),
        in_specs=[pl.BlockSpec((tm, k), lambda i, j: (i, 0)),
                  pl.BlockSpec((k, tn), lambda i, j: (0, j))],
        out_specs=pl.BlockSpec((tm, tn), lambda i, j: (i, j)),
        out_shape=jax.ShapeDtypeStruct((m, n), BF16),
        scratch_shapes=[pltpu.VMEM((tm, k), BF16)],
        compiler_params=_cparams(("parallel", "arbitrary")),
    )(x, w)


def _layer_norm(y, g, b):
    mu = jnp.mean(y, -1, keepdims=True)
    yc = y - mu
    var = jnp.mean(yc * yc, -1, keepdims=True)
    return yc * lax.rsqrt(var + LN_EPS) * g + b


def _split3(x):
    parts = []
    for _ in range(3):
        part = x.astype(BF16)
        parts.append(part)
        x = x - part.astype(F32)
    return parts


def _cumsum_rows(tril, x):
    return sum(jnp.dot(tril, part, preferred_element_type=F32) for part in _split3(x))


def _hgrn_kernel(q_ref, f_ref, i_ref, g_ref, lb_ref, nw_ref, o_ref, st_ref, *, n_chunks):
    @pl.when(pl.program_id(1) == 0)
    def _():
        st_ref[...] = jnp.zeros_like(st_ref)

    c = HG_CHUNK
    nt = (((1,), (1,)), ((), ()))
    row = lax.broadcasted_iota(jnp.int32, (c, c), 0)
    col = lax.broadcasted_iota(jnp.int32, (c, c), 1)
    tril = (row >= col).astype(BF16)
    row1 = lax.broadcasted_iota(jnp.int32, (c, 1), 0)
    half_pos = row1 % HG_HALF
    second_half = row1 % HG_SUB >= HG_HALF
    same_sub = row // HG_SUB == col // HG_SUB

    w = HG_HEADS * HG_DK
    head_lanes = [slice(h * HG_DK, (h + 1) * HG_DK) for h in range(HG_HEADS)]

    def per_head(fn):
        return jnp.concatenate([fn(h, head_lanes[h]) for h in range(HG_HEADS)], axis=1)

    def shift_in_half(x, lag):
        return pltpu.roll(x.reshape(c // HG_HALF, HG_HALF, w), lag, 1).reshape(c, w)

    def chunk(ci, carry):
        r0 = pl.multiple_of(ci * c, c)
        q = q_ref[pl.ds(r0, c), :].astype(F32)
        z = f_ref[pl.ds(r0, c), :].astype(F32)
        v = i_ref[pl.ds(r0, c), :].astype(F32)
        g = g_ref[pl.ds(r0, c), :].astype(F32)
        k = (1.0 - lb_ref[...]) * jax.nn.sigmoid(-z)
        b = _cumsum_rows(tril, jnp.log1p(-k))
        b_last = b[c - 1:c, :]
        vb = v.astype(BF16)
        qe = (q * jnp.exp(b)).astype(BF16)
        o = per_head(lambda h, hl: lax.dot_general(qe[:, hl], st_ref[h].astype(BF16), nt,
                                                  preferred_element_type=F32))
        att_rows = [[jnp.zeros((HG_SUB, c), F32)] * HG_HEADS]
        for sb in range(1, c // HG_SUB):
            lo = sb * HG_SUB
            ref_b = b[lo - 1:lo, :]
            qt = (q[lo:lo + HG_SUB, :] * jnp.exp(b[lo:lo + HG_SUB, :] - ref_b)).astype(BF16)
            kt = (k * jnp.exp(jnp.where(row1 < lo, ref_b - b, -jnp.inf))).astype(BF16)
            att_rows.append([lax.dot_general(qt[:, hl], kt[:, hl], nt, preferred_element_type=F32)
                             for hl in head_lanes])
        mid = jnp.concatenate(
            [jnp.broadcast_to(b[lo + HG_HALF - 1:lo + HG_HALF, :], (HG_SUB, w)) for lo in range(0, c, HG_SUB)],
            axis=0)
        q2 = (q * jnp.exp(jnp.where(second_half, b - mid, -jnp.inf))).astype(BF16)
        k2 = (k * jnp.exp(jnp.where(second_half, -jnp.inf, mid - b))).astype(BF16)
        att_half = [lax.dot_general(q2[:, hl], k2[:, hl], nt, preferred_element_type=F32) for hl in head_lanes]

        def intra(h, hl):
            att = jnp.concatenate([rows[h] for rows in att_rows], axis=0) + jnp.where(same_sub, att_half[h], 0.0)
            return jnp.dot(att.astype(BF16), vb[:, hl], preferred_element_type=F32)

        o = o + per_head(intra)
        for lag in range(HG_HALF):
            if lag == 0:
                ks, bs, vs = k, b, v
            else:
                ks, bs, vs = shift_in_half(k, lag), shift_in_half(b, lag), shift_in_half(v, lag)
            valid = half_pos >= lag
            prod = q * ks * jnp.exp(jnp.where(valid, b - bs, 0.0))
            a = per_head(lambda h, hl: jnp.broadcast_to(
                jnp.sum(prod[:, hl], axis=1, keepdims=True), (c, HG_DK)))
            o = o + jnp.where(valid, a, 0.0) * vs
        khat = (k * jnp.exp(b_last - b)).astype(BF16)
        decay = jnp.exp(b_last)
        for h, hl in enumerate(head_lanes):
            st_ref[h] = st_ref[h] * decay[:, hl] + lax.dot_general(
                vb[:, hl], khat[:, hl], (((0,), (0,)), ((), ())), preferred_element_type=F32)
        sq = o * o
        ms = per_head(lambda h, hl: jnp.broadcast_to(jnp.mean(sq[:, hl], -1, keepdims=True), (c, HG_DK)))
        o = o * lax.rsqrt(ms + 1e-6)
        o_ref[pl.ds(r0, c), :] = (o * nw_ref[...] * (g * jax.nn.sigmoid(g))).astype(o_ref.dtype)
        return carry

    lax.fori_loop(0, n_chunks, chunk, 0)


def _hgrn(proj, lb, norm_w, ts):
    bsz, s, _ = proj.shape
    col0 = COL_HG // HG_WIDTH

    def col_spec(n):
        return pl.BlockSpec((None, ts, HG_WIDTH), lambda b, t: (b, t, col0 + n))

    head_spec = pl.BlockSpec((1, HG_WIDTH), lambda b, t: (0, 0))
    return pl.pallas_call(
        functools.partial(_hgrn_kernel, n_chunks=ts // HG_CHUNK),
        grid=(bsz, s // ts),
        in_specs=[col_spec(0), col_spec(1), col_spec(2), col_spec(3), head_spec, head_spec],
        out_specs=pl.BlockSpec((None, ts, HG_WIDTH), lambda b, t: (b, t, 0)),
        out_shape=jax.ShapeDtypeStruct((bsz, s, HG_WIDTH), BF16),
        scratch_shapes=[pltpu.VMEM((HG_HEADS, HG_DK, HG_DK), F32)],
        compiler_params=_cparams(("parallel", "arbitrary")),
    )(proj, proj, proj, proj, lb, norm_w)


def _compress_kernel(a_ref, pos_ref, w1_ref, w2_ref, o_ref):
    half = CMP_STRIDE * NSA_DH
    a = a_ref[...].astype(F32)
    n = a.shape[0]
    a1 = (a + pos_ref[:, :half]).astype(BF16)
    a2 = (a + pos_ref[:, half:]).astype(BF16)
    y1 = jnp.dot(a1, w1_ref[:half, :], preferred_element_type=F32)
    y2 = jnp.dot(a2, w1_ref[half:, :], preferred_element_type=F32)
    hid = jax.nn.gelu(y1 + pltpu.roll(y2, n - 1, 0))
    o_ref[...] = jnp.dot(hid.astype(BF16), w2_ref[...], preferred_element_type=F32).astype(o_ref.dtype)


def _compress(a, pos, w1, w2):
    _, bsz, g, n, width = a.shape
    return pl.pallas_call(
        _compress_kernel,
        grid=(2, bsz, g),
        in_specs=[pl.BlockSpec((None, None, None, n, width), lambda c, b, h: (c, b, h, 0, 0)),
                  pl.BlockSpec((None, 1, 2 * width), lambda c, b, h: (c, 0, 0)),
                  pl.BlockSpec((None, 2 * width, CMP_HIDDEN), lambda c, b, h: (c, 0, 0)),
                  pl.BlockSpec((None, CMP_HIDDEN, NSA_DH), lambda c, b, h: (c, 0, 0))],
        out_specs=pl.BlockSpec((None, None, None, n, NSA_DH), lambda c, b, h: (c, b, h, 0, 0)),
        out_shape=jax.ShapeDtypeStruct((2, bsz, g, n, NSA_DH), BF16),
        compiler_params=_cparams(("parallel", "parallel", "parallel")),
    )(a, pos, w1, w2)


def _bucket_of(dist):
    n = jnp.maximum(dist, 0)
    exact = REL_BUCKETS // 2
    large = exact + (jnp.log(jnp.maximum(n, exact).astype(F32) / exact)
                     / math.log(REL_MAX_DIST / exact) * (REL_BUCKETS - exact)).astype(jnp.int32)
    return jnp.where(n < exact, n, jnp.minimum(large, REL_BUCKETS - 1))


def _bias_kernel(tab_ref, o_ref, *, dist_fn):
    rows, cols = o_ref.shape[2], o_ref.shape[3]
    row = lax.broadcasted_iota(jnp.int32, (rows, cols), 0)
    col = lax.broadcasted_iota(jnp.int32, (rows, cols), 1)
    bucket = _bucket_of(dist_fn(pl.program_id(0), row, col))
    for h in range(NSA_HEADS):
        tab = jnp.broadcast_to(tab_ref[h:h + 1, :] * LOG2E, (rows, LANES))
        for c0 in range(0, cols, LANES):
            o_ref[h, 0, :, c0:c0 + LANES] = jnp.take_along_axis(tab, bucket[:, c0:c0 + LANES], axis=1)


def _bias_tiles(rel_bias, n_tiles, rows, cols, dist_fn):
    tab = jnp.pad(rel_bias.T, ((0, 0), (0, LANES - REL_BUCKETS)))
    return pl.pallas_call(
        functools.partial(_bias_kernel, dist_fn=dist_fn),
        grid=(n_tiles,),
        in_specs=[pl.BlockSpec((NSA_HEADS, LANES), lambda t: (0, 0))],
        out_specs=pl.BlockSpec((NSA_HEADS, 1, rows, cols), lambda t: (0, t, 0, 0)),
        out_shape=jax.ShapeDtypeStruct((NSA_HEADS, n_tiles, rows, cols), F32),
        compiler_params=_cparams(("parallel",)),
    )(tab)


def _stack_heads(qb):
    tq = qb.shape[0]
    half = lax.broadcasted_iota(jnp.int32, (tq, LANES), 1) // NSA_DH
    return jnp.concatenate(
        [jnp.where(half == hp % 2, qb[:, LANES * (hp // 2):LANES * (hp // 2 + 1)], jnp.zeros((tq, LANES), qb.dtype))
         for hp in range(NSA_HPG)], axis=0)


def _cmp_kernel(q_ref, k_ref, v_ref, bias_ref, c2s_ref, gl_ref, o_ref, qa_ref, *, tq):
    p = NSA_HPG
    t0 = pl.program_id(1) * tq
    bsz, ncp = k_ref.shape[0], k_ref.shape[1]
    bias = bias_ref[:, 0]
    tpos = t0 + lax.broadcasted_iota(jnp.int32, (tq, ncp), 0)
    cend = lax.broadcasted_iota(jnp.int32, (tq, ncp), 1) * CMP_STRIDE + (CMP_LEN - 1)
    mask = (tpos >= cend)[None]
    lane = lax.broadcasted_iota(jnp.int32, (tq, LANES), 1)
    cur = (t0 + lax.broadcasted_iota(jnp.int32, (tq, LANES), 0)) // SEL_BLOCK
    forced = (lane == 0) | (lane == cur) | (lane == cur - 1)
    visible = lane <= cur
    batch = range(bsz)
    nt = (((1,), (1,)), ((), ()))
    qs = [_stack_heads(q_ref[b]) for b in batch]
    lgs = [lax.dot_general(qs[b], k_ref[b], nt, preferred_element_type=F32) for b in batch]
    lgs = [jnp.where(mask, lg.reshape(p, tq, ncp) + bias, NEG_BIG) for lg in lgs]
    es = [jnp.where(mask, jnp.exp2(lg - jnp.max(lg, -1, keepdims=True)), 0.0) for lg in lgs]
    pcs = [e / jnp.maximum(jnp.sum(e, -1, keepdims=True), 1e-30) for e in es]
    os = [jnp.dot(pcs[b].reshape(p * tq, ncp).astype(BF16), v_ref[b], preferred_element_type=F32) for b in batch]
    imps = [sum(jnp.dot(part, c2s_ref[...], preferred_element_type=F32) for part in _split3(jnp.sum(pc, axis=0)))
            for pc in pcs]
    for b in batch:
        o_ref[b] = os[b].reshape(p, tq, NSA_DH) * _head_gates(gl_ref[b], 0)
    scores = [jnp.where(forced, FORCE_SCORE, jnp.where(visible, imp, NEG_BIG)) for imp in imps]
    members = [jnp.zeros((tq, LANES), F32)] * bsz
    for _ in range(N_SEL):
        hits = [lane == jnp.argmax(score, axis=-1, keepdims=True) for score in scores]
        members = [jnp.where(hit, 1.0, member) for hit, member in zip(hits, members)]
        scores = [jnp.where(hit, -jnp.inf, score) for hit, score in zip(hits, scores)]
    for b in batch:
        pen = jnp.where((members[b] > 0.5) & visible, 0.0, NEG_BIG).astype(qa_ref.dtype)
        qa_ref[b] = jnp.concatenate([jnp.concatenate([pen] * p, axis=0), qs[b]], axis=1).reshape(p, tq, 2 * LANES)


def _head_gates(gl, branch):
    gates = jax.nn.sigmoid(gl.astype(F32))
    return jnp.stack([gates[:, 3 * hp + branch:3 * hp + branch + 1] for hp in range(NSA_HPG)])


def _gate_spec(bsz, tq):
    return pl.BlockSpec((bsz, tq, LANES), lambda h, i: (0, i, COL_NG // LANES + h))


def _q_spec(bsz, tq):
    width = NSA_HPG * NSA_DH
    return pl.BlockSpec((bsz, tq, width), lambda h, i: (0, i, COL_NQ // width + h))


def _cmp_branch(proj, kc2, vc, bias_c, c2s, tq):
    bsz, s, _ = proj.shape
    g, p, dh = NSA_GROUPS, NSA_HPG, NSA_DH
    ncp = kc2.shape[2]
    qmap = lambda h, i: (0, h, 0, i, 0)
    return pl.pallas_call(
        functools.partial(_cmp_kernel, tq=tq),
        grid=(g, s // tq),
        in_specs=[_q_spec(bsz, tq),
                  pl.BlockSpec((bsz, None, ncp, LANES), lambda h, i: (0, h, 0, 0)),
                  pl.BlockSpec((bsz, None, ncp, dh), lambda h, i: (0, h, 0, 0)),
                  pl.BlockSpec((p, 1, tq, ncp), lambda h, i: (h, i, 0, 0)),
                  pl.BlockSpec((ncp, LANES), lambda h, i: (0, 0)),
                  _gate_spec(bsz, tq)],
        out_specs=[pl.BlockSpec((bsz, None, p, tq, dh), qmap),
                   pl.BlockSpec((bsz, None, p, tq, 2 * LANES), qmap)],
        out_shape=[jax.ShapeDtypeStruct((bsz, g, p, s, dh), F32),
                   jax.ShapeDtypeStruct((bsz, g, p, s, 2 * LANES), BF16)],
        compiler_params=_cparams(("parallel", "parallel")),
    )(proj, kc2, vc, bias_c, c2s, proj)


def _win_kernel(*refs, tq, n_back):
    nk = n_back + 1
    q_ref, k_refs, v_refs, b_refs = refs[0], refs[1:1 + nk], refs[1 + nk:1 + 2 * nk], refs[1 + 2 * nk:1 + 3 * nk]
    gl_ref, acc_ref, o_ref = refs[1 + 3 * nk:]
    p = NSA_HPG
    bsz = q_ref.shape[0]
    t0 = pl.program_id(1) * tq
    row = lax.broadcasted_iota(jnp.int32, (tq, tq), 0)
    col = lax.broadcasted_iota(jnp.int32, (tq, tq), 1)
    masks = [((row - col + d * tq >= 0) & (row - col + d * tq < WINDOW) & (t0 - d * tq >= 0))[None]
             for d in range(nk)]
    biases = [b_ref[:, 0] for b_ref in b_refs]
    nt = (((1,), (1,)), ((), ()))
    batch = range(bsz)
    qs = [_stack_heads(q_ref[b]) for b in batch]
    lgs = [[lax.dot_general(qs[b], k_ref[b], nt, preferred_element_type=F32).reshape(p, tq, tq)
            for k_ref in k_refs] for b in batch]
    lgs = [[jnp.where(mask, lg + bias, NEG_BIG) for lg, mask, bias in zip(lgs[b], masks, biases)] for b in batch]
    ms = [functools.reduce(jnp.maximum, [jnp.max(lg, -1, keepdims=True) for lg in lgs[b]]) for b in batch]
    es = [[jnp.exp2(lg - ms[b]).reshape(p * tq, tq).astype(BF16) for lg in lgs[b]] for b in batch]
    os = [sum(jnp.dot(e, v_ref[b], preferred_element_type=F32) for e, v_ref in zip(es[b], v_refs)) for b in batch]
    for b in batch:
        o = (os[b] / pltpu.roll(os[b], NSA_DH, 1))[:, :NSA_DH].reshape(p, tq, NSA_DH)
        o_ref[b] = acc_ref[b] + o * _head_gates(gl_ref[b], 2)


def _win_branch(proj, kw2, vw_aug, bias_t, acc, tq):
    bsz, g, p, s, dh = acc.shape
    n_back = WINDOW // tq
    back = [lambda h, i, d=d: (0, h, jnp.maximum(i - d, 0), 0) for d in range(n_back + 1)]
    qmap = lambda h, i: (0, h, 0, i, 0)
    in_specs = ([_q_spec(bsz, tq)]
                + [pl.BlockSpec((bsz, None, tq, LANES), m) for m in back]
                + [pl.BlockSpec((bsz, None, tq, LANES), m) for m in back]
                + [pl.BlockSpec((p, 1, tq, tq), lambda h, i, d=d: (h, d, 0, 0)) for d in range(n_back + 1)]
                + [_gate_spec(bsz, tq), pl.BlockSpec((bsz, None, p, tq, dh), qmap)])
    return pl.pallas_call(
        functools.partial(_win_kernel, tq=tq, n_back=n_back),
        grid=(g, s // tq),
        in_specs=in_specs,
        out_specs=pl.BlockSpec((bsz, None, p, tq, dh), qmap),
        out_shape=jax.ShapeDtypeStruct((bsz, g, p, s, dh), F32),
        input_output_aliases={len(in_specs) - 1: 0},
        compiler_params=_cparams(("parallel", "parallel")),
    )(proj, *([kw2] * (n_back + 1)), *([vw_aug] * (n_back + 1)), *([bias_t] * (n_back + 1)), proj, acc)


def _sel_kernel(it_ref, jt_ref, q_ref, k_ref, v_ref, bias_ref, gl_ref, acc_ref, o_ref, m_sc, a_sc,
                *, tq, tk, bsz):
    p = NSA_HPG
    nq = p * tq
    step = pl.program_id(1)
    i = it_ref[step]
    j = jt_ref[step]
    t0 = i * tq
    s0 = j * tk
    last = s0 + tk > t0

    @pl.when(j == 0)
    def _():
        m_sc[...] = jnp.full_like(m_sc, NEG_BIG)
        a_sc[...] = jnp.zeros_like(a_sc)

    def accumulate(on_diagonal):
        nt = (((1,), (1,)), ((), ()))
        bias = bias_ref[:, 0].reshape(nq, tk)
        if on_diagonal:
            causal = (t0 + lax.broadcasted_iota(jnp.int32, (tq, tk), 0)
                      >= s0 + lax.broadcasted_iota(jnp.int32, (tq, tk), 1))
            causal = jnp.concatenate([causal] * p, axis=0)
        for b in range(bsz):
            lg = lax.dot_general(q_ref[b].reshape(nq, 2 * LANES), k_ref[b], nt, preferred_element_type=F32) + bias
            if on_diagonal:
                lg = jnp.where(causal, lg, NEG_BIG)
            m_old = m_sc[b]
            m_new = jnp.maximum(m_old, jnp.max(lg, -1, keepdims=True))
            e = jnp.exp2(lg - jnp.concatenate([m_new] * (tk // LANES), axis=1))
            a_sc[b] = (jnp.exp2(m_old - m_new) * a_sc[b]
                       + jnp.dot(e.astype(BF16), v_ref[b], preferred_element_type=F32))
            m_sc[b] = m_new

    @pl.when(jnp.logical_not(last))
    def _():
        accumulate(False)

    @pl.when(last)
    def _():
        accumulate(True)
        gate = jnp.stack([_head_gates(gl_ref[b], 1) for b in range(bsz)])
        a = a_sc[...]
        o = (a / pltpu.roll(a, NSA_DH, 2))[:, :, :NSA_DH].reshape(bsz, p, tq, NSA_DH)
        o_ref[...] = (acc_ref[...] + o * gate).astype(o_ref.dtype)


def _sel_branch(q_aug, k_aug, v_aug, bias_t, proj, acc, tq, tk):
    bsz, g, p, s, wq = q_aug.shape
    dh = acc.shape[-1]
    r = tk // tq
    pairs = [(i, j) for i in range(s // tq) for j in range(i // r + 1)]
    it = jnp.asarray([ij[0] for ij in pairs], jnp.int32)
    jt = jnp.asarray([ij[1] for ij in pairs], jnp.int32)
    nd = bias_t.shape[1] - 1
    qmap = lambda h, t, it, jt: (0, h, 0, it[t], 0)
    kmap = lambda h, t, it, jt: (0, h, jt[t], 0)
    grid_spec = pltpu.PrefetchScalarGridSpec(
        num_scalar_prefetch=2,
        grid=(g, len(pairs)),
        in_specs=[pl.BlockSpec((bsz, None, p, tq, wq), qmap),
                  pl.BlockSpec((bsz, None, tk, wq), kmap),
                  pl.BlockSpec((bsz, None, tk, LANES), kmap),
                  pl.BlockSpec((p, 1, tq, tk),
                               lambda h, t, it, jt: (h, jnp.minimum(it[t] - r * jt[t], nd), 0, 0)),
                  pl.BlockSpec((bsz, tq, LANES), lambda h, t, it, jt: (0, it[t], COL_NG // LANES + h)),
                  pl.BlockSpec((bsz, None, p, tq, dh), qmap)],
        out_specs=pl.BlockSpec((bsz, None, p, tq, dh), qmap),
        scratch_shapes=[pltpu.VMEM((bsz, p * tq, LANES), F32),
                        pltpu.VMEM((bsz, p * tq, LANES), F32)],
    )
    return pl.pallas_call(
        functools.partial(_sel_kernel, tq=tq, tk=tk, bsz=bsz),
        grid_spec=grid_spec,
        out_shape=jax.ShapeDtypeStruct((bsz, g, p, s, dh), BF16),
        compiler_params=_cparams(("parallel", "arbitrary")),
    )(it, jt, q_aug, k_aug, v_aug, bias_t, proj, acc)


def _merge_kernel(x_ref, oa_ref, ob_ref, ga_ref, gb_ref, wa_ref, wb_ref, wo_ref, g_ref, b_ref, o_ref):
    ya = jnp.dot(oa_ref[...].astype(BF16), wa_ref[...], preferred_element_type=F32)
    yb = jnp.dot(ob_ref[...].astype(BF16), wb_ref[...], preferred_element_type=F32)
    y = jax.nn.sigmoid(ga_ref[...].astype(F32)) * ya + jax.nn.sigmoid(gb_ref[...].astype(F32)) * yb
    mix = jnp.dot(y.astype(BF16), wo_ref[...], preferred_element_type=F32)
    o_ref[...] = _layer_norm(DN_ALPHA * x_ref[...] + mix, g_ref[...], b_ref[...])


def _merge(x, o_a, o_b, proj, wa, wb, wo, g, b, tm):
    t, d = x.shape
    nga = COL_GA // d
    row = lambda i: (i, 0)
    const = lambda i: (0, 0)
    return pl.pallas_call(
        _merge_kernel,
        grid=(t // tm,),
        in_specs=[pl.BlockSpec((tm, d), row),
                  pl.BlockSpec((tm, HG_WIDTH), row),
                  pl.BlockSpec((tm, NSA_WIDTH), row),
                  pl.BlockSpec((tm, d), lambda i: (i, nga)),
                  pl.BlockSpec((tm, d), lambda i: (i, nga + 1)),
                  pl.BlockSpec((HG_WIDTH, d), const),
                  pl.BlockSpec((NSA_WIDTH, d), const),
                  pl.BlockSpec((d, d), const),
                  pl.BlockSpec((1, d), const),
                  pl.BlockSpec((1, d), const)],
        out_specs=pl.BlockSpec((tm, d), row),
        out_shape=jax.ShapeDtypeStruct((t, d), F32),
        compiler_params=_cparams(("parallel",)),
    )(x, o_a, o_b, proj, proj, wa, wb, wo, g, b)


def _swiglu_step(xb, wg_ref, wu_ref, wd_ref):
    hg = jnp.dot(xb, wg_ref[...], preferred_element_type=F32)
    hu = jnp.dot(xb, wu_ref[...], preferred_element_type=F32)
    h = (hg * jax.nn.sigmoid(hg)) * hu
    return jnp.dot(h.astype(BF16), wd_ref[...], preferred_element_type=F32)


def _ffn_kernel(x_ref, wg_ref, wu_ref, wd_ref, g_ref, b_ref, o_ref):
    x = x_ref[...]
    f = _swiglu_step(x.astype(BF16), wg_ref, wu_ref, wd_ref)
    o_ref[...] = _layer_norm(DN_ALPHA * x + f, g_ref[...], b_ref[...])


def _ffn(x, wg, wu, wd, g, b, tm):
    t, d = x.shape
    f = wg.shape[1]
    const = lambda i: (0, 0)
    once = pl.Buffered(1)
    return pl.pallas_call(
        _ffn_kernel,
        grid=(t // tm,),
        in_specs=[pl.BlockSpec((tm, d), lambda i: (i, 0)),
                  pl.BlockSpec((d, f), const, pipeline_mode=once),
                  pl.BlockSpec((d, f), const, pipeline_mode=once),
                  pl.BlockSpec((f, d), const, pipeline_mode=once),
                  pl.BlockSpec((1, d), const),
                  pl.BlockSpec((1, d), const)],
        out_specs=pl.BlockSpec((tm, d), lambda i: (i, 0)),
        out_shape=jax.ShapeDtypeStruct((t, d), F32),
        compiler_params=_cparams(("parallel",)),
    )(x, wg, wu, wd, g, b)


def _router_kernel(x_ref, w_ref, o_ref):
    logits = jnp.dot(x_ref[...], w_ref[...], preferred_element_type=F32, precision=lax.Precision.HIGHEST)
    lane = lax.broadcasted_iota(jnp.int32, logits.shape, 1).astype(F32)
    logits = jnp.where(lane < N_EXPERTS, logits, -jnp.inf)
    v1 = jnp.max(logits, -1, keepdims=True)
    e1 = jnp.min(jnp.where(logits == v1, lane, float(LANES)), -1, keepdims=True)
    rest = jnp.where(lane == e1, -jnp.inf, logits)
    v2 = jnp.max(rest, -1, keepdims=True)
    e2 = jnp.min(jnp.where(rest == v2, lane, float(LANES)), -1, keepdims=True)
    x2 = jnp.exp(v2 - v1)
    den = 1.0 + x2
    o_ref[...] = jnp.where(lane == 0, e1, jnp.where(lane == 1, e2, jnp.where(
        lane == 2, 1.0 / den, jnp.where(lane == 3, x2 / den, 0.0))))


def _router(x, w, tm):
    t, d = x.shape
    return pl.pallas_call(
        _router_kernel,
        grid=(t // tm,),
        in_specs=[pl.BlockSpec((tm, d), lambda i: (i, 0)), pl.BlockSpec((d, LANES), lambda i: (0, 0))],
        out_specs=pl.BlockSpec((tm, LANES), lambda i: (i, 0)),
        out_shape=jax.ShapeDtypeStruct((t, LANES), F32),
        compiler_params=_cparams(("parallel",)),
    )(x, w)


def _expert_kernel(be_ref, x_ref, wg_ref, wu_ref, wd_ref, o_ref, acc_ref):
    j = pl.program_id(1)

    @pl.when(j == 0)
    def _():
        acc_ref[...] = jnp.zeros_like(acc_ref)

    xb = x_ref[...].astype(BF16)
    hg = jnp.dot(xb, wg_ref[...].astype(BF16), preferred_element_type=F32)
    hu = jnp.dot(xb, wu_ref[...].astype(BF16), preferred_element_type=F32)
    h = (hg * jax.nn.sigmoid(hg)) * hu
    acc_ref[...] += jnp.dot(h.astype(BF16), wd_ref[...].astype(BF16), preferred_element_type=F32)

    @pl.when(j == pl.num_programs(1) - 1)
    def _():
        o_ref[...] = acc_ref[...].astype(o_ref.dtype)


def _experts(blk_e, xs, wg, wu, wd, tf):
    rows, d = xs.shape
    f = wg.shape[2]
    tm = MOE_ROW_BLOCK
    grid_spec = pltpu.PrefetchScalarGridSpec(
        num_scalar_prefetch=1,
        grid=(rows // tm, f // tf),
        in_specs=[pl.BlockSpec((tm, d), lambda i, j, be: (i, 0)),
                  pl.BlockSpec((None, d, tf), lambda i, j, be: (be[i], 0, j)),
                  pl.BlockSpec((None, d, tf), lambda i, j, be: (be[i], 0, j)),
                  pl.BlockSpec((None, tf, d), lambda i, j, be: (be[i], j, 0))],
        out_specs=pl.BlockSpec((tm, d), lambda i, j, be: (i, 0)),
        scratch_shapes=[pltpu.VMEM((tm, d), F32)],
    )
    return pl.pallas_call(
        _expert_kernel,
        grid_spec=grid_spec,
        out_shape=jax.ShapeDtypeStruct((rows, d), F32),
        compiler_params=_cparams(("parallel", "arbitrary")),
    )(blk_e, xs, wg, wu, wd)


def _combine_kernel(x_ref, y1_ref, y2_ref, gt_ref, g_ref, b_ref, o_ref):
    f = y1_ref[...].astype(F32) * gt_ref[:, 2:3] + y2_ref[...].astype(F32) * gt_ref[:, 3:4]
    o_ref[...] = _layer_norm(DN_ALPHA * x_ref[...] + f, g_ref[...], b_ref[...])


def _combine(x, y1, y2, route, g, b, tm):
    t, d = x.shape
    row = lambda i: (i, 0)
    const = lambda i: (0, 0)
    return pl.pallas_call(
        _combine_kernel,
        grid=(t // tm,),
        in_specs=[pl.BlockSpec((tm, d), row), pl.BlockSpec((tm, d), row), pl.BlockSpec((tm, d), row),
                  pl.BlockSpec((tm, LANES), row), pl.BlockSpec((1, d), const), pl.BlockSpec((1, d), const)],
        out_specs=pl.BlockSpec((tm, d), row),
        out_shape=jax.ShapeDtypeStruct((t, d), F32),
        compiler_params=_cparams(("parallel",)),
    )(x, y1, y2, route, g, b)


def _moe(x, w_router, wg, wu, wd, g, b):
    t, d = x.shape
    tk_ = t * TOP_K
    route = _router(x, jnp.pad(w_router, ((0, 0), (0, LANES - N_EXPERTS))), 512)
    flat_e = route[:, :TOP_K].astype(jnp.int32).reshape(-1)
    onehot = (flat_e[:, None] == jnp.arange(N_EXPERTS)[None, :]).astype(jnp.int32)
    csum = jnp.cumsum(onehot, axis=0)
    counts = csum[-1]
    rank = jnp.sum(onehot * csum, axis=1) - 1
    padded = (counts + MOE_ROW_BLOCK - 1) // MOE_ROW_BLOCK * MOE_ROW_BLOCK
    pend = jnp.cumsum(padded)
    dest = (pend - padded)[flat_e] + rank
    n_blocks = -(-(tk_ + N_EXPERTS * (MOE_ROW_BLOCK - 1)) // MOE_ROW_BLOCK)
    n_rows = n_blocks * MOE_ROW_BLOCK
    row_tok = (jnp.arange(n_rows, dtype=jnp.int32) % t).at[dest].set(jnp.arange(tk_, dtype=jnp.int32) // TOP_K)
    blk_e = jnp.minimum(jnp.searchsorted(pend, jnp.arange(n_blocks) * MOE_ROW_BLOCK, side='right'),
                        N_EXPERTS - 1).astype(jnp.int32)
    xs = x[row_tok]
    ys = _experts(blk_e, xs, wg, wu, wd, 512)
    dest2 = dest.reshape(t, TOP_K)
    return _combine(x, ys[dest2[:, 0]], ys[dest2[:, 1]], route, g, b, 512)


def _pack_w_in(w_in):
    offs = np.concatenate([[0], np.cumsum(IN_SIZES)])
    seg = [w_in[:, offs[j]:offs[j + 1]] for j in range(len(IN_SIZES))]
    seg[4] = seg[4] * (NSA_DH ** -0.5 * LOG2E)
    seg[11] = jnp.pad(seg[11].reshape(-1, NSA_GROUPS, 3 * NSA_HPG),
                      ((0, 0), (0, 0), (0, LANES - 3 * NSA_HPG))).reshape(-1, NSA_GROUPS * LANES)
    seg = seg[12:14] + seg[0:12]
    packed = jnp.concatenate(seg + [jnp.zeros((w_in.shape[0], PROJ_WP - PROJ_W), w_in.dtype)], axis=1)
    return packed.astype(BF16)


def _cmp_to_sel(n_cmp_pad, n_cmp):
    cs = np.arange(n_cmp_pad)[:, None] * CMP_STRIDE
    ss = np.arange(LANES)[None, :] * SEL_BLOCK
    overlap = np.clip(np.minimum(cs + CMP_LEN, ss + SEL_BLOCK) - np.maximum(cs, ss), 0, None) / CMP_LEN
    overlap[n_cmp:] = 0.0
    return jnp.asarray(overlap, BF16)


def _token_mixer(x, w_in_p, lb, hg_norm_w, cmp_pos, cmp_w1, cmp_w2, bias_c, bias_t, wa, wb, wo, ln_g, ln_b):
    bsz, s, d = x.shape
    g, p, dh = NSA_GROUPS, NSA_HPG, NSA_DH
    xf = x.reshape(bsz * s, d)
    proj = _project(xf, w_in_p, 1024, PROJ_WP // 2).reshape(bsz, s, PROJ_WP)
    o_a = _hgrn(proj, lb, hg_norm_w, 1024)

    def heads(c0, width):
        return proj[:, :, c0:c0 + width]

    kv = heads(COL_KV, 6 * KV_WIDTH).reshape(bsz, s, 6, g, dh)
    n16 = s // CMP_STRIDE
    a = kv[:, :, 0:2].reshape(bsz, n16, CMP_STRIDE, 2, g, dh).transpose(3, 0, 4, 1, 2, 5)
    a = a.reshape(2, bsz, g, n16, CMP_STRIDE * dh)
    kvc = _compress(a, cmp_pos.reshape(2, 1, CMP_LEN * dh), cmp_w1.astype(BF16), cmp_w2.astype(BF16))
    kvh = kv[:, :, 2:6].astype(BF16).transpose(2, 0, 3, 1, 4)
    n_cmp = (s - CMP_LEN) // CMP_STRIDE + 1
    acc, q_aug = _cmp_branch(proj, jnp.concatenate([kvc[0], kvc[0]], axis=-1), kvc[1], bias_c,
                             _cmp_to_sel(n16, n_cmp), ATT_TQ)
    ones = jnp.ones((bsz, g, s, LANES - dh), BF16)
    acc = _win_branch(proj, jnp.concatenate([kvh[2], kvh[2]], axis=-1), jnp.concatenate([kvh[3], ones], axis=-1),
                      bias_t, acc, ATT_TQ)
    block_of_key = (np.arange(s)[:, None] // SEL_BLOCK == np.arange(LANES)[None, :])
    k_aug = jnp.concatenate([jnp.broadcast_to(jnp.asarray(block_of_key, BF16), (bsz, g, s, LANES)), kvh[0],
                             kvh[0]], axis=-1)
    v_aug = jnp.concatenate([kvh[1], ones], axis=-1)
    o_b = _sel_branch(q_aug, k_aug, v_aug, bias_t, proj, acc, ATT_TQ, ATT_TK)
    o_b = o_b.transpose(0, 3, 1, 2, 4).reshape(bsz * s, NSA_WIDTH)
    return _merge(xf, o_a.reshape(bsz * s, HG_WIDTH), o_b, proj.reshape(bsz * s, PROJ_WP),
                  wa, wb, wo, ln_g, ln_b, 512)


def kernel(x, w_in, hg_lb_logits, hg_norm_w, cmp_pos, cmp_w1, cmp_w2, rel_bias, w_branch_a, w_branch_b,
           w_out, ln1_g, ln1_b, ln2_g, ln2_b, ffn_w_gate, ffn_w_up, ffn_w_down, moe_router, moe_w_gate,
           moe_w_up, moe_w_down):
    bsz, s, d = x.shape
    depth = w_in.shape[0]
    p_lb = jax.nn.softmax(hg_lb_logits.astype(F32), axis=0)
    lbs = jnp.cumsum(p_lb, axis=0) - p_lb[0]
    n16 = s // CMP_STRIDE
    bias_c = _bias_tiles(rel_bias, s // ATT_TQ, ATT_TQ, n16,
                         lambda t, r, c: t * ATT_TQ + r - (c * CMP_STRIDE + CMP_LEN - 1))
    bias_t = _bias_tiles(rel_bias, BIAS_ND + 1, ATT_TQ, ATT_TK, lambda t, r, c: t * ATT_TQ + r - c)
    f_pad = -(-D_FF // LANES) * LANES - D_FF
    xf = x.reshape(bsz * s, d)
    for l in range(depth):
        xf = _token_mixer(xf.reshape(bsz, s, d), _pack_w_in(w_in[l]), lbs[l][None], hg_norm_w[l][None],
                          cmp_pos[l], cmp_w1[l], cmp_w2[l], bias_c, bias_t,
                          w_branch_a[l].astype(BF16), w_branch_b[l].astype(BF16), w_out[l].astype(BF16),
                          ln1_g[l][None], ln1_b[l][None])
        if l % 2 == 0:
            wg = jnp.pad(ffn_w_gate[l // 2], ((0, 0), (0, f_pad))).astype(BF16)
            wu = jnp.pad(ffn_w_up[l // 2], ((0, 0), (0, f_pad))).astype(BF16)
            wd = jnp.pad(ffn_w_down[l // 2], ((0, f_pad), (0, 0))).astype(BF16)
            xf = _ffn(xf, wg, wu, wd, ln2_g[l][None], ln2_b[l][None], 512)
        else:
            xf = _moe(xf, moe_router[l // 2], moe_w_gate[l // 2], moe_w_up[l // 2], moe_w_down[l // 2],
                      ln2_g[l][None], ln2_b[l][None])
    return xf.reshape(bsz, s, d)
```

```python
import functools
import math

import jax
import jax.numpy as jnp
import numpy as np
from jax import lax
from jax.experimental import pallas as pl
from jax.experimental.pallas import tpu as pltpu

F32 = jnp.float32
BF16 = jnp.bfloat16

D_MODEL = 1024
DEPTH = 2
HG_HEADS = 4
HG_DK = 128
HG_WIDTH = HG_HEADS * HG_DK
HG_CHUNK = 64
HG_SUB = 16
HG_HALF = HG_SUB // 2
NSA_HEADS = 8
NSA_GROUPS = 2
NSA_HPG = NSA_HEADS // NSA_GROUPS
NSA_DH = 64
NSA_WIDTH = NSA_HEADS * NSA_DH
KV_WIDTH = NSA_GROUPS * NSA_DH
CMP_LEN = 32
CMP_STRIDE = 16
CMP_HIDDEN = 2 * NSA_DH
SEL_BLOCK = 64
N_SEL = 16
WINDOW = 512
FORCE_SCORE = 1e9
NEG_BIG = -1e30
REL_BUCKETS = 32
REL_MAX_DIST = 2048
D_FF = 2752
N_EXPERTS = 8
TOP_K = 2
D_FF_EXPERT = 3584
MOE_ROW_BLOCK = 1024
DN_ALPHA = (2 * DEPTH) ** 0.25
LN_EPS = 1e-5
IN_SIZES = (HG_WIDTH, HG_WIDTH, HG_WIDTH, HG_WIDTH, NSA_WIDTH,
            KV_WIDTH, KV_WIDTH, KV_WIDTH, KV_WIDTH, KV_WIDTH, KV_WIDTH,
            3 * NSA_HEADS, D_MODEL, D_MODEL)

LANES = 128
LOG2E = 1.0 / math.log(2.0)
COL_GA = 0
COL_HG = 2 * D_MODEL
COL_NQ = COL_HG + 4 * HG_WIDTH
COL_KV = COL_NQ + NSA_WIDTH
COL_NG = COL_KV + 6 * KV_WIDTH
PROJ_W = COL_NG + NSA_GROUPS * LANES
PROJ_TN = 512
PROJ_WP = -(-PROJ_W // PROJ_TN) * PROJ_TN

ATT_TQ = 256
ATT_TK = 512
BIAS_ND = -(-(REL_MAX_DIST + ATT_TK) // ATT_TQ)
VMEM_LIMIT = 48 * 1024 * 1024


def _cparams(sem):
    return pltpu.CompilerParams(dimension_semantics=sem, vmem_limit_bytes=VMEM_LIMIT)


def _proj_kernel(x_ref, w_ref, o_ref, xb_ref):
    @pl.when(pl.program_id(1) == 0)
    def _():
        xb_ref[...] = x_ref[...].astype(BF16)

    o_ref[...] = jnp.dot(xb_ref[...], w_ref[...], preferred_element_type=F32).astype(o_ref.dtype)


def _project(x, w, tm, tn):
    m, k = x.shape
    n = w.shape[1]
    return pl.pallas_call(
        _proj_kernel,
        grid=(m // tm, n // tn),
        in_specs=[pl.BlockSpec((tm, k), lambda i, j: (i, 0)),
                  pl.BlockSpec((k, tn), lambda i, j: (0, j))],
        out_specs=pl.BlockSpec((tm, tn), lambda i, j: (i, j)),
        out_shape=jax.ShapeDtypeStruct((m, n), BF16),
        scratch_shapes=[pltpu.VMEM((tm, k), BF16)],
        compiler_params=_cparams(("parallel", "arbitrary")),
    )(x, w)


def _layer_norm(y, g, b):
    mu = jnp.mean(y, -1, keepdims=True)
    yc = y - mu
    var = jnp.mean(yc * yc, -1, keepdims=True)
    return yc * lax.rsqrt(var + LN_EPS) * g + b


def _split3(x):
    parts = []
    for _ in range(3):
        part = x.astype(BF16)
        parts.append(part)
        x = x - part.astype(F32)
    return parts


def _cumsum_rows(tril, x):
    return sum(jnp.dot(tril, part, preferred_element_type=F32) for part in _split3(x))


def _hgrn_kernel(q_ref, f_ref, i_ref, g_ref, lb_ref, nw_ref, o_ref, st_ref, *, n_chunks):
    @pl.when(pl.program_id(1) == 0)
    def _():
        st_ref[...] = jnp.zeros_like(st_ref)

    c = HG_CHUNK
    nt = (((1,), (1,)), ((), ()))
    row = lax.broadcasted_iota(jnp.int32, (c, c), 0)
    col = lax.broadcasted_iota(jnp.int32, (c, c), 1)
    tril = (row >= col).astype(BF16)
    row1 = lax.broadcasted_iota(jnp.int32, (c, 1), 0)
    half_pos = row1 % HG_HALF
    second_half = row1 % HG_SUB >= HG_HALF
    same_sub = row // HG_SUB == col // HG_SUB

    w = HG_HEADS * HG_DK
    head_lanes = [slice(h * HG_DK, (h + 1) * HG_DK) for h in range(HG_HEADS)]

    def per_head(fn):
        return jnp.concatenate([fn(h, head_lanes[h]) for h in range(HG_HEADS)], axis=1)

    def shift_in_half(x, lag):
        return pltpu.roll(x.reshape(c // HG_HALF, HG_HALF, w), lag, 1).reshape(c, w)

    def chunk(ci, carry):
        r0 = pl.multiple_of(ci * c, c)
        q = q_ref[pl.ds(r0, c), :].astype(F32)
        z = f_ref[pl.ds(r0, c), :].astype(F32)
        v = i_ref[pl.ds(r0, c), :].astype(F32)
        g = g_ref[pl.ds(r0, c), :].astype(F32)
        k = (1.0 - lb_ref[...]) * jax.nn.sigmoid(-z)
        b = _cumsum_rows(tril, jnp.log1p(-k))
        b_last = b[c - 1:c, :]
        vb = v.astype(BF16)
        qe = (q * jnp.exp(b)).astype(BF16)
        o = per_head(lambda h, hl: lax.dot_general(qe[:, hl], st_ref[h].astype(BF16), nt,
                                                  preferred_element_type=F32))
        att_rows = [[jnp.zeros((HG_SUB, c), F32)] * HG_HEADS]
        for sb in range(1, c // HG_SUB):
            lo = sb * HG_SUB
            ref_b = b[lo - 1:lo, :]
            qt = (q[lo:lo + HG_SUB, :] * jnp.exp(b[lo:lo + HG_SUB, :] - ref_b)).astype(BF16)
            kt = (k * jnp.exp(jnp.where(row1 < lo, ref_b - b, -jnp.inf))).astype(BF16)
            att_rows.append([lax.dot_general(qt[:, hl], kt[:, hl], nt, preferred_element_type=F32)
                             for hl in head_lanes])
        mid = jnp.concatenate(
            [jnp.broadcast_to(b[lo + HG_HALF - 1:lo + HG_HALF, :], (HG_SUB, w)) for lo in range(0, c, HG_SUB)],
            axis=0)
        q2 = (q * jnp.exp(jnp.where(second_half, b - mid, -jnp.inf))).astype(BF16)
        k2 = (k * jnp.exp(jnp.where(second_half, -jnp.inf, mid - b))).astype(BF16)
        att_half = [lax.dot_general(q2[:, hl], k2[:, hl], nt, preferred_element_type=F32) for hl in head_lanes]

        def intra(h, hl):
            att = jnp.concatenate([rows[h] for rows in att_rows], axis=0) + jnp.where(same_sub, att_half[h], 0.0)
            return jnp.dot(att.astype(BF16), vb[:, hl], preferred_element_type=F32)

        o = o + per_head(intra)
        for lag in range(HG_HALF):
            if lag == 0:
                ks, bs, vs = k, b, v
            else:
                ks, bs, vs = shift_in_half(k, lag), shift_in_half(b, lag), shift_in_half(v, lag)
            valid = half_pos >= lag
            prod = q * ks * jnp.exp(jnp.where(valid, b - bs, 0.0))
            a = per_head(lambda h, hl: jnp.broadcast_to(
                jnp.sum(prod[:, hl], axis=1, keepdims=True), (c, HG_DK)))
            o = o + jnp.where(valid, a, 0.0) * vs
        khat = (k * jnp.exp(b_last - b)).astype(BF16)
        decay = jnp.exp(b_last)
        for h, hl in enumerate(head_lanes):
            st_ref[h] = st_ref[h] * decay[:, hl] + lax.dot_general(
                vb[:, hl], khat[:, hl], (((0,), (0,)), ((), ())), preferred_element_type=F32)
        sq = o * o
        ms = per_head(lambda h, hl: jnp.broadcast_to(jnp.mean(sq[:, hl], -1, keepdims=True), (c, HG_DK)))
        o = o * lax.rsqrt(ms + 1e-6)
        o_ref[pl.ds(r0, c), :] = (o * nw_ref[...] * (g * jax.nn.sigmoid(g))).astype(o_ref.dtype)
        return carry

    lax.fori_loop(0, n_chunks, chunk, 0)


def _hgrn(proj, lb, norm_w, ts):
    bsz, s, _ = proj.shape
    col0 = COL_HG // HG_WIDTH

    def col_spec(n):
        return pl.BlockSpec((None, ts, HG_WIDTH), lambda b, t: (b, t, col0 + n))

    head_spec = pl.BlockSpec((1, HG_WIDTH), lambda b, t: (0, 0))
    return pl.pallas_call(
        functools.partial(_hgrn_kernel, n_chunks=ts // HG_CHUNK),
        grid=(bsz, s // ts),
        in_specs=[col_spec(0), col_spec(1), col_spec(2), col_spec(3), head_spec, head_spec],
        out_specs=pl.BlockSpec((None, ts, HG_WIDTH), lambda b, t: (b, t, 0)),
        out_shape=jax.ShapeDtypeStruct((bsz, s, HG_WIDTH), BF16),
        scratch_shapes=[pltpu.VMEM((HG_HEADS, HG_DK, HG_DK), F32)],
        compiler_params=_cparams(("parallel", "arbitrary")),
    )(proj, proj, proj, proj, lb, norm_w)


def _compress_kernel(a_ref, pos_ref, w1_ref, w2_ref, o_ref):
    half = CMP_STRIDE * NSA_DH
    a = a_ref[...].astype(F32)
    n = a.shape[0]
    a1 = (a + pos_ref[:, :half]).astype(BF16)
    a2 = (a + pos_ref[:, half:]).astype(BF16)
    y1 = jnp.dot(a1, w1_ref[:half, :], preferred_element_type=F32)
    y2 = jnp.dot(a2, w1_ref[half:, :], preferred_element_type=F32)
    hid = jax.nn.gelu(y1 + pltpu.roll(y2, n - 1, 0))
    o_ref[...] = jnp.dot(hid.astype(BF16), w2_ref[...], preferred_element_type=F32).astype(o_ref.dtype)


def _compress(a, pos, w1, w2):
    _, bsz, g, n, width = a.shape
    return pl.pallas_call(
        _compress_kernel,
        grid=(2, bsz, g),
        in_specs=[pl.BlockSpec((None, None, None, n, width), lambda c, b, h: (c, b, h, 0, 0)),
                  pl.BlockSpec((None, 1, 2 * width), lambda c, b, h: (c, 0, 0)),
                  pl.BlockSpec((None, 2 * width, CMP_HIDDEN), lambda c, b, h: (c, 0, 0)),
                  pl.BlockSpec((None, CMP_HIDDEN, NSA_DH), lambda c, b, h: (c, 0, 0))],
        out_specs=pl.BlockSpec((None, None, None, n, NSA_DH), lambda c, b, h: (c, b, h, 0, 0)),
        out_shape=jax.ShapeDtypeStruct((2, bsz, g, n, NSA_DH), BF16),
        compiler_params=_cparams(("parallel", "parallel", "parallel")),
    )(a, pos, w1, w2)


def _bucket_of(dist):
    n = jnp.maximum(dist, 0)
    exact = REL_BUCKETS // 2
    large = exact + (jnp.log(jnp.maximum(n, exact).astype(F32) / exact)
                     / math.log(REL_MAX_DIST / exact) * (REL_BUCKETS - exact)).astype(jnp.int32)
    return jnp.where(n < exact, n, jnp.minimum(large, REL_BUCKETS - 1))


def _bias_kernel(tab_ref, o_ref, *, dist_fn):
    rows, cols = o_ref.shape[2], o_ref.shape[3]
    row = lax.broadcasted_iota(jnp.int32, (rows, cols), 0)
    col = lax.broadcasted_iota(jnp.int32, (rows, cols), 1)
    bucket = _bucket_of(dist_fn(pl.program_id(0), row, col))
    for h in range(NSA_HEADS):
        tab = jnp.broadcast_to(tab_ref[h:h + 1, :] * LOG2E, (rows, LANES))
        for c0 in range(0, cols, LANES):
            o_ref[h, 0, :, c0:c0 + LANES] = jnp.take_along_axis(tab, bucket[:, c0:c0 + LANES], axis=1)


def _bias_tiles(rel_bias, n_tiles, rows, cols, dist_fn):
    tab = jnp.pad(rel_bias.T, ((0, 0), (0, LANES - REL_BUCKETS)))
    return pl.pallas_call(
        functools.partial(_bias_kernel, dist_fn=dist_fn),
        grid=(n_tiles,),
        in_specs=[pl.BlockSpec((NSA_HEADS, LANES), lambda t: (0, 0))],
        out_specs=pl.BlockSpec((NSA_HEADS, 1, rows, cols), lambda t: (0, t, 0, 0)),
        out_shape=jax.ShapeDtypeStruct((NSA_HEADS, n_tiles, rows, cols), F32),
        compiler_params=_cparams(("parallel",)),
    )(tab)


def _stack_heads(qb):
    tq = qb.shape[0]
    half = lax.broadcasted_iota(jnp.int32, (tq, LANES), 1) // NSA_DH
    return jnp.concatenate(
        [jnp.where(half == hp % 2, qb[:, LANES * (hp // 2):LANES * (hp // 2 + 1)], jnp.zeros((tq, LANES), qb.dtype))
         for hp in range(NSA_HPG)], axis=0)


def _cmp_part(qs, k_ref, v_ref, bias_ref, c2s_ref, gl_ref, qa_ref, *, tq):
    p = NSA_HPG
    t0 = pl.program_id(1) * tq
    bsz, ncp = len(qs), k_ref.shape[1]
    bias = bias_ref[:, 0]
    tpos = t0 + lax.broadcasted_iota(jnp.int32, (tq, ncp), 0)
    cend = lax.broadcasted_iota(jnp.int32, (tq, ncp), 1) * CMP_STRIDE + (CMP_LEN - 1)
    mask = (tpos >= cend)[None]
    lane = lax.broadcasted_iota(jnp.int32, (tq, LANES), 1)
    cur = (t0 + lax.broadcasted_iota(jnp.int32, (tq, LANES), 0)) // SEL_BLOCK
    forced = (lane == 0) | (lane == cur) | (lane == cur - 1)
    visible = lane <= cur
    batch = range(bsz)
    nt = (((1,), (1,)), ((), ()))
    lgs = [lax.dot_general(qs[b], k_ref[b], nt, preferred_element_type=F32) for b in batch]
    lgs = [jnp.where(mask, lg.reshape(p, tq, ncp) + bias, NEG_BIG) for lg in lgs]
    es = [jnp.where(mask, jnp.exp2(lg - jnp.max(lg, -1, keepdims=True)), 0.0) for lg in lgs]
    pcs = [e / jnp.maximum(jnp.sum(e, -1, keepdims=True), 1e-30) for e in es]
    os = [jnp.dot(pcs[b].reshape(p * tq, ncp).astype(BF16), v_ref[b], preferred_element_type=F32) for b in batch]
    imps = [sum(jnp.dot(part, c2s_ref[...], preferred_element_type=F32) for part in _split3(jnp.sum(pc, axis=0)))
            for pc in pcs]
    gated = [os[b].reshape(p, tq, NSA_DH) * _head_gates(gl_ref[b], 0) for b in batch]
    scores =[jnp.where(forced, FORCE_SCORE, jnp.where(visible, imp, NEG_BIG)) for imp in imps]
    members = [jnp.zeros((tq, LANES), F32)] * bsz
    for _ in range(N_SEL):
        hits = [lane == jnp.argmax(score, axis=-1, keepdims=True) for score in scores]
        members = [jnp.where(hit, 1.0, member) for hit, member in zip(hits, members)]
        scores = [jnp.where(hit, -jnp.inf, score) for hit, score in zip(hits, scores)]
    for b in batch:
        pen = jnp.where((members[b] > 0.5) & visible, 0.0, NEG_BIG).astype(qa_ref.dtype)
        qa_ref[b] = jnp.concatenate([jnp.concatenate([pen] * p, axis=0), qs[b]], axis=1).reshape(p, tq, 2 * LANES)
    return gated


def _head_gates(gl, branch):
    gates = jax.nn.sigmoid(gl.astype(F32))
    return jnp.stack([gates[:, 3 * hp + branch:3 * hp + branch + 1] for hp in range(NSA_HPG)])


def _gate_spec(bsz, tq):
    return pl.BlockSpec((bsz, tq, LANES), lambda h, i: (0, i, COL_NG // LANES + h))


def _q_spec(bsz, tq):
    width = NSA_HPG * NSA_DH
    return pl.BlockSpec((bsz, tq, width), lambda h, i: (0, i, COL_NQ // width + h))


def _win_part(qs, k_refs, v_refs, b_refs, gl_ref, *, tq):
    nk = len(k_refs)
    p = NSA_HPG
    bsz = len(qs)
    t0 = pl.program_id(1) * tq
    row = lax.broadcasted_iota(jnp.int32, (tq, tq), 0)
    col = lax.broadcasted_iota(jnp.int32, (tq, tq), 1)
    masks = [((row - col + d * tq >= 0) & (row - col + d * tq < WINDOW) & (t0 - d * tq >= 0))[None]
             for d in range(nk)]
    biases = [b_ref[:, 0] for b_ref in b_refs]
    nt = (((1,), (1,)), ((), ()))
    batch = range(bsz)
    lgs = [[lax.dot_general(qs[b], k_ref[b], nt, preferred_element_type=F32).reshape(p, tq, tq)
            for k_ref in k_refs] for b in batch]
    lgs = [[jnp.where(mask, lg + bias, NEG_BIG) for lg, mask, bias in zip(lgs[b], masks, biases)] for b in batch]
    ms = [functools.reduce(jnp.maximum, [jnp.max(lg, -1, keepdims=True) for lg in lgs[b]]) for b in batch]
    es = [[jnp.exp2(lg - ms[b]).reshape(p * tq, tq).astype(BF16) for lg in lgs[b]] for b in batch]
    os = [sum(jnp.dot(e, v_ref[b], preferred_element_type=F32) for e, v_ref in zip(es[b], v_refs)) for b in batch]
    return [(os[b] / pltpu.roll(os[b], NSA_DH, 1))[:, :NSA_DH].reshape(p, tq, NSA_DH) * _head_gates(gl_ref[b], 2)
            for b in batch]


def _cmp_win_kernel(*refs, tq, n_back):
    nk = n_back + 1
    q_ref, kc_ref, vc_ref, bc_ref, c2s_ref = refs[:5]
    k_refs, v_refs, b_refs = refs[5:5 + nk], refs[5 + nk:5 + 2 * nk], refs[5 + 2 * nk:5 + 3 * nk]
    gl_ref, o_ref, qa_ref = refs[5 + 3 * nk:]
    qs = [_stack_heads(q_ref[b]) for b in range(q_ref.shape[0])]
    o_cmp = _cmp_part(qs, kc_ref, vc_ref, bc_ref, c2s_ref, gl_ref, qa_ref, tq=tq)
    o_win = _win_part(qs, k_refs, v_refs, b_refs, gl_ref, tq=tq)
    for b in range(len(qs)):
        o_ref[b] = o_cmp[b] + o_win[b]


def _cmp_win_branch(proj, kc2, vc, bias_c, c2s, kw2, vw_aug, bias_t, tq):
    bsz, s, _ = proj.shape
    g, p, dh = NSA_GROUPS, NSA_HPG, NSA_DH
    ncp = kc2.shape[2]
    n_back = WINDOW // tq
    back = [lambda h, i, d=d: (0, h, jnp.maximum(i - d, 0), 0) for d in range(n_back + 1)]
    qmap = lambda h, i: (0, h, 0, i, 0)
    in_specs = ([_q_spec(bsz, tq),
                 pl.BlockSpec((bsz, None, ncp, LANES), lambda h, i: (0, h, 0, 0)),
                 pl.BlockSpec((bsz, None, ncp, dh), lambda h, i: (0, h, 0, 0)),
                 pl.BlockSpec((p, 1, tq, ncp), lambda h, i: (h, i, 0, 0)),
                 pl.BlockSpec((ncp, LANES), lambda h, i: (0, 0))]
                + [pl.BlockSpec((bsz, None, tq, LANES), m) for m in back]
                + [pl.BlockSpec((bsz, None, tq, LANES), m) for m in back]
                + [pl.BlockSpec((p, 1, tq, tq), lambda h, i, d=d: (h, d, 0, 0)) for d in range(n_back + 1)]
                + [_gate_spec(bsz, tq)])
    return pl.pallas_call(
        functools.partial(_cmp_win_kernel, tq=tq, n_back=n_back),
        grid=(g, s // tq),
        in_specs=in_specs,
        out_specs=[pl.BlockSpec((bsz, None, p, tq, dh), qmap),
                   pl.BlockSpec((bsz, None, p, tq, 2 * LANES), qmap)],
        out_shape=[jax.ShapeDtypeStruct((bsz, g, p, s, dh), F32),
                   jax.ShapeDtypeStruct((bsz, g, p, s, 2 * LANES), BF16)],
        compiler_params=_cparams(("parallel", "parallel")),
    )(proj, kc2, vc, bias_c, c2s, *([kw2] * (n_back + 1)), *([vw_aug] * (n_back + 1)),
      *([bias_t] * (n_back + 1)), proj)


def _sel_kernel(it_ref, jt_ref, q_ref, k_ref, v_ref, bias_ref, gl_ref, acc_ref, o_ref, m_sc, a_sc,
                *, tq, tk, bsz):
    p = NSA_HPG
    nq = p * tq
    step = pl.program_id(1)
    i = it_ref[step]
    j = jt_ref[step]
    t0 = i * tq
    s0 = j * tk
    last = s0 + tk > t0

    @pl.when(j == 0)
    def _():
        m_sc[...] = jnp.full_like(m_sc, NEG_BIG)
        a_sc[...] = jnp.zeros_like(a_sc)

    def accumulate(on_diagonal):
        nt = (((1,), (1,)), ((), ()))
        bias = bias_ref[:, 0].reshape(nq, tk)
        if on_diagonal:
            causal = (t0 + lax.broadcasted_iota(jnp.int32, (tq, tk), 0)
                      >= s0 + lax.broadcasted_iota(jnp.int32, (tq, tk), 1))
            causal = jnp.concatenate([causal] * p, axis=0)
        for b in range(bsz):
            lg = lax.dot_general(q_ref[b].reshape(nq, 2 * LANES), k_ref[b], nt, preferred_element_type=F32) + bias
            if on_diagonal:
                lg = jnp.where(causal, lg, NEG_BIG)
            m_old = m_sc[b]
            m_new = jnp.maximum(m_old, jnp.max(lg, -1, keepdims=True))
            e = jnp.exp2(lg - jnp.concatenate([m_new] * (tk // LANES), axis=1))
            a_sc[b] = (jnp.exp2(m_old - m_new) * a_sc[b]
                       + jnp.dot(e.astype(BF16), v_ref[b], preferred_element_type=F32))
            m_sc[b] = m_new

    @pl.when(jnp.logical_not(last))
    def _():
        accumulate(False)

    @pl.when(last)
    def _():
        accumulate(True)
        gate = jax.nn.sigmoid(gl_ref[:, :, :, 1:2].astype(F32))
        a = a_sc[...]
        o = (a / pltpu.roll(a, NSA_DH, 2))[:, :, :NSA_DH].reshape(bsz, p, tq, NSA_DH)
        o_ref[...] = (acc_ref[...] + o * gate).astype(o_ref.dtype)


def _sel_branch(q_aug, k_aug, v_aug, bias_t, gl5, acc, tq, tk):
    bsz, g, p, s, wq = q_aug.shape
    dh = acc.shape[-1]
    r = tk // tq
    pairs = [(i, j) for i in range(s // tq) for j in range(i // r + 1)]
    it = jnp.asarray([ij[0] for ij in pairs], jnp.int32)
    jt = jnp.asarray([ij[1] for ij in pairs], jnp.int32)
    nd = bias_t.shape[1] - 1
    qmap = lambda h, t, it, jt: (0, h, 0, it[t], 0)
    kmap = lambda h, t, it, jt: (0, h, jt[t], 0)
    grid_spec = pltpu.PrefetchScalarGridSpec(
        num_scalar_prefetch=2,
        grid=(g, len(pairs)),
        in_specs=[pl.BlockSpec((bsz, None, p, tq, wq), qmap),
                  pl.BlockSpec((bsz, None, tk, wq), kmap),
                  pl.BlockSpec((bsz, None, tk, LANES), kmap),
                  pl.BlockSpec((p, 1, tq, tk),
                               lambda h, t, it, jt: (h, jnp.minimum(it[t] - r * jt[t], nd), 0, 0)),
                  pl.BlockSpec((bsz, None, p, tq, 3), qmap),
                  pl.BlockSpec((bsz, None, p, tq, dh), qmap)],
        out_specs=pl.BlockSpec((bsz, None, p, tq, dh), qmap),
        scratch_shapes=[pltpu.VMEM((bsz, p * tq, LANES), F32),
                        pltpu.VMEM((bsz, p * tq, LANES), F32)],
    )
    return pl.pallas_call(
        functools.partial(_sel_kernel, tq=tq, tk=tk, bsz=bsz),
        grid_spec=grid_spec,
        out_shape=jax.ShapeDtypeStruct((bsz, g, p, s, dh), BF16),
        compiler_params=_cparams(("parallel", "arbitrary")),
    )(it, jt, q_aug, k_aug, v_aug, bias_t, gl5, acc)


def _merge_kernel(x_ref, oa_ref, ob_ref, ga_ref, gb_ref, wa_ref, wb_ref, wo_ref, g_ref, b_ref, o_ref):
    ya = jnp.dot(oa_ref[...].astype(BF16), wa_ref[...], preferred_element_type=F32)
    yb = jnp.dot(ob_ref[...].astype(BF16), wb_ref[...], preferred_element_type=F32)
    y = jax.nn.sigmoid(ga_ref[...].astype(F32)) * ya + jax.nn.sigmoid(gb_ref[...].astype(F32)) * yb
    mix = jnp.dot(y.astype(BF16), wo_ref[...], preferred_element_type=F32)
    o_ref[...] = _layer_norm(DN_ALPHA * x_ref[...] + mix, g_ref[...], b_ref[...])


def _merge(x, o_a, o_b, proj, wa, wb, wo, g, b, tm):
    t, d = x.shape
    nga = COL_GA // d
    row = lambda i: (i, 0)
    const = lambda i: (0, 0)
    return pl.pallas_call(
        _merge_kernel,
        grid=(t // tm,),
        in_specs=[pl.BlockSpec((tm, d), row),
                  pl.BlockSpec((tm, HG_WIDTH), row),
                  pl.BlockSpec((tm, NSA_WIDTH), row),
                  pl.BlockSpec((tm, d), lambda i: (i, nga)),
                  pl.BlockSpec((tm, d), lambda i: (i, nga + 1)),
                  pl.BlockSpec((HG_WIDTH, d), const),
                  pl.BlockSpec((NSA_WIDTH, d), const),
                  pl.BlockSpec((d, d), const),
                  pl.BlockSpec((1, d), const),
                  pl.BlockSpec((1, d), const)],
        out_specs=pl.BlockSpec((tm, d), row),
        out_shape=jax.ShapeDtypeStruct((t, d), F32),
        compiler_params=_cparams(("parallel",)),
    )(x, o_a, o_b, proj, proj, wa, wb, wo, g, b)


def _swiglu_step(xb, wg_ref, wu_ref, wd_ref):
    hg = jnp.dot(xb, wg_ref[...], preferred_element_type=F32)
    hu = jnp.dot(xb, wu_ref[...], preferred_element_type=F32)
    h = (hg * jax.nn.sigmoid(hg)) * hu
    return jnp.dot(h.astype(BF16), wd_ref[...], preferred_element_type=F32)


def _ffn_kernel(x_ref, wg_ref, wu_ref, wd_ref, g_ref, b_ref, o_ref):
    x = x_ref[...]
    f = _swiglu_step(x.astype(BF16), wg_ref, wu_ref, wd_ref)
    o_ref[...] = _layer_norm(DN_ALPHA * x + f, g_ref[...], b_ref[...])


def _ffn(x, wg, wu, wd, g, b, tm):
    t, d = x.shape
    f = wg.shape[1]
    const = lambda i: (0, 0)
    once = pl.Buffered(1)
    return pl.pallas_call(
        _ffn_kernel,
        grid=(t // tm,),
        in_specs=[pl.BlockSpec((tm, d), lambda i: (i, 0)),
                  pl.BlockSpec((d, f), const, pipeline_mode=once),
                  pl.BlockSpec((d, f), const, pipeline_mode=once),
                  pl.BlockSpec((f, d), const, pipeline_mode=once),
                  pl.BlockSpec((1, d), const),
                  pl.BlockSpec((1, d), const)],
        out_specs=pl.BlockSpec((tm, d), lambda i: (i, 0)),
        out_shape=jax.ShapeDtypeStruct((t, d), F32),
        compiler_params=_cparams(("parallel",)),
    )(x, wg, wu, wd, g, b)


def _router_kernel(x_ref, w_ref, o_ref):
    logits = jnp.dot(x_ref[...], w_ref[...], preferred_element_type=F32, precision=lax.Precision.HIGHEST)
    lane = lax.broadcasted_iota(jnp.int32, logits.shape, 1).astype(F32)
    logits = jnp.where(lane < N_EXPERTS, logits, -jnp.inf)
    v1 = jnp.max(logits, -1, keepdims=True)
    e1 = jnp.min(jnp.where(logits == v1, lane, float(LANES)), -1, keepdims=True)
    rest = jnp.where(lane == e1, -jnp.inf, logits)
    v2 = jnp.max(rest, -1, keepdims=True)
    e2 = jnp.min(jnp.where(rest == v2, lane, float(LANES)), -1, keepdims=True)
    x2 = jnp.exp(v2 - v1)
    den = 1.0 + x2
    o_ref[...] = jnp.where(lane == 0, e1, jnp.where(lane == 1, e2, jnp.where(
        lane == 2, 1.0 / den, jnp.where(lane == 3, x2 / den, 0.0))))


def _router(x, w, tm):
    t, d = x.shape
    return pl.pallas_call(
        _router_kernel,
        grid=(t // tm,),
        in_specs=[pl.BlockSpec((tm, d), lambda i: (i, 0)), pl.BlockSpec((d, LANES), lambda i: (0, 0))],
        out_specs=pl.BlockSpec((tm, LANES), lambda i: (i, 0)),
        out_shape=jax.ShapeDtypeStruct((t, LANES), F32),
        compiler_params=_cparams(("parallel",)),
    )(x, w)


def _expert_kernel(be_ref, x_ref, wg_ref, wu_ref, wd_ref, o_ref, acc_ref):
    j = pl.program_id(1)

    @pl.when(j == 0)
    def _():
        acc_ref[...] = jnp.zeros_like(acc_ref)

    xb = x_ref[...].astype(BF16)
    hg = jnp.dot(xb, wg_ref[...].astype(BF16), preferred_element_type=F32)
    hu = jnp.dot(xb, wu_ref[...].astype(BF16), preferred_element_type=F32)
    h = (hg * jax.nn.sigmoid(hg)) * hu
    acc_ref[...] += jnp.dot(h.astype(BF16), wd_ref[...].astype(BF16), preferred_element_type=F32)

    @pl.when(j == pl.num_programs(1) - 1)
    def _():
        o_ref[...] = acc_ref[...].astype(o_ref.dtype)


def _experts(blk_e, xs, wg, wu, wd, tf):
    rows, d = xs.shape
    f = wg.shape[2]
    tm = MOE_ROW_BLOCK
    grid_spec = pltpu.PrefetchScalarGridSpec(
        num_scalar_prefetch=1,
        grid=(rows // tm, f // tf),
        in_specs=[pl.BlockSpec((tm, d), lambda i, j, be: (i, 0)),
                  pl.BlockSpec((None, d, tf), lambda i, j, be: (be[i], 0, j)),
                  pl.BlockSpec((None, d, tf), lambda i, j, be: (be[i], 0, j)),
                  pl.BlockSpec((None, tf, d), lambda i, j, be: (be[i], j, 0))],
        out_specs=pl.BlockSpec((tm, d), lambda i, j, be: (i, 0)),
        scratch_shapes=[pltpu.VMEM((tm, d), F32)],
    )
    return pl.pallas_call(
        _expert_kernel,
        grid_spec=grid_spec,
        out_shape=jax.ShapeDtypeStruct((rows, d), F32),
        compiler_params=_cparams(("parallel", "arbitrary")),
    )(blk_e, xs, wg, wu, wd)


def _combine_kernel(x_ref, y1_ref, y2_ref, gt_ref, g_ref, b_ref, o_ref):
    f = y1_ref[...].astype(F32) * gt_ref[:, 2:3] + y2_ref[...].astype(F32) * gt_ref[:, 3:4]
    o_ref[...] = _layer_norm(DN_ALPHA * x_ref[...] + f, g_ref[...], b_ref[...])


def _combine(x, y1, y2, route, g, b, tm):
    t, d = x.shape
    row = lambda i: (i, 0)
    const = lambda i: (0, 0)
    return pl.pallas_call(
        _combine_kernel,
        grid=(t // tm,),
        in_specs=[pl.BlockSpec((tm, d), row), pl.BlockSpec((tm, d), row), pl.BlockSpec((tm, d), row),
                  pl.BlockSpec((tm, LANES), row), pl.BlockSpec((1, d), const), pl.BlockSpec((1, d), const)],
        out_specs=pl.BlockSpec((tm, d), row),
        out_shape=jax.ShapeDtypeStruct((t, d), F32),
        compiler_params=_cparams(("parallel",)),
    )(x, y1, y2, route, g, b)


def _moe(x, w_router, wg, wu, wd, g, b):
    t, d = x.shape
    tk_ = t * TOP_K
    route = _router(x, jnp.pad(w_router, ((0, 0), (0, LANES - N_EXPERTS))), 512)
    flat_e = route[:, :TOP_K].astype(jnp.int32).reshape(-1)
    onehot = (flat_e[:, None] == jnp.arange(N_EXPERTS)[None, :]).astype(jnp.int32)
    csum = jnp.cumsum(onehot, axis=0)
    counts = csum[-1]
    rank = jnp.sum(onehot * csum, axis=1) - 1
    padded = (counts + MOE_ROW_BLOCK - 1) // MOE_ROW_BLOCK * MOE_ROW_BLOCK
    pend = jnp.cumsum(padded)
    dest = (pend - padded)[flat_e] + rank
    n_blocks = -(-(tk_ + N_EXPERTS * (MOE_ROW_BLOCK - 1)) // MOE_ROW_BLOCK)
    n_rows = n_blocks * MOE_ROW_BLOCK
    row_tok = (jnp.arange(n_rows, dtype=jnp.int32) % t).at[dest].set(jnp.arange(tk_, dtype=jnp.int32) // TOP_K)
    blk_e = jnp.minimum(jnp.searchsorted(pend, jnp.arange(n_blocks) * MOE_ROW_BLOCK, side='right'),
                        N_EXPERTS - 1).astype(jnp.int32)
    xs = x[row_tok]
    ys = _experts(blk_e, xs, wg, wu, wd, 512)
    dest2 = dest.reshape(t, TOP_K)
    return _combine(x, ys[dest2[:, 0]], ys[dest2[:, 1]], route, g, b, 512)


def _pack_w_in(w_in):
    offs = np.concatenate([[0], np.cumsum(IN_SIZES)])
    seg = [w_in[:, offs[j]:offs[j + 1]] for j in range(len(IN_SIZES))]
    seg[4] = seg[4] * (NSA_DH ** -0.5 * LOG2E)
    seg[11] = jnp.pad(seg[11].reshape(-1, NSA_GROUPS, 3 * NSA_HPG),
                      ((0, 0), (0, 0), (0, LANES - 3 * NSA_HPG))).reshape(-1, NSA_GROUPS * LANES)
    seg = seg[12:14] + seg[0:12]
    packed = jnp.concatenate(seg + [jnp.zeros((w_in.shape[0], PROJ_WP - PROJ_W), w_in.dtype)], axis=1)
    return packed.astype(BF16)


def _cmp_to_sel(n_cmp_pad, n_cmp):
    cs = np.arange(n_cmp_pad)[:, None] * CMP_STRIDE
    ss = np.arange(LANES)[None, :] * SEL_BLOCK
    overlap = np.clip(np.minimum(cs + CMP_LEN, ss + SEL_BLOCK) - np.maximum(cs, ss), 0, None) / CMP_LEN
    overlap[n_cmp:] = 0.0
    return jnp.asarray(overlap, BF16)


def _token_mixer(x, w_in_p, lb, hg_norm_w, cmp_pos, cmp_w1, cmp_w2, bias_c, bias_t, wa, wb, wo, ln_g, ln_b):
    bsz, s, d = x.shape
    g, p, dh = NSA_GROUPS, NSA_HPG, NSA_DH
    xf = x.reshape(bsz * s, d)
    proj = _project(xf, w_in_p, 1024, PROJ_WP // 2).reshape(bsz, s, PROJ_WP)
    o_a = _hgrn(proj, lb, hg_norm_w, 1024)

    def heads(c0, width):
        return proj[:, :, c0:c0 + width]

    kv = heads(COL_KV, 6 * KV_WIDTH).reshape(bsz, s, 6, g, dh)
    n16 = s // CMP_STRIDE
    a = kv[:, :, 0:2].reshape(bsz, n16, CMP_STRIDE, 2, g, dh).transpose(3, 0, 4, 1, 2, 5)
    a = a.reshape(2, bsz, g, n16, CMP_STRIDE * dh)
    kvc = _compress(a, cmp_pos.reshape(2, 1, CMP_LEN * dh), cmp_w1.astype(BF16), cmp_w2.astype(BF16))
    kvh = kv[:, :, 2:6].astype(BF16).transpose(2, 0, 3, 1, 4)
    n_cmp = (s - CMP_LEN) // CMP_STRIDE + 1
    ones = jnp.ones((bsz, g, s, LANES - dh), BF16)
    acc, q_aug = _cmp_win_branch(proj, jnp.concatenate([kvc[0], kvc[0]], axis=-1), kvc[1], bias_c,
                                 _cmp_to_sel(n16, n_cmp), jnp.concatenate([kvh[2], kvh[2]], axis=-1),
                                 jnp.concatenate([kvh[3], ones], axis=-1), bias_t, ATT_TQ)
    block_of_key = (np.arange(s)[:, None] // SEL_BLOCK == np.arange(LANES)[None, :])
    k_aug = jnp.concatenate([jnp.broadcast_to(jnp.asarray(block_of_key, BF16), (bsz, g, s, LANES)), kvh[0],
                             kvh[0]], axis=-1)
    v_aug = jnp.concatenate([kvh[1], ones], axis=-1)
    gl5 = heads(COL_NG, g * LANES).reshape(bsz, s, g, LANES)[..., :3 * p].reshape(bsz, s, g, p, 3)
    o_b = _sel_branch(q_aug, k_aug, v_aug, bias_t, gl5.transpose(0, 2, 3, 1, 4), acc, ATT_TQ, ATT_TK)
    o_b = o_b.transpose(0, 3, 1, 2, 4).reshape(bsz * s, NSA_WIDTH)
    return _merge(xf, o_a.reshape(bsz * s, HG_WIDTH), o_b, proj.reshape(bsz * s, PROJ_WP),
                  wa, wb, wo, ln_g, ln_b, 512)


def kernel(x, w_in, hg_lb_logits, hg_norm_w, cmp_pos, cmp_w1, cmp_w2, rel_bias, w_branch_a, w_branch_b,
           w_out, ln1_g, ln1_b, ln2_g, ln2_b, ffn_w_gate, ffn_w_up, ffn_w_down, moe_router, moe_w_gate,
           moe_w_up, moe_w_down):
    bsz, s, d = x.shape
    depth = w_in.shape[0]
    p_lb = jax.nn.softmax(hg_lb_logits.astype(F32), axis=0)
    lbs = jnp.cumsum(p_lb, axis=0) - p_lb[0]
    n16 = s // CMP_STRIDE
    bias_c = _bias_tiles(rel_bias, s // ATT_TQ, ATT_TQ, n16,
                         lambda t, r, c: t * ATT_TQ + r - (c * CMP_STRIDE + CMP_LEN - 1))
    bias_t = _bias_tiles(rel_bias, BIAS_ND + 1, ATT_TQ, ATT_TK, lambda t, r, c: t * ATT_TQ + r - c)
    f_pad = -(-D_FF // LANES) * LANES - D_FF
    xf = x.reshape(bsz * s, d)
    for l in range(depth):
        xf = _token_mixer(xf.reshape(bsz, s, d), _pack_w_in(w_in[l]), lbs[l][None], hg_norm_w[l][None],
                          cmp_pos[l], cmp_w1[l], cmp_w2[l], bias_c, bias_t,
                          w_branch_a[l].astype(BF16), w_branch_b[l].astype(BF16), w_out[l].astype(BF16),
                          ln1_g[l][None], ln1_b[l][None])
        if l % 2 == 0:
            wg = jnp.pad(ffn_w_gate[l // 2], ((0, 0), (0, f_pad))).astype(BF16)
            wu = jnp.pad(ffn_w_up[l // 2], ((0, 0), (0, f_pad))).astype(BF16)
            wd = jnp.pad(ffn_w_down[l // 2], ((0, f_pad), (0, 0))).astype(BF16)
            xf = _ffn(xf, wg, wu, wd, ln2_g[l][None], ln2_b[l][None], 512)
        else:
            xf = _moe(xf, moe_router[l // 2], moe_w_gate[l // 2], moe_w_up[l // 2], moe_w_down[l // 2],
                      ln2_g[l][None], ln2_b[l][None])
    return xf.reshape(bsz, s, d)
```

```python
import functools
import math

import jax
import jax.numpy as jnp
import numpy as np
from jax import lax
from jax.experimental import pallas as pl
from jax.experimental.pallas import tpu as pltpu

F32 = jnp.float32
BF16 = jnp.bfloat16

D_MODEL = 1024
DEPTH = 2
HG_HEADS = 4
HG_DK = 128
HG_WIDTH = HG_HEADS * HG_DK
HG_CHUNK = 64
HG_SUB = 16
HG_HALF = HG_SUB // 2
NSA_HEADS = 8
NSA_GROUPS = 2
NSA_HPG = NSA_HEADS // NSA_GROUPS
NSA_DH = 64
NSA_WIDTH = NSA_HEADS * NSA_DH
KV_WIDTH = NSA_GROUPS * NSA_DH
CMP_LEN = 32
CMP_STRIDE = 16
CMP_HIDDEN = 2 * NSA_DH
SEL_BLOCK = 64
N_SEL = 16
WINDOW = 512
FORCE_SCORE = 1e9
NEG_BIG = -1e30
REL_BUCKETS = 32
REL_MAX_DIST = 2048
D_FF = 2752
N_EXPERTS = 8
TOP_K = 2
D_FF_EXPERT = 3584
MOE_ROW_BLOCK = 1024
DN_ALPHA = (2 * DEPTH) ** 0.25
LN_EPS = 1e-5
IN_SIZES = (HG_WIDTH, HG_WIDTH, HG_WIDTH, HG_WIDTH, NSA_WIDTH,
            KV_WIDTH, KV_WIDTH, KV_WIDTH, KV_WIDTH, KV_WIDTH, KV_WIDTH,
            3 * NSA_HEADS, D_MODEL, D_MODEL)

LANES = 128
LOG2E = 1.0 / math.log(2.0)
COL_GA = 0
COL_HG = 2 * D_MODEL
COL_NQ = COL_HG + 4 * HG_WIDTH
COL_KV = COL_NQ + NSA_WIDTH
COL_NG = COL_KV + 6 * KV_WIDTH
PROJ_W = COL_NG + LANES
PROJ_TN = 512
PROJ_WP = -(-PROJ_W // PROJ_TN) * PROJ_TN

ATT_TQ = 256
ATT_TK = 512
BIAS_ND = -(-(REL_MAX_DIST + ATT_TK) // ATT_TQ)
VMEM_LIMIT = 48 * 1024 * 1024


def _cparams(sem):
    return pltpu.CompilerParams(dimension_semantics=sem, vmem_limit_bytes=VMEM_LIMIT)


def _proj_kernel(x_ref, w_ref, o_ref, xb_ref):
    @pl.when(pl.program_id(1) == 0)
    def _():
        xb_ref[...] = x_ref[...].astype(BF16)

    o_ref[...] = jnp.dot(xb_ref[...], w_ref[...], preferred_element_type=F32).astype(o_ref.dtype)


def _project(x, w, tm, tn):
    m, k = x.shape
    n = w.shape[1]
    return pl.pallas_call(
        _proj_kernel,
        grid=(m // tm, n // tn),
        in_specs=[pl.BlockSpec((tm, k), lambda i, j: (i, 0)),
                  pl.BlockSpec((k, tn), lambda i, j: (0, j))],
        out_specs=pl.BlockSpec((tm, tn), lambda i, j: (i, j)),
        out_shape=jax.ShapeDtypeStruct((m, n), BF16),
        scratch_shapes=[pltpu.VMEM((tm, k), BF16)],
        compiler_params=_cparams(("parallel", "arbitrary")),
    )(x, w)


def _layer_norm(y, g, b):
    mu = jnp.mean(y, -1, keepdims=True)
    yc = y - mu
    var = jnp.mean(yc * yc, -1, keepdims=True)
    return yc * lax.rsqrt(var + LN_EPS) * g + b


def _split3(x):
    parts = []
    for _ in range(3):
        part = x.astype(BF16)
        parts.append(part)
        x = x - part.astype(F32)
    return parts


def _cumsum_rows(tril, x):
    return sum(jnp.dot(tril, part, preferred_element_type=F32) for part in _split3(x))


def _hgrn_kernel(q_ref, f_ref, i_ref, g_ref, lb_ref, nw_ref, o_ref, st_ref, *, n_chunks):
    @pl.when(pl.program_id(1) == 0)
    def _():
        st_ref[...] = jnp.zeros_like(st_ref)

    c = HG_CHUNK
    nt = (((1,), (1,)), ((), ()))
    row = lax.broadcasted_iota(jnp.int32, (c, c), 0)
    col = lax.broadcasted_iota(jnp.int32, (c, c), 1)
    tril = (row >= col).astype(BF16)
    row1 = lax.broadcasted_iota(jnp.int32, (c, 1), 0)
    half_pos = row1 % HG_HALF
    second_half = row1 % HG_SUB >= HG_HALF
    same_sub = row // HG_SUB == col // HG_SUB

    w = HG_HEADS * HG_DK
    head_lanes = [slice(h * HG_DK, (h + 1) * HG_DK) for h in range(HG_HEADS)]

    def per_head(fn):
        return jnp.concatenate([fn(h, head_lanes[h]) for h in range(HG_HEADS)], axis=1)

    def shift_in_half(x, lag):
        return pltpu.roll(x.reshape(c // HG_HALF, HG_HALF, w), lag, 1).reshape(c, w)

    def chunk(ci, carry):
        r0 = pl.multiple_of(ci * c, c)
        q = q_ref[pl.ds(r0, c), :].astype(F32)
        z = f_ref[pl.ds(r0, c), :].astype(F32)
        v = i_ref[pl.ds(r0, c), :].astype(F32)
        g = g_ref[pl.ds(r0, c), :].astype(F32)
        k = (1.0 - lb_ref[...]) * jax.nn.sigmoid(-z)
        b = _cumsum_rows(tril, jnp.log1p(-k)) * LOG2E
        b_last = b[c - 1:c, :]
        vb = v.astype(BF16)
        qe = (q * jnp.exp2(b)).astype(BF16)
        o = per_head(lambda h, hl: lax.dot_general(qe[:, hl], st_ref[h].astype(BF16), nt,
                                                  preferred_element_type=F32))
        att_rows = [[jnp.zeros((HG_SUB, c), F32)] * HG_HEADS]
        for sb in range(1, c // HG_SUB):
            lo = sb * HG_SUB
            ref_b = b[lo - 1:lo, :]
            qt = (q[lo:lo + HG_SUB, :] * jnp.exp2(b[lo:lo + HG_SUB, :] - ref_b)).astype(BF16)
            kt = (k * jnp.exp2(jnp.where(row1 < lo, ref_b - b, -jnp.inf))).astype(BF16)
            att_rows.append([lax.dot_general(qt[:, hl], kt[:, hl], nt, preferred_element_type=F32)
                             for hl in head_lanes])
        mid = jnp.concatenate(
            [jnp.broadcast_to(b[lo + HG_HALF - 1:lo + HG_HALF, :], (HG_SUB, w)) for lo in range(0, c, HG_SUB)],
            axis=0)
        q2 = (q * jnp.exp2(jnp.where(second_half, b - mid, -jnp.inf))).astype(BF16)
        k2 = (k * jnp.exp2(jnp.where(second_half, -jnp.inf, mid - b))).astype(BF16)
        att_half = [lax.dot_general(q2[:, hl], k2[:, hl], nt, preferred_element_type=F32) for hl in head_lanes]

        def intra(h, hl):
            att = jnp.concatenate([rows[h] for rows in att_rows], axis=0) + jnp.where(same_sub, att_half[h], 0.0)
            return jnp.dot(att.astype(BF16), vb[:, hl], preferred_element_type=F32)

        o = o + per_head(intra)
        for lag in range(HG_HALF):
            if lag == 0:
                ks, bs, vs = k, b, v
            else:
                ks, bs, vs = shift_in_half(k, lag), shift_in_half(b, lag), shift_in_half(v, lag)
            valid = half_pos >= lag
            prod = q * ks * jnp.exp2(jnp.where(valid, b - bs, 0.0))
            a = per_head(lambda h, hl: jnp.broadcast_to(
                jnp.sum(prod[:, hl], axis=1, keepdims=True), (c, HG_DK)))
            o = o + jnp.where(valid, a, 0.0) * vs
        khat = (k * jnp.exp2(b_last - b)).astype(BF16)
        decay = jnp.exp2(b_last)
        for h, hl in enumerate(head_lanes):
            st_ref[h] = st_ref[h] * decay[:, hl] + lax.dot_general(
                vb[:, hl], khat[:, hl], (((0,), (0,)), ((), ())), preferred_element_type=F32)
        sq = o * o
        ms = per_head(lambda h, hl: jnp.broadcast_to(jnp.mean(sq[:, hl], -1, keepdims=True), (c, HG_DK)))
        o = o * lax.rsqrt(ms + 1e-6)
        o_ref[pl.ds(r0, c), :] = (o * nw_ref[...] * (g * jax.nn.sigmoid(g))).astype(o_ref.dtype)
        return carry

    lax.fori_loop(0, n_chunks, chunk, 0, unroll=4)


def _hgrn(proj, lb, norm_w, ts):
    bsz, s, _ = proj.shape
    col0 = COL_HG // HG_WIDTH

    def col_spec(n):
        return pl.BlockSpec((None, ts, HG_WIDTH), lambda b, t: (b, t, col0 + n))

    head_spec = pl.BlockSpec((1, HG_WIDTH), lambda b, t: (0, 0))
    return pl.pallas_call(
        functools.partial(_hgrn_kernel, n_chunks=ts // HG_CHUNK),
        grid=(bsz, s // ts),
        in_specs=[col_spec(0), col_spec(1), col_spec(2), col_spec(3), head_spec, head_spec],
        out_specs=pl.BlockSpec((None, ts, HG_WIDTH), lambda b, t: (b, t, 0)),
        out_shape=jax.ShapeDtypeStruct((bsz, s, HG_WIDTH), BF16),
        scratch_shapes=[pltpu.VMEM((HG_HEADS, HG_DK, HG_DK), F32)],
        compiler_params=_cparams(("parallel", "arbitrary")),
    )(proj, proj, proj, proj, lb, norm_w)


def _compress_kernel(a_ref, pos_ref, w1_ref, w2_ref, o_ref):
    half = CMP_STRIDE * NSA_DH
    a = a_ref[...].astype(F32)
    n = a.shape[0]
    a1 = (a + pos_ref[:, :half]).astype(BF16)
    a2 = (a + pos_ref[:, half:]).astype(BF16)
    y1 = jnp.dot(a1, w1_ref[:half, :], preferred_element_type=F32)
    y2 = jnp.dot(a2, w1_ref[half:, :], preferred_element_type=F32)
    hid = jax.nn.gelu(y1 + pltpu.roll(y2, n - 1, 0))
    o_ref[...] = jnp.dot(hid.astype(BF16), w2_ref[...], preferred_element_type=F32).astype(o_ref.dtype)


def _compress(a, pos, w1, w2):
    _, bsz, g, n, width = a.shape
    return pl.pallas_call(
        _compress_kernel,
        grid=(2, bsz, g),
        in_specs=[pl.BlockSpec((None, None, None, n, width), lambda c, b, h: (c, b, h, 0, 0)),
                  pl.BlockSpec((None, 1, 2 * width), lambda c, b, h: (c, 0, 0)),
                  pl.BlockSpec((None, 2 * width, CMP_HIDDEN), lambda c, b, h: (c, 0, 0)),
                  pl.BlockSpec((None, CMP_HIDDEN, NSA_DH), lambda c, b, h: (c, 0, 0))],
        out_specs=pl.BlockSpec((None, None, None, n, NSA_DH), lambda c, b, h: (c, b, h, 0, 0)),
        out_shape=jax.ShapeDtypeStruct((2, bsz, g, n, NSA_DH), BF16),
        compiler_params=_cparams(("parallel", "parallel", "parallel")),
    )(a, pos, w1, w2)


def _bucket_of(dist):
    n = jnp.maximum(dist, 0)
    exact = REL_BUCKETS // 2
    large = exact + (jnp.log(jnp.maximum(n, exact).astype(F32) / exact)
                     / math.log(REL_MAX_DIST / exact) * (REL_BUCKETS - exact)).astype(jnp.int32)
    return jnp.where(n < exact, n, jnp.minimum(large, REL_BUCKETS - 1))


def _bias_kernel(tab_ref, o_ref, *, dist_fn):
    rows, cols = o_ref.shape[2], o_ref.shape[3]
    row = lax.broadcasted_iota(jnp.int32, (rows, cols), 0)
    col = lax.broadcasted_iota(jnp.int32, (rows, cols), 1)
    bucket = _bucket_of(dist_fn(pl.program_id(0), row, col))
    for h in range(NSA_HEADS):
        tab = jnp.broadcast_to(tab_ref[h:h + 1, :] * LOG2E, (rows, LANES))
        for c0 in range(0, cols, LANES):
            o_ref[h, 0, :, c0:c0 + LANES] = jnp.take_along_axis(tab, bucket[:, c0:c0 + LANES], axis=1)


def _bias_tiles(rel_bias, n_tiles, rows, cols, dist_fn):
    tab = jnp.pad(rel_bias.T, ((0, 0), (0, LANES - REL_BUCKETS)))
    return pl.pallas_call(
        functools.partial(_bias_kernel, dist_fn=dist_fn),
        grid=(n_tiles,),
        in_specs=[pl.BlockSpec((NSA_HEADS, LANES), lambda t: (0, 0))],
        out_specs=pl.BlockSpec((NSA_HEADS, 1, rows, cols), lambda t: (0, t, 0, 0)),
        out_shape=jax.ShapeDtypeStruct((NSA_HEADS, n_tiles, rows, cols), F32),
        compiler_params=_cparams(("parallel",)),
    )(tab)


def _stack_heads(qb):
    tq = qb.shape[0]
    half = lax.broadcasted_iota(jnp.int32, (tq, LANES), 1) // NSA_DH
    return jnp.concatenate(
        [jnp.where(half == hp % 2, qb[:, LANES * (hp // 2):LANES * (hp // 2 + 1)], jnp.zeros((tq, LANES), qb.dtype))
         for hp in range(NSA_HPG)], axis=0)


def _cmp_kernel(q_ref, k_ref, v_ref, bias_ref, c2s_ref, gl_ref, o_ref, qa_ref, *, tq):
    p = NSA_HPG
    t0 = pl.program_id(1) * tq
    bsz, ncp = k_ref.shape[0], k_ref.shape[1]
    bias = bias_ref[:, 0]
    tpos = t0 + lax.broadcasted_iota(jnp.int32, (tq, ncp), 0)
    cend = lax.broadcasted_iota(jnp.int32, (tq, ncp), 1) * CMP_STRIDE + (CMP_LEN - 1)
    mask = (tpos >= cend)[None]
    lane = lax.broadcasted_iota(jnp.int32, (tq, LANES), 1)
    cur = (t0 + lax.broadcasted_iota(jnp.int32, (tq, LANES), 0)) // SEL_BLOCK
    forced = (lane == 0) | (lane == cur) | (lane == cur - 1)
    visible = lane <= cur
    batch = range(bsz)
    nt = (((1,), (1,)), ((), ()))
    qs = [_stack_heads(q_ref[b]) for b in batch]
    lgs = [lax.dot_general(qs[b], k_ref[b], nt, preferred_element_type=F32) for b in batch]
    lgs = [jnp.where(mask, lg.reshape(p, tq, ncp) + bias, NEG_BIG) for lg in lgs]
    es = [jnp.where(mask, jnp.exp2(lg - jnp.max(lg, -1, keepdims=True)), 0.0) for lg in lgs]
    pcs = [e / jnp.maximum(jnp.sum(e, -1, keepdims=True), 1e-30) for e in es]
    os = [jnp.dot(pcs[b].reshape(p * tq, ncp).astype(BF16), v_ref[b], preferred_element_type=F32) for b in batch]
    imps = [sum(jnp.dot(part, c2s_ref[...], preferred_element_type=F32) for part in _split3(jnp.sum(pc, axis=0)))
            for pc in pcs]
    for b in batch:
        o_ref[b] = os[b].reshape(p, tq, NSA_DH) * jax.nn.sigmoid(gl_ref[b, :, :, 0:1].astype(F32))
    scores = [jnp.where(forced, FORCE_SCORE, jnp.where(visible, imp, NEG_BIG)) for imp in imps]
    members = [jnp.zeros((tq, LANES), F32)] * bsz
    for _ in range(N_SEL):
        hits = [lane == jnp.argmax(score, axis=-1, keepdims=True) for score in scores]
        members = [jnp.where(hit, 1.0, member) for hit, member in zip(hits, members)]
        scores = [jnp.where(hit, -jnp.inf, score) for hit, score in zip(hits, scores)]
    for b in batch:
        pen = jnp.where((members[b] > 0.5) & visible, 0.0, NEG_BIG).astype(qa_ref.dtype)
        qa_ref[b] = jnp.concatenate([jnp.concatenate([pen] * p, axis=0), qs[b]], axis=1).reshape(p, tq, 2 * LANES)


def _q_spec(bsz, tq):
    width = NSA_HPG * NSA_DH
    return pl.BlockSpec((bsz, tq, width), lambda h, i: (0, i, COL_NQ // width + h))


def _cmp_branch(proj, kc2, vc, bias_c, c2s, gl5, tq):
    bsz, s, _ = proj.shape
    g, p, dh = NSA_GROUPS, NSA_HPG, NSA_DH
    ncp = kc2.shape[2]
    qmap = lambda h, i: (0, h, 0, i, 0)
    return pl.pallas_call(
        functools.partial(_cmp_kernel, tq=tq),
        grid=(g, s // tq),
        in_specs=[_q_spec(bsz, tq),
                  pl.BlockSpec((bsz, None, ncp, LANES), lambda h, i: (0, h, 0, 0)),
                  pl.BlockSpec((bsz, None, ncp, dh), lambda h, i: (0, h, 0, 0)),
                  pl.BlockSpec((p, 1, tq, ncp), lambda h, i: (h, i, 0, 0)),
                  pl.BlockSpec((ncp, LANES), lambda h, i: (0, 0)),
                  pl.BlockSpec((bsz, None, p, tq, 3), qmap)],
        out_specs=[pl.BlockSpec((bsz, None, p, tq, dh), qmap),
                   pl.BlockSpec((bsz, None, p, tq, 2 * LANES), qmap)],
        out_shape=[jax.ShapeDtypeStruct((bsz, g, p, s, dh), F32),
                   jax.ShapeDtypeStruct((bsz, g, p, s, 2 * LANES), BF16)],
        compiler_params=_cparams(("parallel", "parallel")),
    )(proj, kc2, vc, bias_c, c2s, gl5)


def _win_kernel(*refs, tq, n_back):
    nk = n_back + 1
    q_ref, k_refs, v_refs, b_refs = refs[0], refs[1:1 + nk], refs[1 + nk:1 + 2 * nk], refs[1 + 2 * nk:1 + 3 * nk]
    gl_ref, acc_ref, o_ref = refs[1 + 3 * nk:]
    p = NSA_HPG
    bsz = q_ref.shape[0]
    t0 = pl.program_id(1) * tq
    row = lax.broadcasted_iota(jnp.int32, (tq, tq), 0)
    col = lax.broadcasted_iota(jnp.int32, (tq, tq), 1)
    masks = [((row - col + d * tq >= 0) & (row - col + d * tq < WINDOW) & (t0 - d * tq >= 0))[None]
             for d in range(nk)]
    biases = [b_ref[:, 0] for b_ref in b_refs]
    nt = (((1,), (1,)), ((), ()))
    batch = range(bsz)
    qs = [_stack_heads(q_ref[b]) for b in batch]
    lgs = [[lax.dot_general(qs[b], k_ref[b], nt, preferred_element_type=F32).reshape(p, tq, tq)
            for k_ref in k_refs] for b in batch]
    lgs = [[jnp.where(mask, lg + bias, NEG_BIG) for lg, mask, bias in zip(lgs[b], masks, biases)] for b in batch]
    ms = [functools.reduce(jnp.maximum, [jnp.max(lg, -1, keepdims=True) for lg in lgs[b]]) for b in batch]
    es = [[jnp.exp2(lg - ms[b]).reshape(p * tq, tq).astype(BF16) for lg in lgs[b]] for b in batch]
    os = [sum(jnp.dot(e, v_ref[b], preferred_element_type=F32) for e, v_ref in zip(es[b], v_refs)) for b in batch]
    for b in batch:
        o = (os[b] / pltpu.roll(os[b], NSA_DH, 1))[:, :NSA_DH].reshape(p, tq, NSA_DH)
        o_ref[b] = acc_ref[b] + o * jax.nn.sigmoid(gl_ref[b, :, :, 2:3].astype(F32))


def _win_branch(proj, kw2, vw_aug, bias_t, gl5, acc, tq):
    bsz, g, p, s, dh = acc.shape
    n_back = WINDOW // tq
    back = [lambda h, i, d=d: (0, h, jnp.maximum(i - d, 0), 0) for d in range(n_back + 1)]
    qmap = lambda h, i: (0, h, 0, i, 0)
    in_specs = ([_q_spec(bsz, tq)]
                + [pl.BlockSpec((bsz, None, tq, LANES), m) for m in back]
                + [pl.BlockSpec((bsz, None, tq, LANES), m) for m in back]
                + [pl.BlockSpec((p, 1, tq, tq), lambda h, i, d=d: (h, d, 0, 0)) for d in range(n_back + 1)]
                + [pl.BlockSpec((bsz, None, p, tq, 3), qmap), pl.BlockSpec((bsz, None, p, tq, dh), qmap)])
    return pl.pallas_call(
        functools.partial(_win_kernel, tq=tq, n_back=n_back),
        grid=(g, s // tq),
        in_specs=in_specs,
        out_specs=pl.BlockSpec((bsz, None, p, tq, dh), qmap),
        out_shape=jax.ShapeDtypeStruct((bsz, g, p, s, dh), F32),
        input_output_aliases={len(in_specs) - 1: 0},
        compiler_params=_cparams(("parallel", "parallel")),
    )(proj, *([kw2] * (n_back + 1)), *([vw_aug] * (n_back + 1)), *([bias_t] * (n_back + 1)), gl5, acc)


def _sel_kernel(it_ref, jt_ref, q_ref, k_ref, v_ref, bias_ref, gl_ref, acc_ref, o_ref, m_sc, a_sc,
                *, tq, tk, bsz):
    p = NSA_HPG
    nq = p * tq
    step = pl.program_id(1)
    i = it_ref[step]
    j = jt_ref[step]
    t0 = i * tq
    s0 = j * tk
    last = s0 + tk > t0

    @pl.when(j == 0)
    def _():
        m_sc[...] = jnp.full_like(m_sc, NEG_BIG)
        a_sc[...] = jnp.zeros_like(a_sc)

    def accumulate(on_diagonal):
        nt = (((1,), (1,)), ((), ()))
        bias = bias_ref[:, 0].reshape(nq, tk)
        if on_diagonal:
            causal = (t0 + lax.broadcasted_iota(jnp.int32, (tq, tk), 0)
                      >= s0 + lax.broadcasted_iota(jnp.int32, (tq, tk), 1))
            causal = jnp.concatenate([causal] * p, axis=0)
        for b in range(bsz):
            lg = lax.dot_general(q_ref[b].reshape(nq, 2 * LANES), k_ref[b], nt, preferred_element_type=F32) + bias
            if on_diagonal:
                lg = jnp.where(causal, lg, NEG_BIG)
            m_old = m_sc[b]
            m_new = jnp.maximum(m_old, jnp.max(lg, -1, keepdims=True))
            e = jnp.exp2(lg - jnp.concatenate([m_new] * (tk // LANES), axis=1))
            a_sc[b] = (jnp.exp2(m_old - m_new) * a_sc[b]
                       + jnp.dot(e.astype(BF16), v_ref[b], preferred_element_type=F32))
            m_sc[b] = m_new

    @pl.when(jnp.logical_not(last))
    def _():
        accumulate(False)

    @pl.when(last)
    def _():
        accumulate(True)
        gate = jax.nn.sigmoid(gl_ref[:, :, :, 1:2].astype(F32))
        a = a_sc[...]
        o = (a / pltpu.roll(a, NSA_DH, 2))[:, :, :NSA_DH].reshape(bsz, p, tq, NSA_DH)
        o_ref[...] = (acc_ref[...] + o * gate).astype(o_ref.dtype)


def _sel_branch(q_aug, k_aug, v_aug, bias_t, gl5, acc, tq, tk):
    bsz, g, p, s, wq = q_aug.shape
    dh = acc.shape[-1]
    r = tk // tq
    pairs = [(i, j) for i in range(s // tq) for j in range(i // r + 1)]
    it = jnp.asarray([ij[0] for ij in pairs], jnp.int32)
    jt = jnp.asarray([ij[1] for ij in pairs], jnp.int32)
    nd = bias_t.shape[1] - 1
    qmap = lambda h, t, it, jt: (0, h, 0, it[t], 0)
    kmap = lambda h, t, it, jt: (0, h, jt[t], 0)
    grid_spec = pltpu.PrefetchScalarGridSpec(
        num_scalar_prefetch=2,
        grid=(g, len(pairs)),
        in_specs=[pl.BlockSpec((bsz, None, p, tq, wq), qmap),
                  pl.BlockSpec((bsz, None, tk, wq), kmap),
                  pl.BlockSpec((bsz, None, tk, LANES), kmap),
                  pl.BlockSpec((p, 1, tq, tk),
                               lambda h, t, it, jt: (h, jnp.minimum(it[t] - r * jt[t], nd), 0, 0)),
                  pl.BlockSpec((bsz, None, p, tq, 3), qmap),
                  pl.BlockSpec((bsz, None, p, tq, dh), qmap)],
        out_specs=pl.BlockSpec((bsz, None, p, tq, dh), qmap),
        scratch_shapes=[pltpu.VMEM((bsz, p * tq, LANES), F32),
                        pltpu.VMEM((bsz, p * tq, LANES), F32)],
    )
    return pl.pallas_call(
        functools.partial(_sel_kernel, tq=tq, tk=tk, bsz=bsz),
        grid_spec=grid_spec,
        out_shape=jax.ShapeDtypeStruct((bsz, g, p, s, dh), BF16),
        compiler_params=_cparams(("parallel", "arbitrary")),
    )(it, jt, q_aug, k_aug, v_aug, bias_t, gl5, acc)


def _merge_kernel(x_ref, oa_ref, ob_ref, ga_ref, gb_ref, wa_ref, wb_ref, wo_ref, g_ref, b_ref, o_ref):
    ya = jnp.dot(oa_ref[...].astype(BF16), wa_ref[...], preferred_element_type=F32)
    yb = jnp.dot(ob_ref[...].astype(BF16), wb_ref[...], preferred_element_type=F32)
    y = jax.nn.sigmoid(ga_ref[...].astype(F32)) * ya + jax.nn.sigmoid(gb_ref[...].astype(F32)) * yb
    mix = jnp.dot(y.astype(BF16), wo_ref[...], preferred_element_type=F32)
    o_ref[...] = _layer_norm(DN_ALPHA * x_ref[...] + mix, g_ref[...], b_ref[...])


def _merge(x, o_a, o_b, proj, wa, wb, wo, g, b, tm):
    t, d = x.shape
    nga = COL_GA // d
    row = lambda i: (i, 0)
    const = lambda i: (0, 0)
    return pl.pallas_call(
        _merge_kernel,
        grid=(t // tm,),
        in_specs=[pl.BlockSpec((tm, d), row),
                  pl.BlockSpec((tm, HG_WIDTH), row),
                  pl.BlockSpec((tm, NSA_WIDTH), row),
                  pl.BlockSpec((tm, d), lambda i: (i, nga)),
                  pl.BlockSpec((tm, d), lambda i: (i, nga + 1)),
                  pl.BlockSpec((HG_WIDTH, d), const),
                  pl.BlockSpec((NSA_WIDTH, d), const),
                  pl.BlockSpec((d, d), const),
                  pl.BlockSpec((1, d), const),
                  pl.BlockSpec((1, d), const)],
        out_specs=pl.BlockSpec((tm, d), row),
        out_shape=jax.ShapeDtypeStruct((t, d), F32),
        compiler_params=_cparams(("parallel",)),
    )(x, o_a, o_b, proj, proj, wa, wb, wo, g, b)


def _swiglu_step(xb, wg_ref, wu_ref, wd_ref):
    hg = jnp.dot(xb, wg_ref[...], preferred_element_type=F32)
    hu = jnp.dot(xb, wu_ref[...], preferred_element_type=F32)
    h = (hg * jax.nn.sigmoid(hg)) * hu
    return jnp.dot(h.astype(BF16), wd_ref[...], preferred_element_type=F32)


def _ffn_kernel(x_ref, wg_ref, wu_ref, wd_ref, g_ref, b_ref, o_ref):
    x = x_ref[...]
    f = _swiglu_step(x.astype(BF16), wg_ref, wu_ref, wd_ref)
    o_ref[...] = _layer_norm(DN_ALPHA * x + f, g_ref[...], b_ref[...])


def _ffn(x, wg, wu, wd, g, b, tm):
    t, d = x.shape
    f = wg.shape[1]
    const = lambda i: (0, 0)
    once = pl.Buffered(1)
    return pl.pallas_call(
        _ffn_kernel,
        grid=(t // tm,),
        in_specs=[pl.BlockSpec((tm, d), lambda i: (i, 0)),
                  pl.BlockSpec((d, f), const, pipeline_mode=once),
                  pl.BlockSpec((d, f), const, pipeline_mode=once),
                  pl.BlockSpec((f, d), const, pipeline_mode=once),
                  pl.BlockSpec((1, d), const),
                  pl.BlockSpec((1, d), const)],
        out_specs=pl.BlockSpec((tm, d), lambda i: (i, 0)),
        out_shape=jax.ShapeDtypeStruct((t, d), F32),
        compiler_params=_cparams(("parallel",)),
    )(x, wg, wu, wd, g, b)


def _router_kernel(x_ref, w_ref, o_ref):
    xs, ws = _split3(x_ref[...]), _split3(w_ref[...])
    logits = sum(jnp.dot(xs[i], ws[j], preferred_element_type=F32) for i, j in ((0, 0), (0, 1), (1, 0)))
    lane =lax.broadcasted_iota(jnp.int32, logits.shape, 1).astype(F32)
    logits = jnp.where(lane < N_EXPERTS, logits, -jnp.inf)
    v1 = jnp.max(logits, -1, keepdims=True)
    e1 = jnp.min(jnp.where(logits == v1, lane, float(LANES)), -1, keepdims=True)
    rest = jnp.where(lane == e1, -jnp.inf, logits)
    v2 = jnp.max(rest, -1, keepdims=True)
    e2 = jnp.min(jnp.where(rest == v2, lane, float(LANES)), -1, keepdims=True)
    x2 = jnp.exp(v2 - v1)
    den = 1.0 + x2
    o_ref[...] = jnp.where(lane == 0, e1, jnp.where(lane == 1, e2, jnp.where(
        lane == 2, 1.0 / den, jnp.where(lane == 3, x2 / den, 0.0))))


def _router(x, w, tm):
    t, d = x.shape
    return pl.pallas_call(
        _router_kernel,
        grid=(t // tm,),
        in_specs=[pl.BlockSpec((tm, d), lambda i: (i, 0)), pl.BlockSpec((d, LANES), lambda i: (0, 0))],
        out_specs=pl.BlockSpec((tm, LANES), lambda i: (i, 0)),
        out_shape=jax.ShapeDtypeStruct((t, LANES), F32),
        compiler_params=_cparams(("parallel",)),
    )(x, w)


def _expert_kernel(be_ref, x_ref, wg_ref, wu_ref, wd_ref, o_ref, acc_ref):
    j = pl.program_id(1)

    @pl.when(j == 0)
    def _():
        acc_ref[...] = jnp.zeros_like(acc_ref)

    xb = x_ref[...].astype(BF16)
    hg = jnp.dot(xb, wg_ref[...].astype(BF16), preferred_element_type=F32)
    hu = jnp.dot(xb, wu_ref[...].astype(BF16), preferred_element_type=F32)
    h = (hg * jax.nn.sigmoid(hg)) * hu
    acc_ref[...] += jnp.dot(h.astype(BF16), wd_ref[...].astype(BF16), preferred_element_type=F32)

    @pl.when(j == pl.num_programs(1) - 1)
    def _():
        o_ref[...] = acc_ref[...].astype(o_ref.dtype)


def _experts(blk_e, xs, wg, wu, wd, tf):
    rows, d = xs.shape
    f = wg.shape[2]
    tm = MOE_ROW_BLOCK
    grid_spec = pltpu.PrefetchScalarGridSpec(
        num_scalar_prefetch=1,
        grid=(rows // tm, f // tf),
        in_specs=[pl.BlockSpec((tm, d), lambda i, j, be: (i, 0)),
                  pl.BlockSpec((None, d, tf), lambda i, j, be: (be[i], 0, j)),
                  pl.BlockSpec((None, d, tf), lambda i, j, be: (be[i], 0, j)),
                  pl.BlockSpec((None, tf, d), lambda i, j, be: (be[i], j, 0))],
        out_specs=pl.BlockSpec((tm, d), lambda i, j, be: (i, 0)),
        scratch_shapes=[pltpu.VMEM((tm, d), F32)],
    )
    return pl.pallas_call(
        _expert_kernel,
        grid_spec=grid_spec,
        out_shape=jax.ShapeDtypeStruct((rows, d), F32),
        compiler_params=_cparams(("parallel", "arbitrary")),
    )(blk_e, xs, wg, wu, wd)


def _combine_kernel(x_ref, y1_ref, y2_ref, gt_ref, g_ref, b_ref, o_ref):
    f = y1_ref[...].astype(F32) * gt_ref[:, 2:3] + y2_ref[...].astype(F32) * gt_ref[:, 3:4]
    o_ref[...] = _layer_norm(DN_ALPHA * x_ref[...] + f, g_ref[...], b_ref[...])


def _combine(x, y1, y2, route, g, b, tm):
    t, d = x.shape
    row = lambda i: (i, 0)
    const = lambda i: (0, 0)
    return pl.pallas_call(
        _combine_kernel,
        grid=(t // tm,),
        in_specs=[pl.BlockSpec((tm, d), row), pl.BlockSpec((tm, d), row), pl.BlockSpec((tm, d), row),
                  pl.BlockSpec((tm, LANES), row), pl.BlockSpec((1, d), const), pl.BlockSpec((1, d), const)],
        out_specs=pl.BlockSpec((tm, d), row),
        out_shape=jax.ShapeDtypeStruct((t, d), F32),
        compiler_params=_cparams(("parallel",)),
    )(x, y1, y2, route, g, b)


def _moe(x, w_router, wg, wu, wd, g, b):
    t, d = x.shape
    tk_ = t * TOP_K
    route = _router(x, jnp.pad(w_router, ((0, 0), (0, LANES - N_EXPERTS))), 512)
    flat_e = route[:, :TOP_K].astype(jnp.int32).reshape(-1)
    onehot = (flat_e[:, None] == jnp.arange(N_EXPERTS)[None, :]).astype(jnp.int32)
    csum = jnp.cumsum(onehot, axis=0)
    counts = csum[-1]
    rank = jnp.sum(onehot * csum, axis=1) - 1
    padded = (counts + MOE_ROW_BLOCK - 1) // MOE_ROW_BLOCK * MOE_ROW_BLOCK
    pend = jnp.cumsum(padded)
    dest = (pend - padded)[flat_e] + rank
    n_blocks = -(-(tk_ + N_EXPERTS * (MOE_ROW_BLOCK - 1)) // MOE_ROW_BLOCK)
    n_rows = n_blocks * MOE_ROW_BLOCK
    row_tok = (jnp.arange(n_rows, dtype=jnp.int32) % t).at[dest].set(jnp.arange(tk_, dtype=jnp.int32) // TOP_K)
    blk_e = jnp.minimum(jnp.searchsorted(pend, jnp.arange(n_blocks) * MOE_ROW_BLOCK, side='right'),
                        N_EXPERTS - 1).astype(jnp.int32)
    xs = x[row_tok]
    ys = _experts(blk_e, xs, wg, wu, wd, 512)
    dest2 = dest.reshape(t, TOP_K)
    return _combine(x, ys[dest2[:, 0]], ys[dest2[:, 1]], route, g, b, 512)


def _pack_w_in(w_in):
    offs = np.concatenate([[0], np.cumsum(IN_SIZES)])
    seg = [w_in[:, offs[j]:offs[j + 1]] for j in range(len(IN_SIZES))]
    seg[4] = seg[4] * (NSA_DH ** -0.5 * LOG2E)
    seg[11] = jnp.pad(seg[11], ((0, 0), (0, LANES - 3 * NSA_HEADS)))
    seg = seg[12:14] + seg[0:12]
    packed = jnp.concatenate(seg + [jnp.zeros((w_in.shape[0], PROJ_WP - PROJ_W), w_in.dtype)], axis=1)
    return packed.astype(BF16)


def _cmp_to_sel(n_cmp_pad, n_cmp):
    cs = np.arange(n_cmp_pad)[:, None] * CMP_STRIDE
    ss = np.arange(LANES)[None, :] * SEL_BLOCK
    overlap = np.clip(np.minimum(cs + CMP_LEN, ss + SEL_BLOCK) - np.maximum(cs, ss), 0, None) / CMP_LEN
    overlap[n_cmp:] = 0.0
    return jnp.asarray(overlap, BF16)


def _token_mixer(x, w_in_p, lb, hg_norm_w, cmp_pos, cmp_w1, cmp_w2, bias_c, bias_t, wa, wb, wo, ln_g, ln_b):
    bsz, s, d = x.shape
    g, p, dh = NSA_GROUPS, NSA_HPG, NSA_DH
    xf = x.reshape(bsz * s, d)
    proj = _project(xf, w_in_p, 1024, PROJ_WP // 2).reshape(bsz, s, PROJ_WP)
    o_a = _hgrn(proj, lb, hg_norm_w, 1024)

    def heads(c0, width):
        return proj[:, :, c0:c0 + width]

    kv = heads(COL_KV, 6 * KV_WIDTH).reshape(bsz, s, 6, g, dh)
    n16 = s // CMP_STRIDE
    a = kv[:, :, 0:2].reshape(bsz, n16, CMP_STRIDE, 2, g, dh).transpose(3, 0, 4, 1, 2, 5)
    a = a.reshape(2, bsz, g, n16, CMP_STRIDE * dh)
    kvc = _compress(a, cmp_pos.reshape(2, 1, CMP_LEN * dh), cmp_w1.astype(BF16), cmp_w2.astype(BF16))
    kvh = kv[:, :, 2:6].astype(BF16).transpose(2, 0, 3, 1, 4)
    gl5 = heads(COL_NG, 3 * NSA_HEADS).reshape(bsz, s, g, p, 3).transpose(0, 2, 3, 1, 4)
    n_cmp = (s - CMP_LEN) // CMP_STRIDE + 1
    acc, q_aug = _cmp_branch(proj, jnp.concatenate([kvc[0], kvc[0]], axis=-1), kvc[1], bias_c,
                             _cmp_to_sel(n16, n_cmp), gl5, ATT_TQ)
    ones = jnp.ones((bsz, g, s, LANES - dh), BF16)
    acc = _win_branch(proj, jnp.concatenate([kvh[2], kvh[2]], axis=-1), jnp.concatenate([kvh[3], ones], axis=-1),
                      bias_t, gl5, acc, ATT_TQ)
    block_of_key = (np.arange(s)[:, None] // SEL_BLOCK == np.arange(LANES)[None, :])
    k_aug = jnp.concatenate([jnp.broadcast_to(jnp.asarray(block_of_key, BF16), (bsz, g, s, LANES)), kvh[0],
                             kvh[0]], axis=-1)
    v_aug = jnp.concatenate([kvh[1], ones], axis=-1)
    o_b = _sel_branch(q_aug, k_aug, v_aug, bias_t, gl5, acc, ATT_TQ, ATT_TK)
    o_b = o_b.transpose(0, 3, 1, 2, 4).reshape(bsz * s, NSA_WIDTH)
    return _merge(xf, o_a.reshape(bsz * s, HG_WIDTH), o_b, proj.reshape(bsz * s, PROJ_WP),
                  wa, wb, wo, ln_g, ln_b, 512)


def kernel(x, w_in, hg_lb_logits, hg_norm_w, cmp_pos, cmp_w1, cmp_w2, rel_bias, w_branch_a, w_branch_b,
           w_out, ln1_g, ln1_b, ln2_g, ln2_b, ffn_w_gate, ffn_w_up, ffn_w_down, moe_router, moe_w_gate,
           moe_w_up, moe_w_down):
    bsz, s, d = x.shape
    depth = w_in.shape[0]
    p_lb = jax.nn.softmax(hg_lb_logits.astype(F32), axis=0)
    lbs = jnp.cumsum(p_lb, axis=0) - p_lb[0]
    n16 = s // CMP_STRIDE
    bias_c = _bias_tiles(rel_bias, s // ATT_TQ, ATT_TQ, n16,
                         lambda t, r, c: t * ATT_TQ + r - (c * CMP_STRIDE + CMP_LEN - 1))
    bias_t = _bias_tiles(rel_bias, BIAS_ND + 1, ATT_TQ, ATT_TK, lambda t, r, c: t * ATT_TQ + r - c)
    f_pad = -(-D_FF // LANES) * LANES - D_FF
    xf = x.reshape(bsz * s, d)
    for l in range(depth):
        xf = _token_mixer(xf.reshape(bsz, s, d), _pack_w_in(w_in[l]), lbs[l][None], hg_norm_w[l][None],
                          cmp_pos[l], cmp_w1[l], cmp_w2[l], bias_c, bias_t,
                          w_branch_a[l].astype(BF16), w_branch_b[l].astype(BF16), w_out[l].astype(BF16),
                          ln1_g[l][None], ln1_b[l][None])
        if l % 2 == 0:
            wg = jnp.pad(ffn_w_gate[l // 2], ((0, 0), (0, f_pad))).astype(BF16)
            wu = jnp.pad(ffn_w_up[l // 2], ((0, 0), (0, f_pad))).astype(BF16)
            wd = jnp.pad(ffn_w_down[l // 2], ((0, f_pad), (0, 0))).astype(BF16)
            xf = _ffn(xf, wg, wu, wd, ln2_g[l][None], ln2_b[l][None], 512)
        else:
            xf = _moe(xf, moe_router[l // 2], moe_w_gate[l // 2], moe_w_up[l // 2], moe_w_down[l // 2],
                      ln2_g[l][None], ln2_b[l][None])
    return xf.reshape(bsz, s, d)
```

```python
import functools
import math

import jax
import jax.numpy as jnp
import numpy as np
from jax import lax
from jax.experimental import pallas as pl
from jax.experimental.pallas import tpu as pltpu

F32 = jnp.float32
BF16 = jnp.bfloat16

D_MODEL = 1024
DEPTH = 2
HG_HEADS = 4
HG_DK = 128
HG_WIDTH = HG_HEADS * HG_DK
HG_CHUNK = 64
HG_SUB = 16
HG_HALF = HG_SUB // 2
NSA_HEADS = 8
NSA_GROUPS = 2
NSA_HPG = NSA_HEADS // NSA_GROUPS
NSA_DH = 64
NSA_WIDTH = NSA_HEADS * NSA_DH
KV_WIDTH = NSA_GROUPS * NSA_DH
CMP_LEN = 32
CMP_STRIDE = 16
CMP_HIDDEN = 2 * NSA_DH
SEL_BLOCK = 64
N_SEL = 16
WINDOW = 512
FORCE_SCORE = 1e9
NEG_BIG = -1e30
REL_BUCKETS = 32
REL_MAX_DIST = 2048
D_FF = 2752
N_EXPERTS = 8
TOP_K = 2
D_FF_EXPERT = 3584
MOE_ROW_BLOCK = 1024
DN_ALPHA = (2 * DEPTH) ** 0.25
LN_EPS = 1e-5
IN_SIZES = (HG_WIDTH, HG_WIDTH, HG_WIDTH, HG_WIDTH, NSA_WIDTH,
            KV_WIDTH, KV_WIDTH, KV_WIDTH, KV_WIDTH, KV_WIDTH, KV_WIDTH,
            3 * NSA_HEADS, D_MODEL, D_MODEL)

LANES = 128
LOG2E = 1.0 / math.log(2.0)
COL_GA = 0
COL_HG = 2 * D_MODEL
COL_NQ = COL_HG + 4 * HG_WIDTH
COL_KV = COL_NQ + NSA_WIDTH
COL_NG = COL_KV + 6 * KV_WIDTH
PROJ_W = COL_NG + LANES
PROJ_TN = 512
PROJ_WP = -(-PROJ_W // PROJ_TN) * PROJ_TN

ATT_TQ = 256
ATT_TK = 512
BIAS_ND = -(-(REL_MAX_DIST + ATT_TK) // ATT_TQ)
VMEM_LIMIT = 48 * 1024 * 1024


def _cparams(sem):
    return pltpu.CompilerParams(dimension_semantics=sem, vmem_limit_bytes=VMEM_LIMIT)


def _proj_kernel(x_ref, w_ref, o_ref, xb_ref):
    @pl.when(pl.program_id(1) == 0)
    def _():
        xb_ref[...] = x_ref[...].astype(BF16)

    o_ref[...] = jnp.dot(xb_ref[...], w_ref[...], preferred_element_type=F32).astype(o_ref.dtype)


def _project(x, w, tm, tn):
    m, k = x.shape
    n = w.shape[1]
    return pl.pallas_call(
        _proj_kernel,
        grid=(m // tm, n // tn),
        in_specs=[pl.BlockSpec((tm, k), lambda i, j: (i, 0)),
                  pl.BlockSpec((k, tn), lambda i, j: (0, j))],
        out_specs=pl.BlockSpec((tm, tn), lambda i, j: (i, j)),
        out_shape=jax.ShapeDtypeStruct((m, n), BF16),
        scratch_shapes=[pltpu.VMEM((tm, k), BF16)],
        compiler_params=_cparams(("parallel", "arbitrary")),
    )(x, w)


def _layer_norm(y, g, b):
    mu = jnp.mean(y, -1, keepdims=True)
    yc = y - mu
    var = jnp.mean(yc * yc, -1, keepdims=True)
    return yc * lax.rsqrt(var + LN_EPS) * g + b


def _split3(x):
    parts = []
    for _ in range(3):
        part = x.astype(BF16)
        parts.append(part)
        x = x - part.astype(F32)
    return parts


def _cumsum_rows(tril, x):
    return sum(jnp.dot(tril, part, preferred_element_type=F32) for part in _split3(x))


def _hgrn_kernel(q_ref, f_ref, i_ref, g_ref, lb_ref, nw_ref, o_ref, st_ref, *, n_chunks):
    @pl.when(pl.program_id(1) == 0)
    def _():
        st_ref[...] = jnp.zeros_like(st_ref)

    c = HG_CHUNK
    nt = (((1,), (1,)), ((), ()))
    row = lax.broadcasted_iota(jnp.int32, (c, c), 0)
    col = lax.broadcasted_iota(jnp.int32, (c, c), 1)
    tril = (row >= col).astype(BF16)
    row1 = lax.broadcasted_iota(jnp.int32, (c, 1), 0)
    half_pos = row1 % HG_HALF
    second_half = row1 % HG_SUB >= HG_HALF
    same_sub = row // HG_SUB == col // HG_SUB

    w = HG_HEADS * HG_DK
    head_lanes = [slice(h * HG_DK, (h + 1) * HG_DK) for h in range(HG_HEADS)]

    def per_head(fn):
        return jnp.concatenate([fn(h, head_lanes[h]) for h in range(HG_HEADS)], axis=1)

    def shift_in_half(x, lag):
        return pltpu.roll(x.reshape(c // HG_HALF, HG_HALF, w), lag, 1).reshape(c, w)

    def chunk(ci, carry):
        r0 = pl.multiple_of(ci * c, c)
        q = q_ref[pl.ds(r0, c), :].astype(F32)
        z = f_ref[pl.ds(r0, c), :].astype(F32)
        v = i_ref[pl.ds(r0, c), :].astype(F32)
        g = g_ref[pl.ds(r0, c), :].astype(F32)
        k = (1.0 - lb_ref[...]) * jax.nn.sigmoid(-z)
        b = _cumsum_rows(tril, jnp.log1p(-k)) * LOG2E
        b_last = b[c - 1:c, :]
        vb = v.astype(BF16)
        qe = (q * jnp.exp2(b)).astype(BF16)
        o = per_head(lambda h, hl: lax.dot_general(qe[:, hl], st_ref[h].astype(BF16), nt,
                                                  preferred_element_type=F32))
        att_rows = [[jnp.zeros((HG_SUB, c), F32)] * HG_HEADS]
        for sb in range(1, c // HG_SUB):
            lo = sb * HG_SUB
            ref_b = b[lo - 1:lo, :]
            qt = (q[lo:lo + HG_SUB, :] * jnp.exp2(b[lo:lo + HG_SUB, :] - ref_b)).astype(BF16)
            kt = (k * jnp.exp2(jnp.where(row1 < lo, ref_b - b, -jnp.inf))).astype(BF16)
            att_rows.append([lax.dot_general(qt[:, hl], kt[:, hl], nt, preferred_element_type=F32)
                             for hl in head_lanes])
        mid = jnp.concatenate(
            [jnp.broadcast_to(b[lo + HG_HALF - 1:lo + HG_HALF, :], (HG_SUB, w)) for lo in range(0, c, HG_SUB)],
            axis=0)
        q2 = (q * jnp.exp2(jnp.where(second_half, b - mid, -jnp.inf))).astype(BF16)
        k2 = (k * jnp.exp2(jnp.where(second_half, -jnp.inf, mid - b))).astype(BF16)
        att_half = [lax.dot_general(q2[:, hl], k2[:, hl], nt, preferred_element_type=F32) for hl in head_lanes]

        def intra(h, hl):
            att = jnp.concatenate([rows[h] for rows in att_rows], axis=0) + jnp.where(same_sub, att_half[h], 0.0)
            return jnp.dot(att.astype(BF16), vb[:, hl], preferred_element_type=F32)

        o = o + per_head(intra)
        for lag in range(HG_HALF):
            if lag == 0:
                ks, bs, vs = k, b, v
            else:
                ks, bs, vs = shift_in_half(k, lag), shift_in_half(b, lag), shift_in_half(v, lag)
            valid = half_pos >= lag
            prod = q * ks * jnp.exp2(jnp.where(valid, b - bs, 0.0))
            a = per_head(lambda h, hl: jnp.broadcast_to(
                jnp.sum(prod[:, hl], axis=1, keepdims=True), (c, HG_DK)))
            o = o + jnp.where(valid, a, 0.0) * vs
        khat = (k * jnp.exp2(b_last - b)).astype(BF16)
        decay = jnp.exp2(b_last)
        for h, hl in enumerate(head_lanes):
            st_ref[h] = st_ref[h] * decay[:, hl] + lax.dot_general(
                vb[:, hl], khat[:, hl], (((0,), (0,)), ((), ())), preferred_element_type=F32)
        sq = o * o
        ms = per_head(lambda h, hl: jnp.broadcast_to(jnp.mean(sq[:, hl], -1, keepdims=True), (c, HG_DK)))
        o = o * lax.rsqrt(ms + 1e-6)
        o_ref[pl.ds(r0, c), :] = (o * nw_ref[...] * (g * jax.nn.sigmoid(g))).astype(o_ref.dtype)
        return carry

    lax.fori_loop(0, n_chunks, chunk, 0, unroll=4)


def _hgrn(proj, lb, norm_w, ts):
    bsz, s, _ = proj.shape
    col0 = COL_HG // HG_WIDTH

    def col_spec(n):
        return pl.BlockSpec((None, ts, HG_WIDTH), lambda b, t: (b, t, col0 + n))

    head_spec = pl.BlockSpec((1, HG_WIDTH), lambda b, t: (0, 0))
    return pl.pallas_call(
        functools.partial(_hgrn_kernel, n_chunks=ts // HG_CHUNK),
        grid=(bsz, s // ts),
        in_specs=[col_spec(0), col_spec(1), col_spec(2), col_spec(3), head_spec, head_spec],
        out_specs=pl.BlockSpec((None, ts, HG_WIDTH), lambda b, t: (b, t, 0)),
        out_shape=jax.ShapeDtypeStruct((bsz, s, HG_WIDTH), BF16),
        scratch_shapes=[pltpu.VMEM((HG_HEADS, HG_DK, HG_DK), F32)],
        compiler_params=_cparams(("parallel", "arbitrary")),
    )(proj, proj, proj, proj, lb, norm_w)


def _compress_kernel(a_ref, pos_ref, w1_ref, w2_ref, o_ref):
    half = CMP_STRIDE * NSA_DH
    a = a_ref[...].astype(F32)
    n = a.shape[0]
    a1 = (a + pos_ref[:, :half]).astype(BF16)
    a2 = (a + pos_ref[:, half:]).astype(BF16)
    y1 = jnp.dot(a1, w1_ref[:half, :], preferred_element_type=F32)
    y2 = jnp.dot(a2, w1_ref[half:, :], preferred_element_type=F32)
    hid = jax.nn.gelu(y1 + pltpu.roll(y2, n - 1, 0))
    o_ref[...] = jnp.dot(hid.astype(BF16), w2_ref[...], preferred_element_type=F32).astype(o_ref.dtype)


def _compress(a, pos, w1, w2):
    _, bsz, g, n, width = a.shape
    return pl.pallas_call(
        _compress_kernel,
        grid=(2, bsz, g),
        in_specs=[pl.BlockSpec((None, None, None, n, width), lambda c, b, h: (c, b, h, 0, 0)),
                  pl.BlockSpec((None, 1, 2 * width), lambda c, b, h: (c, 0, 0)),
                  pl.BlockSpec((None, 2 * width, CMP_HIDDEN), lambda c, b, h: (c, 0, 0)),
                  pl.BlockSpec((None, CMP_HIDDEN, NSA_DH), lambda c, b, h: (c, 0, 0))],
        out_specs=pl.BlockSpec((None, None, None, n, NSA_DH), lambda c, b, h: (c, b, h, 0, 0)),
        out_shape=jax.ShapeDtypeStruct((2, bsz, g, n, NSA_DH), BF16),
        compiler_params=_cparams(("parallel", "parallel", "parallel")),
    )(a, pos, w1, w2)


def _bucket_of(dist):
    n = jnp.maximum(dist, 0)
    exact = REL_BUCKETS // 2
    large = exact + (jnp.log(jnp.maximum(n, exact).astype(F32) / exact)
                     / math.log(REL_MAX_DIST / exact) * (REL_BUCKETS - exact)).astype(jnp.int32)
    return jnp.where(n < exact, n, jnp.minimum(large, REL_BUCKETS - 1))


def _bias_kernel(tab_ref, o_ref, *, dist_fn):
    rows, cols = o_ref.shape[2], o_ref.shape[3]
    row = lax.broadcasted_iota(jnp.int32, (rows, cols), 0)
    col = lax.broadcasted_iota(jnp.int32, (rows, cols), 1)
    bucket = _bucket_of(dist_fn(pl.program_id(0), row, col))
    for h in range(NSA_HEADS):
        tab = jnp.broadcast_to(tab_ref[h:h + 1, :] * LOG2E, (rows, LANES))
        for c0 in range(0, cols, LANES):
            o_ref[h, 0, :, c0:c0 + LANES] = jnp.take_along_axis(tab, bucket[:, c0:c0 + LANES], axis=1)


def _bias_tiles(rel_bias, n_tiles, rows, cols, dist_fn):
    tab = jnp.pad(rel_bias.T, ((0, 0), (0, LANES - REL_BUCKETS)))
    return pl.pallas_call(
        functools.partial(_bias_kernel, dist_fn=dist_fn),
        grid=(n_tiles,),
        in_specs=[pl.BlockSpec((NSA_HEADS, LANES), lambda t: (0, 0))],
        out_specs=pl.BlockSpec((NSA_HEADS, 1, rows, cols), lambda t: (0, t, 0, 0)),
        out_shape=jax.ShapeDtypeStruct((NSA_HEADS, n_tiles, rows, cols), F32),
        compiler_params=_cparams(("parallel",)),
    )(tab)


def _stack_heads(qb):
    tq = qb.shape[0]
    half = lax.broadcasted_iota(jnp.int32, (tq, LANES), 1) // NSA_DH
    return jnp.concatenate(
        [jnp.where(half == hp % 2, qb[:, LANES * (hp // 2):LANES * (hp // 2 + 1)], jnp.zeros((tq, LANES), qb.dtype))
         for hp in range(NSA_HPG)], axis=0)


def _cmp_kernel(q_ref, k_ref, v_ref, bias_ref, c2s_ref, gl_ref, o_ref, qa_ref, *, tq):
    p = NSA_HPG
    t0 = pl.program_id(1) * tq
    bsz, ncp = k_ref.shape[0], k_ref.shape[1]
    bias = bias_ref[:, 0]
    tpos = t0 + lax.broadcasted_iota(jnp.int32, (tq, ncp), 0)
    cend = lax.broadcasted_iota(jnp.int32, (tq, ncp), 1) * CMP_STRIDE + (CMP_LEN - 1)
    mask = (tpos >= cend)[None]
    lane = lax.broadcasted_iota(jnp.int32, (tq, LANES), 1)
    cur = (t0 + lax.broadcasted_iota(jnp.int32, (tq, LANES), 0)) // SEL_BLOCK
    forced = (lane == 0) | (lane == cur) | (lane == cur - 1)
    visible = lane <= cur
    batch = range(bsz)
    nt = (((1,), (1,)), ((), ()))
    qs = [_stack_heads(q_ref[b]) for b in batch]
    lgs = [lax.dot_general(qs[b], k_ref[b], nt, preferred_element_type=F32) for b in batch]
    lgs = [jnp.where(mask, lg.reshape(p, tq, ncp) + bias, NEG_BIG) for lg in lgs]
    es = [jnp.where(mask, jnp.exp2(lg - jnp.max(lg, -1, keepdims=True)), 0.0) for lg in lgs]
    pcs = [e / jnp.maximum(jnp.sum(e, -1, keepdims=True), 1e-30) for e in es]
    os = [jnp.dot(pcs[b].reshape(p * tq, ncp).astype(BF16), v_ref[b], preferred_element_type=F32) for b in batch]
    imps = [sum(jnp.dot(part, c2s_ref[...], preferred_element_type=F32) for part in _split3(jnp.sum(pc, axis=0)))
            for pc in pcs]
    for b in batch:
        o_ref[b] = _heads_to_lanes(os[b], jax.nn.sigmoid(gl_ref[b, :, :, 0:1].astype(F32)), tq, False)
    scores = [jnp.where(forced, FORCE_SCORE, jnp.where(visible, imp, NEG_BIG)) for imp in imps]
    members = [jnp.zeros((tq, LANES), F32)] * bsz
    for _ in range(N_SEL):
        hits = [lane == jnp.argmax(score, axis=-1, keepdims=True) for score in scores]
        members = [jnp.where(hit, 1.0, member) for hit, member in zip(hits, members)]
        scores = [jnp.where(hit, -jnp.inf, score) for hit, score in zip(hits, scores)]
    for b in batch:
        pen = jnp.where((members[b] > 0.5) & visible, 0.0, NEG_BIG).astype(qa_ref.dtype)
        qa_ref[b] = jnp.concatenate([jnp.concatenate([pen] * p, axis=0), qs[b]], axis=1).reshape(p, tq, 2 * LANES)


def _heads_to_lanes(x, gates, tq, normalise):
    low = lax.broadcasted_iota(jnp.int32, (tq, LANES), 1) < NSA_DH
    if normalise:
        swapped = pltpu.roll(x, NSA_DH, 1)
    pieces = []
    for pair in range(NSA_HPG // 2):
        ev = slice(2 * pair * tq, (2 * pair + 1) * tq)
        od = slice((2 * pair + 1) * tq, (2 * pair + 2) * tq)
        if normalise:
            even, odd = x[ev] / swapped[ev], swapped[od] / x[od]
        else:
            even, odd = x[ev], x[od]
        pieces.append(jnp.where(low, even * gates[2 * pair], odd * gates[2 * pair + 1]))
    return jnp.concatenate(pieces, axis=1)


def _out_spec(bsz, tq):
    return pl.BlockSpec((bsz, tq, NSA_HPG * NSA_DH), lambda h, i: (0, i, h))


def _q_spec(bsz, tq):
    width = NSA_HPG * NSA_DH
    return pl.BlockSpec((bsz, tq, width), lambda h, i: (0, i, COL_NQ // width + h))


def _cmp_branch(proj, kc2, vc, bias_c, c2s, gl5, tq):
    bsz, s, _ = proj.shape
    g, p, dh = NSA_GROUPS, NSA_HPG, NSA_DH
    ncp = kc2.shape[2]
    qmap = lambda h, i: (0, h, 0, i, 0)
    return pl.pallas_call(
        functools.partial(_cmp_kernel, tq=tq),
        grid=(g, s // tq),
        in_specs=[_q_spec(bsz, tq),
                  pl.BlockSpec((bsz, None, ncp, LANES), lambda h, i: (0, h, 0, 0)),
                  pl.BlockSpec((bsz, None, ncp, LANES), lambda h, i: (0, h, 0, 0)),
                  pl.BlockSpec((p, 1, tq, ncp), lambda h, i: (h, i, 0, 0)),
                  pl.BlockSpec((ncp, LANES), lambda h, i: (0, 0)),
                  pl.BlockSpec((bsz, None, p, tq, 3), qmap)],
        out_specs=[_out_spec(bsz, tq),
                   pl.BlockSpec((bsz, None, p, tq, 2 * LANES), qmap)],
        out_shape=[jax.ShapeDtypeStruct((bsz, s, NSA_WIDTH), F32),
                   jax.ShapeDtypeStruct((bsz, g, p, s, 2 * LANES), BF16)],
        compiler_params=_cparams(("parallel", "parallel")),
    )(proj, kc2, vc, bias_c, c2s, gl5)


def _win_kernel(*refs, tq, n_back):
    nk = n_back + 1
    q_ref, k_refs, v_refs, b_refs = refs[0], refs[1:1 + nk], refs[1 + nk:1 + 2 * nk], refs[1 + 2 * nk:1 + 3 * nk]
    gl_ref, acc_ref, o_ref = refs[1 + 3 * nk:]
    p = NSA_HPG
    bsz = q_ref.shape[0]
    t0 = pl.program_id(1) * tq
    row = lax.broadcasted_iota(jnp.int32, (tq, tq), 0)
    col = lax.broadcasted_iota(jnp.int32, (tq, tq), 1)
    masks = [((row - col + d * tq >= 0) & (row - col + d * tq < WINDOW) & (t0 - d * tq >= 0))[None]
             for d in range(nk)]
    biases = [b_ref[:, 0] for b_ref in b_refs]
    nt = (((1,), (1,)), ((), ()))
    batch = range(bsz)
    qs = [_stack_heads(q_ref[b]) for b in batch]
    lgs = [[lax.dot_general(qs[b], k_ref[b], nt, preferred_element_type=F32).reshape(p, tq, tq)
            for k_ref in k_refs] for b in batch]
    lgs = [[jnp.where(mask, lg + bias, NEG_BIG) for lg, mask, bias in zip(lgs[b], masks, biases)] for b in batch]
    ms = [functools.reduce(jnp.maximum, [jnp.max(lg, -1, keepdims=True) for lg in lgs[b]]) for b in batch]
    es = [[jnp.exp2(lg - ms[b]).reshape(p * tq, tq).astype(BF16) for lg in lgs[b]] for b in batch]
    os = [sum(jnp.dot(e, v_ref[b], preferred_element_type=F32) for e, v_ref in zip(es[b], v_refs)) for b in batch]
    for b in batch:
        o_ref[b] = acc_ref[b] + _heads_to_lanes(os[b], jax.nn.sigmoid(gl_ref[b, :, :, 2:3].astype(F32)), tq, True)


def _win_branch(proj, kw2, vw_aug, bias_t, gl5, acc, tq):
    bsz, s, _ = acc.shape
    g, p, dh = NSA_GROUPS, NSA_HPG, NSA_DH
    n_back = WINDOW // tq
    back = [lambda h, i, d=d: (0, h, jnp.maximum(i - d, 0), 0) for d in range(n_back + 1)]
    qmap = lambda h, i: (0, h, 0, i, 0)
    in_specs = ([_q_spec(bsz, tq)]
                + [pl.BlockSpec((bsz, None, tq, LANES), m) for m in back]
                + [pl.BlockSpec((bsz, None, tq, LANES), m) for m in back]
                + [pl.BlockSpec((p, 1, tq, tq), lambda h, i, d=d: (h, d, 0, 0)) for d in range(n_back + 1)]
                + [pl.BlockSpec((bsz, None, p, tq, 3), qmap), _out_spec(bsz, tq)])
    return pl.pallas_call(
        functools.partial(_win_kernel, tq=tq, n_back=n_back),
        grid=(g, s // tq),
        in_specs=in_specs,
        out_specs=_out_spec(bsz, tq),
        out_shape=jax.ShapeDtypeStruct((bsz, s, NSA_WIDTH), F32),
        input_output_aliases={len(in_specs) - 1: 0},
        compiler_params=_cparams(("parallel", "parallel")),
    )(proj, *([kw2] * (n_back + 1)), *([vw_aug] * (n_back + 1)), *([bias_t] * (n_back + 1)), gl5, acc)


def _sel_kernel(it_ref, jt_ref, q_ref, k_ref, v_ref, bias_ref, gl_ref, acc_ref, o_ref, m_sc, a_sc,
                *, tq, tk, bsz):
    p = NSA_HPG
    nq = p * tq
    step = pl.program_id(1)
    i = it_ref[step]
    j = jt_ref[step]
    t0 = i * tq
    s0 = j * tk
    last = s0 + tk > t0

    @pl.when(j == 0)
    def _():
        m_sc[...] = jnp.full_like(m_sc, NEG_BIG)
        a_sc[...] = jnp.zeros_like(a_sc)

    def accumulate(on_diagonal):
        nt = (((1,), (1,)), ((), ()))
        bias = bias_ref[:, 0].reshape(nq, tk)
        if on_diagonal:
            causal = (t0 + lax.broadcasted_iota(jnp.int32, (tq, tk), 0)
                      >= s0 + lax.broadcasted_iota(jnp.int32, (tq, tk), 1))
            causal = jnp.concatenate([causal] * p, axis=0)
        for b in range(bsz):
            lg = lax.dot_general(q_ref[b].reshape(nq, 2 * LANES), k_ref[b], nt, preferred_element_type=F32) + bias
            if on_diagonal:
                lg = jnp.where(causal, lg, NEG_BIG)
            m_old = m_sc[b]
            m_new = jnp.maximum(m_old, jnp.max(lg, -1, keepdims=True))
            e = jnp.exp2(lg - jnp.concatenate([m_new] * (tk // LANES), axis=1))
            a_sc[b] = (jnp.exp2(m_old - m_new) * a_sc[b]
                       + jnp.dot(e.astype(BF16), v_ref[b], preferred_element_type=F32))
            m_sc[b] = m_new

    @pl.when(jnp.logical_not(last))
    def _():
        accumulate(False)

    @pl.when(last)
    def _():
        accumulate(True)
        gate = jax.nn.sigmoid(gl_ref[:, :, :, 1:2].astype(F32))
        for b in range(bsz):
            o_ref[b] = (acc_ref[b] + _heads_to_lanes(a_sc[b], gate[b], tq, True)).astype(o_ref.dtype)


def _sel_branch(q_aug, k_aug, v_aug, bias_t, gl5, acc, tq, tk):
    bsz, g, p, s, wq = q_aug.shape
    r = tk // tq
    pairs = [(i, j) for i in range(s // tq) for j in range(i // r + 1)]
    it = jnp.asarray([ij[0] for ij in pairs], jnp.int32)
    jt = jnp.asarray([ij[1] for ij in pairs], jnp.int32)
    nd = bias_t.shape[1] - 1
    qmap = lambda h, t, it, jt: (0, h, 0, it[t], 0)
    kmap = lambda h, t, it, jt: (0, h, jt[t], 0)
    omap = lambda h, t, it, jt: (0, it[t], h)
    grid_spec = pltpu.PrefetchScalarGridSpec(
        num_scalar_prefetch=2,
        grid=(g, len(pairs)),
        in_specs=[pl.BlockSpec((bsz, None, p, tq, wq), qmap),
                  pl.BlockSpec((bsz, None, tk, wq), kmap),
                  pl.BlockSpec((bsz, None, tk, LANES), kmap),
                  pl.BlockSpec((p, 1, tq, tk),
                               lambda h, t, it, jt: (h, jnp.minimum(it[t] - r * jt[t], nd), 0, 0)),
                  pl.BlockSpec((bsz, None, p, tq, 3), qmap),
                  pl.BlockSpec((bsz, tq, p * NSA_DH), omap)],
        out_specs=pl.BlockSpec((bsz, tq, p * NSA_DH), omap),
        scratch_shapes=[pltpu.VMEM((bsz, p * tq, LANES), F32),
                        pltpu.VMEM((bsz, p * tq, LANES), F32)],
    )
    return pl.pallas_call(
        functools.partial(_sel_kernel, tq=tq, tk=tk, bsz=bsz),
        grid_spec=grid_spec,
        out_shape=jax.ShapeDtypeStruct((bsz, s, NSA_WIDTH), BF16),
        compiler_params=_cparams(("parallel", "arbitrary")),
    )(it, jt, q_aug, k_aug, v_aug, bias_t, gl5, acc)


def _merge_kernel(x_ref, oa_ref, ob_ref, ga_ref, gb_ref, wa_ref, wb_ref, wo_ref, g_ref, b_ref, o_ref):
    ya = jnp.dot(oa_ref[...].astype(BF16), wa_ref[...], preferred_element_type=F32)
    yb = jnp.dot(ob_ref[...].astype(BF16), wb_ref[...], preferred_element_type=F32)
    y = jax.nn.sigmoid(ga_ref[...].astype(F32)) * ya + jax.nn.sigmoid(gb_ref[...].astype(F32)) * yb
    mix = jnp.dot(y.astype(BF16), wo_ref[...], preferred_element_type=F32)
    o_ref[...] = _layer_norm(DN_ALPHA * x_ref[...] + mix, g_ref[...], b_ref[...])


def _merge(x, o_a, o_b, proj, wa, wb, wo, g, b, tm):
    t, d = x.shape
    nga = COL_GA // d
    row = lambda i: (i, 0)
    const = lambda i: (0, 0)
    return pl.pallas_call(
        _merge_kernel,
        grid=(t // tm,),
        in_specs=[pl.BlockSpec((tm, d), row),
                  pl.BlockSpec((tm, HG_WIDTH), row),
                  pl.BlockSpec((tm, NSA_WIDTH), row),
                  pl.BlockSpec((tm, d), lambda i: (i, nga)),
                  pl.BlockSpec((tm, d), lambda i: (i, nga + 1)),
                  pl.BlockSpec((HG_WIDTH, d), const),
                  pl.BlockSpec((NSA_WIDTH, d), const),
                  pl.BlockSpec((d, d), const),
                  pl.BlockSpec((1, d), const),
                  pl.BlockSpec((1, d), const)],
        out_specs=pl.BlockSpec((tm, d), row),
        out_shape=jax.ShapeDtypeStruct((t, d), F32),
        compiler_params=_cparams(("parallel",)),
    )(x, o_a, o_b, proj, proj, wa, wb, wo, g, b)


def _swiglu_step(xb, wg_ref, wu_ref, wd_ref):
    hg = jnp.dot(xb, wg_ref[...], preferred_element_type=F32)
    hu = jnp.dot(xb, wu_ref[...], preferred_element_type=F32)
    h = (hg * jax.nn.sigmoid(hg)) * hu
    return jnp.dot(h.astype(BF16), wd_ref[...], preferred_element_type=F32)


def _ffn_kernel(x_ref, wg_ref, wu_ref, wd_ref, g_ref, b_ref, o_ref):
    x = x_ref[...]
    f = _swiglu_step(x.astype(BF16), wg_ref, wu_ref, wd_ref)
    o_ref[...] = _layer_norm(DN_ALPHA * x + f, g_ref[...], b_ref[...])


def _ffn(x, wg, wu, wd, g, b, tm):
    t, d = x.shape
    f = wg.shape[1]
    const = lambda i: (0, 0)
    once = pl.Buffered(1)
    return pl.pallas_call(
        _ffn_kernel,
        grid=(t // tm,),
        in_specs=[pl.BlockSpec((tm, d), lambda i: (i, 0)),
                  pl.BlockSpec((d, f), const, pipeline_mode=once),
                  pl.BlockSpec((d, f), const, pipeline_mode=once),
                  pl.BlockSpec((f, d), const, pipeline_mode=once),
                  pl.BlockSpec((1, d), const),
                  pl.BlockSpec((1, d), const)],
        out_specs=pl.BlockSpec((tm, d), lambda i: (i, 0)),
        out_shape=jax.ShapeDtypeStruct((t, d), F32),
        compiler_params=_cparams(("parallel",)),
    )(x, wg, wu, wd, g, b)


def _router_kernel(x_ref, w_ref, o_ref):
    xs, ws = _split3(x_ref[...]), _split3(w_ref[...])
    logits = sum(jnp.dot(xs[i], ws[j], preferred_element_type=F32) for i, j in ((0, 0), (0, 1), (1, 0)))
    lane =lax.broadcasted_iota(jnp.int32, logits.shape, 1).astype(F32)
    logits = jnp.where(lane < N_EXPERTS, logits, -jnp.inf)
    v1 = jnp.max(logits, -1, keepdims=True)
    e1 = jnp.min(jnp.where(logits == v1, lane, float(LANES)), -1, keepdims=True)
    rest = jnp.where(lane == e1, -jnp.inf, logits)
    v2 = jnp.max(rest, -1, keepdims=True)
    e2 = jnp.min(jnp.where(rest == v2, lane, float(LANES)), -1, keepdims=True)
    x2 = jnp.exp(v2 - v1)
    den = 1.0 + x2
    o_ref[...] = jnp.where(lane == 0, e1, jnp.where(lane == 1, e2, jnp.where(
        lane == 2, 1.0 / den, jnp.where(lane == 3, x2 / den, 0.0))))


def _router(x, w, tm):
    t, d = x.shape
    return pl.pallas_call(
        _router_kernel,
        grid=(t // tm,),
        in_specs=[pl.BlockSpec((tm, d), lambda i: (i, 0)), pl.BlockSpec((d, LANES), lambda i: (0, 0))],
        out_specs=pl.BlockSpec((tm, LANES), lambda i: (i, 0)),
        out_shape=jax.ShapeDtypeStruct((t, LANES), F32),
        compiler_params=_cparams(("parallel",)),
    )(x, w)


def _expert_kernel(be_ref, x_ref, wg_ref, wu_ref, wd_ref, o_ref, acc_ref):
    j = pl.program_id(1)

    @pl.when(j == 0)
    def _():
        acc_ref[...] = jnp.zeros_like(acc_ref)

    xb = x_ref[...].astype(BF16)
    hg = jnp.dot(xb, wg_ref[...].astype(BF16), preferred_element_type=F32)
    hu = jnp.dot(xb, wu_ref[...].astype(BF16), preferred_element_type=F32)
    h = (hg * jax.nn.sigmoid(hg)) * hu
    acc_ref[...] += jnp.dot(h.astype(BF16), wd_ref[...].astype(BF16), preferred_element_type=F32)

    @pl.when(j == pl.num_programs(1) - 1)
    def _():
        o_ref[...] = acc_ref[...].astype(o_ref.dtype)


def _experts(blk_e, xs, wg, wu, wd, tf):
    rows, d = xs.shape
    f = wg.shape[2]
    tm = MOE_ROW_BLOCK
    grid_spec = pltpu.PrefetchScalarGridSpec(
        num_scalar_prefetch=1,
        grid=(rows // tm, f // tf),
        in_specs=[pl.BlockSpec((tm, d), lambda i, j, be: (i, 0)),
                  pl.BlockSpec((None, d, tf), lambda i, j, be: (be[i], 0, j)),
                  pl.BlockSpec((None, d, tf), lambda i, j, be: (be[i], 0, j)),
                  pl.BlockSpec((None, tf, d), lambda i, j, be: (be[i], j, 0))],
        out_specs=pl.BlockSpec((tm, d), lambda i, j, be: (i, 0)),
        scratch_shapes=[pltpu.VMEM((tm, d), F32)],
    )
    return pl.pallas_call(
        _expert_kernel,
        grid_spec=grid_spec,
        out_shape=jax.ShapeDtypeStruct((rows, d), F32),
        compiler_params=_cparams(("parallel", "arbitrary")),
    )(blk_e, xs, wg, wu, wd)


def _combine_kernel(x_ref, y1_ref, y2_ref, gt_ref, g_ref, b_ref, o_ref):
    f = y1_ref[...].astype(F32) * gt_ref[:, 2:3] + y2_ref[...].astype(F32) * gt_ref[:, 3:4]
    o_ref[...] = _layer_norm(DN_ALPHA * x_ref[...] + f, g_ref[...], b_ref[...])


def _combine(x, y1, y2, route, g, b, tm):
    t, d = x.shape
    row = lambda i: (i, 0)
    const = lambda i: (0, 0)
    return pl.pallas_call(
        _combine_kernel,
        grid=(t // tm,),
        in_specs=[pl.BlockSpec((tm, d), row), pl.BlockSpec((tm, d), row), pl.BlockSpec((tm, d), row),
                  pl.BlockSpec((tm, LANES), row), pl.BlockSpec((1, d), const), pl.BlockSpec((1, d), const)],
        out_specs=pl.BlockSpec((tm, d), row),
        out_shape=jax.ShapeDtypeStruct((t, d), F32),
        compiler_params=_cparams(("parallel",)),
    )(x, y1, y2, route, g, b)


def _moe(x, w_router, wg, wu, wd, g, b):
    t, d = x.shape
    tk_ = t * TOP_K
    route = _router(x, jnp.pad(w_router, ((0, 0), (0, LANES - N_EXPERTS))), 512)
    flat_e = route[:, :TOP_K].astype(jnp.int32).reshape(-1)
    onehot = (flat_e[:, None] == jnp.arange(N_EXPERTS)[None, :]).astype(jnp.int32)
    csum = jnp.cumsum(onehot, axis=0)
    counts = csum[-1]
    rank = jnp.sum(onehot * csum, axis=1) - 1
    padded = (counts + MOE_ROW_BLOCK - 1) // MOE_ROW_BLOCK * MOE_ROW_BLOCK
    pend = jnp.cumsum(padded)
    dest = (pend - padded)[flat_e] + rank
    n_blocks = -(-(tk_ + N_EXPERTS * (MOE_ROW_BLOCK - 1)) // MOE_ROW_BLOCK)
    n_rows = n_blocks * MOE_ROW_BLOCK
    row_tok = (jnp.arange(n_rows, dtype=jnp.int32) % t).at[dest].set(jnp.arange(tk_, dtype=jnp.int32) // TOP_K)
    blk_e = jnp.minimum(jnp.searchsorted(pend, jnp.arange(n_blocks) * MOE_ROW_BLOCK, side='right'),
                        N_EXPERTS - 1).astype(jnp.int32)
    xs = x[row_tok]
    ys = _experts(blk_e, xs, wg, wu, wd, 512)
    dest2 = dest.reshape(t, TOP_K)
    return _combine(x, ys[dest2[:, 0]], ys[dest2[:, 1]], route, g, b, 512)


def _pack_w_in(w_in):
    offs = np.concatenate([[0], np.cumsum(IN_SIZES)])
    seg = [w_in[:, offs[j]:offs[j + 1]] for j in range(len(IN_SIZES))]
    seg[4] = seg[4] * (NSA_DH ** -0.5 * LOG2E)
    seg[11] = jnp.pad(seg[11], ((0, 0), (0, LANES - 3 * NSA_HEADS)))
    seg = seg[12:14] + seg[0:12]
    packed = jnp.concatenate(seg + [jnp.zeros((w_in.shape[0], PROJ_WP - PROJ_W), w_in.dtype)], axis=1)
    return packed.astype(BF16)


def _cmp_to_sel(n_cmp_pad, n_cmp):
    cs = np.arange(n_cmp_pad)[:, None] * CMP_STRIDE
    ss = np.arange(LANES)[None, :] * SEL_BLOCK
    overlap = np.clip(np.minimum(cs + CMP_LEN, ss + SEL_BLOCK) - np.maximum(cs, ss), 0, None) / CMP_LEN
    overlap[n_cmp:] = 0.0
    return jnp.asarray(overlap, BF16)


def _token_mixer(x, w_in_p, lb, hg_norm_w, cmp_pos, cmp_w1, cmp_w2, bias_c, bias_t, wa, wb, wo, ln_g, ln_b):
    bsz, s, d = x.shape
    g, p, dh = NSA_GROUPS, NSA_HPG, NSA_DH
    xf = x.reshape(bsz * s, d)
    proj = _project(xf, w_in_p, 1024, PROJ_WP // 2).reshape(bsz, s, PROJ_WP)
    o_a = _hgrn(proj, lb, hg_norm_w, 1024)

    def heads(c0, width):
        return proj[:, :, c0:c0 + width]

    kv = heads(COL_KV, 6 * KV_WIDTH).reshape(bsz, s, 6, g, dh)
    n16 = s // CMP_STRIDE
    a = kv[:, :, 0:2].reshape(bsz, n16, CMP_STRIDE, 2, g, dh).transpose(3, 0, 4, 1, 2, 5)
    a = a.reshape(2, bsz, g, n16, CMP_STRIDE * dh)
    kvc = _compress(a, cmp_pos.reshape(2, 1, CMP_LEN * dh), cmp_w1.astype(BF16), cmp_w2.astype(BF16))
    kvh = kv[:, :, 2:6].astype(BF16).transpose(2, 0, 3, 1, 4)
    gl5 = heads(COL_NG, 3 * NSA_HEADS).reshape(bsz, s, g, p, 3).transpose(0, 2, 3, 1, 4)
    n_cmp = (s - CMP_LEN) // CMP_STRIDE + 1
    acc, q_aug = _cmp_branch(proj, jnp.concatenate([kvc[0], kvc[0]], axis=-1),
                             jnp.concatenate([kvc[1], kvc[1]], axis=-1), bias_c,
                             _cmp_to_sel(n16, n_cmp), gl5, ATT_TQ)
    ones = jnp.ones((bsz, g, s, LANES - dh), BF16)
    acc = _win_branch(proj, jnp.concatenate([kvh[2], kvh[2]], axis=-1), jnp.concatenate([kvh[3], ones], axis=-1),
                      bias_t, gl5, acc, ATT_TQ)
    block_of_key = (np.arange(s)[:, None] // SEL_BLOCK == np.arange(LANES)[None, :])
    k_aug = jnp.concatenate([jnp.broadcast_to(jnp.asarray(block_of_key, BF16), (bsz, g, s, LANES)), kvh[0],
                             kvh[0]], axis=-1)
    v_aug = jnp.concatenate([kvh[1], ones], axis=-1)
    o_b = _sel_branch(q_aug, k_aug, v_aug, bias_t, gl5, acc, ATT_TQ, ATT_TK)
    o_b = o_b.reshape(bsz * s, NSA_WIDTH)
    return _merge(xf, o_a.reshape(bsz * s, HG_WIDTH), o_b, proj.reshape(bsz * s, PROJ_WP),
                  wa, wb, wo, ln_g, ln_b, 512)


def kernel(x, w_in, hg_lb_logits, hg_norm_w, cmp_pos, cmp_w1, cmp_w2, rel_bias, w_branch_a, w_branch_b,
           w_out, ln1_g, ln1_b, ln2_g, ln2_b, ffn_w_gate, ffn_w_up, ffn_w_down, moe_router, moe_w_gate,
           moe_w_up, moe_w_down):
    bsz, s, d = x.shape
    depth = w_in.shape[0]
    p_lb = jax.nn.softmax(hg_lb_logits.astype(F32), axis=0)
    lbs = jnp.cumsum(p_lb, axis=0) - p_lb[0]
    n16 = s // CMP_STRIDE
    bias_c = _bias_tiles(rel_bias, s // ATT_TQ, ATT_TQ, n16,
                         lambda t, r, c: t * ATT_TQ + r - (c * CMP_STRIDE + CMP_LEN - 1))
    bias_t = _bias_tiles(rel_bias, BIAS_ND + 1, ATT_TQ, ATT_TK, lambda t, r, c: t * ATT_TQ + r - c)
    f_pad = -(-D_FF // LANES) * LANES - D_FF
    xf = x.reshape(bsz * s, d)
    for l in range(depth):
        xf = _token_mixer(xf.reshape(bsz, s, d), _pack_w_in(w_in[l]), lbs[l][None], hg_norm_w[l][None],
                          cmp_pos[l], cmp_w1[l], cmp_w2[l], bias_c, bias_t,
                          w_branch_a[l].astype(BF16), w_branch_b[l].astype(BF16), w_out[l].astype(BF16),
                          ln1_g[l][None], ln1_b[l][None])
        if l % 2 == 0:
            wg = jnp.pad(ffn_w_gate[l // 2], ((0, 0), (0, f_pad))).astype(BF16)
            wu = jnp.pad(ffn_w_up[l // 2], ((0, 0), (0, f_pad))).astype(BF16)
            wd = jnp.pad(ffn_w_down[l // 2], ((0, f_pad), (0, 0))).astype(BF16)
            xf = _ffn(xf, wg, wu, wd, ln2_g[l][None], ln2_b[l][None], 512)
        else:
            xf = _moe(xf, moe_router[l // 2], moe_w_gate[l // 2], moe_w_up[l // 2], moe_w_down[l // 2],
                      ln2_g[l][None], ln2_b[l][None])
    return xf.reshape(bsz, s, d)
```

```python
import functools
import math

import jax
import jax.numpy as jnp
import numpy as np
from jax import lax
from jax.experimental import pallas as pl
from jax.experimental.pallas import tpu as pltpu

F32 = jnp.float32
BF16 = jnp.bfloat16

D_MODEL = 1024
DEPTH = 2
HG_HEADS = 4
HG_DK = 128
HG_WIDTH = HG_HEADS * HG_DK
HG_CHUNK = 64
HG_SUB = 16
HG_HALF = HG_SUB // 2
NSA_HEADS = 8
NSA_GROUPS = 2
NSA_HPG = NSA_HEADS // NSA_GROUPS
NSA_DH = 64
NSA_WIDTH = NSA_HEADS * NSA_DH
KV_WIDTH = NSA_GROUPS * NSA_DH
CMP_LEN = 32
CMP_STRIDE = 16
CMP_HIDDEN = 2 * NSA_DH
SEL_BLOCK = 64
N_SEL = 16
WINDOW = 512
FORCE_SCORE = 1e9
NEG_BIG = -1e30
REL_BUCKETS = 32
REL_MAX_DIST = 2048
D_FF = 2752
N_EXPERTS = 8
TOP_K = 2
D_FF_EXPERT = 3584
MOE_ROW_BLOCK = 1024
DN_ALPHA = (2 * DEPTH) ** 0.25
LN_EPS = 1e-5
IN_SIZES = (HG_WIDTH, HG_WIDTH, HG_WIDTH, HG_WIDTH, NSA_WIDTH,
            KV_WIDTH, KV_WIDTH, KV_WIDTH, KV_WIDTH, KV_WIDTH, KV_WIDTH,
            3 * NSA_HEADS, D_MODEL, D_MODEL)

LANES = 128
LOG2E = 1.0 / math.log(2.0)
COL_GA = 0
COL_HG = 2 * D_MODEL
COL_NQ = COL_HG + 4 * HG_WIDTH
COL_KV = COL_NQ + NSA_WIDTH
COL_NG = COL_KV + 6 * KV_WIDTH
PROJ_W = COL_NG + LANES
PROJ_TN = 512
PROJ_WP = -(-PROJ_W // PROJ_TN) * PROJ_TN

ATT_TQ = 256
ATT_TK = 512
BIAS_ND = -(-(REL_MAX_DIST + ATT_TK) // ATT_TQ)
VMEM_LIMIT = 48 * 1024 * 1024


def _cparams(sem):
    return pltpu.CompilerParams(dimension_semantics=sem, vmem_limit_bytes=VMEM_LIMIT)


def _proj_kernel(x_ref, w_ref, o_ref, xb_ref):
    @pl.when(pl.program_id(1) == 0)
    def _():
        xb_ref[...] = x_ref[...].astype(BF16)

    o_ref[...] = jnp.dot(xb_ref[...], w_ref[...], preferred_element_type=F32).astype(o_ref.dtype)


def _project(x, w, tm, tn):
    m, k = x.shape
    n = w.shape[1]
    return pl.pallas_call(
        _proj_kernel,
        grid=(m // tm, n // tn),
        in_specs=[pl.BlockSpec((tm, k), lambda i, j: (i, 0)),
                  pl.BlockSpec((k, tn), lambda i, j: (0, j))],
        out_specs=pl.BlockSpec((tm, tn), lambda i, j: (i, j)),
        out_shape=jax.ShapeDtypeStruct((m, n), BF16),
        scratch_shapes=[pltpu.VMEM((tm, k), BF16)],
        compiler_params=_cparams(("parallel", "arbitrary")),
    )(x, w)


def _layer_norm(y, g, b):
    mu = jnp.mean(y, -1, keepdims=True)
    yc = y - mu
    var = jnp.mean(yc * yc, -1, keepdims=True)
    return yc * lax.rsqrt(var + LN_EPS) * g + b


def _split3(x):
    parts = []
    for _ in range(3):
        part = x.astype(BF16)
        parts.append(part)
        x = x - part.astype(F32)
    return parts


def _cumsum_rows(tril, x):
    return sum(jnp.dot(tril, part, preferred_element_type=F32) for part in _split3(x))


def _hgrn_kernel(q_ref, f_ref, i_ref, g_ref, lb_ref, nw_ref, o_ref, st_ref, *, n_chunks):
    @pl.when(pl.program_id(1) == 0)
    def _():
        st_ref[...] = jnp.zeros_like(st_ref)

    c = HG_CHUNK
    nt = (((1,), (1,)), ((), ()))
    row = lax.broadcasted_iota(jnp.int32, (c, c), 0)
    col = lax.broadcasted_iota(jnp.int32, (c, c), 1)
    tril = (row >= col).astype(BF16)
    row1 = lax.broadcasted_iota(jnp.int32, (c, 1), 0)
    half_pos = row1 % HG_HALF
    second_half = row1 % HG_SUB >= HG_HALF
    same_sub = row // HG_SUB == col // HG_SUB

    w = HG_HEADS * HG_DK
    head_lanes = [slice(h * HG_DK, (h + 1) * HG_DK) for h in range(HG_HEADS)]

    def per_head(fn):
        return jnp.concatenate([fn(h, head_lanes[h]) for h in range(HG_HEADS)], axis=1)

    def shift_in_half(x, lag):
        return pltpu.roll(x.reshape(c // HG_HALF, HG_HALF, w), lag, 1).reshape(c, w)

    def chunk(ci, carry):
        r0 = pl.multiple_of(ci * c, c)
        q = q_ref[pl.ds(r0, c), :].astype(F32)
        z = f_ref[pl.ds(r0, c), :].astype(F32)
        v = i_ref[pl.ds(r0, c), :].astype(F32)
        g = g_ref[pl.ds(r0, c), :].astype(F32)
        k = (1.0 - lb_ref[...]) * jax.nn.sigmoid(-z)
        b = _cumsum_rows(tril, jnp.log1p(-k)) * LOG2E
        b_last = b[c - 1:c, :]
        vb = v.astype(BF16)
        qe = (q * jnp.exp2(b)).astype(BF16)
        o = per_head(lambda h, hl: lax.dot_general(qe[:, hl], st_ref[h].astype(BF16), nt,
                                                  preferred_element_type=F32))
        att_rows = [[jnp.zeros((HG_SUB, c), F32)] * HG_HEADS]
        for sb in range(1, c // HG_SUB):
            lo = sb * HG_SUB
            ref_b = b[lo - 1:lo, :]
            qt = (q[lo:lo + HG_SUB, :] * jnp.exp2(b[lo:lo + HG_SUB, :] - ref_b)).astype(BF16)
            kt = (k * jnp.exp2(jnp.where(row1 < lo, ref_b - b, -jnp.inf))).astype(BF16)
            att_rows.append([lax.dot_general(qt[:, hl], kt[:, hl], nt, preferred_element_type=F32)
                             for hl in head_lanes])
        mid = jnp.concatenate(
            [jnp.broadcast_to(b[lo + HG_HALF - 1:lo + HG_HALF, :], (HG_SUB, w)) for lo in range(0, c, HG_SUB)],
            axis=0)
        q2 = (q * jnp.exp2(jnp.where(second_half, b - mid, -jnp.inf))).astype(BF16)
        k2 = (k * jnp.exp2(jnp.where(second_half, -jnp.inf, mid - b))).astype(BF16)
        att_half = [lax.dot_general(q2[:, hl], k2[:, hl], nt, preferred_element_type=F32) for hl in head_lanes]

        def intra(h, hl):
            att = jnp.concatenate([rows[h] for rows in att_rows], axis=0) + jnp.where(same_sub, att_half[h], 0.0)
            return jnp.dot(att.astype(BF16), vb[:, hl], preferred_element_type=F32)

        o = o + per_head(intra)
        for lag in range(HG_HALF):
            if lag == 0:
                ks, bs, vs = k, b, v
            else:
                ks, bs, vs = shift_in_half(k, lag), shift_in_half(b, lag), shift_in_half(v, lag)
            valid = half_pos >= lag
            prod = q * ks * jnp.exp2(jnp.where(valid, b - bs, 0.0))
            a = per_head(lambda h, hl: jnp.broadcast_to(
                jnp.sum(prod[:, hl], axis=1, keepdims=True), (c, HG_DK)))
            o = o + jnp.where(valid, a, 0.0) * vs
        khat = (k * jnp.exp2(b_last - b)).astype(BF16)
        decay = jnp.exp2(b_last)
        for h, hl in enumerate(head_lanes):
            st_ref[h] = st_ref[h] * decay[:, hl] + lax.dot_general(
                vb[:, hl], khat[:, hl], (((0,), (0,)), ((), ())), preferred_element_type=F32)
        sq = o * o
        ms = per_head(lambda h, hl: jnp.broadcast_to(jnp.mean(sq[:, hl], -1, keepdims=True), (c, HG_DK)))
        o = o * lax.rsqrt(ms + 1e-6)
        o_ref[pl.ds(r0, c), :] = (o * nw_ref[...] * (g * jax.nn.sigmoid(g))).astype(o_ref.dtype)
        return carry

    lax.fori_loop(0, n_chunks, chunk, 0, unroll=4)


def _hgrn(proj, lb, norm_w, ts):
    bsz, s, _ = proj.shape
    col0 = COL_HG // HG_WIDTH

    def col_spec(n):
        return pl.BlockSpec((None, ts, HG_WIDTH), lambda b, t: (b, t, col0 + n))

    head_spec = pl.BlockSpec((1, HG_WIDTH), lambda b, t: (0, 0))
    return pl.pallas_call(
        functools.partial(_hgrn_kernel, n_chunks=ts // HG_CHUNK),
        grid=(bsz, s // ts),
        in_specs=[col_spec(0), col_spec(1), col_spec(2), col_spec(3), head_spec, head_spec],
        out_specs=pl.BlockSpec((None, ts, HG_WIDTH), lambda b, t: (b, t, 0)),
        out_shape=jax.ShapeDtypeStruct((bsz, s, HG_WIDTH), BF16),
        scratch_shapes=[pltpu.VMEM((HG_HEADS, HG_DK, HG_DK), F32)],
        compiler_params=_cparams(("parallel", "arbitrary")),
    )(proj, proj, proj, proj, lb, norm_w)


def _compress_kernel(a_ref, pos_ref, w1_ref, w2_ref, o_ref):
    half = CMP_STRIDE * NSA_DH
    a = a_ref[...].astype(F32)
    n = a.shape[0]
    a1 = (a + pos_ref[:, :half]).astype(BF16)
    a2 = (a + pos_ref[:, half:]).astype(BF16)
    y1 = jnp.dot(a1, w1_ref[:half, :], preferred_element_type=F32)
    y2 = jnp.dot(a2, w1_ref[half:, :], preferred_element_type=F32)
    hid = jax.nn.gelu(y1 + pltpu.roll(y2, n - 1, 0))
    o_ref[...] = jnp.dot(hid.astype(BF16), w2_ref[...], preferred_element_type=F32).astype(o_ref.dtype)


def _compress(a, pos, w1, w2):
    _, bsz, g, n, width = a.shape
    return pl.pallas_call(
        _compress_kernel,
        grid=(2, bsz, g),
        in_specs=[pl.BlockSpec((None, None, None, n, width), lambda c, b, h: (c, b, h, 0, 0)),
                  pl.BlockSpec((None, 1, 2 * width), lambda c, b, h: (c, 0, 0)),
                  pl.BlockSpec((None, 2 * width, CMP_HIDDEN), lambda c, b, h: (c, 0, 0)),
                  pl.BlockSpec((None, CMP_HIDDEN, NSA_DH), lambda c, b, h: (c, 0, 0))],
        out_specs=pl.BlockSpec((None, None, None, n, NSA_DH), lambda c, b, h: (c, b, h, 0, 0)),
        out_shape=jax.ShapeDtypeStruct((2, bsz, g, n, NSA_DH), BF16),
        compiler_params=_cparams(("parallel", "parallel", "parallel")),
    )(a, pos, w1, w2)


def _bucket_of(dist):
    n = jnp.maximum(dist, 0)
    exact = REL_BUCKETS // 2
    large = exact + (jnp.log(jnp.maximum(n, exact).astype(F32) / exact)
                     / math.log(REL_MAX_DIST / exact) * (REL_BUCKETS - exact)).astype(jnp.int32)
    return jnp.where(n < exact, n, jnp.minimum(large, REL_BUCKETS - 1))


def _bias_kernel(tab_ref, o_ref, *, dist_fn):
    rows, cols = o_ref.shape[2], o_ref.shape[3]
    row = lax.broadcasted_iota(jnp.int32, (rows, cols), 0)
    col = lax.broadcasted_iota(jnp.int32, (rows, cols), 1)
    bucket = _bucket_of(dist_fn(pl.program_id(0), row, col))
    for h in range(NSA_HEADS):
        tab = jnp.broadcast_to(tab_ref[h:h + 1, :] * LOG2E, (rows, LANES))
        for c0 in range(0, cols, LANES):
            o_ref[h, 0, :, c0:c0 + LANES] = jnp.take_along_axis(tab, bucket[:, c0:c0 + LANES], axis=1)


def _bias_tiles(rel_bias, n_tiles, rows, cols, dist_fn):
    tab = jnp.pad(rel_bias.T, ((0, 0), (0, LANES - REL_BUCKETS)))
    return pl.pallas_call(
        functools.partial(_bias_kernel, dist_fn=dist_fn),
        grid=(n_tiles,),
        in_specs=[pl.BlockSpec((NSA_HEADS, LANES), lambda t: (0, 0))],
        out_specs=pl.BlockSpec((NSA_HEADS, 1, rows, cols), lambda t: (0, t, 0, 0)),
        out_shape=jax.ShapeDtypeStruct((NSA_HEADS, n_tiles, rows, cols), F32),
        compiler_params=_cparams(("parallel",)),
    )(tab)


def _stack_heads(qb):
    tq = qb.shape[0]
    half = lax.broadcasted_iota(jnp.int32, (tq, LANES), 1) // NSA_DH
    return jnp.concatenate(
        [jnp.where(half == hp % 2, qb[:, LANES * (hp // 2):LANES * (hp // 2 + 1)], jnp.zeros((tq, LANES), qb.dtype))
         for hp in range(NSA_HPG)], axis=0)


def _cmp_kernel(q_ref, k_ref, v_ref, bias_ref, c2s_ref, gl_ref, o_ref, qa_ref, *, tq):
    p = NSA_HPG
    t0 = pl.program_id(1) * tq
    bsz, ncp = k_ref.shape[0], k_ref.shape[1]
    bias = bias_ref[:, 0]
    tpos = t0 + lax.broadcasted_iota(jnp.int32, (tq, ncp), 0)
    cend = lax.broadcasted_iota(jnp.int32, (tq, ncp), 1) * CMP_STRIDE + (CMP_LEN - 1)
    mask = (tpos >= cend)[None]
    lane = lax.broadcasted_iota(jnp.int32, (tq, LANES), 1)
    cur = (t0 + lax.broadcasted_iota(jnp.int32, (tq, LANES), 0)) // SEL_BLOCK
    forced = (lane == 0) | (lane == cur) | (lane == cur - 1)
    visible = lane <= cur
    batch = range(bsz)
    nt = (((1,), (1,)), ((), ()))
    qs = [_stack_heads(q_ref[b]) for b in batch]
    lgs = [lax.dot_general(qs[b], k_ref[b], nt, preferred_element_type=F32) for b in batch]
    lgs = [jnp.where(mask, lg.reshape(p, tq, ncp) + bias, NEG_BIG) for lg in lgs]
    es = [jnp.where(mask, jnp.exp2(lg - jnp.max(lg, -1, keepdims=True)), 0.0) for lg in lgs]
    pcs = [e / jnp.maximum(jnp.sum(e, -1, keepdims=True), 1e-30) for e in es]
    os = [jnp.dot(pcs[b].reshape(p * tq, ncp).astype(BF16), v_ref[b], preferred_element_type=F32) for b in batch]
    imps = [sum(jnp.dot(part, c2s_ref[...], preferred_element_type=F32) for part in _split3(jnp.sum(pc, axis=0)))
            for pc in pcs]
    for b in batch:
        o_ref[b] = _heads_to_lanes(os[b], jax.nn.sigmoid(gl_ref[b, :, :, 0:1].astype(F32)), tq, False)
    scores = [jnp.where(forced, FORCE_SCORE, jnp.where(visible, imp, NEG_BIG)) for imp in imps]
    members = [jnp.zeros((tq, LANES), F32)] * bsz
    for _ in range(N_SEL):
        hits = [lane == jnp.argmax(score, axis=-1, keepdims=True) for score in scores]
        members = [jnp.where(hit, 1.0, member) for hit, member in zip(hits, members)]
        scores = [jnp.where(hit, -jnp.inf, score) for hit, score in zip(hits, scores)]
    for b in batch:
        pen = jnp.where((members[b] > 0.5) & visible, 0.0, NEG_BIG).astype(qa_ref.dtype)
        qa_ref[b] = jnp.concatenate([jnp.concatenate([pen] * p, axis=0), qs[b]], axis=1).reshape(p, tq, 2 * LANES)


def _heads_to_lanes(x, gates, tq, normalise):
    low = lax.broadcasted_iota(jnp.int32, (tq, LANES), 1) < NSA_DH
    if normalise:
        swapped = pltpu.roll(x, NSA_DH, 1)
    pieces = []
    for pair in range(NSA_HPG // 2):
        ev = slice(2 * pair * tq, (2 * pair + 1) * tq)
        od = slice((2 * pair + 1) * tq, (2 * pair + 2) * tq)
        if normalise:
            even, odd = x[ev] / swapped[ev], swapped[od] / x[od]
        else:
            even, odd = x[ev], x[od]
        pieces.append(jnp.where(low, even * gates[2 * pair], odd * gates[2 * pair + 1]))
    return jnp.concatenate(pieces, axis=1)


def _out_spec(bsz, tq):
    return pl.BlockSpec((bsz, tq, NSA_HPG * NSA_DH), lambda h, i: (0, i, h))


def _q_spec(bsz, tq):
    width = NSA_HPG * NSA_DH
    return pl.BlockSpec((bsz, tq, width), lambda h, i: (0, i, COL_NQ // width + h))


def _cmp_branch(proj, kc2, vc, bias_c, c2s, gl5, tq):
    bsz, s, _ = proj.shape
    g, p, dh = NSA_GROUPS, NSA_HPG, NSA_DH
    ncp = kc2.shape[2]
    qmap = lambda h, i: (0, h, 0, i, 0)
    return pl.pallas_call(
        functools.partial(_cmp_kernel, tq=tq),
        grid=(g, s // tq),
        in_specs=[_q_spec(bsz, tq),
                  pl.BlockSpec((bsz, None, ncp, LANES), lambda h, i: (0, h, 0, 0)),
                  pl.BlockSpec((bsz, None, ncp, LANES), lambda h, i: (0, h, 0, 0)),
                  pl.BlockSpec((p, 1, tq, ncp), lambda h, i: (h, i, 0, 0)),
                  pl.BlockSpec((ncp, LANES), lambda h, i: (0, 0)),
                  pl.BlockSpec((bsz, None, p, tq, 3), qmap)],
        out_specs=[_out_spec(bsz, tq),
                   pl.BlockSpec((bsz, None, p, tq, 2 * LANES), qmap)],
        out_shape=[jax.ShapeDtypeStruct((bsz, s, NSA_WIDTH), F32),
                   jax.ShapeDtypeStruct((bsz, g, p, s, 2 * LANES), BF16)],
        compiler_params=_cparams(("parallel", "parallel")),
    )(proj, kc2, vc, bias_c, c2s, gl5)


def _win_kernel(*refs, tq, n_back):
    nk = n_back + 1
    q_ref, k_refs, v_refs, b_refs = refs[0], refs[1:1 + nk], refs[1 + nk:1 + 2 * nk], refs[1 + 2 * nk:1 + 3 * nk]
    gl_ref, acc_ref, o_ref = refs[1 + 3 * nk:]
    p = NSA_HPG
    bsz = q_ref.shape[0]
    t0 = pl.program_id(1) * tq
    row = lax.broadcasted_iota(jnp.int32, (tq, tq), 0)
    col = lax.broadcasted_iota(jnp.int32, (tq, tq), 1)
    masks = [((row - col + d * tq >= 0) & (row - col + d * tq < WINDOW) & (t0 - d * tq >= 0))[None]
             for d in range(nk)]
    biases = [b_ref[:, 0] for b_ref in b_refs]
    nt = (((1,), (1,)), ((), ()))
    batch = range(bsz)
    qs = [_stack_heads(q_ref[b]) for b in batch]
    lgs = [[lax.dot_general(qs[b], k_ref[b], nt, preferred_element_type=F32).reshape(p, tq, tq)
            for k_ref in k_refs] for b in batch]
    lgs = [[jnp.where(mask, lg + bias, NEG_BIG) for lg, mask, bias in zip(lgs[b], masks, biases)] for b in batch]
    ms = [functools.reduce(jnp.maximum, [jnp.max(lg, -1, keepdims=True) for lg in lgs[b]]) for b in batch]
    es = [[jnp.exp2(lg - ms[b]).reshape(p * tq, tq).astype(BF16) for lg in lgs[b]] for b in batch]
    os = [sum(jnp.dot(e, v_ref[b], preferred_element_type=F32) for e, v_ref in zip(es[b], v_refs)) for b in batch]
    for b in batch:
        o_ref[b] = acc_ref[b] + _heads_to_lanes(os[b], jax.nn.sigmoid(gl_ref[b, :, :, 2:3].astype(F32)), tq, True)


def _win_branch(proj, kw2, vw_aug, bias_t, gl5, acc, tq):
    bsz, s, _ = acc.shape
    g, p, dh = NSA_GROUPS, NSA_HPG, NSA_DH
    n_back = WINDOW // tq
    back = [lambda h, i, d=d: (0, h, jnp.maximum(i - d, 0), 0) for d in range(n_back + 1)]
    qmap = lambda h, i: (0, h, 0, i, 0)
    in_specs = ([_q_spec(bsz, tq)]
                + [pl.BlockSpec((bsz, None, tq, LANES), m) for m in back]
                + [pl.BlockSpec((bsz, None, tq, LANES), m) for m in back]
                + [pl.BlockSpec((p, 1, tq, tq), lambda h, i, d=d: (h, d, 0, 0)) for d in range(n_back + 1)]
                + [pl.BlockSpec((bsz, None, p, tq, 3), qmap), _out_spec(bsz, tq)])
    return pl.pallas_call(
        functools.partial(_win_kernel, tq=tq, n_back=n_back),
        grid=(g, s // tq),
        in_specs=in_specs,
        out_specs=_out_spec(bsz, tq),
        out_shape=jax.ShapeDtypeStruct((bsz, s, NSA_WIDTH), F32),
        input_output_aliases={len(in_specs) - 1: 0},
        compiler_params=_cparams(("parallel", "parallel")),
    )(proj, *([kw2] * (n_back + 1)), *([vw_aug] * (n_back + 1)), *([bias_t] * (n_back + 1)), gl5, acc)


def _sel_kernel(it_ref, jt_ref, q_ref, k_ref, v_ref, bias_ref, gl_ref, acc_ref, o_ref, m_sc, a_sc,
                *, tq, tk, bsz):
    p = NSA_HPG
    nq = p * tq
    step = pl.program_id(1)
    i = it_ref[step]
    j = jt_ref[step]
    t0 = i * tq
    s0 = j * tk
    last = s0 + tk > t0

    @pl.when(j == 0)
    def _():
        m_sc[...] = jnp.full_like(m_sc, NEG_BIG)
        a_sc[...] = jnp.zeros_like(a_sc)

    def accumulate(on_diagonal):
        nt = (((1,), (1,)), ((), ()))
        bias = bias_ref[:, 0].reshape(nq, tk)
        if on_diagonal:
            causal = (t0 + lax.broadcasted_iota(jnp.int32, (tq, tk), 0)
                      >= s0 + lax.broadcasted_iota(jnp.int32, (tq, tk), 1))
            causal = jnp.concatenate([causal] * p, axis=0)
        for b in range(bsz):
            lg = lax.dot_general(q_ref[b].reshape(nq, 2 * LANES), k_ref[b], nt, preferred_element_type=F32) + bias
            if on_diagonal:
                lg = jnp.where(causal, lg, NEG_BIG)
            m_old = m_sc[b]
            m_new = jnp.maximum(m_old, jnp.max(lg, -1, keepdims=True))
            e = jnp.exp2(lg - jnp.concatenate([m_new] * (tk // LANES), axis=1))
            a_sc[b] = (jnp.exp2(m_old - m_new) * a_sc[b]
                       + jnp.dot(e.astype(BF16), v_ref[b], preferred_element_type=F32))
            m_sc[b] = m_new

    @pl.when(jnp.logical_not(last))
    def _():
        accumulate(False)

    @pl.when(last)
    def _():
        accumulate(True)
        gate = jax.nn.sigmoid(gl_ref[:, :, :, 1:2].astype(F32))
        for b in range(bsz):
            o_ref[b] = (acc_ref[b] + _heads_to_lanes(a_sc[b], gate[b], tq, True)).astype(o_ref.dtype)


def _sel_branch(q_aug, k_aug, v_aug, bias_t, gl5, acc, tq, tk):
    bsz, g, p, s, wq = q_aug.shape
    r = tk // tq
    pairs = [(i, j) for i in range(s // tq) for j in range(i // r + 1)]
    it = jnp.asarray([ij[0] for ij in pairs], jnp.int32)
    jt = jnp.asarray([ij[1] for ij in pairs], jnp.int32)
    nd = bias_t.shape[1] - 1
    qmap = lambda h, t, it, jt: (0, h, 0, it[t], 0)
    kmap = lambda h, t, it, jt: (0, h, jt[t], 0)
    omap = lambda h, t, it, jt: (0, it[t], h)
    grid_spec = pltpu.PrefetchScalarGridSpec(
        num_scalar_prefetch=2,
        grid=(g, len(pairs)),
        in_specs=[pl.BlockSpec((bsz, None, p, tq, wq), qmap),
                  pl.BlockSpec((bsz, None, tk, wq), kmap),
                  pl.BlockSpec((bsz, None, tk, LANES), kmap),
                  pl.BlockSpec((p, 1, tq, tk),
                               lambda h, t, it, jt: (h, jnp.minimum(it[t] - r * jt[t], nd), 0, 0)),
                  pl.BlockSpec((bsz, None, p, tq, 3), qmap),
                  pl.BlockSpec((bsz, tq, p * NSA_DH), omap)],
        out_specs=pl.BlockSpec((bsz, tq, p * NSA_DH), omap),
        scratch_shapes=[pltpu.VMEM((bsz, p * tq, LANES), F32),
                        pltpu.VMEM((bsz, p * tq, LANES), F32)],
    )
    return pl.pallas_call(
        functools.partial(_sel_kernel, tq=tq, tk=tk, bsz=bsz),
        grid_spec=grid_spec,
        out_shape=jax.ShapeDtypeStruct((bsz, s, NSA_WIDTH), BF16),
        compiler_params=_cparams(("parallel", "arbitrary")),
    )(it, jt, q_aug, k_aug, v_aug, bias_t, gl5, acc)


def _merge_kernel(x_ref, oa_ref, ob_ref, ga_ref, gb_ref, wa_ref, wb_ref, wo_ref, g_ref, b_ref, o_ref):
    ya = jnp.dot(oa_ref[...].astype(BF16), wa_ref[...], preferred_element_type=F32)
    yb = jnp.dot(ob_ref[...].astype(BF16), wb_ref[...], preferred_element_type=F32)
    y = jax.nn.sigmoid(ga_ref[...].astype(F32)) * ya + jax.nn.sigmoid(gb_ref[...].astype(F32)) * yb
    mix = jnp.dot(y.astype(BF16), wo_ref[...], preferred_element_type=F32)
    o_ref[...] = _layer_norm(DN_ALPHA * x_ref[...] + mix, g_ref[...], b_ref[...])


def _merge(x, o_a, o_b, proj, wa, wb, wo, g, b, tm):
    t, d = x.shape
    nga = COL_GA // d
    row = lambda i: (i, 0)
    const = lambda i: (0, 0)
    return pl.pallas_call(
        _merge_kernel,
        grid=(t // tm,),
        in_specs=[pl.BlockSpec((tm, d), row),
                  pl.BlockSpec((tm, HG_WIDTH), row),
                  pl.BlockSpec((tm, NSA_WIDTH), row),
                  pl.BlockSpec((tm, d), lambda i: (i, nga)),
                  pl.BlockSpec((tm, d), lambda i: (i, nga + 1)),
                  pl.BlockSpec((HG_WIDTH, d), const),
                  pl.BlockSpec((NSA_WIDTH, d), const),
                  pl.BlockSpec((d, d), const),
                  pl.BlockSpec((1, d), const),
                  pl.BlockSpec((1, d), const)],
        out_specs=pl.BlockSpec((tm, d), row),
        out_shape=jax.ShapeDtypeStruct((t, d), F32),
        compiler_params=_cparams(("parallel",)),
    )(x, o_a, o_b, proj, proj, wa, wb, wo, g, b)


def _swiglu_step(xb, wg_ref, wu_ref, wd_ref):
    hg = jnp.dot(xb, wg_ref[...], preferred_element_type=F32)
    hu = jnp.dot(xb, wu_ref[...], preferred_element_type=F32)
    h = (hg * jax.nn.sigmoid(hg)) * hu
    return jnp.dot(h.astype(BF16), wd_ref[...], preferred_element_type=F32)


def _ffn_kernel(x_ref, wg_ref, wu_ref, wd_ref, g_ref, b_ref, o_ref):
    x = x_ref[...]
    f = _swiglu_step(x.astype(BF16), wg_ref, wu_ref, wd_ref)
    o_ref[...] = _layer_norm(DN_ALPHA * x + f, g_ref[...], b_ref[...])


def _ffn(x, wg, wu, wd, g, b, tm):
    t, d = x.shape
    f = wg.shape[1]
    const = lambda i: (0, 0)
    once = pl.Buffered(1)
    return pl.pallas_call(
        _ffn_kernel,
        grid=(t // tm,),
        in_specs=[pl.BlockSpec((tm, d), lambda i: (i, 0)),
                  pl.BlockSpec((d, f), const, pipeline_mode=once),
                  pl.BlockSpec((d, f), const, pipeline_mode=once),
                  pl.BlockSpec((f, d), const, pipeline_mode=once),
                  pl.BlockSpec((1, d), const),
                  pl.BlockSpec((1, d), const)],
        out_specs=pl.BlockSpec((tm, d), lambda i: (i, 0)),
        out_shape=jax.ShapeDtypeStruct((t, d), F32),
        compiler_params=_cparams(("parallel",)),
    )(x, wg, wu, wd, g, b)


def _router_kernel(x_ref, w_ref, o_ref):
    xs, ws = _split3(x_ref[...]), _split3(w_ref[...])
    logits = sum(jnp.dot(xs[i], ws[j], preferred_element_type=F32) for i, j in ((0, 0), (0, 1), (1, 0)))
    lane =lax.broadcasted_iota(jnp.int32, logits.shape, 1).astype(F32)
    logits = jnp.where(lane < N_EXPERTS, logits, -jnp.inf)
    v1 = jnp.max(logits, -1, keepdims=True)
    e1 = jnp.min(jnp.where(logits == v1, lane, float(LANES)), -1, keepdims=True)
    rest = jnp.where(lane == e1, -jnp.inf, logits)
    v2 = jnp.max(rest, -1, keepdims=True)
    e2 = jnp.min(jnp.where(rest == v2, lane, float(LANES)), -1, keepdims=True)
    x2 = jnp.exp(v2 - v1)
    den = 1.0 + x2
    o_ref[...] = jnp.where(lane == 0, e1, jnp.where(lane == 1, e2, jnp.where(
        lane == 2, 1.0 / den, jnp.where(lane == 3, x2 / den, 0.0))))


def _router(x, w, tm):
    t, d = x.shape
    return pl.pallas_call(
        _router_kernel,
        grid=(t // tm,),
        in_specs=[pl.BlockSpec((tm, d), lambda i: (i, 0)), pl.BlockSpec((d, LANES), lambda i: (0, 0))],
        out_specs=pl.BlockSpec((tm, LANES), lambda i: (i, 0)),
        out_shape=jax.ShapeDtypeStruct((t, LANES), F32),
        compiler_params=_cparams(("parallel",)),
    )(x, w)


def _expert_kernel(be_ref, x_ref, wg_ref, wu_ref, wd_ref, o_ref, acc_ref):
    j = pl.program_id(1)

    @pl.when(j == 0)
    def _():
        acc_ref[...] = jnp.zeros_like(acc_ref)

    xb = x_ref[...].astype(BF16)
    hg = jnp.dot(xb, wg_ref[...].astype(BF16), preferred_element_type=F32)
    hu = jnp.dot(xb, wu_ref[...].astype(BF16), preferred_element_type=F32)
    h = (hg * jax.nn.sigmoid(hg)) * hu
    acc_ref[...] += jnp.dot(h.astype(BF16), wd_ref[...].astype(BF16), preferred_element_type=F32)

    @pl.when(j == pl.num_programs(1) - 1)
    def _():
        o_ref[...] = acc_ref[...].astype(o_ref.dtype)


def _experts(blk_e, xs, wg, wu, wd, tf):
    rows, d = xs.shape
    f = wg.shape[2]
    tm = MOE_ROW_BLOCK
    grid_spec = pltpu.PrefetchScalarGridSpec(
        num_scalar_prefetch=1,
        grid=(rows // tm, f // tf),
        in_specs=[pl.BlockSpec((tm, d), lambda i, j, be: (i, 0)),
                  pl.BlockSpec((None, d, tf), lambda i, j, be: (be[i], 0, j)),
                  pl.BlockSpec((None, d, tf), lambda i, j, be: (be[i], 0, j)),
                  pl.BlockSpec((None, tf, d), lambda i, j, be: (be[i], j, 0))],
        out_specs=pl.BlockSpec((tm, d), lambda i, j, be: (i, 0)),
        scratch_shapes=[pltpu.VMEM((tm, d), F32)],
    )
    return pl.pallas_call(
        _expert_kernel,
        grid_spec=grid_spec,
        out_shape=jax.ShapeDtypeStruct((rows, d), F32),
        compiler_params=_cparams(("parallel", "arbitrary")),
    )(blk_e, xs, wg, wu, wd)


def _combine_kernel(x_ref, y1_ref, y2_ref, gt_ref, g_ref, b_ref, o_ref):
    f = y1_ref[...].astype(F32) * gt_ref[:, 2:3] + y2_ref[...].astype(F32) * gt_ref[:, 3:4]
    o_ref[...] = _layer_norm(DN_ALPHA * x_ref[...] + f, g_ref[...], b_ref[...])


def _combine(x, y1, y2, route, g, b, tm):
    t, d = x.shape
    row = lambda i: (i, 0)
    const = lambda i: (0, 0)
    return pl.pallas_call(
        _combine_kernel,
        grid=(t // tm,),
        in_specs=[pl.BlockSpec((tm, d), row), pl.BlockSpec((tm, d), row), pl.BlockSpec((tm, d), row),
                  pl.BlockSpec((tm, LANES), row), pl.BlockSpec((1, d), const), pl.BlockSpec((1, d), const)],
        out_specs=pl.BlockSpec((tm, d), row),
        out_shape=jax.ShapeDtypeStruct((t, d), F32),
        compiler_params=_cparams(("parallel",)),
    )(x, y1, y2, route, g, b)


def _moe(x, w_router, wg, wu, wd, g, b):
    t, d = x.shape
    tk_ = t * TOP_K
    route = _router(x, jnp.pad(w_router, ((0, 0), (0, LANES - N_EXPERTS))), 512)
    flat_e = route[:, :TOP_K].astype(jnp.int32).reshape(-1)
    onehot = (flat_e[:, None] == jnp.arange(N_EXPERTS)[None, :]).astype(jnp.int32)
    csum = jnp.cumsum(onehot, axis=0)
    counts = csum[-1]
    rank = jnp.sum(onehot * csum, axis=1) - 1
    padded = (counts + MOE_ROW_BLOCK - 1) // MOE_ROW_BLOCK * MOE_ROW_BLOCK
    pend = jnp.cumsum(padded)
    dest = (pend - padded)[flat_e] + rank
    n_blocks = -(-(tk_ + N_EXPERTS * (MOE_ROW_BLOCK - 1)) // MOE_ROW_BLOCK)
    n_rows = n_blocks * MOE_ROW_BLOCK
    row_tok = (jnp.arange(n_rows, dtype=jnp.int32) % t).at[dest].set(
        jnp.arange(tk_, dtype=jnp.int32) // TOP_K, unique_indices=True, mode='promise_in_bounds')
    blk_e = jnp.minimum(jnp.searchsorted(pend, jnp.arange(n_blocks) * MOE_ROW_BLOCK, side='right'),
                        N_EXPERTS - 1).astype(jnp.int32)
    xs = x.at[row_tok].get(mode='promise_in_bounds')
    ys = _experts(blk_e, xs, wg, wu, wd, 512)
    dest2 = dest.reshape(t, TOP_K)
    y1 = ys.at[dest2[:, 0]].get(mode='promise_in_bounds')
    y2 = ys.at[dest2[:, 1]].get(mode='promise_in_bounds')
    return _combine(x, y1, y2, route, g, b, 512)


def _pack_w_in(w_in):
    offs = np.concatenate([[0], np.cumsum(IN_SIZES)])
    seg = [w_in[:, offs[j]:offs[j + 1]] for j in range(len(IN_SIZES))]
    seg[4] = seg[4] * (NSA_DH ** -0.5 * LOG2E)
    seg[11] = jnp.pad(seg[11], ((0, 0), (0, LANES - 3 * NSA_HEADS)))
    seg = seg[12:14] + seg[0:12]
    packed = jnp.concatenate(seg + [jnp.zeros((w_in.shape[0], PROJ_WP - PROJ_W), w_in.dtype)], axis=1)
    return packed.astype(BF16)


def _cmp_to_sel(n_cmp_pad, n_cmp):
    cs = np.arange(n_cmp_pad)[:, None] * CMP_STRIDE
    ss = np.arange(LANES)[None, :] * SEL_BLOCK
    overlap = np.clip(np.minimum(cs + CMP_LEN, ss + SEL_BLOCK) - np.maximum(cs, ss), 0, None) / CMP_LEN
    overlap[n_cmp:] = 0.0
    return jnp.asarray(overlap, BF16)


def _token_mixer(x, w_in_p, lb, hg_norm_w, cmp_pos, cmp_w1, cmp_w2, bias_c, bias_t, wa, wb, wo, ln_g, ln_b):
    bsz, s, d = x.shape
    g, p, dh = NSA_GROUPS, NSA_HPG, NSA_DH
    xf = x.reshape(bsz * s, d)
    proj = _project(xf, w_in_p, 1024, PROJ_WP // 2).reshape(bsz, s, PROJ_WP)
    o_a = _hgrn(proj, lb, hg_norm_w, 1024)

    def heads(c0, width):
        return proj[:, :, c0:c0 + width]

    kv = heads(COL_KV, 6 * KV_WIDTH).reshape(bsz, s, 6, g, dh)
    n16 = s // CMP_STRIDE
    a = kv[:, :, 0:2].reshape(bsz, n16, CMP_STRIDE, 2, g, dh).transpose(3, 0, 4, 1, 2, 5)
    a = a.reshape(2, bsz, g, n16, CMP_STRIDE * dh)
    kvc = _compress(a, cmp_pos.reshape(2, 1, CMP_LEN * dh), cmp_w1.astype(BF16), cmp_w2.astype(BF16))
    kvh = kv[:, :, 2:6].astype(BF16).transpose(2, 0, 3, 1, 4)
    gl5 = heads(COL_NG, 3 * NSA_HEADS).reshape(bsz, s, g, p, 3).transpose(0, 2, 3, 1, 4)
    n_cmp = (s - CMP_LEN) // CMP_STRIDE + 1
    acc, q_aug = _cmp_branch(proj, jnp.concatenate([kvc[0], kvc[0]], axis=-1),
                             jnp.concatenate([kvc[1], kvc[1]], axis=-1), bias_c,
                             _cmp_to_sel(n16, n_cmp), gl5, ATT_TQ)
    ones = jnp.ones((bsz, g, s, LANES - dh), BF16)
    acc = _win_branch(proj, jnp.concatenate([kvh[2], kvh[2]], axis=-1), jnp.concatenate([kvh[3], ones], axis=-1),
                      bias_t, gl5, acc, ATT_TQ)
    block_of_key = (np.arange(s)[:, None] // SEL_BLOCK == np.arange(LANES)[None, :])
    k_aug = jnp.concatenate([jnp.broadcast_to(jnp.asarray(block_of_key, BF16), (bsz, g, s, LANES)), kvh[0],
                             kvh[0]], axis=-1)
    v_aug = jnp.concatenate([kvh[1], ones], axis=-1)
    o_b = _sel_branch(q_aug, k_aug, v_aug, bias_t, gl5, acc, ATT_TQ, ATT_TK)
    o_b = o_b.reshape(bsz * s, NSA_WIDTH)
    return _merge(xf, o_a.reshape(bsz * s, HG_WIDTH), o_b, proj.reshape(bsz * s, PROJ_WP),
                  wa, wb, wo, ln_g, ln_b, 512)


def kernel(x, w_in, hg_lb_logits, hg_norm_w, cmp_pos, cmp_w1, cmp_w2, rel_bias, w_branch_a, w_branch_b,
           w_out, ln1_g, ln1_b, ln2_g, ln2_b, ffn_w_gate, ffn_w_up, ffn_w_down, moe_router, moe_w_gate,
           moe_w_up, moe_w_down):
    bsz, s, d = x.shape
    depth = w_in.shape[0]
    p_lb = jax.nn.softmax(hg_lb_logits.astype(F32), axis=0)
    lbs = jnp.cumsum(p_lb, axis=0) - p_lb[0]
    n16 = s // CMP_STRIDE
    bias_c = _bias_tiles(rel_bias, s // ATT_TQ, ATT_TQ, n16,
                         lambda t, r, c: t * ATT_TQ + r - (c * CMP_STRIDE + CMP_LEN - 1))
    bias_t = _bias_tiles(rel_bias, BIAS_ND + 1, ATT_TQ, ATT_TK, lambda t, r, c: t * ATT_TQ + r - c)
    f_pad = -(-D_FF // LANES) * LANES - D_FF
    xf = x.reshape(bsz * s, d)
    for l in range(depth):
        xf = _token_mixer(xf.reshape(bsz, s, d), _pack_w_in(w_in[l]), lbs[l][None], hg_norm_w[l][None],
                          cmp_pos[l], cmp_w1[l], cmp_w2[l], bias_c, bias_t,
                          w_branch_a[l].astype(BF16), w_branch_b[l].astype(BF16), w_out[l].astype(BF16),
                          ln1_g[l][None], ln1_b[l][None])
        if l % 2 == 0:
            wg = jnp.pad(ffn_w_gate[l // 2], ((0, 0), (0, f_pad))).astype(BF16)
            wu = jnp.pad(ffn_w_up[l // 2], ((0, 0), (0, f_pad))).astype(BF16)
            wd = jnp.pad(ffn_w_down[l // 2], ((0, f_pad), (0, 0))).astype(BF16)
            xf = _ffn(xf, wg, wu, wd, ln2_g[l][None], ln2_b[l][None], 512)
        else:
            xf = _moe(xf, moe_router[l // 2], moe_w_gate[l // 2], moe_w_up[l // 2], moe_w_down[l // 2],
                      ln2_g[l][None], ln2_b[l][None])
    return xf.reshape(bsz, s, d)
```

```python
import functools
import math

import jax
import jax.numpy as jnp
import numpy as np
from jax import lax
from jax.experimental import pallas as pl
from jax.experimental.pallas import tpu as pltpu

F32 = jnp.float32
BF16 = jnp.bfloat16

D_MODEL = 1024
DEPTH = 2
HG_HEADS = 4
HG_DK = 128
HG_WIDTH = HG_HEADS * HG_DK
HG_CHUNK = 64
HG_SUB = 16
HG_HALF = HG_SUB // 2
NSA_HEADS = 8
NSA_GROUPS = 2
NSA_HPG = NSA_HEADS // NSA_GROUPS
NSA_DH = 64
NSA_WIDTH = NSA_HEADS * NSA_DH
KV_WIDTH = NSA_GROUPS * NSA_DH
CMP_LEN = 32
CMP_STRIDE = 16
CMP_HIDDEN = 2 * NSA_DH
SEL_BLOCK = 64
N_SEL = 16
WINDOW = 512
FORCE_SCORE = 1e9
NEG_BIG = -1e30
REL_BUCKETS = 32
REL_MAX_DIST = 2048
D_FF = 2752
N_EXPERTS = 8
TOP_K = 2
D_FF_EXPERT = 3584
MOE_ROW_BLOCK = 1024
DN_ALPHA = (2 * DEPTH) ** 0.25
LN_EPS = 1e-5
IN_SIZES = (HG_WIDTH, HG_WIDTH, HG_WIDTH, HG_WIDTH, NSA_WIDTH,
            KV_WIDTH, KV_WIDTH, KV_WIDTH, KV_WIDTH, KV_WIDTH, KV_WIDTH,
            3 * NSA_HEADS, D_MODEL, D_MODEL)

LANES = 128
LOG2E = 1.0 / math.log(2.0)
COL_GA = 0
COL_HG = 2 * D_MODEL
COL_NQ = COL_HG + 4 * HG_WIDTH
COL_KV = COL_NQ + NSA_WIDTH
COL_NG = COL_KV + 6 * KV_WIDTH
PROJ_W = COL_NG + LANES
PROJ_TN = 512
PROJ_WP = -(-PROJ_W // PROJ_TN) * PROJ_TN

ATT_TQ = 256
ATT_TK = 512
BIAS_ND = -(-(REL_MAX_DIST + ATT_TK) // ATT_TQ)
VMEM_LIMIT = 48 * 1024 * 1024


def _cparams(sem):
    return pltpu.CompilerParams(dimension_semantics=sem, vmem_limit_bytes=VMEM_LIMIT)


def _proj_kernel(x_ref, w_ref, o_ref, xb_ref):
    @pl.when(pl.program_id(1) == 0)
    def _():
        xb_ref[...] = x_ref[...].astype(BF16)

    o_ref[...] = jnp.dot(xb_ref[...], w_ref[...], preferred_element_type=F32).astype(o_ref.dtype)


def _project(x, w, tm, tn):
    m, k = x.shape
    n = w.shape[1]
    return pl.pallas_call(
        _proj_kernel,
        grid=(m // tm, n // tn),
        in_specs=[pl.BlockSpec((tm, k), lambda i, j: (i, 0)),
                  pl.BlockSpec((k, tn), lambda i, j: (0, j))],
        out_specs=pl.BlockSpec((tm, tn), lambda i, j: (i, j)),
        out_shape=jax.ShapeDtypeStruct((m, n), BF16),
        scratch_shapes=[pltpu.VMEM((tm, k), BF16)],
        compiler_params=_cparams(("parallel", "arbitrary")),
    )(x, w)


def _layer_norm(y, g, b):
    mu = jnp.mean(y, -1, keepdims=True)
    yc = y - mu
    var = jnp.mean(yc * yc, -1, keepdims=True)
    return yc * lax.rsqrt(var + LN_EPS) * g + b


def _split3(x):
    parts = []
    for _ in range(3):
        part = x.astype(BF16)
        parts.append(part)
        x = x - part.astype(F32)
    return parts


def _cumsum_rows(tril, x):
    return sum(jnp.dot(tril, part, preferred_element_type=F32) for part in _split3(x))


def _hgrn_kernel(q_ref, f_ref, i_ref, g_ref, lb_ref, nw_ref, o_ref, st_ref, *, n_chunks):
    @pl.when(pl.program_id(1) == 0)
    def _():
        st_ref[...] = jnp.zeros_like(st_ref)

    c = HG_CHUNK
    nt = (((1,), (1,)), ((), ()))
    row = lax.broadcasted_iota(jnp.int32, (c, c), 0)
    col = lax.broadcasted_iota(jnp.int32, (c, c), 1)
    tril = (row >= col).astype(BF16)
    row1 = lax.broadcasted_iota(jnp.int32, (c, 1), 0)
    half_pos = row1 % HG_HALF
    second_half = row1 % HG_SUB >= HG_HALF
    same_sub = row // HG_SUB == col // HG_SUB

    w = HG_HEADS * HG_DK
    head_lanes = [slice(h * HG_DK, (h + 1) * HG_DK) for h in range(HG_HEADS)]

    def per_head(fn):
        return jnp.concatenate([fn(h, head_lanes[h]) for h in range(HG_HEADS)], axis=1)

    def shift_in_half(x, lag):
        return pltpu.roll(x.reshape(c // HG_HALF, HG_HALF, w), lag, 1).reshape(c, w)

    def chunk(ci, carry):
        r0 = pl.multiple_of(ci * c, c)
        q = q_ref[pl.ds(r0, c), :].astype(F32)
        z = f_ref[pl.ds(r0, c), :].astype(F32)
        v = i_ref[pl.ds(r0, c), :].astype(F32)
        g = g_ref[pl.ds(r0, c), :].astype(F32)
        k = (1.0 - lb_ref[...]) * jax.nn.sigmoid(-z)
        b = _cumsum_rows(tril, jnp.log1p(-k)) * LOG2E
        b_last = b[c - 1:c, :]
        vb = v.astype(BF16)
        qe = (q * jnp.exp2(b)).astype(BF16)
        o = per_head(lambda h, hl: lax.dot_general(qe[:, hl], st_ref[h].astype(BF16), nt,
                                                  preferred_element_type=F32))
        att_rows = [[jnp.zeros((HG_SUB, c), F32)] * HG_HEADS]
        for sb in range(1, c // HG_SUB):
            lo = sb * HG_SUB
            ref_b = b[lo - 1:lo, :]
            qt = (q[lo:lo + HG_SUB, :] * jnp.exp2(b[lo:lo + HG_SUB, :] - ref_b)).astype(BF16)
            kt = (k * jnp.exp2(jnp.where(row1 < lo, ref_b - b, -jnp.inf))).astype(BF16)
            att_rows.append([lax.dot_general(qt[:, hl], kt[:, hl], nt, preferred_element_type=F32)
                             for hl in head_lanes])
        mid = jnp.concatenate(
            [jnp.broadcast_to(b[lo + HG_HALF - 1:lo + HG_HALF, :], (HG_SUB, w)) for lo in range(0, c, HG_SUB)],
            axis=0)
        q2 = (q * jnp.exp2(jnp.where(second_half, b - mid, -jnp.inf))).astype(BF16)
        k2 = (k * jnp.exp2(jnp.where(second_half, -jnp.inf, mid - b))).astype(BF16)
        att_half = [lax.dot_general(q2[:, hl], k2[:, hl], nt, preferred_element_type=F32) for hl in head_lanes]

        def intra(h, hl):
            att = jnp.concatenate([rows[h] for rows in att_rows], axis=0) + jnp.where(same_sub, att_half[h], 0.0)
            return jnp.dot(att.astype(BF16), vb[:, hl], preferred_element_type=F32)

        o = o + per_head(intra)
        for lag in range(HG_HALF):
            if lag == 0:
                ks, bs, vs = k, b, v
            else:
                ks, bs, vs = shift_in_half(k, lag), shift_in_half(b, lag), shift_in_half(v, lag)
            valid = half_pos >= lag
            prod = q * ks * jnp.exp2(jnp.where(valid, b - bs, 0.0))
            a = per_head(lambda h, hl: jnp.broadcast_to(
                jnp.sum(prod[:, hl], axis=1, keepdims=True), (c, HG_DK)))
            o = o + jnp.where(valid, a, 0.0) * vs
        khat = (k * jnp.exp2(b_last - b)).astype(BF16)
        decay = jnp.exp2(b_last)
        for h, hl in enumerate(head_lanes):
            st_ref[h] = st_ref[h] * decay[:, hl] + lax.dot_general(
                vb[:, hl], khat[:, hl], (((0,), (0,)), ((), ())), preferred_element_type=F32)
        sq = o * o
        ms = per_head(lambda h, hl: jnp.broadcast_to(jnp.mean(sq[:, hl], -1, keepdims=True), (c, HG_DK)))
        o = o * lax.rsqrt(ms + 1e-6)
        o_ref[pl.ds(r0, c), :] = (o * nw_ref[...] * (g * jax.nn.sigmoid(g))).astype(o_ref.dtype)
        return carry

    lax.fori_loop(0, n_chunks, chunk, 0, unroll=4)


def _hgrn(proj, lb, norm_w, ts):
    bsz, s, _ = proj.shape
    col0 = COL_HG // HG_WIDTH

    def col_spec(n):
        return pl.BlockSpec((None, ts, HG_WIDTH), lambda b, t: (b, t, col0 + n))

    head_spec = pl.BlockSpec((1, HG_WIDTH), lambda b, t: (0, 0))
    return pl.pallas_call(
        functools.partial(_hgrn_kernel, n_chunks=ts // HG_CHUNK),
        grid=(bsz, s // ts),
        in_specs=[col_spec(0), col_spec(1), col_spec(2), col_spec(3), head_spec, head_spec],
        out_specs=pl.BlockSpec((None, ts, HG_WIDTH), lambda b, t: (b, t, 0)),
        out_shape=jax.ShapeDtypeStruct((bsz, s, HG_WIDTH), BF16),
        scratch_shapes=[pltpu.VMEM((HG_HEADS, HG_DK, HG_DK), F32)],
        compiler_params=_cparams(("parallel", "arbitrary")),
    )(proj, proj, proj, proj, lb, norm_w)


def _compress_kernel(a_ref, pos_ref, w1_ref, w2_ref, o_ref):
    half = CMP_STRIDE * NSA_DH
    a = a_ref[...].astype(F32)
    n = a.shape[0]
    a1 = (a + pos_ref[:, :half]).astype(BF16)
    a2 = (a + pos_ref[:, half:]).astype(BF16)
    y1 = jnp.dot(a1, w1_ref[:half, :], preferred_element_type=F32)
    y2 = jnp.dot(a2, w1_ref[half:, :], preferred_element_type=F32)
    hid = jax.nn.gelu(y1 + pltpu.roll(y2, n - 1, 0))
    o_ref[...] = jnp.dot(hid.astype(BF16), w2_ref[...], preferred_element_type=F32).astype(o_ref.dtype)


def _compress(a, pos, w1, w2):
    _, bsz, g, n, width = a.shape
    return pl.pallas_call(
        _compress_kernel,
        grid=(2, bsz, g),
        in_specs=[pl.BlockSpec((None, None, None, n, width), lambda c, b, h: (c, b, h, 0, 0)),
                  pl.BlockSpec((None, 1, 2 * width), lambda c, b, h: (c, 0, 0)),
                  pl.BlockSpec((None, 2 * width, CMP_HIDDEN), lambda c, b, h: (c, 0, 0)),
                  pl.BlockSpec((None, CMP_HIDDEN, NSA_DH), lambda c, b, h: (c, 0, 0))],
        out_specs=pl.BlockSpec((None, None, None, n, NSA_DH), lambda c, b, h: (c, b, h, 0, 0)),
        out_shape=jax.ShapeDtypeStruct((2, bsz, g, n, NSA_DH), BF16),
        compiler_params=_cparams(("parallel", "parallel", "parallel")),
    )(a, pos, w1, w2)


def _bucket_of(dist):
    n = jnp.maximum(dist, 0)
    exact = REL_BUCKETS // 2
    large = exact + (jnp.log(jnp.maximum(n, exact).astype(F32) / exact)
                     / math.log(REL_MAX_DIST / exact) * (REL_BUCKETS - exact)).astype(jnp.int32)
    return jnp.where(n < exact, n, jnp.minimum(large, REL_BUCKETS - 1))


def _bias_kernel(tab_ref, o_ref, *, dist_fn):
    rows, cols = o_ref.shape[2], o_ref.shape[3]
    row = lax.broadcasted_iota(jnp.int32, (rows, cols), 0)
    col = lax.broadcasted_iota(jnp.int32, (rows, cols), 1)
    bucket = _bucket_of(dist_fn(pl.program_id(0), row, col))
    for h in range(NSA_HEADS):
        tab = jnp.broadcast_to(tab_ref[h:h + 1, :] * LOG2E, (rows, LANES))
        for c0 in range(0, cols, LANES):
            o_ref[h, 0, :, c0:c0 + LANES] = jnp.take_along_axis(tab, bucket[:, c0:c0 + LANES], axis=1)


def _bias_tiles(rel_bias, n_tiles, rows, cols, dist_fn):
    tab = jnp.pad(rel_bias.T, ((0, 0), (0, LANES - REL_BUCKETS)))
    return pl.pallas_call(
        functools.partial(_bias_kernel, dist_fn=dist_fn),
        grid=(n_tiles,),
        in_specs=[pl.BlockSpec((NSA_HEADS, LANES), lambda t: (0, 0))],
        out_specs=pl.BlockSpec((NSA_HEADS, 1, rows, cols), lambda t: (0, t, 0, 0)),
        out_shape=jax.ShapeDtypeStruct((NSA_HEADS, n_tiles, rows, cols), F32),
        compiler_params=_cparams(("parallel",)),
    )(tab)


def _stack_heads(qb):
    tq = qb.shape[0]
    half = lax.broadcasted_iota(jnp.int32, (tq, LANES), 1) // NSA_DH
    return jnp.concatenate(
        [jnp.where(half == hp % 2, qb[:, LANES * (hp // 2):LANES * (hp // 2 + 1)], jnp.zeros((tq, LANES), qb.dtype))
         for hp in range(NSA_HPG)], axis=0)


def _cmp_kernel(q_ref, k_ref, v_ref, bias_ref, c2s_ref, gl_ref, o_ref, qa_ref, *, tq):
    p = NSA_HPG
    t0 = pl.program_id(1) * tq
    bsz, ncp = k_ref.shape[0], k_ref.shape[1]
    bias = bias_ref[:, 0]
    tpos = t0 + lax.broadcasted_iota(jnp.int32, (tq, ncp), 0)
    cend = lax.broadcasted_iota(jnp.int32, (tq, ncp), 1) * CMP_STRIDE + (CMP_LEN - 1)
    mask = (tpos >= cend)[None]
    lane = lax.broadcasted_iota(jnp.int32, (tq, LANES), 1)
    cur = (t0 + lax.broadcasted_iota(jnp.int32, (tq, LANES), 0)) // SEL_BLOCK
    forced = (lane == 0) | (lane == cur) | (lane == cur - 1)
    visible = lane <= cur
    batch = range(bsz)
    nt = (((1,), (1,)), ((), ()))
    qs = [_stack_heads(q_ref[b]) for b in batch]
    lgs = [lax.dot_general(qs[b], k_ref[b], nt, preferred_element_type=F32) for b in batch]
    lgs = [jnp.where(mask, lg.reshape(p, tq, ncp) + bias, NEG_BIG) for lg in lgs]
    es = [jnp.where(mask, jnp.exp2(lg - jnp.max(lg, -1, keepdims=True)), 0.0) for lg in lgs]
    pcs = [e / jnp.maximum(jnp.sum(e, -1, keepdims=True), 1e-30) for e in es]
    os = [jnp.dot(pcs[b].reshape(p * tq, ncp).astype(BF16), v_ref[b], preferred_element_type=F32) for b in batch]
    imps = [sum(jnp.dot(part, c2s_ref[...], preferred_element_type=F32) for part in _split3(jnp.sum(pc, axis=0)))
            for pc in pcs]
    for b in batch:
        o_ref[b] = _heads_to_lanes(os[b], jax.nn.sigmoid(gl_ref[b, :, :, 0:1].astype(F32)), tq, False)
    scores = [jnp.where(forced, FORCE_SCORE, jnp.where(visible, imp, NEG_BIG)) for imp in imps]
    members = [jnp.zeros((tq, LANES), F32)] * bsz
    for _ in range(N_SEL):
        hits = [lane == jnp.argmax(score, axis=-1, keepdims=True) for score in scores]
        members = [jnp.where(hit, 1.0, member) for hit, member in zip(hits, members)]
        scores = [jnp.where(hit, -jnp.inf, score) for hit, score in zip(hits, scores)]
    for b in batch:
        pen = jnp.where((members[b] > 0.5) & visible, 0.0, NEG_BIG).astype(qa_ref.dtype)
        qa_ref[b] = jnp.concatenate([jnp.concatenate([pen] * p, axis=0), qs[b]], axis=1).reshape(p, tq, 2 * LANES)


def _heads_to_lanes(x, gates, tq, normalise):
    low = lax.broadcasted_iota(jnp.int32, (tq, LANES), 1) < NSA_DH
    if normalise:
        swapped = pltpu.roll(x, NSA_DH, 1)
    pieces = []
    for pair in range(NSA_HPG // 2):
        ev = slice(2 * pair * tq, (2 * pair + 1) * tq)
        od = slice((2 * pair + 1) * tq, (2 * pair + 2) * tq)
        if normalise:
            even, odd = x[ev] / swapped[ev], swapped[od] / x[od]
        else:
            even, odd = x[ev], x[od]
        pieces.append(jnp.where(low, even * gates[2 * pair], odd * gates[2 * pair + 1]))
    return jnp.concatenate(pieces, axis=1)


def _out_spec(bsz, tq):
    return pl.BlockSpec((bsz, tq, NSA_HPG * NSA_DH), lambda h, i: (0, i, h))


def _q_spec(bsz, tq):
    width = NSA_HPG * NSA_DH
    return pl.BlockSpec((bsz, tq, width), lambda h, i: (0, i, COL_NQ // width + h))


def _cmp_branch(proj, kc2, vc, bias_c, c2s, gl5, tq):
    bsz, s, _ = proj.shape
    g, p, dh = NSA_GROUPS, NSA_HPG, NSA_DH
    ncp = kc2.shape[2]
    qmap = lambda h, i: (0, h, 0, i, 0)
    return pl.pallas_call(
        functools.partial(_cmp_kernel, tq=tq),
        grid=(g, s // tq),
        in_specs=[_q_spec(bsz, tq),
                  pl.BlockSpec((bsz, None, ncp, LANES), lambda h, i: (0, h, 0, 0)),
                  pl.BlockSpec((bsz, None, ncp, LANES), lambda h, i: (0, h, 0, 0)),
                  pl.BlockSpec((p, 1, tq, ncp), lambda h, i: (h, i, 0, 0)),
                  pl.BlockSpec((ncp, LANES), lambda h, i: (0, 0)),
                  pl.BlockSpec((bsz, None, p, tq, 3), qmap)],
        out_specs=[_out_spec(bsz, tq),
                   pl.BlockSpec((bsz, None, p, tq, 2 * LANES), qmap)],
        out_shape=[jax.ShapeDtypeStruct((bsz, s, NSA_WIDTH), F32),
                   jax.ShapeDtypeStruct((bsz, g, p, s, 2 * LANES), BF16)],
        compiler_params=_cparams(("parallel", "parallel")),
    )(proj, kc2, vc, bias_c, c2s, gl5)


def _win_kernel(*refs, tq, n_back):
    nk = n_back + 1
    q_ref, k_refs, v_refs, b_refs = refs[0], refs[1:1 + nk], refs[1 + nk:1 + 2 * nk], refs[1 + 2 * nk:1 + 3 * nk]
    gl_ref, acc_ref, o_ref = refs[1 + 3 * nk:]
    p = NSA_HPG
    bsz = q_ref.shape[0]
    t0 = pl.program_id(1) * tq
    row = lax.broadcasted_iota(jnp.int32, (tq, tq), 0)
    col = lax.broadcasted_iota(jnp.int32, (tq, tq), 1)
    masks = [((row - col + d * tq >= 0) & (row - col + d * tq < WINDOW) & (t0 - d * tq >= 0))[None]
             for d in range(nk)]
    biases = [b_ref[:, 0] for b_ref in b_refs]
    nt = (((1,), (1,)), ((), ()))
    batch = range(bsz)
    qs = [_stack_heads(q_ref[b]) for b in batch]
    lgs = [[lax.dot_general(qs[b], k_ref[b], nt, preferred_element_type=F32).reshape(p, tq, tq)
            for k_ref in k_refs] for b in batch]
    lgs = [[jnp.where(mask, lg + bias, NEG_BIG) for lg, mask, bias in zip(lgs[b], masks, biases)] for b in batch]
    ms = [functools.reduce(jnp.maximum, [jnp.max(lg, -1, keepdims=True) for lg in lgs[b]]) for b in batch]
    es = [[jnp.exp2(lg - ms[b]).reshape(p * tq, tq).astype(BF16) for lg in lgs[b]] for b in batch]
    os = [sum(jnp.dot(e, v_ref[b], preferred_element_type=F32) for e, v_ref in zip(es[b], v_refs)) for b in batch]
    for b in batch:
        o_ref[b] = acc_ref[b] + _heads_to_lanes(os[b], jax.nn.sigmoid(gl_ref[b, :, :, 2:3].astype(F32)), tq, True)


def _win_branch(proj, kvx, bias_t, gl5, acc, tq):
    bsz, s, _ = acc.shape
    g, p, dh = NSA_GROUPS, NSA_HPG, NSA_DH
    n_back = WINDOW // tq
    back_k = [lambda h, i, d=d: (0, h, jnp.maximum(i - d, 0), 3) for d in range(n_back + 1)]
    back_v = [lambda h, i, d=d: (0, h, jnp.maximum(i - d, 0), 4) for d in range(n_back + 1)]
    qmap = lambda h, i: (0, h, 0, i, 0)
    in_specs = ([_q_spec(bsz, tq)]
                + [pl.BlockSpec((bsz, None, tq, LANES), m) for m in back_k]
                + [pl.BlockSpec((bsz, None, tq, LANES), m) for m in back_v]
                + [pl.BlockSpec((p, 1, tq, tq), lambda h, i, d=d: (h, d, 0, 0)) for d in range(n_back + 1)]
                + [pl.BlockSpec((bsz, None, p, tq, 3), qmap), _out_spec(bsz, tq)])
    return pl.pallas_call(
        functools.partial(_win_kernel, tq=tq, n_back=n_back),
        grid=(g, s // tq),
        in_specs=in_specs,
        out_specs=_out_spec(bsz, tq),
        out_shape=jax.ShapeDtypeStruct((bsz, s, NSA_WIDTH), F32),
        input_output_aliases={len(in_specs) - 1: 0},
        compiler_params=_cparams(("parallel", "parallel")),
    )(proj, *([kvx] * (2 * n_back + 2)), *([bias_t] * (n_back + 1)), gl5, acc)


def _sel_kernel(it_ref, jt_ref, q_ref, k_ref, v_ref, bias_ref, gl_ref, acc_ref, o_ref, m_sc, a_sc,
                *, tq, tk, bsz):
    p = NSA_HPG
    nq = p * tq
    step = pl.program_id(1)
    i = it_ref[step]
    j = jt_ref[step]
    t0 = i * tq
    s0 = j * tk
    last = s0 + tk > t0

    @pl.when(j == 0)
    def _():
        m_sc[...] = jnp.full_like(m_sc, NEG_BIG)
        a_sc[...] = jnp.zeros_like(a_sc)

    def accumulate(on_diagonal):
        nt = (((1,), (1,)), ((), ()))
        bias = bias_ref[:, 0].reshape(nq, tk)
        if on_diagonal:
            causal = (t0 + lax.broadcasted_iota(jnp.int32, (tq, tk), 0)
                      >= s0 + lax.broadcasted_iota(jnp.int32, (tq, tk), 1))
            causal = jnp.concatenate([causal] * p, axis=0)
        for b in range(bsz):
            lg = lax.dot_general(q_ref[b].reshape(nq, 2 * LANES), k_ref[b], nt, preferred_element_type=F32) + bias
            if on_diagonal:
                lg = jnp.where(causal, lg, NEG_BIG)
            m_old = m_sc[b]
            m_new = jnp.maximum(m_old, jnp.max(lg, -1, keepdims=True))
            e = jnp.exp2(lg - jnp.concatenate([m_new] * (tk // LANES), axis=1))
            a_sc[b] = (jnp.exp2(m_old - m_new) * a_sc[b]
                       + jnp.dot(e.astype(BF16), v_ref[b], preferred_element_type=F32))
            m_sc[b] = m_new

    @pl.when(jnp.logical_not(last))
    def _():
        accumulate(False)

    @pl.when(last)
    def _():
        accumulate(True)
        gate = jax.nn.sigmoid(gl_ref[:, :, :, 1:2].astype(F32))
        for b in range(bsz):
            o_ref[b] = (acc_ref[b] + _heads_to_lanes(a_sc[b], gate[b], tq, True)).astype(o_ref.dtype)


def _sel_branch(q_aug, kvx, bias_t, gl5, acc, tq, tk):
    bsz, g, p, s, wq = q_aug.shape
    r = tk // tq
    pairs = [(i, j) for i in range(s // tq) for j in range(i // r + 1)]
    it = jnp.asarray([ij[0] for ij in pairs], jnp.int32)
    jt = jnp.asarray([ij[1] for ij in pairs], jnp.int32)
    nd = bias_t.shape[1] - 1
    qmap = lambda h, t, it, jt: (0, h, 0, it[t], 0)
    kmap = lambda h, t, it, jt: (0, h, jt[t], 0)
    omap = lambda h, t, it, jt: (0, it[t], h)
    grid_spec = pltpu.PrefetchScalarGridSpec(
        num_scalar_prefetch=2,
        grid=(g, len(pairs)),
        in_specs=[pl.BlockSpec((bsz, None, p, tq, wq), qmap),
                  pl.BlockSpec((bsz, None, tk, wq), kmap),
                  pl.BlockSpec((bsz, None, tk, LANES), lambda h, t, it, jt: (0, h, jt[t], 2)),
                  pl.BlockSpec((p, 1, tq, tk),
                               lambda h, t, it, jt: (h, jnp.minimum(it[t] - r * jt[t], nd), 0, 0)),
                  pl.BlockSpec((bsz, None, p, tq, 3), qmap),
                  pl.BlockSpec((bsz, tq, p * NSA_DH), omap)],
        out_specs=pl.BlockSpec((bsz, tq, p * NSA_DH), omap),
        scratch_shapes=[pltpu.VMEM((bsz, p * tq, LANES), F32),
                        pltpu.VMEM((bsz, p * tq, LANES), F32)],
    )
    return pl.pallas_call(
        functools.partial(_sel_kernel, tq=tq, tk=tk, bsz=bsz),
        grid_spec=grid_spec,
        out_shape=jax.ShapeDtypeStruct((bsz, s, NSA_WIDTH), BF16),
        compiler_params=_cparams(("parallel", "arbitrary")),
    )(it, jt, q_aug, kvx, kvx, bias_t, gl5, acc)


def _merge_kernel(x_ref, oa_ref, ob_ref, ga_ref, gb_ref, wa_ref, wb_ref, wo_ref, g_ref, b_ref, o_ref):
    ya = jnp.dot(oa_ref[...].astype(BF16), wa_ref[...], preferred_element_type=F32)
    yb = jnp.dot(ob_ref[...].astype(BF16), wb_ref[...], preferred_element_type=F32)
    y = jax.nn.sigmoid(ga_ref[...].astype(F32)) * ya + jax.nn.sigmoid(gb_ref[...].astype(F32)) * yb
    mix = jnp.dot(y.astype(BF16), wo_ref[...], preferred_element_type=F32)
    o_ref[...] = _layer_norm(DN_ALPHA * x_ref[...] + mix, g_ref[...], b_ref[...])


def _merge(x, o_a, o_b, proj, wa, wb, wo, g, b, tm):
    t, d = x.shape
    nga = COL_GA // d
    row = lambda i: (i, 0)
    const = lambda i: (0, 0)
    return pl.pallas_call(
        _merge_kernel,
        grid=(t // tm,),
        in_specs=[pl.BlockSpec((tm, d), row),
                  pl.BlockSpec((tm, HG_WIDTH), row),
                  pl.BlockSpec((tm, NSA_WIDTH), row),
                  pl.BlockSpec((tm, d), lambda i: (i, nga)),
                  pl.BlockSpec((tm, d), lambda i: (i, nga + 1)),
                  pl.BlockSpec((HG_WIDTH, d), const),
                  pl.BlockSpec((NSA_WIDTH, d), const),
                  pl.BlockSpec((d, d), const),
                  pl.BlockSpec((1, d), const),
                  pl.BlockSpec((1, d), const)],
        out_specs=pl.BlockSpec((tm, d), row),
        out_shape=jax.ShapeDtypeStruct((t, d), F32),
        compiler_params=_cparams(("parallel",)),
    )(x, o_a, o_b, proj, proj, wa, wb, wo, g, b)


def _swiglu_step(xb, wg_ref, wu_ref, wd_ref):
    hg = jnp.dot(xb, wg_ref[...], preferred_element_type=F32)
    hu = jnp.dot(xb, wu_ref[...], preferred_element_type=F32)
    h = (hg * jax.nn.sigmoid(hg)) * hu
    return jnp.dot(h.astype(BF16), wd_ref[...], preferred_element_type=F32)


def _ffn_kernel(x_ref, wg_ref, wu_ref, wd_ref, g_ref, b_ref, o_ref):
    x = x_ref[...]
    f = _swiglu_step(x.astype(BF16), wg_ref, wu_ref, wd_ref)
    o_ref[...] = _layer_norm(DN_ALPHA * x + f, g_ref[...], b_ref[...])


def _ffn(x, wg, wu, wd, g, b, tm):
    t, d = x.shape
    f = wg.shape[1]
    const = lambda i: (0, 0)
    once = pl.Buffered(1)
    return pl.pallas_call(
        _ffn_kernel,
        grid=(t // tm,),
        in_specs=[pl.BlockSpec((tm, d), lambda i: (i, 0)),
                  pl.BlockSpec((d, f), const, pipeline_mode=once),
                  pl.BlockSpec((d, f), const, pipeline_mode=once),
                  pl.BlockSpec((f, d), const, pipeline_mode=once),
                  pl.BlockSpec((1, d), const),
                  pl.BlockSpec((1, d), const)],
        out_specs=pl.BlockSpec((tm, d), lambda i: (i, 0)),
        out_shape=jax.ShapeDtypeStruct((t, d), F32),
        compiler_params=_cparams(("parallel",)),
    )(x, wg, wu, wd, g, b)


def _router_kernel(x_ref, w_ref, o_ref):
    xs, ws = _split3(x_ref[...]), _split3(w_ref[...])
    logits = sum(jnp.dot(xs[i], ws[j], preferred_element_type=F32) for i, j in ((0, 0), (0, 1), (1, 0)))
    lane =lax.broadcasted_iota(jnp.int32, logits.shape, 1).astype(F32)
    logits = jnp.where(lane < N_EXPERTS, logits, -jnp.inf)
    v1 = jnp.max(logits, -1, keepdims=True)
    e1 = jnp.min(jnp.where(logits == v1, lane, float(LANES)), -1, keepdims=True)
    rest = jnp.where(lane == e1, -jnp.inf, logits)
    v2 = jnp.max(rest, -1, keepdims=True)
    e2 = jnp.min(jnp.where(rest == v2, lane, float(LANES)), -1, keepdims=True)
    x2 = jnp.exp(v2 - v1)
    den = 1.0 + x2
    o_ref[...] = jnp.where(lane == 0, e1, jnp.where(lane == 1, e2, jnp.where(
        lane == 2, 1.0 / den, jnp.where(lane == 3, x2 / den, 0.0))))


def _router(x, w, tm):
    t, d = x.shape
    return pl.pallas_call(
        _router_kernel,
        grid=(t // tm,),
        in_specs=[pl.BlockSpec((tm, d), lambda i: (i, 0)), pl.BlockSpec((d, LANES), lambda i: (0, 0))],
        out_specs=pl.BlockSpec((tm, LANES), lambda i: (i, 0)),
        out_shape=jax.ShapeDtypeStruct((t, LANES), F32),
        compiler_params=_cparams(("parallel",)),
    )(x, w)


def _expert_kernel(be_ref, x_ref, wg_ref, wu_ref, wd_ref, o_ref, acc_ref):
    j = pl.program_id(1)

    @pl.when(j == 0)
    def _():
        acc_ref[...] = jnp.zeros_like(acc_ref)

    xb = x_ref[...].astype(BF16)
    hg = jnp.dot(xb, wg_ref[...].astype(BF16), preferred_element_type=F32)
    hu = jnp.dot(xb, wu_ref[...].astype(BF16), preferred_element_type=F32)
    h = (hg * jax.nn.sigmoid(hg)) * hu
    acc_ref[...] += jnp.dot(h.astype(BF16), wd_ref[...].astype(BF16), preferred_element_type=F32)

    @pl.when(j == pl.num_programs(1) - 1)
    def _():
        o_ref[...] = acc_ref[...].astype(o_ref.dtype)


def _experts(blk_e, xs, wg, wu, wd, tf):
    rows, d = xs.shape
    f = wg.shape[2]
    tm = MOE_ROW_BLOCK
    grid_spec = pltpu.PrefetchScalarGridSpec(
        num_scalar_prefetch=1,
        grid=(rows // tm, f // tf),
        in_specs=[pl.BlockSpec((tm, d), lambda i, j, be: (i, 0)),
                  pl.BlockSpec((None, d, tf), lambda i, j, be: (be[i], 0, j)),
                  pl.BlockSpec((None, d, tf), lambda i, j, be: (be[i], 0, j)),
                  pl.BlockSpec((None, tf, d), lambda i, j, be: (be[i], j, 0))],
        out_specs=pl.BlockSpec((tm, d), lambda i, j, be: (i, 0)),
        scratch_shapes=[pltpu.VMEM((tm, d), F32)],
    )
    return pl.pallas_call(
        _expert_kernel,
        grid_spec=grid_spec,
        out_shape=jax.ShapeDtypeStruct((rows, d), F32),
        compiler_params=_cparams(("parallel", "arbitrary")),
    )(blk_e, xs, wg, wu, wd)


def _combine_kernel(x_ref, y1_ref, y2_ref, gt_ref, g_ref, b_ref, o_ref):
    f = y1_ref[...].astype(F32) * gt_ref[:, 2:3] + y2_ref[...].astype(F32) * gt_ref[:, 3:4]
    o_ref[...] = _layer_norm(DN_ALPHA * x_ref[...] + f, g_ref[...], b_ref[...])


def _combine(x, y1, y2, route, g, b, tm):
    t, d = x.shape
    row = lambda i: (i, 0)
    const = lambda i: (0, 0)
    return pl.pallas_call(
        _combine_kernel,
        grid=(t // tm,),
        in_specs=[pl.BlockSpec((tm, d), row), pl.BlockSpec((tm, d), row), pl.BlockSpec((tm, d), row),
                  pl.BlockSpec((tm, LANES), row), pl.BlockSpec((1, d), const), pl.BlockSpec((1, d), const)],
        out_specs=pl.BlockSpec((tm, d), row),
        out_shape=jax.ShapeDtypeStruct((t, d), F32),
        compiler_params=_cparams(("parallel",)),
    )(x, y1, y2, route, g, b)


def _moe(x, w_router, wg, wu, wd, g, b):
    t, d = x.shape
    tk_ = t * TOP_K
    route = _router(x, jnp.pad(w_router, ((0, 0), (0, LANES - N_EXPERTS))), 512)
    flat_e = route[:, :TOP_K].astype(jnp.int32).reshape(-1)
    onehot = (flat_e[:, None] == jnp.arange(N_EXPERTS)[None, :]).astype(jnp.int32)
    csum = jnp.cumsum(onehot, axis=0)
    counts = csum[-1]
    rank = jnp.sum(onehot * csum, axis=1) - 1
    padded = (counts + MOE_ROW_BLOCK - 1) // MOE_ROW_BLOCK * MOE_ROW_BLOCK
    pend = jnp.cumsum(padded)
    dest = (pend - padded)[flat_e] + rank
    n_blocks = -(-(tk_ + N_EXPERTS * (MOE_ROW_BLOCK - 1)) // MOE_ROW_BLOCK)
    n_rows = n_blocks * MOE_ROW_BLOCK
    row_tok = (jnp.arange(n_rows, dtype=jnp.int32) % t).at[dest].set(jnp.arange(tk_, dtype=jnp.int32) // TOP_K)
    blk_e = jnp.minimum(jnp.searchsorted(pend, jnp.arange(n_blocks) * MOE_ROW_BLOCK, side='right'),
                        N_EXPERTS - 1).astype(jnp.int32)
    xs = x[row_tok]
    ys = _experts(blk_e, xs, wg, wu, wd, 512)
    dest2 = dest.reshape(t, TOP_K)
    return _combine(x, ys[dest2[:, 0]], ys[dest2[:, 1]], route, g, b, 512)


def _pack_w_in(w_in):
    offs = np.concatenate([[0], np.cumsum(IN_SIZES)])
    seg = [w_in[:, offs[j]:offs[j + 1]] for j in range(len(IN_SIZES))]
    seg[4] = seg[4] * (NSA_DH ** -0.5 * LOG2E)
    seg[11] = jnp.pad(seg[11], ((0, 0), (0, LANES - 3 * NSA_HEADS)))
    seg = seg[12:14] + seg[0:12]
    packed = jnp.concatenate(seg + [jnp.zeros((w_in.shape[0], PROJ_WP - PROJ_W), w_in.dtype)], axis=1)
    return packed.astype(BF16)


def _cmp_to_sel(n_cmp_pad, n_cmp):
    cs = np.arange(n_cmp_pad)[:, None] * CMP_STRIDE
    ss = np.arange(LANES)[None, :] * SEL_BLOCK
    overlap = np.clip(np.minimum(cs + CMP_LEN, ss + SEL_BLOCK) - np.maximum(cs, ss), 0, None) / CMP_LEN
    overlap[n_cmp:] = 0.0
    return jnp.asarray(overlap, BF16)


def _token_mixer(x, w_in_p, lb, hg_norm_w, cmp_pos, cmp_w1, cmp_w2, bias_c, bias_t, wa, wb, wo, ln_g, ln_b):
    bsz, s, d = x.shape
    g, p, dh = NSA_GROUPS, NSA_HPG, NSA_DH
    xf = x.reshape(bsz * s, d)
    proj = _project(xf, w_in_p, 1024, PROJ_WP // 2).reshape(bsz, s, PROJ_WP)
    o_a = _hgrn(proj, lb, hg_norm_w, 1024)

    def heads(c0, width):
        return proj[:, :, c0:c0 + width]

    kv = heads(COL_KV, 6 * KV_WIDTH).reshape(bsz, s, 6, g, dh)
    n16 = s // CMP_STRIDE
    a = kv[:, :, 0:2].reshape(bsz, n16, CMP_STRIDE, 2, g, dh).transpose(3, 0, 4, 1, 2, 5)
    a = a.reshape(2, bsz, g, n16, CMP_STRIDE * dh)
    kvc = _compress(a, cmp_pos.reshape(2, 1, CMP_LEN * dh), cmp_w1.astype(BF16), cmp_w2.astype(BF16))
    gl5 = heads(COL_NG, 3 * NSA_HEADS).reshape(bsz, s, g, p, 3).transpose(0, 2, 3, 1, 4)
    n_cmp = (s - CMP_LEN) // CMP_STRIDE + 1
    acc, q_aug = _cmp_branch(proj, jnp.concatenate([kvc[0], kvc[0]], axis=-1),
                             jnp.concatenate([kvc[1], kvc[1]], axis=-1), bias_c,
                             _cmp_to_sel(n16, n_cmp), gl5, ATT_TQ)
    def per_group(j):
        return kv[:, :, j].astype(BF16).transpose(0, 2, 1, 3)

    ones = jnp.ones((bsz, g, s, LANES - dh), BF16)
    block_of_key = (np.arange(s)[:, None] // SEL_BLOCK == np.arange(LANES)[None, :])
    kvx = jnp.concatenate([jnp.broadcast_to(jnp.asarray(block_of_key, BF16), (bsz, g, s, LANES)),
                           per_group(2), per_group(2), per_group(3), ones,
                           per_group(4), per_group(4), per_group(5), ones], axis=-1)
    acc = _win_branch(proj, kvx, bias_t, gl5, acc, ATT_TQ)
    o_b = _sel_branch(q_aug, kvx, bias_t, gl5, acc, ATT_TQ, ATT_TK)
    o_b = o_b.reshape(bsz * s, NSA_WIDTH)
    return _merge(xf, o_a.reshape(bsz * s, HG_WIDTH), o_b, proj.reshape(bsz * s, PROJ_WP),
                  wa, wb, wo, ln_g, ln_b, 512)


def kernel(x, w_in, hg_lb_logits, hg_norm_w, cmp_pos, cmp_w1, cmp_w2, rel_bias, w_branch_a, w_branch_b,
           w_out, ln1_g, ln1_b, ln2_g, ln2_b, ffn_w_gate, ffn_w_up, ffn_w_down, moe_router, moe_w_gate,
           moe_w_up, moe_w_down):
    bsz, s, d = x.shape
    depth = w_in.shape[0]
    p_lb = jax.nn.softmax(hg_lb_logits.astype(F32), axis=0)
    lbs = jnp.cumsum(p_lb, axis=0) - p_lb[0]
    n16 = s // CMP_STRIDE
    bias_c = _bias_tiles(rel_bias, s // ATT_TQ, ATT_TQ, n16,
                         lambda t, r, c: t * ATT_TQ + r - (c * CMP_STRIDE + CMP_LEN - 1))
    bias_t = _bias_tiles(rel_bias, BIAS_ND + 1, ATT_TQ, ATT_TK, lambda t, r, c: t * ATT_TQ + r - c)
    f_pad = -(-D_FF // LANES) * LANES - D_FF
    xf = x.reshape(bsz * s, d)
    for l in range(depth):
        xf = _token_mixer(xf.reshape(bsz, s, d), _pack_w_in(w_in[l]), lbs[l][None], hg_norm_w[l][None],
                          cmp_pos[l], cmp_w1[l], cmp_w2[l], bias_c, bias_t,
                          w_branch_a[l].astype(BF16), w_branch_b[l].astype(BF16), w_out[l].astype(BF16),
                          ln1_g[l][None], ln1_b[l][None])
        if l % 2 == 0:
            wg = jnp.pad(ffn_w_gate[l // 2], ((0, 0), (0, f_pad))).astype(BF16)
            wu = jnp.pad(ffn_w_up[l // 2], ((0, 0), (0, f_pad))).astype(BF16)
            wd = jnp.pad(ffn_w_down[l // 2], ((0, f_pad), (0, 0))).astype(BF16)
            xf = _ffn(xf, wg, wu, wd, ln2_g[l][None], ln2_b[l][None], 512)
        else:
            xf = _moe(xf, moe_router[l // 2], moe_w_gate[l // 2], moe_w_up[l // 2], moe_w_down[l // 2],
                      ln2_g[l][None], ln2_b[l][None])
    return xf.reshape(bsz, s, d)
```

```python
import functools
import math

import jax
import jax.numpy as jnp
import numpy as np
from jax import lax
from jax.experimental import pallas as pl
from jax.experimental.pallas import tpu as pltpu

F32 = jnp.float32
BF16 = jnp.bfloat16

D_MODEL = 1024
DEPTH = 2
HG_HEADS = 4
HG_DK = 128
HG_WIDTH = HG_HEADS * HG_DK
HG_CHUNK = 64
HG_SUB = 16
HG_HALF = HG_SUB // 2
NSA_HEADS = 8
NSA_GROUPS = 2
NSA_HPG = NSA_HEADS // NSA_GROUPS
NSA_DH = 64
NSA_WIDTH = NSA_HEADS * NSA_DH
KV_WIDTH = NSA_GROUPS * NSA_DH
CMP_LEN = 32
CMP_STRIDE = 16
CMP_HIDDEN = 2 * NSA_DH
SEL_BLOCK = 64
N_SEL = 16
WINDOW = 512
FORCE_SCORE = 1e9
NEG_BIG = -1e30
REL_BUCKETS = 32
REL_MAX_DIST = 2048
D_FF = 2752
N_EXPERTS = 8
TOP_K = 2
D_FF_EXPERT = 3584
MOE_ROW_BLOCK = 1024
DN_ALPHA = (2 * DEPTH) ** 0.25
LN_EPS = 1e-5
IN_SIZES = (HG_WIDTH, HG_WIDTH, HG_WIDTH, HG_WIDTH, NSA_WIDTH,
            KV_WIDTH, KV_WIDTH, KV_WIDTH, KV_WIDTH, KV_WIDTH, KV_WIDTH,
            3 * NSA_HEADS, D_MODEL, D_MODEL)

LANES = 128
LOG2E = 1.0 / math.log(2.0)
COL_GA = 0
COL_HG = 2 * D_MODEL
COL_NQ = COL_HG + 4 * HG_WIDTH
COL_KV = COL_NQ + NSA_WIDTH
COL_NG = COL_KV + 6 * KV_WIDTH
PROJ_W = COL_NG + LANES
PROJ_TN = 512
PROJ_WP = -(-PROJ_W // PROJ_TN) * PROJ_TN

ATT_TQ = 256
ATT_TK = 512
BIAS_ND = -(-(REL_MAX_DIST + ATT_TK) // ATT_TQ)
VMEM_LIMIT = 48 * 1024 * 1024


def _cparams(sem):
    return pltpu.CompilerParams(dimension_semantics=sem, vmem_limit_bytes=VMEM_LIMIT)


def _proj_kernel(x_ref, w_ref, o_ref, xb_ref):
    @pl.when(pl.program_id(1) == 0)
    def _():
        xb_ref[...] = x_ref[...].astype(BF16)

    o_ref[...] = jnp.dot(xb_ref[...], w_ref[...], preferred_element_type=F32).astype(o_ref.dtype)


def _project(x, w, tm, tn):
    m, k = x.shape
    n = w.shape[1]
    return pl.pallas_call(
        _proj_kernel,
        grid=(m // tm, n // tn),
        in_specs=[pl.BlockSpec((tm, k), lambda i, j: (i, 0)),
                  pl.BlockSpec((k, tn), lambda i, j: (0, j))],
        out_specs=pl.BlockSpec((tm, tn), lambda i, j: (i, j)),
        out_shape=jax.ShapeDtypeStruct((m, n), BF16),
        scratch_shapes=[pltpu.VMEM((tm, k), BF16)],
        compiler_params=_cparams(("parallel", "arbitrary")),
    )(x, w)


def _layer_norm(y, g, b):
    mu = jnp.mean(y, -1, keepdims=True)
    yc = y - mu
    var = jnp.mean(yc * yc, -1, keepdims=True)
    return yc * lax.rsqrt(var + LN_EPS) * g + b


def _split3(x):
    parts = []
    for _ in range(3):
        part = x.astype(BF16)
        parts.append(part)
        x = x - part.astype(F32)
    return parts


def _cumsum_rows(tril, x):
    return sum(jnp.dot(tril, part, preferred_element_type=F32) for part in _split3(x))


def _hgrn_kernel(q_ref, f_ref, i_ref, g_ref, lb_ref, nw_ref, o_ref, st_ref, *, n_chunks):
    @pl.when(pl.program_id(1) == 0)
    def _():
        st_ref[...] = jnp.zeros_like(st_ref)

    c = HG_CHUNK
    nt = (((1,), (1,)), ((), ()))
    row = lax.broadcasted_iota(jnp.int32, (c, c), 0)
    col = lax.broadcasted_iota(jnp.int32, (c, c), 1)
    tril = (row >= col).astype(BF16)
    row1 = lax.broadcasted_iota(jnp.int32, (c, 1), 0)
    half_pos = row1 % HG_HALF
    second_half = row1 % HG_SUB >= HG_HALF
    same_sub = row // HG_SUB == col // HG_SUB

    w = HG_HEADS * HG_DK
    head_lanes = [slice(h * HG_DK, (h + 1) * HG_DK) for h in range(HG_HEADS)]

    def per_head(fn):
        return jnp.concatenate([fn(h, head_lanes[h]) for h in range(HG_HEADS)], axis=1)

    def shift_in_half(x, lag):
        return pltpu.roll(x.reshape(c // HG_HALF, HG_HALF, w), lag, 1).reshape(c, w)

    def chunk(ci, carry):
        r0 = pl.multiple_of(ci * c, c)
        q = q_ref[pl.ds(r0, c), :].astype(F32)
        z = f_ref[pl.ds(r0, c), :].astype(F32)
        v = i_ref[pl.ds(r0, c), :].astype(F32)
        g = g_ref[pl.ds(r0, c), :].astype(F32)
        k = (1.0 - lb_ref[...]) * jax.nn.sigmoid(-z)
        b = _cumsum_rows(tril, jnp.log1p(-k)) * LOG2E
        b_last = b[c - 1:c, :]
        vb = v.astype(BF16)
        qe = (q * jnp.exp2(b)).astype(BF16)
        o = per_head(lambda h, hl: lax.dot_general(qe[:, hl], st_ref[h].astype(BF16), nt,
                                                  preferred_element_type=F32))
        att_rows = [[jnp.zeros((HG_SUB, c), F32)] * HG_HEADS]
        for sb in range(1, c // HG_SUB):
            lo = sb * HG_SUB
            ref_b = b[lo - 1:lo, :]
            qt = (q[lo:lo + HG_SUB, :] * jnp.exp2(b[lo:lo + HG_SUB, :] - ref_b)).astype(BF16)
            kt = (k * jnp.exp2(jnp.where(row1 < lo, ref_b - b, -jnp.inf))).astype(BF16)
            att_rows.append([lax.dot_general(qt[:, hl], kt[:, hl], nt, preferred_element_type=F32)
                             for hl in head_lanes])
        mid = jnp.concatenate(
            [jnp.broadcast_to(b[lo + HG_HALF - 1:lo + HG_HALF, :], (HG_SUB, w)) for lo in range(0, c, HG_SUB)],
            axis=0)
        q2 = (q * jnp.exp2(jnp.where(second_half, b - mid, -jnp.inf))).astype(BF16)
        k2 = (k * jnp.exp2(jnp.where(second_half, -jnp.inf, mid - b))).astype(BF16)
        att_half = [lax.dot_general(q2[:, hl], k2[:, hl], nt, preferred_element_type=F32) for hl in head_lanes]

        def intra(h, hl):
            att = jnp.concatenate([rows[h] for rows in att_rows], axis=0) + jnp.where(same_sub, att_half[h], 0.0)
            return jnp.dot(att.astype(BF16), vb[:, hl], preferred_element_type=F32)

        o = o + per_head(intra)
        for lag in range(HG_HALF):
            if lag == 0:
                ks, bs, vs = k, b, v
            else:
                ks, bs, vs = shift_in_half(k, lag), shift_in_half(b, lag), shift_in_half(v, lag)
            valid = half_pos >= lag
            prod = q * ks * jnp.exp2(b - bs)
            a = per_head(lambda h, hl: jnp.broadcast_to(
                jnp.sum(prod[:, hl], axis=1, keepdims=True), (c, HG_DK)))
            o = o + jnp.where(valid, a, 0.0) * vs
        khat = (k * jnp.exp2(b_last - b)).astype(BF16)
        decay = jnp.exp2(b_last)
        for h, hl in enumerate(head_lanes):
            st_ref[h] = st_ref[h] * decay[:, hl] + lax.dot_general(
                vb[:, hl], khat[:, hl], (((0,), (0,)), ((), ())), preferred_element_type=F32)
        sq = o * o
        ms = per_head(lambda h, hl: jnp.broadcast_to(jnp.mean(sq[:, hl], -1, keepdims=True), (c, HG_DK)))
        o = o * lax.rsqrt(ms + 1e-6)
        o_ref[pl.ds(r0, c), :] = (o * nw_ref[...] * (g * jax.nn.sigmoid(g))).astype(o_ref.dtype)
        return carry

    lax.fori_loop(0, n_chunks, chunk, 0, unroll=4)


def _hgrn(proj, lb, norm_w, ts):
    bsz, s, _ = proj.shape
    col0 = COL_HG // HG_WIDTH

    def col_spec(n):
        return pl.BlockSpec((None, ts, HG_WIDTH), lambda b, t: (b, t, col0 + n))

    head_spec = pl.BlockSpec((1, HG_WIDTH), lambda b, t: (0, 0))
    return pl.pallas_call(
        functools.partial(_hgrn_kernel, n_chunks=ts // HG_CHUNK),
        grid=(bsz, s // ts),
        in_specs=[col_spec(0), col_spec(1), col_spec(2), col_spec(3), head_spec, head_spec],
        out_specs=pl.BlockSpec((None, ts, HG_WIDTH), lambda b, t: (b, t, 0)),
        out_shape=jax.ShapeDtypeStruct((bsz, s, HG_WIDTH), BF16),
        scratch_shapes=[pltpu.VMEM((HG_HEADS, HG_DK, HG_DK), F32)],
        compiler_params=_cparams(("parallel", "arbitrary")),
    )(proj, proj, proj, proj, lb, norm_w)


def _compress_kernel(a_ref, pos_ref, w1_ref, w2_ref, o_ref):
    half = CMP_STRIDE * NSA_DH
    a = a_ref[...].astype(F32)
    n = a.shape[0]
    a1 = (a + pos_ref[:, :half]).astype(BF16)
    a2 = (a + pos_ref[:, half:]).astype(BF16)
    y1 = jnp.dot(a1, w1_ref[:half, :], preferred_element_type=F32)
    y2 = jnp.dot(a2, w1_ref[half:, :], preferred_element_type=F32)
    hid = jax.nn.gelu(y1 + pltpu.roll(y2, n - 1, 0))
    o_ref[...] = jnp.dot(hid.astype(BF16), w2_ref[...], preferred_element_type=F32).astype(o_ref.dtype)


def _compress(a, pos, w1, w2):
    _, bsz, g, n, width = a.shape
    return pl.pallas_call(
        _compress_kernel,
        grid=(2, bsz, g),
        in_specs=[pl.BlockSpec((None, None, None, n, width), lambda c, b, h: (c, b, h, 0, 0)),
                  pl.BlockSpec((None, 1, 2 * width), lambda c, b, h: (c, 0, 0)),
                  pl.BlockSpec((None, 2 * width, CMP_HIDDEN), lambda c, b, h: (c, 0, 0)),
                  pl.BlockSpec((None, CMP_HIDDEN, NSA_DH), lambda c, b, h: (c, 0, 0))],
        out_specs=pl.BlockSpec((None, None, None, n, NSA_DH), lambda c, b, h: (c, b, h, 0, 0)),
        out_shape=jax.ShapeDtypeStruct((2, bsz, g, n, NSA_DH), BF16),
        compiler_params=_cparams(("parallel", "parallel", "parallel")),
    )(a, pos, w1, w2)


def _bucket_of(dist):
    n = jnp.maximum(dist, 0)
    exact = REL_BUCKETS // 2
    large = exact + (jnp.log(jnp.maximum(n, exact).astype(F32) / exact)
                     / math.log(REL_MAX_DIST / exact) * (REL_BUCKETS - exact)).astype(jnp.int32)
    return jnp.where(n < exact, n, jnp.minimum(large, REL_BUCKETS - 1))


def _bias_kernel(tab_ref, o_ref, *, dist_fn):
    rows, cols = o_ref.shape[2], o_ref.shape[3]
    row = lax.broadcasted_iota(jnp.int32, (rows, cols), 0)
    col = lax.broadcasted_iota(jnp.int32, (rows, cols), 1)
    bucket = _bucket_of(dist_fn(pl.program_id(0), row, col))
    for h in range(NSA_HEADS):
        tab = jnp.broadcast_to(tab_ref[h:h + 1, :] * LOG2E, (rows, LANES))
        for c0 in range(0, cols, LANES):
            o_ref[h, 0, :, c0:c0 + LANES] = jnp.take_along_axis(tab, bucket[:, c0:c0 + LANES], axis=1)


def _bias_tiles(rel_bias, n_tiles, rows, cols, dist_fn):
    tab = jnp.pad(rel_bias.T, ((0, 0), (0, LANES - REL_BUCKETS)))
    return pl.pallas_call(
        functools.partial(_bias_kernel, dist_fn=dist_fn),
        grid=(n_tiles,),
        in_specs=[pl.BlockSpec((NSA_HEADS, LANES), lambda t: (0, 0))],
        out_specs=pl.BlockSpec((NSA_HEADS, 1, rows, cols), lambda t: (0, t, 0, 0)),
        out_shape=jax.ShapeDtypeStruct((NSA_HEADS, n_tiles, rows, cols), F32),
        compiler_params=_cparams(("parallel",)),
    )(tab)


def _stack_heads(qb):
    tq = qb.shape[0]
    half = lax.broadcasted_iota(jnp.int32, (tq, LANES), 1) // NSA_DH
    return jnp.concatenate(
        [jnp.where(half == hp % 2, qb[:, LANES * (hp // 2):LANES * (hp // 2 + 1)], jnp.zeros((tq, LANES), qb.dtype))
         for hp in range(NSA_HPG)], axis=0)


def _cmp_kernel(q_ref, k_ref, v_ref, bias_ref, c2s_ref, gl_ref, o_ref, qa_ref, *, tq):
    p = NSA_HPG
    t0 = pl.program_id(1) * tq
    bsz, ncp = k_ref.shape[0], k_ref.shape[1]
    bias = bias_ref[:, 0]
    tpos = t0 + lax.broadcasted_iota(jnp.int32, (tq, ncp), 0)
    cend = lax.broadcasted_iota(jnp.int32, (tq, ncp), 1) * CMP_STRIDE + (CMP_LEN - 1)
    mask = (tpos >= cend)[None]
    lane = lax.broadcasted_iota(jnp.int32, (tq, LANES), 1)
    cur = (t0 + lax.broadcasted_iota(jnp.int32, (tq, LANES), 0)) // SEL_BLOCK
    forced = (lane == 0) | (lane == cur) | (lane == cur - 1)
    visible = lane <= cur
    batch = range(bsz)
    nt = (((1,), (1,)), ((), ()))
    qs = [_stack_heads(q_ref[b]) for b in batch]
    lgs = [lax.dot_general(qs[b], k_ref[b], nt, preferred_element_type=F32) for b in batch]
    lgs = [jnp.where(mask, lg.reshape(p, tq, ncp) + bias, NEG_BIG) for lg in lgs]
    es = [jnp.where(mask, jnp.exp2(lg - jnp.max(lg, -1, keepdims=True)), 0.0) for lg in lgs]
    pcs = [e / jnp.maximum(jnp.sum(e, -1, keepdims=True), 1e-30) for e in es]
    os = [jnp.dot(pcs[b].reshape(p * tq, ncp).astype(BF16), v_ref[b], preferred_element_type=F32) for b in batch]
    imps = [sum(jnp.dot(part, c2s_ref[...], preferred_element_type=F32) for part in _split3(jnp.sum(pc, axis=0)))
            for pc in pcs]
    for b in batch:
        o_ref[b] = _heads_to_lanes(os[b], jax.nn.sigmoid(gl_ref[b, :, :, 0:1].astype(F32)), tq, False)
    scores = [jnp.where(forced, FORCE_SCORE, jnp.where(visible, imp, NEG_BIG)) for imp in imps]
    for _ in range(N_SEL):
        hits = [lane == jnp.argmax(score, axis=-1, keepdims=True) for score in scores]
        scores = [jnp.where(hit, -jnp.inf, score) for hit, score in zip(hits, scores)]
    for b in batch:
        pen = jnp.where((scores[b] == -jnp.inf) & visible, 0.0, NEG_BIG).astype(qa_ref.dtype)
        qa_ref[b] = jnp.concatenate([jnp.concatenate([pen] * p, axis=0), qs[b]], axis=1).reshape(p, tq, 2 * LANES)


def _heads_to_lanes(x, gates, tq, normalise):
    low = lax.broadcasted_iota(jnp.int32, (tq, LANES), 1) < NSA_DH
    if normalise:
        swapped = pltpu.roll(x, NSA_DH, 1)
    pieces = []
    for pair in range(NSA_HPG // 2):
        ev = slice(2 * pair * tq, (2 * pair + 1) * tq)
        od = slice((2 * pair + 1) * tq, (2 * pair + 2) * tq)
        if normalise:
            even, odd = x[ev] / swapped[ev], swapped[od] / x[od]
        else:
            even, odd = x[ev], x[od]
        pieces.append(jnp.where(low, even * gates[2 * pair], odd * gates[2 * pair + 1]))
    return jnp.concatenate(pieces, axis=1)


def _out_spec(bsz, tq):
    return pl.BlockSpec((bsz, tq, NSA_HPG * NSA_DH), lambda h, i: (0, i, h))


def _q_spec(bsz, tq):
    width = NSA_HPG * NSA_DH
    return pl.BlockSpec((bsz, tq, width), lambda h, i: (0, i, COL_NQ // width + h))


def _cmp_branch(proj, kc2, vc, bias_c, c2s, gl5, tq):
    bsz, s, _ = proj.shape
    g, p, dh = NSA_GROUPS, NSA_HPG, NSA_DH
    ncp = kc2.shape[2]
    qmap = lambda h, i: (0, h, 0, i, 0)
    return pl.pallas_call(
        functools.partial(_cmp_kernel, tq=tq),
        grid=(g, s // tq),
        in_specs=[_q_spec(bsz, tq),
                  pl.BlockSpec((bsz, None, ncp, LANES), lambda h, i: (0, h, 0, 0)),
                  pl.BlockSpec((bsz, None, ncp, LANES), lambda h, i: (0, h, 0, 0)),
                  pl.BlockSpec((p, 1, tq, ncp), lambda h, i: (h, i, 0, 0)),
                  pl.BlockSpec((ncp, LANES), lambda h, i: (0, 0)),
                  pl.BlockSpec((bsz, None, p, tq, 3), qmap)],
        out_specs=[_out_spec(bsz, tq),
                   pl.BlockSpec((bsz, None, p, tq, 2 * LANES), qmap)],
        out_shape=[jax.ShapeDtypeStruct((bsz, s, NSA_WIDTH), F32),
                   jax.ShapeDtypeStruct((bsz, g, p, s, 2 * LANES), BF16)],
        compiler_params=_cparams(("parallel", "parallel")),
    )(proj, kc2, vc, bias_c, c2s, gl5)


def _win_kernel(*refs, tq, n_back):
    nk = n_back + 1
    q_ref, k_refs, v_refs, b_refs = refs[0], refs[1:1 + nk], refs[1 + nk:1 + 2 * nk], refs[1 + 2 * nk:1 + 3 * nk]
    gl_ref, acc_ref, o_ref = refs[1 + 3 * nk:]
    p = NSA_HPG
    bsz = q_ref.shape[0]
    t0 = pl.program_id(1) * tq
    row = lax.broadcasted_iota(jnp.int32, (tq, tq), 0)
    col = lax.broadcasted_iota(jnp.int32, (tq, tq), 1)
    masks = [((row - col + d * tq >= 0) & (row - col + d * tq < WINDOW) & (t0 - d * tq >= 0))[None]
             for d in range(nk)]
    biases = [b_ref[:, 0] for b_ref in b_refs]
    nt = (((1,), (1,)), ((), ()))
    batch = range(bsz)
    qs = [_stack_heads(q_ref[b]) for b in batch]
    lgs = [[lax.dot_general(qs[b], k_ref[b], nt, preferred_element_type=F32).reshape(p, tq, tq)
            for k_ref in k_refs] for b in batch]
    lgs = [[jnp.where(mask, lg + bias, NEG_BIG) for lg, mask, bias in zip(lgs[b], masks, biases)] for b in batch]
    ms = [functools.reduce(jnp.maximum, [jnp.max(lg, -1, keepdims=True) for lg in lgs[b]]) for b in batch]
    es = [[jnp.exp2(lg - ms[b]).reshape(p * tq, tq).astype(BF16) for lg in lgs[b]] for b in batch]
    os = [sum(jnp.dot(e, v_ref[b], preferred_element_type=F32) for e, v_ref in zip(es[b], v_refs)) for b in batch]
    for b in batch:
        o_ref[b] = acc_ref[b] + _heads_to_lanes(os[b], jax.nn.sigmoid(gl_ref[b, :, :, 2:3].astype(F32)), tq, True)


def _win_branch(proj, kw2, vw_aug, bias_t, gl5, acc, tq):
    bsz, s, _ = acc.shape
    g, p, dh = NSA_GROUPS, NSA_HPG, NSA_DH
    n_back = WINDOW // tq
    back = [lambda h, i, d=d: (0, h, jnp.maximum(i - d, 0), 0) for d in range(n_back + 1)]
    qmap = lambda h, i: (0, h, 0, i, 0)
    in_specs = ([_q_spec(bsz, tq)]
                + [pl.BlockSpec((bsz, None, tq, LANES), m) for m in back]
                + [pl.BlockSpec((bsz, None, tq, LANES), m) for m in back]
                + [pl.BlockSpec((p, 1, tq, tq), lambda h, i, d=d: (h, d, 0, 0)) for d in range(n_back + 1)]
                + [pl.BlockSpec((bsz, None, p, tq, 3), qmap), _out_spec(bsz, tq)])
    return pl.pallas_call(
        functools.partial(_win_kernel, tq=tq, n_back=n_back),
        grid=(g, s // tq),
        in_specs=in_specs,
        out_specs=_out_spec(bsz, tq),
        out_shape=jax.ShapeDtypeStruct((bsz, s, NSA_WIDTH), F32),
        input_output_aliases={len(in_specs) - 1: 0},
        compiler_params=_cparams(("parallel", "parallel")),
    )(proj, *([kw2] * (n_back + 1)), *([vw_aug] * (n_back + 1)), *([bias_t] * (n_back + 1)), gl5, acc)


def _sel_kernel(it_ref, jt_ref, q_ref, k_ref, v_ref, bias_ref, gl_ref, acc_ref, o_ref, m_sc, a_sc,
                *, tq, tk, bsz):
    p = NSA_HPG
    nq = p * tq
    step = pl.program_id(1)
    i = it_ref[step]
    j = jt_ref[step]
    t0 = i * tq
    s0 = j * tk
    last = s0 + tk > t0

    @pl.when(j == 0)
    def _():
        m_sc[...] = jnp.full_like(m_sc, NEG_BIG)
        a_sc[...] = jnp.zeros_like(a_sc)

    def accumulate(on_diagonal):
        nt = (((1,), (1,)), ((), ()))
        bias = bias_ref[:, 0].reshape(nq, tk)
        if on_diagonal:
            causal = (t0 + lax.broadcasted_iota(jnp.int32, (tq, tk), 0)
                      >= s0 + lax.broadcasted_iota(jnp.int32, (tq, tk), 1))
            causal = jnp.concatenate([causal] * p, axis=0)
        for b in range(bsz):
            lg = lax.dot_general(q_ref[b].reshape(nq, 2 * LANES), k_ref[b], nt, preferred_element_type=F32) + bias
            if on_diagonal:
                lg = jnp.where(causal, lg, NEG_BIG)
            m_old = m_sc[b]
            m_new = jnp.maximum(m_old, jnp.max(lg, -1, keepdims=True))
            e = jnp.exp2(lg - jnp.concatenate([m_new] * (tk // LANES), axis=1))
            a_sc[b] = (jnp.exp2(m_old - m_new) * a_sc[b]
                       + jnp.dot(e.astype(BF16), v_ref[b], preferred_element_type=F32))
            m_sc[b] = m_new

    @pl.when(jnp.logical_not(last))
    def _():
        accumulate(False)

    @pl.when(last)
    def _():
        accumulate(True)
        gate = jax.nn.sigmoid(gl_ref[:, :, :, 1:2].astype(F32))
        for b in range(bsz):
            o_ref[b] = (acc_ref[b] + _heads_to_lanes(a_sc[b], gate[b], tq, True)).astype(o_ref.dtype)


def _sel_branch(q_aug, k_aug, v_aug, bias_t, gl5, acc, tq, tk):
    bsz, g, p, s, wq = q_aug.shape
    r = tk // tq
    pairs = [(i, j) for i in range(s // tq) for j in range(i // r + 1)]
    it = jnp.asarray([ij[0] for ij in pairs], jnp.int32)
    jt = jnp.asarray([ij[1] for ij in pairs], jnp.int32)
    nd = bias_t.shape[1] - 1
    qmap = lambda h, t, it, jt: (0, h, 0, it[t], 0)
    kmap = lambda h, t, it, jt: (0, h, jt[t], 0)
    omap = lambda h, t, it, jt: (0, it[t], h)
    grid_spec = pltpu.PrefetchScalarGridSpec(
        num_scalar_prefetch=2,
        grid=(g, len(pairs)),
        in_specs=[pl.BlockSpec((bsz, None, p, tq, wq), qmap),
                  pl.BlockSpec((bsz, None, tk, wq), kmap),
                  pl.BlockSpec((bsz, None, tk, LANES), kmap),
                  pl.BlockSpec((p, 1, tq, tk),
                               lambda h, t, it, jt: (h, jnp.minimum(it[t] - r * jt[t], nd), 0, 0)),
                  pl.BlockSpec((bsz, None, p, tq, 3), qmap),
                  pl.BlockSpec((bsz, tq, p * NSA_DH), omap)],
        out_specs=pl.BlockSpec((bsz, tq, p * NSA_DH), omap),
        scratch_shapes=[pltpu.VMEM((bsz, p * tq, LANES), F32),
                        pltpu.VMEM((bsz, p * tq, LANES), F32)],
    )
    return pl.pallas_call(
        functools.partial(_sel_kernel, tq=tq, tk=tk, bsz=bsz),
        grid_spec=grid_spec,
        out_shape=jax.ShapeDtypeStruct((bsz, s, NSA_WIDTH), BF16),
        compiler_params=_cparams(("parallel", "arbitrary")),
    )(it, jt, q_aug, k_aug, v_aug, bias_t, gl5, acc)


def _merge_kernel(x_ref, oa_ref, ob_ref, ga_ref, gb_ref, wa_ref, wb_ref, wo_ref, g_ref, b_ref, o_ref):
    ya = jnp.dot(oa_ref[...].astype(BF16), wa_ref[...], preferred_element_type=F32)
    yb = jnp.dot(ob_ref[...].astype(BF16), wb_ref[...], preferred_element_type=F32)
    y = jax.nn.sigmoid(ga_ref[...].astype(F32)) * ya + jax.nn.sigmoid(gb_ref[...].astype(F32)) * yb
    mix = jnp.dot(y.astype(BF16), wo_ref[...], preferred_element_type=F32)
    o_ref[...] = _layer_norm(DN_ALPHA * x_ref[...] + mix, g_ref[...], b_ref[...])


def _merge(x, o_a, o_b, proj, wa, wb, wo, g, b, tm):
    t, d = x.shape
    nga = COL_GA // d
    row = lambda i: (i, 0)
    const = lambda i: (0, 0)
    return pl.pallas_call(
        _merge_kernel,
        grid=(t // tm,),
        in_specs=[pl.BlockSpec((tm, d), row),
                  pl.BlockSpec((tm, HG_WIDTH), row),
                  pl.BlockSpec((tm, NSA_WIDTH), row),
                  pl.BlockSpec((tm, d), lambda i: (i, nga)),
                  pl.BlockSpec((tm, d), lambda i: (i, nga + 1)),
                  pl.BlockSpec((HG_WIDTH, d), const),
                  pl.BlockSpec((NSA_WIDTH, d), const),
                  pl.BlockSpec((d, d), const),
                  pl.BlockSpec((1, d), const),
                  pl.BlockSpec((1, d), const)],
        out_specs=pl.BlockSpec((tm, d), row),
        out_shape=jax.ShapeDtypeStruct((t, d), F32),
        compiler_params=_cparams(("parallel",)),
    )(x, o_a, o_b, proj, proj, wa, wb, wo, g, b)


def _swiglu_step(xb, wg_ref, wu_ref, wd_ref):
    hg = jnp.dot(xb, wg_ref[...], preferred_element_type=F32)
    hu = jnp.dot(xb, wu_ref[...], preferred_element_type=F32)
    h = (hg * jax.nn.sigmoid(hg)) * hu
    return jnp.dot(h.astype(BF16), wd_ref[...], preferred_element_type=F32)


def _ffn_kernel(x_ref, wg_ref, wu_ref, wd_ref, g_ref, b_ref, o_ref):
    x = x_ref[...]
    f = _swiglu_step(x.astype(BF16), wg_ref, wu_ref, wd_ref)
    o_ref[...] = _layer_norm(DN_ALPHA * x + f, g_ref[...], b_ref[...])


def _ffn(x, wg, wu, wd, g, b, tm):
    t, d = x.shape
    f = wg.shape[1]
    const = lambda i: (0, 0)
    once = pl.Buffered(1)
    return pl.pallas_call(
        _ffn_kernel,
        grid=(t // tm,),
        in_specs=[pl.BlockSpec((tm, d), lambda i: (i, 0)),
                  pl.BlockSpec((d, f), const, pipeline_mode=once),
                  pl.BlockSpec((d, f), const, pipeline_mode=once),
                  pl.BlockSpec((f, d), const, pipeline_mode=once),
                  pl.BlockSpec((1, d), const),
                  pl.BlockSpec((1, d), const)],
        out_specs=pl.BlockSpec((tm, d), lambda i: (i, 0)),
        out_shape=jax.ShapeDtypeStruct((t, d), F32),
        compiler_params=_cparams(("parallel",)),
    )(x, wg, wu, wd, g, b)


def _router_kernel(x_ref, w_ref, o_ref):
    xs, ws = _split3(x_ref[...]), _split3(w_ref[...])
    logits = sum(jnp.dot(xs[i], ws[j], preferred_element_type=F32) for i, j in ((0, 0), (0, 1), (1, 0)))
    lane =lax.broadcasted_iota(jnp.int32, logits.shape, 1).astype(F32)
    logits = jnp.where(lane < N_EXPERTS, logits, -jnp.inf)
    v1 = jnp.max(logits, -1, keepdims=True)
    e1 = jnp.min(jnp.where(logits == v1, lane, float(LANES)), -1, keepdims=True)
    rest = jnp.where(lane == e1, -jnp.inf, logits)
    v2 = jnp.max(rest, -1, keepdims=True)
    e2 = jnp.min(jnp.where(rest == v2, lane, float(LANES)), -1, keepdims=True)
    x2 = jnp.exp(v2 - v1)
    den = 1.0 + x2
    o_ref[...] = jnp.where(lane == 0, e1, jnp.where(lane == 1, e2, jnp.where(
        lane == 2, 1.0 / den, jnp.where(lane == 3, x2 / den, 0.0))))


def _router(x, w, tm):
    t, d = x.shape
    return pl.pallas_call(
        _router_kernel,
        grid=(t // tm,),
        in_specs=[pl.BlockSpec((tm, d), lambda i: (i, 0)), pl.BlockSpec((d, LANES), lambda i: (0, 0))],
        out_specs=pl.BlockSpec((tm, LANES), lambda i: (i, 0)),
        out_shape=jax.ShapeDtypeStruct((t, LANES), F32),
        compiler_params=_cparams(("parallel",)),
    )(x, w)


def _expert_kernel(be_ref, x_ref, wg_ref, wu_ref, wd_ref, o_ref, acc_ref):
    j = pl.program_id(1)

    @pl.when(j == 0)
    def _():
        acc_ref[...] = jnp.zeros_like(acc_ref)

    xb = x_ref[...].astype(BF16)
    hg = jnp.dot(xb, wg_ref[...].astype(BF16), preferred_element_type=F32)
    hu = jnp.dot(xb, wu_ref[...].astype(BF16), preferred_element_type=F32)
    h = (hg * jax.nn.sigmoid(hg)) * hu
    acc_ref[...] += jnp.dot(h.astype(BF16), wd_ref[...].astype(BF16), preferred_element_type=F32)

    @pl.when(j == pl.num_programs(1) - 1)
    def _():
        o_ref[...] = acc_ref[...].astype(o_ref.dtype)


def _experts(blk_e, xs, wg, wu, wd, tf):
    rows, d = xs.shape
    f = wg.shape[2]
    tm = MOE_ROW_BLOCK
    grid_spec = pltpu.PrefetchScalarGridSpec(
        num_scalar_prefetch=1,
        grid=(rows // tm, f // tf),
        in_specs=[pl.BlockSpec((tm, d), lambda i, j, be: (i, 0)),
                  pl.BlockSpec((None, d, tf), lambda i, j, be: (be[i], 0, j)),
                  pl.BlockSpec((None, d, tf), lambda i, j, be: (be[i], 0, j)),
                  pl.BlockSpec((None, tf, d), lambda i, j, be: (be[i], j, 0))],
        out_specs=pl.BlockSpec((tm, d), lambda i, j, be: (i, 0)),
        scratch_shapes=[pltpu.VMEM((tm, d), F32)],
    )
    return pl.pallas_call(
        _expert_kernel,
        grid_spec=grid_spec,
        out_shape=jax.ShapeDtypeStruct((rows, d), F32),
        compiler_params=_cparams(("parallel", "arbitrary")),
    )(blk_e, xs, wg, wu, wd)


def _combine_kernel(x_ref, y1_ref, y2_ref, gt_ref, g_ref, b_ref, o_ref):
    f = y1_ref[...].astype(F32) * gt_ref[:, 2:3] + y2_ref[...].astype(F32) * gt_ref[:, 3:4]
    o_ref[...] = _layer_norm(DN_ALPHA * x_ref[...] + f, g_ref[...], b_ref[...])


def _combine(x, y1, y2, route, g, b, tm):
    t, d = x.shape
    row = lambda i: (i, 0)
    const = lambda i: (0, 0)
    return pl.pallas_call(
        _combine_kernel,
        grid=(t // tm,),
        in_specs=[pl.BlockSpec((tm, d), row), pl.BlockSpec((tm, d), row), pl.BlockSpec((tm, d), row),
                  pl.BlockSpec((tm, LANES), row), pl.BlockSpec((1, d), const), pl.BlockSpec((1, d), const)],
        out_specs=pl.BlockSpec((tm, d), row),
        out_shape=jax.ShapeDtypeStruct((t, d), F32),
        compiler_params=_cparams(("parallel",)),
    )(x, y1, y2, route, g, b)


def _moe(x, w_router, wg, wu, wd, g, b):
    t, d = x.shape
    tk_ = t * TOP_K
    route = _router(x, jnp.pad(w_router, ((0, 0), (0, LANES - N_EXPERTS))), 512)
    flat_e = route[:, :TOP_K].astype(jnp.int32).reshape(-1)
    onehot = (flat_e[:, None] == jnp.arange(N_EXPERTS)[None, :]).astype(jnp.int32)
    csum = jnp.cumsum(onehot, axis=0)
    counts = csum[-1]
    rank = jnp.sum(onehot * csum, axis=1) - 1
    padded = (counts + MOE_ROW_BLOCK - 1) // MOE_ROW_BLOCK * MOE_ROW_BLOCK
    pend = jnp.cumsum(padded)
    dest = (pend - padded)[flat_e] + rank
    n_blocks = -(-(tk_ + N_EXPERTS * (MOE_ROW_BLOCK - 1)) // MOE_ROW_BLOCK)
    n_rows = n_blocks * MOE_ROW_BLOCK
    row_tok = (jnp.arange(n_rows, dtype=jnp.int32) % t).at[dest].set(jnp.arange(tk_, dtype=jnp.int32) // TOP_K)
    blk_e = jnp.minimum(jnp.searchsorted(pend, jnp.arange(n_blocks) * MOE_ROW_BLOCK, side='right'),
                        N_EXPERTS - 1).astype(jnp.int32)
    xs = x[row_tok]
    ys = _experts(blk_e, xs, wg, wu, wd, 512)
    dest2 = dest.reshape(t, TOP_K)
    return _combine(x, ys[dest2[:, 0]], ys[dest2[:, 1]], route, g, b, 512)


def _pack_w_in(w_in):
    offs = np.concatenate([[0], np.cumsum(IN_SIZES)])
    seg = [w_in[:, offs[j]:offs[j + 1]] for j in range(len(IN_SIZES))]
    seg[4] = seg[4] * (NSA_DH ** -0.5 * LOG2E)
    seg[11] = jnp.pad(seg[11], ((0, 0), (0, LANES - 3 * NSA_HEADS)))
    seg = seg[12:14] + seg[0:12]
    packed = jnp.concatenate(seg + [jnp.zeros((w_in.shape[0], PROJ_WP - PROJ_W), w_in.dtype)], axis=1)
    return packed.astype(BF16)


def _cmp_to_sel(n_cmp_pad, n_cmp):
    cs = np.arange(n_cmp_pad)[:, None] * CMP_STRIDE
    ss = np.arange(LANES)[None, :] * SEL_BLOCK
    overlap = np.clip(np.minimum(cs + CMP_LEN, ss + SEL_BLOCK) - np.maximum(cs, ss), 0, None) / CMP_LEN
    overlap[n_cmp:] = 0.0
    return jnp.asarray(overlap, BF16)


def _token_mixer(x, w_in_p, lb, hg_norm_w, cmp_pos, cmp_w1, cmp_w2, bias_c, bias_t, wa, wb, wo, ln_g, ln_b):
    bsz, s, d = x.shape
    g, p, dh = NSA_GROUPS, NSA_HPG, NSA_DH
    xf = x.reshape(bsz * s, d)
    proj = _project(xf, w_in_p, 1024, PROJ_WP // 2).reshape(bsz, s, PROJ_WP)
    o_a = _hgrn(proj, lb, hg_norm_w, 1024)

    def heads(c0, width):
        return proj[:, :, c0:c0 + width]

    kv = heads(COL_KV, 6 * KV_WIDTH).reshape(bsz, s, 6, g, dh)
    n16 = s // CMP_STRIDE
    a = kv[:, :, 0:2].reshape(bsz, n16, CMP_STRIDE, 2, g, dh).transpose(3, 0, 4, 1, 2, 5)
    a = a.reshape(2, bsz, g, n16, CMP_STRIDE * dh)
    kvc = _compress(a, cmp_pos.reshape(2, 1, CMP_LEN * dh), cmp_w1.astype(BF16), cmp_w2.astype(BF16))
    kvh = kv[:, :, 2:6].astype(BF16).transpose(2, 0, 3, 1, 4)
    gl5 = heads(COL_NG, 3 * NSA_HEADS).reshape(bsz, s, g, p, 3).transpose(0, 2, 3, 1, 4)
    n_cmp = (s - CMP_LEN) // CMP_STRIDE + 1
    acc, q_aug = _cmp_branch(proj, jnp.concatenate([kvc[0], kvc[0]], axis=-1),
                             jnp.concatenate([kvc[1], kvc[1]], axis=-1), bias_c,
                             _cmp_to_sel(n16, n_cmp), gl5, ATT_TQ)
    ones = jnp.ones((bsz, g, s, LANES - dh), BF16)
    acc = _win_branch(proj, jnp.concatenate([kvh[2], kvh[2]], axis=-1), jnp.concatenate([kvh[3], ones], axis=-1),
                      bias_t, gl5, acc, ATT_TQ)
    block_of_key = (np.arange(s)[:, None] // SEL_BLOCK == np.arange(LANES)[None, :])
    k_aug = jnp.concatenate([jnp.broadcast_to(jnp.asarray(block_of_key, BF16), (bsz, g, s, LANES)), kvh[0],
                             kvh[0]], axis=-1)
    v_aug = jnp.concatenate([kvh[1], ones], axis=-1)
    o_b = _sel_branch(q_aug, k_aug, v_aug, bias_t, gl5, acc, ATT_TQ, ATT_TK)
    o_b = o_b.reshape(bsz * s, NSA_WIDTH)
    return _merge(xf, o_a.reshape(bsz * s, HG_WIDTH), o_b, proj.reshape(bsz * s, PROJ_WP),
                  wa, wb, wo, ln_g, ln_b, 512)


def kernel(x, w_in, hg_lb_logits, hg_norm_w, cmp_pos, cmp_w1, cmp_w2, rel_bias, w_branch_a, w_branch_b,
           w_out, ln1_g, ln1_b, ln2_g, ln2_b, ffn_w_gate, ffn_w_up, ffn_w_down, moe_router, moe_w_gate,
           moe_w_up, moe_w_down):
    bsz, s, d = x.shape
    depth = w_in.shape[0]
    p_lb = jax.nn.softmax(hg_lb_logits.astype(F32), axis=0)
    lbs = jnp.cumsum(p_lb, axis=0) - p_lb[0]
    n16 = s // CMP_STRIDE
    bias_c = _bias_tiles(rel_bias, s // ATT_TQ, ATT_TQ, n16,
                         lambda t, r, c: t * ATT_TQ + r - (c * CMP_STRIDE + CMP_LEN - 1))
    bias_t = _bias_tiles(rel_bias, BIAS_ND + 1, ATT_TQ, ATT_TK, lambda t, r, c: t * ATT_TQ + r - c)
    f_pad = -(-D_FF // LANES) * LANES - D_FF
    xf = x.reshape(bsz * s, d)
    for l in range(depth):
        xf = _token_mixer(xf.reshape(bsz, s, d), _pack_w_in(w_in[l]), lbs[l][None], hg_norm_w[l][None],
                          cmp_pos[l], cmp_w1[l], cmp_w2[l], bias_c, bias_t,
                          w_branch_a[l].astype(BF16), w_branch_b[l].astype(BF16), w_out[l].astype(BF16),
                          ln1_g[l][None], ln1_b[l][None])
        if l % 2 == 0:
            wg = jnp.pad(ffn_w_gate[l // 2], ((0, 0), (0, f_pad))).astype(BF16)
            wu = jnp.pad(ffn_w_up[l // 2], ((0, 0), (0, f_pad))).astype(BF16)
            wd = jnp.pad(ffn_w_down[l // 2], ((0, f_pad), (0, 0))).astype(BF16)
            xf = _ffn(xf, wg, wu, wd, ln2_g[l][None], ln2_b[l][None], 512)
        else:
            xf = _moe(xf, moe_router[l // 2], moe_w_gate[l // 2], moe_w_up[l // 2], moe_w_down[l // 2],
                      ln2_g[l][None], ln2_b[l][None])
    return xf.reshape(bsz, s, d)
```
